```python
import math
import jax
import jax.numpy as jnp
from jax import lax
import numpy as np

D_MODEL = 2048
BATCH = 2
SEQ = 4096
DEPTH = 2

GRID_W = 64
CTX_LEN = 256
NORM_EPS = 1e-6
N_MOD = 6
SHORT_CONV = 3

HY_WIDTH = 1024
HY_BANDS = 16
HY_EMB = 2 * HY_BANDS + 1
HY_FILTER_ORDER = 64
HY_FAST_DECAY = 0.3
HY_SLOW_DECAY = 1.5
HY_DECAY_TARGET = 1e-2

NA_HEADS = 16
NA_HEAD_DIM = 64
NA_WIDTH = NA_HEADS * NA_HEAD_DIM
NA_ROWS = 8
NA_COLS = 16

RW_HEADS = 16
RW_HEAD_DIM = 64
RW_WIDTH = RW_HEADS * RW_HEAD_DIM
RW_DECAY_LORA = 64
RW_AAA_LORA = 64
RW_GATE_LORA = 128
RW_GN_EPS = 64e-5
RW_COLS = 3 * RW_WIDTH + 2 * RW_DECAY_LORA + 2 * RW_AAA_LORA + RW_GATE_LORA
RW_SPLITS = [RW_WIDTH, 2 * RW_WIDTH, 3 * RW_WIDTH, 3 * RW_WIDTH + 2 * RW_DECAY_LORA,
             3 * RW_WIDTH + 2 * RW_DECAY_LORA + 2 * RW_AAA_LORA]

N_BRANCH = 3
COL_HY = 3 * HY_WIDTH
COL_NA = COL_HY + 3 * NA_WIDTH
COL_RW = COL_NA + RW_COLS
N_COLS = COL_RW + N_BRANCH * D_MODEL

N_EXPERTS = 64
TOP_K = 6
N_GROUPS = 8
TOPK_GROUPS = 4
EXPERT_FF = 512
SHARED_FF = 512
ROUTE_SCALE = 2.5
EXPERT_BLOCK = 128

kernel_name = 'hybrid_hyena_natten_rwkv7_moe_dit'


def rmsnorm(x, g):
    xf = x.astype(jnp.float32)
    y = xf * lax.rsqrt(jnp.mean(xf * xf, axis=-1, keepdims=True) + NORM_EPS)
    return (y * g.astype(jnp.float32)).astype(x.dtype)


def split_heads(t, n_heads):
    return t.reshape(t.shape[:-1] + (n_heads, t.shape[-1] // n_heads))


def short_conv(u, w):
    T = u.shape[1]
    half = SHORT_CONV // 2
    up = jnp.pad(u, ((0, 0), (half, half), (0, 0)))
    y = up[:, 0:T] * w[0]
    for j in range(1, SHORT_CONV):
        y = y + up[:, j:j + T] * w[j]
    return y


def hyena_filters(L, w1, b1, w2, b2, freq, w3):
    f32 = jnp.float32
    t = jnp.linspace(0.0, 1.0, L, dtype=f32)[:, None]
    omega = (2.0 * math.pi / L) * jnp.arange(L, dtype=f32)[:, None]
    bands = jnp.linspace(1e-4, HY_BANDS - 1, HY_BANDS, dtype=f32)[None, :]
    z = jnp.concatenate([t, jnp.cos(omega * bands), -jnp.sin(omega * bands)], axis=-1)
    hdn = jnp.sin(freq[0].astype(f32) * (z @ w1.astype(f32) + b1.astype(f32)))
    hdn = jnp.sin(freq[1].astype(f32) * (hdn @ w2.astype(f32) + b2.astype(f32)))
    h = (hdn @ w3.astype(f32)).reshape(L, 2, 2, HY_WIDTH)
    deltas = jnp.abs(jnp.linspace(math.log(HY_DECAY_TARGET) / HY_SLOW_DECAY,
                                  math.log(HY_DECAY_TARGET) / HY_FAST_DECAY, HY_WIDTH, dtype=f32))
    h = h * jnp.exp(-t * deltas)[:, None, None, :]
    return h * lax.rsqrt(jnp.sum(h * h, axis=(0, 2), keepdims=True))


def long_conv(u, h, bias):
    L, C = h.shape[0], h.shape[2]
    k = jnp.concatenate([h[:, 0], jnp.zeros((1, C), h.dtype), h[:0:-1, 1]], axis=0)
    uf = jnp.fft.rfft(u.astype(jnp.float32), n=2 * L, axis=1)
    kf = jnp.fft.rfft(k, n=2 * L, axis=0)
    y = jnp.fft.irfft(uf * kf[None], n=2 * L, axis=1)[:, :L]
    return (y + u.astype(jnp.float32) * bias.astype(jnp.float32)).astype(u.dtype)


def hyena(cols, conv_w, filt, bias):
    u = short_conv(cols, conv_w)
    x1, x2, v = jnp.split(u, 3, axis=-1)
    z = x1 * long_conv(v, filt[:, 0], bias[0])
    return x2 * long_conv(z, filt[:, 1], bias[1])


def na_latent(q, k, v, k_ctx, v_ctx, rpb):
    B, L, H, dh = q.shape
    R = L // GRID_W
    kr = min(NA_ROWS, R)
    scale = dh ** -0.5
    qg = jnp.moveaxis(q.reshape(B, R, GRID_W, H, dh), 1, 0)
    kg = k.reshape(B, R, GRID_W, H, dh)
    vg = v.reshape(B, R, GRID_W, H, dh)
    rows = jnp.arange(R)
    row_idx = jnp.clip(rows - kr // 2, 0, R - kr)[:, None] + jnp.arange(kr)[None, :]
    cols = jnp.arange(GRID_W)
    col_start = jnp.clip(cols - NA_COLS // 2, 0, GRID_W - NA_COLS)[:, None]
    in_win = (cols[None, :] >= col_start) & (cols[None, :] < col_start + NA_COLS)
    rel_col = jnp.clip(cols[None, :] - cols[:, None], 1 - NA_COLS, NA_COLS - 1) + NA_COLS - 1
    rpb_col = rpb.astype(jnp.float32)[:, :, rel_col]
    n_loc = kr * GRID_W

    def row_block(args):
        r, ridx, q_r = args
        k_blk = kg[:, ridx]
        v_blk = vg[:, ridx]
        s_loc = jnp.einsum('bqhd,bikhd->bhqik', q_r, k_blk).astype(jnp.float32) * scale
        bias = jnp.transpose(rpb_col[:, ridx - r + NA_ROWS - 1], (0, 2, 1, 3))
        s_loc = jnp.where(in_win[:, None, :], s_loc + bias[None], -jnp.inf)
        s_ctx = jnp.einsum('bqhd,bchd->bhqc', q_r, k_ctx).astype(jnp.float32) * scale
        s = jnp.concatenate([s_loc.reshape(B, H, GRID_W, n_loc), s_ctx], axis=-1)
        p = jax.nn.softmax(s, axis=-1).astype(v.dtype)
        p_loc = p[..., :n_loc].reshape(B, H, GRID_W, kr, GRID_W)
        return (jnp.einsum('bhqik,bikhd->bqhd', p_loc, v_blk)
                + jnp.einsum('bhqc,bchd->bqhd', p[..., n_loc:], v_ctx))

    out = lax.map(row_block, (rows, row_idx, qg))
    return jnp.moveaxis(out, 0, 1).reshape(B, L, H * dh)


def ctx_attn(q, k, v):
    B, T, H, dh = q.shape
    s = jnp.einsum('bqhd,bkhd->bhqk', q, k).astype(jnp.float32) * dh ** -0.5
    p = jax.nn.softmax(s, axis=-1).astype(v.dtype)
    return jnp.einsum('bhqk,bkhd->bqhd', p, v).reshape(B, T, H * dh)


def rwkv_prep(cols, shift_w, w0, w2, a0, a2, g2, k_k, k_a):
    u = short_conv(cols, shift_w).astype(jnp.float32)
    B, T, _ = u.shape
    r, k, v, wlo, alo, glo = jnp.split(u, RW_SPLITS, axis=-1)
    wlo = wlo.reshape(B, T, 2, RW_DECAY_LORA)
    alo = alo.reshape(B, T, 2, RW_AAA_LORA)
    w = -jax.nn.softplus(-(w0 + jnp.einsum('btdr,drc->btdc', jnp.tanh(wlo), w2))) - 0.5
    decay = jnp.exp(-jnp.exp(w))
    a = jax.nn.sigmoid(a0 + jnp.einsum('btdr,drc->btdc', alo, a2))
    g = jax.nn.sigmoid(glo) @ g2
    kk = split_heads(k * k_k, RW_HEADS)
    kk = kk / jnp.maximum(jnp.sqrt(jnp.sum(kk * kk, axis=-1, keepdims=True)), 1e-12)
    kd = k[:, :, None, :] * (1.0 + (a - 1.0) * k_a)
    hh = lambda t: split_heads(t, RW_HEADS)
    return hh(r), hh(v), kk, hh(decay), hh(a), hh(kd), g


def wkv_scan(r, w, k, v, kk, a, state0, reverse, with_out):
    xs = tuple(jnp.moveaxis(t, 1, 0) for t in (r, w, k, v, -kk, kk * a))

    def step(S, inp):
        r_t, w_t, k_t, v_t, a_t, b_t = inp
        sa = jnp.einsum('bhij,bhj->bhi', S, a_t)
        S = S * w_t[:, :, None, :] + sa[..., None] * b_t[:, :, None, :] + v_t[..., None] * k_t[:, :, None, :]
        return S, (jnp.einsum('bhij,bhj->bhi', S, r_t) if with_out else None)

    S, ys = lax.scan(step, state0, xs, reverse=reverse)
    return (jnp.moveaxis(ys, 0, 1) if with_out else None), S


def rwkv_out(y_f, y_b, r, kd, v, g, r_k, ln_w, ln_b, dtype):
    y = y_f + y_b
    B, T = y.shape[0], y.shape[1]
    mu = jnp.mean(y, axis=-1, keepdims=True)
    var = jnp.mean(jnp.square(y - mu), axis=-1, keepdims=True)
    y = (y - mu) * lax.rsqrt(var + RW_GN_EPS)
    bonus = jnp.sum(jnp.sum(r[:, :, None] * kd * r_k, axis=-1, keepdims=True) * v[:, :, None], axis=2)
    y = y.reshape(B, T, RW_WIDTH) * ln_w + ln_b + bonus.reshape(B, T, RW_WIDTH)
    return (y * g).astype(dtype)


def rwkv_mixer(rw_lat, rw_ctx, lp, with_ctx):
    args = (lp['rw_shift'], lp['rw_w0'], lp['rw_w2'], lp['rw_a0'], lp['rw_a2'], lp['rw_g2'], lp['rw_kk'], lp['rw_ka'])
    out_args = (lp['rw_rk'], lp['rw_ln_w'], lp['rw_ln_b'])
    rc, vc, kkc, dc, ac, kdc, gc = rwkv_prep(rw_ctx, *args)
    s0 = jnp.zeros((rw_ctx.shape[0], RW_HEADS, RW_HEAD_DIM, RW_HEAD_DIM), jnp.float32)
    ycf, s_f = wkv_scan(rc, dc[:, :, 0], kdc[:, :, 0], vc, kkc, ac[:, :, 0], s0, False, with_ctx)
    ycb, s_b = wkv_scan(rc, dc[:, :, 1], kdc[:, :, 1], vc, kkc, ac[:, :, 1], s0, True, with_ctx)
    rl, vl, kkl, dl, al, kdl, gl = rwkv_prep(rw_lat, *args)
    ylf, _ = wkv_scan(rl, dl[:, :, 0], kdl[:, :, 0], vl, kkl, al[:, :, 0], s_f, False, True)
    ylb, _ = wkv_scan(rl, dl[:, :, 1], kdl[:, :, 1], vl, kkl, al[:, :, 1], s_b, True, True)
    o_lat = rwkv_out(ylf, ylb, rl, kdl, vl, gl, *out_args, rw_lat.dtype)
    o_ctx = rwkv_out(ycf, ycb, rc, kdc, vc, gc, *out_args, rw_ctx.dtype) if with_ctx else None
    return o_lat, o_ctx


def branch_merge(o_a, o_b, o_c, gates, lp):
    g_a, g_b, g_c = jnp.split(gates, N_BRANCH, axis=-1)
    m = (jax.nn.sigmoid(g_a) * (o_a @ lp['proj_a'])
         + jax.nn.sigmoid(g_b) * (o_b @ lp['proj_b'])
         + jax.nn.sigmoid(g_c) * (o_c @ lp['proj_c']))
    return m @ lp['w_out']


def swiglu(h, wg, wu, wd):
    return (jax.nn.silu(h @ wg) * (h @ wu)) @ wd


def route(h, router_w, router_b):
    T = h.shape[0]
    scores = jax.nn.sigmoid((h @ router_w).astype(jnp.float32))
    biased = scores + router_b.astype(jnp.float32)
    grp = biased.reshape(T, N_GROUPS, N_EXPERTS // N_GROUPS)
    grp_score = jnp.sum(lax.top_k(grp, 2)[0], axis=-1)
    _, g_sel = lax.top_k(grp_score, TOPK_GROUPS)
    g_mask = jnp.any(g_sel[:, :, None] == jnp.arange(N_GROUPS)[None, None, :], axis=1)
    e_mask = jnp.repeat(g_mask, N_EXPERTS // N_GROUPS, axis=1)
    _, e_idx = lax.top_k(jnp.where(e_mask, biased, -jnp.inf), TOP_K)
    w = jnp.take_along_axis(scores, e_idx, axis=1)
    return e_idx, w / jnp.sum(w, axis=-1, keepdims=True) * ROUTE_SCALE


def routed_experts(h, e_idx, e_w, w_gate, w_up, w_down):
    T, D = h.shape
    n = T * TOP_K
    flat_e = e_idx.reshape(-1)
    order = jnp.argsort(flat_e)
    e_sorted = flat_e[order]
    tok_sorted = (order // TOP_K).astype(jnp.int32)
    counts = jnp.bincount(flat_e, length=N_EXPERTS)
    padded = (counts + EXPERT_BLOCK - 1) // EXPERT_BLOCK * EXPERT_BLOCK
    pad_end = jnp.cumsum(padded)
    pad_start = pad_end - padded
    grp_start = jnp.cumsum(counts) - counts
    dest = pad_start[e_sorted] + jnp.arange(n) - grp_start[e_sorted]
    n_blocks = -(-n // EXPERT_BLOCK) + N_EXPERTS
    n_slots = n_blocks * EXPERT_BLOCK
    slot_tok = jnp.full((n_slots,), T, jnp.int32).at[dest].set(tok_sorted)
    slot_w = jnp.zeros((n_slots,), h.dtype).at[dest].set(e_w.reshape(-1)[order].astype(h.dtype))
    block_e = jnp.minimum(jnp.searchsorted(pad_end, jnp.arange(n_blocks) * EXPERT_BLOCK, side='right'), N_EXPERTS - 1)
    h_pad = jnp.concatenate([h, jnp.zeros((1, D), h.dtype)], axis=0)

    def run_block(args):
        toks, wts, e = args
        return swiglu(h_pad[toks], w_gate[e], w_up[e], w_down[e]) * wts[:, None]

    y = lax.map(run_block, (slot_tok.reshape(n_blocks, EXPERT_BLOCK), slot_w.reshape(n_blocks, EXPERT_BLOCK), block_e))
    return jnp.zeros((T + 1, D), h.dtype).at[slot_tok].add(y.reshape(n_slots, D))[:T]


def moe(h, lp):
    e_idx, e_w = route(h, lp['router_w'], lp['router_b'])
    return (routed_experts(h, e_idx, e_w, lp['exp_gate'], lp['exp_up'], lp['exp_down'])
            + swiglu(h, lp['sh_gate'], lp['sh_up'], lp['sh_down']))


def trunk_layer(x, xc, mod, mod_c, lp, with_ctx):
    B, L, D = x.shape
    sh1, sc1, g1, sh2, sc2, g2 = jnp.split(mod, N_MOD, axis=-1)
    sh1c, sc1c, g1c, sh2c, sc2c, g2c = jnp.split(mod_c, N_MOD, axis=-1)
    h = rmsnorm(x, lp['norm1_g']) * (1 + sc1) + sh1
    hc = rmsnorm(xc, lp['norm1_g']) * (1 + sc1c) + sh1c
    hy, na, rw, gates = jnp.split(h @ lp['w_in'], [COL_HY, COL_NA, COL_RW], axis=-1)
    if with_ctx:
        hy_c, na_c, rw_c, gates_c = jnp.split(hc @ lp['w_in'], [COL_HY, COL_NA, COL_RW], axis=-1)
        q_c, k_c, v_c = jnp.split(na_c, 3, axis=-1)
    else:
        k_c, v_c, rw_c = jnp.split(hc @ lp['w_in'][:, COL_HY + NA_WIDTH:COL_RW], [NA_WIDTH, 2 * NA_WIDTH], axis=-1)
    hy_args = (lp['hy_w1'], lp['hy_b1'], lp['hy_w2'], lp['hy_b2'], lp['hy_freq'], lp['hy_w3'])
    o_a = hyena(hy, lp['hy_conv'], hyena_filters(L, *hy_args), lp['hy_bias'])
    q, k, v = (split_heads(t, NA_HEADS) for t in jnp.split(na, 3, axis=-1))
    k_c = split_heads(k_c, NA_HEADS)
    v_c = split_heads(v_c, NA_HEADS)
    o_b = na_latent(q, k, v, k_c, v_c, lp['na_rpb'])
    o_c, o_c_ctx = rwkv_mixer(rw, rw_c, lp, with_ctx)
    x = x + g1 * branch_merge(o_a, o_b, o_c, gates, lp)
    h2 = rmsnorm(x, lp['norm2_g']) * (1 + sc2) + sh2
    if not with_ctx:
        return x + g2 * moe(h2.reshape(B * L, D), lp).reshape(B, L, D), None
    Lc = xc.shape[1]
    o_a_c = hyena(hy_c, lp['hy_conv'], hyena_filters(Lc, *hy_args), lp['hy_bias'])
    o_b_c = ctx_attn(split_heads(q_c, NA_HEADS), k_c, v_c)
    xc = xc + g1c * branch_merge(o_a_c, o_b_c, o_c_ctx, gates_c, lp)
    h2c = rmsnorm(xc, lp['norm2_g']) * (1 + sc2c) + sh2c
    f = moe(jnp.concatenate([h2.reshape(B * L, D), h2c.reshape(B * Lc, D)], axis=0), lp)
    return x + g2 * f[:B * L].reshape(B, L, D), xc + g2c * f[B * L:].reshape(B, Lc, D)


def setup_inputs(seed: int = 0) -> dict:
    key = jax.random.key(seed)
    ks = iter(jax.random.split(key, 48))
    D = D_MODEL

    def nrm(shape, scale):
        return scale * jax.random.normal(next(ks), shape, jnp.float32)

    ident_tap = jnp.zeros((SHORT_CONV,), jnp.float32).at[SHORT_CONV // 2].set(1.0)
    return {
        'x': nrm((BATCH, SEQ, D), 1.0),
        'c': nrm((BATCH, D), 1.0),
        'ctx': nrm((BATCH, CTX_LEN, D), 1.0),
        'c_ctx': nrm((D,), 1.0),
        'mod_w': nrm((DEPTH, D, N_MOD * D), 0.5 * D ** -0.5),
        'mod_b': nrm((DEPTH, N_MOD * D), 0.02),
        'norm1_g': 1.0 + nrm((DEPTH, D), 0.02),
        'norm2_g': 1.0 + nrm((DEPTH, D), 0.02),
        'w_in': nrm((DEPTH, D, N_COLS), D ** -0.5),
        'hy_conv': nrm((DEPTH, SHORT_CONV, 3 * HY_WIDTH), 0.5),
        'hy_w1': nrm((DEPTH, HY_EMB, HY_FILTER_ORDER), HY_EMB ** -0.5),
        'hy_b1': nrm((DEPTH, HY_FILTER_ORDER), 0.1),
        'hy_w2': nrm((DEPTH, HY_FILTER_ORDER, HY_FILTER_ORDER), HY_FILTER_ORDER ** -0.5),
        'hy_b2': nrm((DEPTH, HY_FILTER_ORDER), 0.1),
        'hy_freq': 1.0 + nrm((DEPTH, 2, HY_FILTER_ORDER), 0.1),
        'hy_w3': nrm((DEPTH, HY_FILTER_ORDER, 4 * HY_WIDTH), HY_FILTER_ORDER ** -0.5),
        'hy_bias': nrm((DEPTH, 2, HY_WIDTH), 0.1),
        'na_rpb': nrm((DEPTH, NA_HEADS, 2 * NA_ROWS - 1, 2 * NA_COLS - 1), 0.02),
        'rw_shift': ident_tap[None, :, None] + nrm((DEPTH, SHORT_CONV, RW_COLS), 0.1),
        'rw_w0': jnp.linspace(-6.0, -1.0, RW_WIDTH, dtype=jnp.float32)[None, None, :] + nrm((DEPTH, 2, RW_WIDTH), 0.1),
        'rw_w2': nrm((DEPTH, 2, RW_DECAY_LORA, RW_WIDTH), 0.1 * RW_DECAY_LORA ** -0.5),
        'rw_a0': nrm((DEPTH, 2, RW_WIDTH), 0.1),
        'rw_a2': nrm((DEPTH, 2, RW_AAA_LORA, RW_WIDTH), 0.1 * RW_AAA_LORA ** -0.5),
        'rw_g2': nrm((DEPTH, RW_GATE_LORA, RW_WIDTH), RW_GATE_LORA ** -0.5),
        'rw_kk': 0.85 + nrm((DEPTH, RW_WIDTH), 0.02),
        'rw_ka': 1.0 + nrm((DEPTH, RW_WIDTH), 0.02),
        'rw_rk': nrm((DEPTH, RW_HEADS, RW_HEAD_DIM), 0.1),
        'rw_ln_w': 1.0 + nrm((DEPTH, RW_WIDTH), 0.02),
        'rw_ln_b': nrm((DEPTH, RW_WIDTH), 0.02),
        'proj_a': nrm((DEPTH, HY_WIDTH, D), HY_WIDTH ** -0.5),
        'proj_b': nrm((DEPTH, NA_WIDTH, D), NA_WIDTH ** -0.5),
        'proj_c': nrm((DEPTH, RW_WIDTH, D), RW_WIDTH ** -0.5),
        'w_out': nrm((DEPTH, D, D), D ** -0.5),
        'router_w': nrm((DEPTH, D, N_EXPERTS), D ** -0.5),
        'router_b': nrm((DEPTH, N_EXPERTS), 0.01),
        'exp_gate': nrm((DEPTH, N_EXPERTS, D, EXPERT_FF), D ** -0.5),
        'exp_up': nrm((DEPTH, N_EXPERTS, D, EXPERT_FF), D ** -0.5),
        'exp_down': nrm((DEPTH, N_EXPERTS, EXPERT_FF, D), EXPERT_FF ** -0.5),
        'sh_gate': nrm((DEPTH, D, SHARED_FF), D ** -0.5),
        'sh_up': nrm((DEPTH, D, SHARED_FF), D ** -0.5),
        'sh_down': nrm((DEPTH, SHARED_FF, D), SHARED_FF ** -0.5),
        'final_g': 1.0 + nrm((D,), 0.02),
    }


def reference(x, c, ctx, c_ctx, mod_w, mod_b, norm1_g, norm2_g, w_in, hy_conv, hy_w1, hy_b1, hy_w2, hy_b2,
              hy_freq, hy_w3, hy_bias, na_rpb, rw_shift, rw_w0, rw_w2, rw_a0, rw_a2, rw_g2, rw_kk, rw_ka,
              rw_rk, rw_ln_w, rw_ln_b, proj_a, proj_b, proj_c, w_out, router_w, router_b, exp_gate, exp_up,
              exp_down, sh_gate, sh_up, sh_down, final_g):
    s_lat = jax.nn.silu(c)
    s_ctx = jax.nn.silu(c_ctx)
    xc = ctx
    for i in range(DEPTH):
        lp = {
            'norm1_g': norm1_g[i], 'norm2_g': norm2_g[i], 'w_in': w_in[i],
            'hy_conv': hy_conv[i], 'hy_w1': hy_w1[i], 'hy_b1': hy_b1[i], 'hy_w2': hy_w2[i], 'hy_b2': hy_b2[i],
            'hy_freq': hy_freq[i], 'hy_w3': hy_w3[i], 'hy_bias': hy_bias[i], 'na_rpb': na_rpb[i],
            'rw_shift': rw_shift[i], 'rw_w0': rw_w0[i], 'rw_w2': rw_w2[i], 'rw_a0': rw_a0[i], 'rw_a2': rw_a2[i],
            'rw_g2': rw_g2[i], 'rw_kk': rw_kk[i], 'rw_ka': rw_ka[i], 'rw_rk': rw_rk[i],
            'rw_ln_w': rw_ln_w[i], 'rw_ln_b': rw_ln_b[i],
            'proj_a': proj_a[i], 'proj_b': proj_b[i], 'proj_c': proj_c[i], 'w_out': w_out[i],
            'router_w': router_w[i], 'router_b': router_b[i], 'exp_gate': exp_gate[i], 'exp_up': exp_up[i],
            'exp_down': exp_down[i], 'sh_gate': sh_gate[i], 'sh_up': sh_up[i], 'sh_down': sh_down[i],
        }
        mod = (s_lat @ mod_w[i] + mod_b[i])[:, None, :]
        mod_c = (s_ctx @ mod_w[i] + mod_b[i])[None, None, :]
        x, xc = trunk_layer(x, xc, mod, mod_c, lp, i < DEPTH - 1)
    return rmsnorm(x, final_g)
```

```python
import functools
import math

import jax
import jax.numpy as jnp
import numpy as np
from jax import lax
from jax.experimental import pallas as pl
from jax.experimental.pallas import tpu as pltpu

F32 = jnp.float32
BF16 = jnp.bfloat16
HI = lax.Precision.HIGHEST

GRID_W = 64
NORM_EPS = 1e-6
N_MOD = 6
SHORT_CONV = 3
HY_WIDTH = 1024
HY_BANDS = 16
HY_EMB = 2 * HY_BANDS + 1
HY_FILTER_ORDER = 64
HY_FAST_DECAY = 0.3
HY_SLOW_DECAY = 1.5
HY_DECAY_TARGET = 1e-2
NA_HEADS = 16
NA_HEAD_DIM = 64
NA_WIDTH = NA_HEADS * NA_HEAD_DIM
NA_ROWS = 8
NA_COLS = 16
RW_HEADS = 16
RW_HEAD_DIM = 64
RW_WIDTH = RW_HEADS * RW_HEAD_DIM
RW_DECAY_LORA = 64
RW_AAA_LORA = 64
RW_GATE_LORA = 128
RW_GN_EPS = 64e-5
RW_COLS = 3 * RW_WIDTH + 2 * RW_DECAY_LORA + 2 * RW_AAA_LORA + RW_GATE_LORA
RW_SPLITS = [RW_WIDTH, 2 * RW_WIDTH, 3 * RW_WIDTH, 3 * RW_WIDTH + 2 * RW_DECAY_LORA,
             3 * RW_WIDTH + 2 * RW_DECAY_LORA + 2 * RW_AAA_LORA]
N_BRANCH = 3
N_EXPERTS = 64
TOP_K = 6
N_GROUPS = 8
TOPK_GROUPS = 4
ROUTE_SCALE = 2.5

LANES = 128
VMEM_LIMIT = 56 * 1024 * 1024
TM = 512
MOE_BLOCK = 128
WKV_CHUNK = 64
WKV_GROUP = 4


def _cparams(sem):
    return pltpu.CompilerParams(dimension_semantics=sem, vmem_limit_bytes=VMEM_LIMIT)


def _dot(a, b, prec=None):
    return jnp.dot(a, b, preferred_element_type=F32, precision=prec)


def _dot_nt(a, b, prec=None):
    return lax.dot_general(a, b, (((1,), (1,)), ((), ())), preferred_element_type=F32, precision=prec)


def _dot_tn(a, b, prec=None):
    return lax.dot_general(a, b, (((0,), (0,)), ((), ())), preferred_element_type=F32, precision=prec)


def _small_mm_kernel(a_ref, w_ref, b_ref, o_ref):
    o_ref[...] = _dot(a_ref[...], w_ref[...], HI) + b_ref[...]


def small_matmul_bias(a, w, b, tn=1536):
    m, k = a.shape
    n = w.shape[1]
    return pl.pallas_call(
        _small_mm_kernel,
        grid=(n // tn,),
        in_specs=[pl.BlockSpec((m, k), lambda j: (0, 0)),
                  pl.BlockSpec((k, tn), lambda j: (0, j)),
                  pl.BlockSpec((1, tn), lambda j: (0, j))],
        out_specs=pl.BlockSpec((m, tn), lambda j: (0, j)),
        out_shape=jax.ShapeDtypeStruct((m, n), F32),
        compiler_params=_cparams(("arbitrary",)),
        name="mod_matmul",
    )(a, w, b.reshape(1, n))


def _normmod_mm_kernel(grp_ref, x_ref, g_ref, sh_ref, sc_ref, w_ref, o_ref, h_ref, *, hi):
    del grp_ref

    @pl.when(pl.program_id(1) == 0)
    def _():
        x = x_ref[...]
        y = x * lax.rsqrt(jnp.mean(x * x, axis=-1, keepdims=True) + NORM_EPS)
        y = y * g_ref[...]
        h_ref[...] = (y * (1.0 + sc_ref[0]) + sh_ref[0]).astype(h_ref.dtype)

    o_ref[...] = _dot(h_ref[...], w_ref[...], HI if hi else None).astype(o_ref.dtype)


def normmod_matmul(x, g, shift3, scale3, grp, w, tn, out_dtype, hi=False):
    m, d = x.shape
    n = w.shape[1]
    grid_spec = pltpu.PrefetchScalarGridSpec(
        num_scalar_prefetch=1,
        grid=(m // TM, n // tn),
        in_specs=[pl.BlockSpec((TM, d), lambda i, j, grp: (i, 0)),
                  pl.BlockSpec((1, d), lambda i, j, grp: (0, 0)),
                  pl.BlockSpec((1, 1, d), lambda i, j, grp: (grp[i], 0, 0)),
                  pl.BlockSpec((1, 1, d), lambda i, j, grp: (grp[i], 0, 0)),
                  pl.BlockSpec((d, tn), lambda i, j, grp: (0, j))],
        out_specs=pl.BlockSpec((TM, tn), lambda i, j, grp: (i, j)),
        scratch_shapes=[pltpu.VMEM((TM, d), F32 if hi else BF16)],
    )
    return pl.pallas_call(
        functools.partial(_normmod_mm_kernel, hi=hi),
        grid_spec=grid_spec,
        out_shape=jax.ShapeDtypeStruct((m, n), out_dtype),
        compiler_params=_cparams(("arbitrary", "arbitrary")),
        name="normmod_matmul",
    )(grp, x, g.reshape(1, d), shift3, scale3, w)


def _normmod_kernel(grp_ref, x_ref, g_ref, sh_ref, sc_ref, o_ref):
    del grp_ref
    x = x_ref[...]
    y = x * lax.rsqrt(jnp.mean(x * x, axis=-1, keepdims=True) + NORM_EPS)
    o_ref[...] = ((y * g_ref[...]) * (1.0 + sc_ref[0]) + sh_ref[0]).astype(o_ref.dtype)


def normmod(x, g, shift3, scale3, grp, out_dtype):
    m, d = x.shape
    grid_spec = pltpu.PrefetchScalarGridSpec(
        num_scalar_prefetch=1,
        grid=(m // TM,),
        in_specs=[pl.BlockSpec((TM, d), lambda i, grp: (i, 0)),
                  pl.BlockSpec((1, d), lambda i, grp: (0, 0)),
                  pl.BlockSpec((1, 1, d), lambda i, grp: (grp[i], 0, 0)),
                  pl.BlockSpec((1, 1, d), lambda i, grp: (grp[i], 0, 0))],
        out_specs=pl.BlockSpec((TM, d), lambda i, grp: (i, 0)),
    )
    return pl.pallas_call(
        _normmod_kernel,
        grid_spec=grid_spec,
        out_shape=jax.ShapeDtypeStruct((m, d), out_dtype),
        compiler_params=_cparams(("arbitrary",)),
        name="normmod",
    )(grp, x, g.reshape(1, d), shift3, scale3)


def _merge_kernel(oa_ref, ob_ref, oc_ref, ga_ref, gb_ref, gc_ref, pa_ref, pb_ref, pc_ref, o_ref):
    m = jax.nn.sigmoid(ga_ref[...]) * _dot(oa_ref[...], pa_ref[...])
    m = m + jax.nn.sigmoid(gb_ref[...]) * _dot(ob_ref[...], pb_ref[...])
    m = m + jax.nn.sigmoid(gc_ref[...]) * _dot(oc_ref[...], pc_ref[...])
    o_ref[...] = m.astype(o_ref.dtype)


def branch_merge(o_a, o_b, o_c, gates, pa, pb, pc, tn=512):
    m, k = o_a.shape
    d = pa.shape[1]
    nj = d // tn
    o_spec = pl.BlockSpec((TM, k), lambda i, j: (i, 0))
    p_spec = pl.BlockSpec((k, tn), lambda i, j: (0, j))
    return pl.pallas_call(
        _merge_kernel,
        grid=(m // TM, nj),
        in_specs=[o_spec, o_spec, o_spec,
                  pl.BlockSpec((TM, tn), lambda i, j: (i, j)),
                  pl.BlockSpec((TM, tn), lambda i, j: (i, j + nj)),
                  pl.BlockSpec((TM, tn), lambda i, j: (i, j + 2 * nj)),
                  p_spec, p_spec, p_spec],
        out_specs=pl.BlockSpec((TM, tn), lambda i, j: (i, j)),
        out_shape=jax.ShapeDtypeStruct((m, d), BF16),
        compiler_params=_cparams(("arbitrary", "arbitrary")),
        name="branch_merge",
    )(o_a, o_b, o_c, gates, gates, gates, pa, pb, pc)


def _resid_mm_kernel(grp_ref, a_ref, w_ref, x_ref, gate_ref, o_ref):
    del grp_ref
    o_ref[...] = x_ref[...] + gate_ref[0] * _dot(a_ref[...], w_ref[...])


def resid_matmul(a, w, x, gate3, grp, tn=512):
    m, k = a.shape
    d = w.shape[1]
    grid_spec = pltpu.PrefetchScalarGridSpec(
        num_scalar_prefetch=1,
        grid=(m // TM, d // tn),
        in_specs=[pl.BlockSpec((TM, k), lambda i, j, grp: (i, 0)),
                  pl.BlockSpec((k, tn), lambda i, j, grp: (0, j)),
                  pl.BlockSpec((TM, tn), lambda i, j, grp: (i, j)),
                  pl.BlockSpec((1, 1, tn), lambda i, j, grp: (grp[i], 0, j))],
        out_specs=pl.BlockSpec((TM, tn), lambda i, j, grp: (i, j)),
    )
    return pl.pallas_call(
        _resid_mm_kernel,
        grid_spec=grid_spec,
        out_shape=jax.ShapeDtypeStruct((m, d), F32),
        compiler_params=_cparams(("arbitrary", "arbitrary")),
        name="resid_matmul",
    )(grp, a, w, x, gate3)


def _rmsnorm_kernel(x_ref, g_ref, o_ref):
    x = x_ref[...]
    y = x * lax.rsqrt(jnp.mean(x * x, axis=-1, keepdims=True) + NORM_EPS)
    o_ref[...] = y * g_ref[...]


def rmsnorm_rows(x, g):
    m, d = x.shape
    return pl.pallas_call(
        _rmsnorm_kernel,
        grid=(m // TM,),
        in_specs=[pl.BlockSpec((TM, d), lambda i: (i, 0)), pl.BlockSpec((1, d), lambda i: (0, 0))],
        out_specs=pl.BlockSpec((TM, d), lambda i: (i, 0)),
        out_shape=jax.ShapeDtypeStruct((m, d), F32),
        compiler_params=_cparams(("arbitrary",)),
        name="final_rmsnorm",
    )(x, g.reshape(1, d))


def _hyfilt_kernel(z_ref, w1_ref, b1_ref, w2_ref, b2_ref, fr_ref, w3_ref, dl_ref, h_ref, ss_ref):
    z = z_ref[...]
    hdn = jnp.sin(fr_ref[0:1, :] * (_dot(z, w1_ref[...], HI) + b1_ref[...]))
    hdn = jnp.sin(fr_ref[1:2, :] * (_dot(hdn, w2_ref[...], HI) + b2_ref[...]))
    h = _dot(hdn, w3_ref[...], HI)
    h = h * jnp.exp(-z[:, 0:1] * dl_ref[...])
    h_ref[...] = h

    @pl.when(pl.program_id(0) == 0)
    def _():
        ss_ref[...] = jnp.zeros_like(ss_ref)

    ss_ref[...] += jnp.sum(h * h, axis=0, keepdims=True)


def hyena_filters_raw(L, w1, b1, w2, b2, freq, w3):
    t = np.linspace(0.0, 1.0, L, dtype=np.float32)[:, None]
    omega = np.float32(2.0 * math.pi / L) * np.arange(L, dtype=np.float32)[:, None]
    bands = np.linspace(1e-4, HY_BANDS - 1, HY_BANDS, dtype=np.float32)[None, :]
    z = np.concatenate([t, np.cos(omega * bands), -np.sin(omega * bands),
                        np.zeros((L, HY_FILTER_ORDER - HY_EMB), np.float32)], axis=-1).astype(np.float32)
    w1p = jnp.concatenate([w1, jnp.zeros((HY_FILTER_ORDER - HY_EMB, HY_FILTER_ORDER), F32)], axis=0)
    deltas = np.abs(np.linspace(math.log(HY_DECAY_TARGET) / HY_SLOW_DECAY,
                                math.log(HY_DECAY_TARGET) / HY_FAST_DECAY, HY_WIDTH, dtype=np.float32))
    dl4 = np.tile(deltas, 4)[None, :]
    tl = min(L, 256)
    n = 4 * HY_WIDTH
    fo = HY_FILTER_ORDER
    full = lambda shape: pl.BlockSpec(shape, lambda i: (0, 0))
    return pl.pallas_call(
        _hyfilt_kernel,
        grid=(L // tl,),
        in_specs=[pl.BlockSpec((tl, fo), lambda i: (i, 0)), full((fo, fo)), full((1, fo)), full((fo, fo)),
                  full((1, fo)), full((2, fo)), full((fo, n)), full((1, n))],
        out_specs=[pl.BlockSpec((tl, n), lambda i: (i, 0)), full((1, n))],
        out_shape=[jax.ShapeDtypeStruct((L, n), F32), jax.ShapeDtypeStruct((1, n), F32)],
        compiler_params=_cparams(("arbitrary",)),
        name="hyena_filters",
    )(jnp.asarray(z), w1p, b1.reshape(1, fo), w2, b2.reshape(1, fo), freq, w3, jnp.asarray(dl4))


def _filter_scale(ss):
    s = ss.reshape(2, 2, HY_WIDTH)
    rs = lax.rsqrt(jnp.sum(s, axis=1, keepdims=True))
    return jnp.broadcast_to(rs, (2, 2, HY_WIDTH)).reshape(1, 4 * HY_WIDTH)


FFT_N1 = 128
FFT_N2 = 64


@functools.lru_cache(maxsize=None)
def _fft_tables():
    n1, n2 = FFT_N1, FFT_N2
    n = n1 * n2
    a = np.arange(n1 // 2)[None, None, :]
    k1 = np.arange(n1)[None, :, None]
    b = np.arange(n2)[:, None, None]
    theta = 2.0 * np.pi * ((a * k1 % n1) / n1 + (b * k1) / n)
    g = np.concatenate([np.cos(theta), -np.sin(theta)], axis=1)
    ig = np.concatenate([np.cos(theta), -np.sin(theta)], axis=1).transpose(0, 2, 1) / n
    k2 = np.arange(n2)[:, None]
    bb = np.arange(n2)[None, :]
    ph = 2.0 * np.pi * (k2 * bb % n2) / n2
    fr, fi = np.cos(ph), -np.sin(ph)
    f2 = np.block([[fr, -fi], [fi, fr]])
    if2 = np.block([[fr, fi], [-fi, fr]])
    return (jnp.asarray(g, BF16), jnp.asarray(f2, BF16), jnp.asarray(if2, BF16), jnp.asarray(ig, BF16))


def _fft_stage1(u_ref, g_ref, sr_ref, si_ref):
    n1, n2 = FFT_N1, FFT_N2

    def body(b, carry):
        xb = u_ref[pl.ds(b, n1 // 2, stride=n2), :].astype(BF16)
        a = _dot(g_ref[b], xb)
        sr_ref[pl.ds(b, n1, stride=n2), :] = a[:n1]
        si_ref[pl.ds(b, n1, stride=n2), :] = a[n1:]
        return carry

    lax.fori_loop(0, n2, body, 0)


def _hyconv_kernel(xm_ref, u_ref, hr_ref, hi_ref, bias_ref, g_ref, f2_ref, if2_ref, ig_ref, o_ref, sr_ref, si_ref):
    n1, n2 = FFT_N1, FFT_N2
    _fft_stage1(u_ref, g_ref, sr_ref, si_ref)

    def stage2(k1, carry):
        rows = pl.ds(pl.multiple_of(k1 * n2, n2), n2)
        z = jnp.concatenate([sr_ref[rows, :], si_ref[rows, :]], axis=0).astype(BF16)
        x = _dot(f2_ref[...], z)
        xr, xi = x[:n2], x[n2:]
        hr, hi = hr_ref[rows, :], hi_ref[rows, :]
        y = jnp.concatenate([xr * hr - xi * hi, xr * hi + xi * hr], axis=0).astype(BF16)
        bb = _dot(if2_ref[...], y)
        sr_ref[rows, :] = bb[:n2]
        si_ref[rows, :] = bb[n2:]
        return carry

    lax.fori_loop(0, n1, stage2, 0)
    bias = bias_ref[...]

    def stage3(b, carry):
        st = jnp.concatenate([sr_ref[pl.ds(b, n1, stride=n2), :], si_ref[pl.ds(b, n1, stride=n2), :]], axis=0)
        yb = _dot(ig_ref[b], st.astype(BF16))
        rows = pl.ds(b, n1 // 2, stride=n2)
        ub = u_ref[rows, :]
        o_ref[rows, :] = xm_ref[rows, :] * (yb + ub * bias)
        return carry

    lax.fori_loop(0, n2, stage3, 0)


def hyena_conv(xm_arr, xm_col, u_arr, u_col, hr, hi, h_col, bias, n_batch, L):
    assert L == FFT_N1 * FFT_N2 // 2
    cb = LANES
    nct = HY_WIDTH // cb
    n = 2 * L
    g, f2, if2, ig = _fft_tables()
    const3 = lambda shape: pl.BlockSpec(shape, lambda b, c: (0, 0, 0))
    const2 = lambda shape: pl.BlockSpec(shape, lambda b, c: (0, 0))
    return pl.pallas_call(
        _hyconv_kernel,
        grid=(n_batch, nct),
        in_specs=[pl.BlockSpec((L, cb), lambda b, c: (b, xm_col * nct + c)),
                  pl.BlockSpec((L, cb), lambda b, c: (b, u_col * nct + c)),
                  pl.BlockSpec((n, cb), lambda b, c: (0, h_col * nct + c)),
                  pl.BlockSpec((n, cb), lambda b, c: (0, h_col * nct + c)),
                  pl.BlockSpec((1, cb), lambda b, c: (0, c)),
                  const3(g.shape), const2(f2.shape), const2(if2.shape), const3(ig.shape)],
        out_specs=pl.BlockSpec((L, cb), lambda b, c: (b, c)),
        out_shape=jax.ShapeDtypeStruct((n_batch * L, HY_WIDTH), F32),
        scratch_shapes=[pltpu.VMEM((n, cb), F32), pltpu.VMEM((n, cb), F32)],
        compiler_params=_cparams(("arbitrary", "arbitrary")),
        name="hyena_conv",
    )(xm_arr, u_arr, hr, hi, bias.reshape(1, HY_WIDTH), g, f2, if2, ig)


def _hyspec_kernel(h0_ref, h1_ref, rs_ref, g_ref, f2_ref, hr_ref, hi_ref, s0r, s0i, s1r, s1i):
    n1, n2 = FFT_N1, FFT_N2
    _fft_stage1(h0_ref, g_ref, s0r, s0i)
    _fft_stage1(h1_ref, g_ref, s1r, s1i)
    rs = rs_ref[...]
    h10 = h1_ref[0:1, :]

    def stage2(k1, carry):
        rows = pl.ds(pl.multiple_of(k1 * n2, n2), n2)
        z0 = jnp.concatenate([s0r[rows, :], s0i[rows, :]], axis=0).astype(BF16)
        z1 = jnp.concatenate([s1r[rows, :], s1i[rows, :]], axis=0).astype(BF16)
        x0 = _dot(f2_ref[...], z0)
        x1 = _dot(f2_ref[...], z1)
        hr_ref[rows, :] = rs * (x0[:n2] + x1[:n2] - h10)
        hi_ref[rows, :] = rs * (x0[n2:] - x1[n2:])
        return carry

    lax.fori_loop(0, n1, stage2, 0)


def hyena_filter_spectrum(h_raw, rs, L):
    assert L == FFT_N1 * FFT_N2 // 2
    cb = LANES
    nct = HY_WIDTH // cb
    n = 2 * L
    g, f2, _, _ = _fft_tables()
    out_spec = pl.BlockSpec((n, cb), lambda o, c: (0, o * nct + c))
    scr = pltpu.VMEM((n, cb), F32)
    return pl.pallas_call(
        _hyspec_kernel,
        grid=(2, nct),
        in_specs=[pl.BlockSpec((L, cb), lambda o, c: (0, (2 * o) * nct + c)),
                  pl.BlockSpec((L, cb), lambda o, c: (0, (2 * o + 1) * nct + c)),
                  pl.BlockSpec((1, cb), lambda o, c: (0, (2 * o) * nct + c)),
                  pl.BlockSpec(g.shape, lambda o, c: (0, 0, 0)),
                  pl.BlockSpec(f2.shape, lambda o, c: (0, 0))],
        out_specs=[out_spec, out_spec],
        out_shape=[jax.ShapeDtypeStruct((n, 2 * HY_WIDTH), F32)] * 2,
        scratch_shapes=[scr, scr, scr, scr],
        compiler_params=_cparams(("arbitrary", "arbitrary")),
        name="hyena_filter_spectrum",
    )(h_raw, h_raw, rs, g, f2)


@functools.lru_cache(maxsize=None)
def _dense_dft_tables(L):
    n = 2 * L
    k = np.arange(n)[:, None]
    t = np.arange(L)[None, :]
    ph = 2.0 * np.pi * (k * t % n) / n
    fwd = np.concatenate([np.cos(ph), -np.sin(ph)], axis=0)
    inv = np.concatenate([np.cos(ph), -np.sin(ph)], axis=0).T / n
    return jnp.asarray(fwd, F32), jnp.asarray(inv, F32)


def _hyena_small_kernel(x1_ref, x2_ref, v_ref, h_ref0a, h_ref0b, h_ref1a, h_ref1b, rs0_ref, rs1_ref,
                        b0_ref, b1_ref, fwd_ref, inv_ref, o_ref, *, L):
    n = 2 * L
    fwd = fwd_ref[...]
    inv = inv_ref[...]

    def conv(u, ha_ref, hb_ref, rs_ref, bias_ref):
        ha, hb = ha_ref[...], hb_ref[...]
        ka = _dot(fwd, ha, HI)
        kb = _dot(fwd, hb, HI)
        rs = rs_ref[...]
        kr = rs * (ka[:n] + kb[:n] - hb[0:1, :])
        ki = rs * (ka[n:] - kb[n:])
        uf = _dot(fwd, u, HI)
        ur, ui = uf[:n], uf[n:]
        y = jnp.concatenate([ur * kr - ui * ki, ur * ki + ui * kr], axis=0)
        return _dot(inv, y, HI) + u * bias_ref[...]

    v = v_ref[...]
    z = x1_ref[...] * conv(v, h_ref0a, h_ref0b, rs0_ref, b0_ref)
    o_ref[...] = (x2_ref[...] * conv(z, h_ref1a, h_ref1b, rs1_ref, b1_ref)).astype(o_ref.dtype)


def hyena_small(u_arr, row0_blocks, n_batch, L, h_raw, rs, bias):
    cb = LANES
    nct = HY_WIDTH // cb
    fwd, inv = _dense_dft_tables(L)
    uspec = lambda col: pl.BlockSpec((L, cb), lambda b, c: (row0_blocks + b, col * nct + c))
    hspec = lambda col: pl.BlockSpec((L, cb), lambda b, c: (0, col * nct + c))
    rspec = lambda col: pl.BlockSpec((1, cb), lambda b, c: (0, col * nct + c))
    bspec = pl.BlockSpec((1, cb), lambda b, c: (0, c))
    bias_0 = bias[0].reshape(1, HY_WIDTH)
    bias_1 = bias[1].reshape(1, HY_WIDTH)
    return pl.pallas_call(
        functools.partial(_hyena_small_kernel, L=L),
        grid=(n_batch, nct),
        in_specs=[uspec(0), uspec(1), uspec(2), hspec(0), hspec(1), hspec(2), hspec(3), rspec(0), rspec(2),
                  bspec, bspec,
                  pl.BlockSpec(fwd.shape, lambda b, c: (0, 0)), pl.BlockSpec(inv.shape, lambda b, c: (0, 0))],
        out_specs=pl.BlockSpec((L, cb), lambda b, c: (b, c)),
        out_shape=jax.ShapeDtypeStruct((n_batch * L, HY_WIDTH), BF16),
        compiler_params=_cparams(("arbitrary", "arbitrary")),
        name="hyena_ctx",
    )(u_arr, u_arr, u_arr, h_raw, h_raw, h_raw, h_raw, rs, rs, bias_0, bias_1, fwd, inv)


def _na_bias_table(rpb):
    cols = np.arange(GRID_W)
    col_start = np.clip(cols - NA_COLS // 2, 0, GRID_W - NA_COLS)[:, None]
    in_win = (cols[None, :] >= col_start) & (cols[None, :] < col_start + NA_COLS)
    rel_col = np.clip(cols[None, :] - cols[:, None], 1 - NA_COLS, NA_COLS - 1) + NA_COLS - 1
    tbl = rpb.astype(F32)[:, :, rel_col]
    tbl = jnp.where(jnp.asarray(in_win)[None, None], tbl, -jnp.inf)
    return jnp.concatenate([tbl[:, :-1], tbl[:, 1:]], axis=-1)


def _na_kernel(*refs, n_rows):
    q_ref = refs[0]
    k_refs = refs[1:1 + NA_ROWS]
    v_refs = refs[1 + NA_ROWS:1 + 2 * NA_ROWS]
    kc_ref, vc_ref, tbl_ref, o_ref = refs[1 + 2 * NA_ROWS:]
    r = pl.program_id(1)
    start = jnp.clip(r - NA_ROWS // 2, 0, n_rows - NA_ROWS)
    d0 = start - r + NA_ROWS - 1
    dh = NA_HEAD_DIM
    q = q_ref[...] * (dh ** -0.5)
    for h in range(NA_HEADS):
        hs = slice(h * dh, (h + 1) * dh)
        qh = q[:, hs]
        s_tiles = []
        v_tiles = []
        for p in range(NA_ROWS // 2):
            kp = jnp.concatenate([k_refs[2 * p][:, hs], k_refs[2 * p + 1][:, hs]], axis=0)
            s_tiles.append(_dot_nt(qh, kp) + tbl_ref[h, d0 + 2 * p])
            v_tiles.append(jnp.concatenate([v_refs[2 * p][:, hs], v_refs[2 * p + 1][:, hs]], axis=0))
        s_tiles.append(_dot_nt(qh, kc_ref[:, hs]))
        v_tiles.append(vc_ref[:, hs])
        m = s_tiles[0].max(axis=-1, keepdims=True)
        for s in s_tiles[1:]:
            m = jnp.maximum(m, s.max(axis=-1, keepdims=True))
        l = jnp.zeros_like(m)
        acc = jnp.zeros((GRID_W, dh), F32)
        for s, vt in zip(s_tiles, v_tiles):
            p_ = jnp.exp(s - m)
            l = l + p_.sum(axis=-1, keepdims=True)
            acc = acc + _dot(p_.astype(BF16), vt)
        o_ref[:, hs] = (acc / l).astype(o_ref.dtype)


def na_latent(na, rpb, n_batch, L, Lc):
    n_rows = L // GRID_W
    assert n_rows >= NA_ROWS
    tbl = _na_bias_table(rpb)
    w = NA_WIDTH
    ctx_blk0 = n_batch * L // Lc

    def kv_spec(i, col):
        def imap(b, r):
            start = jnp.clip(r - NA_ROWS // 2, 0, n_rows - NA_ROWS)
            return (b * n_rows + start + i, col)
        return pl.BlockSpec((GRID_W, w), imap)

    in_specs = ([pl.BlockSpec((GRID_W, w), lambda b, r: (b * n_rows + r, 0))]
                + [kv_spec(i, 1) for i in range(NA_ROWS)] + [kv_spec(i, 2) for i in range(NA_ROWS)]
                + [pl.BlockSpec((Lc, w), lambda b, r: (ctx_blk0 + b, 1)),
                   pl.BlockSpec((Lc, w), lambda b, r: (ctx_blk0 + b, 2)),
                   pl.BlockSpec(tbl.shape, lambda b, r: (0, 0, 0, 0))])
    return pl.pallas_call(
        functools.partial(_na_kernel, n_rows=n_rows),
        grid=(n_batch, n_rows),
        in_specs=in_specs,
        out_specs=pl.BlockSpec((GRID_W, w), lambda b, r: (b * n_rows + r, 0)),
        out_shape=jax.ShapeDtypeStruct((n_batch * L, w), BF16),
        compiler_params=_cparams(("arbitrary", "arbitrary")),
        name="na_latent",
    )(*([na] * (3 + 2 * NA_ROWS)), tbl)


def _ctx_attn_kernel(q_ref, k_ref, v_ref, o_ref):
    dh = NA_HEAD_DIM
    q = q_ref[...] * (dh ** -0.5)
    for h in range(NA_HEADS):
        hs = slice(h * dh, (h + 1) * dh)
        s = _dot_nt(q[:, hs], k_ref[:, hs])
        p_ = jnp.exp(s - s.max(axis=-1, keepdims=True))
        acc = _dot(p_.astype(BF16), v_ref[:, hs])
        o_ref[:, hs] = (acc / p_.sum(axis=-1, keepdims=True)).astype(o_ref.dtype)


def ctx_attn(na, n_batch, L, Lc):
    w = NA_WIDTH
    blk0 = n_batch * L // Lc
    spec = lambda col: pl.BlockSpec((Lc, w), lambda b: (blk0 + b, col))
    return pl.pallas_call(
        _ctx_attn_kernel,
        grid=(n_batch,),
        in_specs=[spec(0), spec(1), spec(2)],
        out_specs=pl.BlockSpec((Lc, w), lambda b: (b, 0)),
        out_shape=jax.ShapeDtypeStruct((n_batch * Lc, w), BF16),
        compiler_params=_cparams(("arbitrary",)),
        name="ctx_attn",
    )(na, na, na)


@functools.lru_cache(maxsize=None)
def _wkv_masks():
    c, g = WKV_CHUNK, WKV_GROUP
    t = np.arange(c)[:, None]
    s = np.arange(c)[None, :]
    tinc = np.stack([(s <= t), (s >= t)]).astype(np.float32)
    strict = np.stack([(s < t), (s > t)]).astype(np.float32)
    tile = lambda m: np.tile(m, (1,) * (m.ndim - 1) + (g,))
    blk = lambda n: (t // n == s // n)
    blk16 = tile(blk(16).astype(np.float32))
    off32 = tile((blk(32) & ~blk(16)).astype(np.float32))
    off64 = tile((~blk(32)).astype(np.float32))
    eye = tile((t == s).astype(np.float32))
    rr = np.arange(g * c)
    hm = (rr[:, None] // c == np.arange(g * RW_HEAD_DIM)[None, :] // RW_HEAD_DIM).astype(np.float32)
    return tuple(jnp.asarray(m) for m in (tinc, tile(strict), tile(tinc), blk16, off32, off64, eye, hm))


def _wkv_kernel(r_ref, v_ref, kk_ref, lw_ref, av_ref, kd_ref, tinc_ref, strict_ref, incl_ref, blk16_ref, off32_ref,
                off64_ref, eye_ref, hm_ref, y_ref, state_ref):
    c, g = WKV_CHUNK, WKV_GROUP
    gw = g * RW_HEAD_DIM

    @pl.when(pl.program_id(2) == 0)
    def _():
        state_ref[...] = jnp.zeros_like(state_ref)

    lw = lw_ref[0]
    cum = _dot(tinc_ref[0], lw, HI)
    tot = jnp.sum(lw, axis=0, keepdims=True)
    e_in = jnp.exp(cum)
    e_ex = jnp.exp(cum - lw)
    e_neg = jnp.exp(-cum)
    e_rem = jnp.exp(tot - cum)
    e_tot = jnp.exp(tot)
    kk = kk_ref[...]
    b_vec = kk * av_ref[0]
    kd = kd_ref[0]
    at_all = -kk * e_ex
    rt_all = r_ref[...] * e_in
    bt_all = b_vec * e_neg
    kt_all = kd * e_neg
    bp_all = b_vec * e_rem
    kp_all = kd * e_rem
    v_all = v_ref[...]
    hm = hm_ref[...]
    strict, incl = strict_ref[0], incl_ref[0]
    blk16, off32, off64, eye = blk16_ref[...], off32_ref[...], off64_ref[...], eye_ref[...]

    def bdiag(z):
        return jnp.concatenate([z] * g, axis=0) * hm

    def pm(x4, z):
        return _dot(x4, bdiag(z), HI)

    for gi in range(RW_HEADS // g):
        sl = slice(gi * gw, (gi + 1) * gw)
        ar = jnp.concatenate([at_all[:, sl], rt_all[:, sl]], axis=0)
        pb = _dot_nt(ar, bdiag(bt_all[:, sl]), HI)
        pk = _dot_nt(ar, bdiag(kt_all[:, sl]), HI)
        a_ab, a_rb = pb[:c] * strict, pb[c:] * incl
        a_ak, a_rk = pk[:c] * strict, pk[c:] * incl
        ad = a_ab * blk16
        a2 = pm(ad, ad)
        a4 = pm(a2, a2)
        a8 = pm(a4, a4)
        tinv = eye + ad
        tinv = tinv + pm(tinv, a2)
        tinv = tinv + pm(tinv, a4)
        tinv = tinv + pm(tinv, a8)
        tinv = tinv + pm(pm(tinv, a_ab * off32), tinv)
        tinv = tinv + pm(pm(tinv, a_ab * off64), tinv)
        s0 = state_ref[gi]
        vv = v_all[:, sl]
        ars = _dot_nt(ar, s0, HI)
        u = pm(tinv, ars[:c] + pm(a_ak, vv))
        y_ref[0, :, sl] = ars[c:] + pm(a_rb, u) + pm(a_rk, vv)
        uv = jnp.concatenate([u, vv], axis=0)
        bk = jnp.concatenate([bp_all[:, sl], kp_all[:, sl]], axis=0)
        state_ref[gi] = s0 * e_tot[:, sl] + hm * _dot_tn(uv, bk, HI)


def wkv_scan(r, v, kk, lw, av, kd, n_batch, L, Lc):
    c = WKV_CHUNK
    rows, w = r.shape
    nc, nl = Lc // c, L // c
    masks = _wkv_masks()

    def blk(d, b, s):
        j_ctx = jnp.where(d == 0, s, nc - 1 - s)
        j_lat = jnp.where(d == 0, s - nc, nl - 1 - (s - nc))
        return jnp.where(s < nc, (n_batch * L + b * Lc) // c + j_ctx, (b * L) // c + j_lat)

    shared = pl.BlockSpec((c, w), lambda d, b, s: (blk(d, b, s), 0))
    perdir = pl.BlockSpec((1, c, w), lambda d, b, s: (d, blk(d, b, s), 0))
    dmask = lambda m: pl.BlockSpec((1,) + m.shape[1:], lambda d, b, s: (d, 0, 0))
    cmask = lambda m: pl.BlockSpec(m.shape, lambda d, b, s: (0, 0))
    tinc, strict, incl, blk16, off32, off64, eye, hm = masks
    gw = WKV_GROUP * RW_HEAD_DIM
    return pl.pallas_call(
        _wkv_kernel,
        grid=(2, n_batch, nc + nl),
        in_specs=[shared, shared, shared, perdir, perdir, perdir, dmask(tinc), dmask(strict), dmask(incl),
                  cmask(blk16), cmask(off32), cmask(off64), cmask(eye), cmask(hm)],
        out_specs=perdir,
        out_shape=jax.ShapeDtypeStruct((2, rows, w), F32),
        scratch_shapes=[pltpu.VMEM((RW_HEADS // WKV_GROUP, gw, gw), F32)],
        compiler_params=_cparams(("arbitrary", "arbitrary", "arbitrary")),
        name="wkv_scan",
    )(r, v, kk, lw, av, kd, *masks)


def _moe_kernel(be_ref, nb_ref, x_ref, sw_ref, wg_ref, wu_ref, wd_ref, o_ref, wg_s, wu_s, wd_s):
    i = pl.program_id(0)
    prev = be_ref[jnp.maximum(i - 1, 0)]

    @pl.when((i == 0) | (be_ref[i] != prev))
    def _():
        wg_s[...] = wg_ref[0].astype(BF16)
        wu_s[...] = wu_ref[0].astype(BF16)
        wd_s[...] = wd_ref[0].astype(BF16)

    @pl.when(i < nb_ref[0])
    def _():
        x = x_ref[...]
        hmid = (jax.nn.silu(_dot(x, wg_s[...])) * _dot(x, wu_s[...])).astype(BF16)
        o_ref[...] = _dot(hmid, wd_s[...]) * sw_ref[...]

    @pl.when(i >= nb_ref[0])
    def _():
        o_ref[...] = jnp.zeros_like(o_ref)


def grouped_swiglu(xg, slot_w, block_e, n_used, w_gate, w_up, w_down):
    n_slots, d = xg.shape
    ff = w_gate.shape[-1]
    nb = n_slots // MOE_BLOCK
    grid_spec = pltpu.PrefetchScalarGridSpec(
        num_scalar_prefetch=2,
        grid=(nb,),
        in_specs=[pl.BlockSpec((MOE_BLOCK, d), lambda i, be, nu: (i, 0)),
                  pl.BlockSpec((MOE_BLOCK, 1), lambda i, be, nu: (i, 0)),
                  pl.BlockSpec((1, d, ff), lambda i, be, nu: (be[i], 0, 0)),
                  pl.BlockSpec((1, d, ff), lambda i, be, nu: (be[i], 0, 0)),
                  pl.BlockSpec((1, ff, d), lambda i, be, nu: (be[i], 0, 0))],
        out_specs=pl.BlockSpec((MOE_BLOCK, d), lambda i, be, nu: (i, 0)),
        scratch_shapes=[pltpu.VMEM((d, ff), BF16), pltpu.VMEM((d, ff), BF16), pltpu.VMEM((ff, d), BF16)],
    )
    return pl.pallas_call(
        _moe_kernel,
        grid_spec=grid_spec,
        out_shape=jax.ShapeDtypeStruct((n_slots, d), F32),
        compiler_params=_cparams(("arbitrary",)),
        name="grouped_swiglu",
    )(block_e, n_used, xg, slot_w, w_gate, w_up, w_down)


def _short_conv(u, w):
    T = u.shape[1]
    half = SHORT_CONV // 2
    up = jnp.pad(u, ((0, 0), (half, half), (0, 0)))
    y = up[:, 0:T] * w[0]
    for j in range(1, SHORT_CONV):
        y = y + up[:, j:j + T] * w[j]
    return y


def _rwkv_prep(cols, shift_w, w0, w2, a0, a2, g2, k_k, k_a):
    u = _short_conv(cols, shift_w)
    B, T, _ = u.shape
    r, k, v, wlo, alo, glo = jnp.split(u, RW_SPLITS, axis=-1)
    wlo = wlo.reshape(B, T, 2, RW_DECAY_LORA)
    alo = alo.reshape(B, T, 2, RW_AAA_LORA)
    w = -jax.nn.softplus(-(w0 + jnp.einsum('btdr,drc->btdc', jnp.tanh(wlo), w2))) - 0.5
    lw = -jnp.exp(w)
    a = jax.nn.sigmoid(a0 + jnp.einsum('btdr,drc->btdc', alo, a2))
    g = jax.nn.sigmoid(glo) @ g2
    kk = (k * k_k).reshape(B, T, RW_HEADS, RW_HEAD_DIM)
    kk = kk / jnp.maximum(jnp.sqrt(jnp.sum(kk * kk, axis=-1, keepdims=True)), 1e-12)
    kd = k[:, :, None, :] * (1.0 + (a - 1.0) * k_a)
    n = B * T
    flat = lambda t: t.reshape(n, RW_WIDTH)
    perdir = lambda t: jnp.moveaxis(t, 2, 0).reshape(2, n, RW_WIDTH)
    return flat(r), flat(v), flat(kk), perdir(lw), perdir(a), perdir(kd), flat(g)


def _rwkv_out(y, r, kd, v, g, r_k, ln_w, ln_b):
    n = y.shape[0]
    hh = lambda t: t.reshape(t.shape[:-1] + (RW_HEADS, RW_HEAD_DIM))
    yh = hh(y)
    mu = jnp.mean(yh, axis=-1, keepdims=True)
    var = jnp.mean(jnp.square(yh - mu), axis=-1, keepdims=True)
    yh = (yh - mu) * lax.rsqrt(var + RW_GN_EPS)
    rk = jnp.sum(hh(r)[None] * hh(kd) * r_k, axis=-1, keepdims=True)
    bonus = jnp.sum(rk * hh(v)[None], axis=0)
    out = yh.reshape(n, RW_WIDTH) * ln_w + ln_b + bonus.reshape(n, RW_WIDTH)
    return out * g


def _route(logits, router_b):
    T = logits.shape[0]
    scores = jax.nn.sigmoid(logits)
    biased = scores + router_b.astype(F32)
    grp = biased.reshape(T, N_GROUPS, N_EXPERTS // N_GROUPS)
    grp_score = jnp.sum(lax.top_k(grp, 2)[0], axis=-1)
    _, g_sel = lax.top_k(grp_score, TOPK_GROUPS)
    g_mask = jnp.any(g_sel[:, :, None] == jnp.arange(N_GROUPS)[None, None, :], axis=1)
    e_mask = jnp.repeat(g_mask, N_EXPERTS // N_GROUPS, axis=1)
    _, e_idx = lax.top_k(jnp.where(e_mask, biased, -jnp.inf), TOP_K)
    w = jnp.take_along_axis(scores, e_idx, axis=1)
    return e_idx, w / jnp.sum(w, axis=-1, keepdims=True) * ROUTE_SCALE


def _moe(h, logits, router_b, w_gate, w_up, w_down, sh_gate, sh_up, sh_down):
    T, D = h.shape
    e_idx, e_w = _route(logits, router_b)
    n = T * TOP_K
    flat_e = e_idx.reshape(-1)
    order = jnp.argsort(flat_e)
    e_sorted = flat_e[order]
    tok_sorted = (order // TOP_K).astype(jnp.int32)
    counts = jnp.bincount(flat_e, length=N_EXPERTS)
    padded = (counts + MOE_BLOCK - 1) // MOE_BLOCK * MOE_BLOCK
    pad_end = jnp.cumsum(padded)
    pad_start = pad_end - padded
    grp_start = jnp.cumsum(counts) - counts
    dest = (pad_start[e_sorted] + jnp.arange(n) - grp_start[e_sorted]).astype(jnp.int32)
    n_blocks = -(-n // MOE_BLOCK) + N_EXPERTS
    n_slots = n_blocks * MOE_BLOCK
    slot_tok = jnp.full((n_slots,), T, jnp.int32).at[dest].set(tok_sorted)
    slot_w = jnp.zeros((n_slots,), F32).at[dest].set(e_w.reshape(-1)[order])
    block_e = jnp.minimum(jnp.searchsorted(pad_end, jnp.arange(n_blocks) * MOE_BLOCK, side='right'),
                          N_EXPERTS - 1).astype(jnp.int32)
    n_used = (pad_end[-1] // MOE_BLOCK).astype(jnp.int32).reshape(1)
    h_pad = jnp.concatenate([h, jnp.zeros((1, D), h.dtype)], axis=0)
    y = grouped_swiglu(h_pad[slot_tok], slot_w.reshape(n_slots, 1), block_e, n_used, w_gate, w_up, w_down)
    pos = jnp.zeros((n,), jnp.int32).at[order].set(dest).reshape(T, TOP_K)
    routed = jnp.sum(y[pos], axis=1)
    nb_sh = T // MOE_BLOCK
    shared = grouped_swiglu(h, jnp.ones((T, 1), F32), jnp.zeros((nb_sh,), jnp.int32),
                            jnp.full((1,), nb_sh, jnp.int32), sh_gate[None], sh_up[None], sh_down[None])
    return routed + shared


def kernel(x, c, ctx, c_ctx, mod_w, mod_b, norm1_g, norm2_g, w_in, hy_conv, hy_w1, hy_b1, hy_w2, hy_b2, hy_freq,
           hy_w3, hy_bias, na_rpb, rw_shift, rw_w0, rw_w2, rw_a0, rw_a2, rw_g2, rw_kk, rw_ka, rw_rk, rw_ln_w,
           rw_ln_b, proj_a, proj_b, proj_c, w_out, router_w, router_b, exp_gate, exp_up, exp_down, sh_gate, sh_up,
           sh_down, final_g):
    B, L, D = x.shape
    Lc = ctx.shape[1]
    depth = mod_w.shape[0]
    n_lat, n_ctx = B * L, B * Lc
    assert L % TM == 0 and n_ctx % TM == 0 and L % WKV_CHUNK == 0 and Lc % WKV_CHUNK == 0
    col_hy = 3 * HY_WIDTH
    col_na = col_hy + 3 * NA_WIDTH
    col_rw = col_na + RW_COLS

    xs = jnp.concatenate([x.reshape(n_lat, D), ctx.reshape(n_ctx, D)], axis=0)
    grp_all = jnp.asarray(np.concatenate([np.repeat(np.arange(B), L // TM), np.full(n_ctx // TM, B)]), jnp.int32)
    s8 = jnp.zeros((8, D), F32).at[:B].set(jax.nn.silu(c)).at[B].set(jax.nn.silu(c_ctx))

    for i in range(depth):
        with_ctx = i < depth - 1
        mod = small_matmul_bias(s8, mod_w[i], mod_b[i])[:B + 1].reshape(B + 1, 1, N_MOD * D)
        sh1, sc1, g1, sh2, sc2, g2 = (mod[:, :, j * D:(j + 1) * D] for j in range(N_MOD))
        w_bf = w_in[i].astype(BF16)
        proj = functools.partial(normmod_matmul, xs, norm1_g[i], sh1, sc1, grp_all)
        hy = proj(w_bf[:, :col_hy], 512, F32)
        na = proj(w_bf[:, col_hy:col_na], 512, BF16)
        rw = proj(w_bf[:, col_na:col_rw], 384, F32)
        gates = proj(w_bf[:, col_rw:], 512, F32)

        hy_args = (hy_w1[i], hy_b1[i], hy_w2[i], hy_b2[i], hy_freq[i], hy_w3[i])
        u_lat = _short_conv(hy[:n_lat].reshape(B, L, col_hy), hy_conv[i]).reshape(n_lat, col_hy)
        h_raw, ss = hyena_filters_raw(L, *hy_args)
        rs = _filter_scale(ss)
        hr, hi = hyena_filter_spectrum(h_raw, rs, L)
        z = hyena_conv(u_lat, 0, u_lat, 2, hr, hi, 0, hy_bias[i][0], B, L)
        o_a = hyena_conv(u_lat, 1, z, 0, hr, hi, 1, hy_bias[i][1], B, L).astype(BF16)
        o_b = na_latent(na, na_rpb[i], B, L, Lc)
        rw_args = (rw_shift[i], rw_w0[i], rw_w2[i], rw_a0[i], rw_a2[i], rw_g2[i], rw_kk[i], rw_ka[i])
        prep_l = _rwkv_prep(rw[:n_lat].reshape(B, L, RW_COLS), *rw_args)
        prep_c = _rwkv_prep(rw[n_lat:].reshape(B, Lc, RW_COLS), *rw_args)
        r_, v_, kk_, lw_, av_, kd_, gg_ = (jnp.concatenate([a, b], axis=a.ndim - 2) for a, b in zip(prep_l, prep_c))
        y = wkv_scan(r_, v_, kk_, lw_, av_, kd_, B, L, Lc)
        o_c = _rwkv_out(y[0] + y[1], r_, kd_, v_, gg_, rw_rk[i], rw_ln_w[i], rw_ln_b[i]).astype(BF16)

        if with_ctx:
            u_ctx = _short_conv(hy[n_lat:].reshape(B, Lc, col_hy), hy_conv[i]).reshape(n_ctx, col_hy)
            h_raw_c, ss_c = hyena_filters_raw(Lc, *hy_args)
            o_a_c = hyena_small(u_ctx, 0, B, Lc, h_raw_c, _filter_scale(ss_c), hy_bias[i])
            o_a = jnp.concatenate([o_a, o_a_c], axis=0)
            o_b = jnp.concatenate([o_b, ctx_attn(na, B, L, Lc)], axis=0)
            m_rows = n_lat + n_ctx
        else:
            m_rows = n_lat
        grp = grp_all[:m_rows // TM]
        merged = branch_merge(o_a, o_b, o_c[:m_rows], gates[:m_rows], proj_a[i].astype(BF16),
                              proj_b[i].astype(BF16), proj_c[i].astype(BF16))
        xs = resid_matmul(merged, w_out[i].astype(BF16), xs[:m_rows], g1, grp)

        rw_pad = jnp.concatenate([router_w[i], jnp.zeros((D, LANES - N_EXPERTS), F32)], axis=1)
        logits = normmod_matmul(xs, norm2_g[i], sh2, sc2, grp, rw_pad, LANES, F32, hi=True)[:, :N_EXPERTS]
        h2 = normmod(xs, norm2_g[i], sh2, sc2, grp, BF16)
        f = _moe(h2, logits, router_b[i], exp_gate[i], exp_up[i], exp_down[i], sh_gate[i], sh_up[i], sh_down[i])
        xs = (xs.reshape(-1, TM, D) + g2[grp] * f.reshape(-1, TM, D)).reshape(m_rows, D)

    return rmsnorm_rows(xs[:n_lat], final_g).reshape(B, L, D)
```

```python
import functools
import math

import jax
import jax.numpy as jnp
import numpy as np
from jax import lax
from jax.experimental import pallas as pl
from jax.experimental.pallas import tpu as pltpu

F32 = jnp.float32
BF16 = jnp.bfloat16
HI = lax.Precision.HIGHEST

GRID_W = 64
NORM_EPS = 1e-6
N_MOD = 6
SHORT_CONV = 3
HY_WIDTH = 1024
HY_BANDS = 16
HY_EMB = 2 * HY_BANDS + 1
HY_FILTER_ORDER = 64
HY_FAST_DECAY = 0.3
HY_SLOW_DECAY = 1.5
HY_DECAY_TARGET = 1e-2
NA_HEADS = 16
NA_HEAD_DIM = 64
NA_WIDTH = NA_HEADS * NA_HEAD_DIM
NA_ROWS = 8
NA_COLS = 16
RW_HEADS = 16
RW_HEAD_DIM = 64
RW_WIDTH = RW_HEADS * RW_HEAD_DIM
RW_DECAY_LORA = 64
RW_AAA_LORA = 64
RW_GATE_LORA = 128
RW_GN_EPS = 64e-5
RW_COLS = 3 * RW_WIDTH + 2 * RW_DECAY_LORA + 2 * RW_AAA_LORA + RW_GATE_LORA
RW_SPLITS = [RW_WIDTH, 2 * RW_WIDTH, 3 * RW_WIDTH, 3 * RW_WIDTH + 2 * RW_DECAY_LORA,
             3 * RW_WIDTH + 2 * RW_DECAY_LORA + 2 * RW_AAA_LORA]
N_BRANCH = 3
N_EXPERTS = 64
TOP_K = 6
N_GROUPS = 8
TOPK_GROUPS = 4
ROUTE_SCALE = 2.5

LANES = 128
VMEM_LIMIT = 56 * 1024 * 1024
TM = 512
MOE_BLOCK = 256
WKV_CHUNK = 64
WKV_GROUP = 4


def _cparams(sem):
    return pltpu.CompilerParams(dimension_semantics=sem, vmem_limit_bytes=VMEM_LIMIT)


def _dot(a, b, prec=None):
    return jnp.dot(a, b, preferred_element_type=F32, precision=prec)


def _dot_nt(a, b, prec=None):
    return lax.dot_general(a, b, (((1,), (1,)), ((), ())), preferred_element_type=F32, precision=prec)


def _dot_tn(a, b, prec=None):
    return lax.dot_general(a, b, (((0,), (0,)), ((), ())), preferred_element_type=F32, precision=prec)


def _small_mm_kernel(a_ref, w_ref, b_ref, o_ref):
    o_ref[...] = _dot(a_ref[...], w_ref[...], HI) + b_ref[...]


def small_matmul_bias(a, w, b, tn=1536):
    m, k = a.shape
    n = w.shape[1]
    return pl.pallas_call(
        _small_mm_kernel,
        grid=(n // tn,),
        in_specs=[pl.BlockSpec((m, k), lambda j: (0, 0)),
                  pl.BlockSpec((k, tn), lambda j: (0, j)),
                  pl.BlockSpec((1, tn), lambda j: (0, j))],
        out_specs=pl.BlockSpec((m, tn), lambda j: (0, j)),
        out_shape=jax.ShapeDtypeStruct((m, n), F32),
        compiler_params=_cparams(("arbitrary",)),
        name="mod_matmul",
    )(a, w, b.reshape(1, n))


def _normmod_mm_kernel(grp_ref, x_ref, g_ref, sh_ref, sc_ref, w_ref, o_ref, h_ref, *, hi):
    del grp_ref

    @pl.when(pl.program_id(1) == 0)
    def _():
        x = x_ref[...]
        y = x * lax.rsqrt(jnp.mean(x * x, axis=-1, keepdims=True) + NORM_EPS)
        y = y * g_ref[...]
        h_ref[...] = (y * (1.0 + sc_ref[0]) + sh_ref[0]).astype(h_ref.dtype)

    o_ref[...] = _dot(h_ref[...], w_ref[...], HI if hi else None).astype(o_ref.dtype)


def normmod_matmul(x, g, shift3, scale3, grp, w, tn, out_dtype, hi=False):
    m, d = x.shape
    n = w.shape[1]
    grid_spec = pltpu.PrefetchScalarGridSpec(
        num_scalar_prefetch=1,
        grid=(m // TM, n // tn),
        in_specs=[pl.BlockSpec((TM, d), lambda i, j, grp: (i, 0)),
                  pl.BlockSpec((1, d), lambda i, j, grp: (0, 0)),
                  pl.BlockSpec((1, 1, d), lambda i, j, grp: (grp[i], 0, 0)),
                  pl.BlockSpec((1, 1, d), lambda i, j, grp: (grp[i], 0, 0)),
                  pl.BlockSpec((d, tn), lambda i, j, grp: (0, j))],
        out_specs=pl.BlockSpec((TM, tn), lambda i, j, grp: (i, j)),
        scratch_shapes=[pltpu.VMEM((TM, d), F32 if hi else BF16)],
    )
    return pl.pallas_call(
        functools.partial(_normmod_mm_kernel, hi=hi),
        grid_spec=grid_spec,
        out_shape=jax.ShapeDtypeStruct((m, n), out_dtype),
        compiler_params=_cparams(("arbitrary", "arbitrary")),
        name="normmod_matmul",
    )(grp, x, g.reshape(1, d), shift3, scale3, w)


def _normmod_kernel(grp_ref, x_ref, g_ref, sh_ref, sc_ref, o_ref):
    del grp_ref
    x = x_ref[...]
    y = x * lax.rsqrt(jnp.mean(x * x, axis=-1, keepdims=True) + NORM_EPS)
    o_ref[...] = ((y * g_ref[...]) * (1.0 + sc_ref[0]) + sh_ref[0]).astype(o_ref.dtype)


def normmod(x, g, shift3, scale3, grp, out_dtype):
    m, d = x.shape
    grid_spec = pltpu.PrefetchScalarGridSpec(
        num_scalar_prefetch=1,
        grid=(m // TM,),
        in_specs=[pl.BlockSpec((TM, d), lambda i, grp: (i, 0)),
                  pl.BlockSpec((1, d), lambda i, grp: (0, 0)),
                  pl.BlockSpec((1, 1, d), lambda i, grp: (grp[i], 0, 0)),
                  pl.BlockSpec((1, 1, d), lambda i, grp: (grp[i], 0, 0))],
        out_specs=pl.BlockSpec((TM, d), lambda i, grp: (i, 0)),
    )
    return pl.pallas_call(
        _normmod_kernel,
        grid_spec=grid_spec,
        out_shape=jax.ShapeDtypeStruct((m, d), out_dtype),
        compiler_params=_cparams(("arbitrary",)),
        name="normmod",
    )(grp, x, g.reshape(1, d), shift3, scale3)


def _merge_kernel(oa_ref, ob_ref, oc_ref, ga_ref, gb_ref, gc_ref, pa_ref, pb_ref, pc_ref, o_ref):
    m = jax.nn.sigmoid(ga_ref[...]) * _dot(oa_ref[...], pa_ref[...])
    m = m + jax.nn.sigmoid(gb_ref[...]) * _dot(ob_ref[...], pb_ref[...])
    m = m + jax.nn.sigmoid(gc_ref[...]) * _dot(oc_ref[...], pc_ref[...])
    o_ref[...] = m.astype(o_ref.dtype)


def branch_merge(m, o_a, o_b, o_c, gates, pa, pb, pc, tn=512):
    k = o_a.shape[1]
    d = pa.shape[1]
    nj = d // tn
    o_spec = pl.BlockSpec((TM, k), lambda i, j: (i, 0))
    p_spec = pl.BlockSpec((k, tn), lambda i, j: (0, j))
    return pl.pallas_call(
        _merge_kernel,
        grid=(m // TM, nj),
        in_specs=[o_spec, o_spec, o_spec,
                  pl.BlockSpec((TM, tn), lambda i, j: (i, j)),
                  pl.BlockSpec((TM, tn), lambda i, j: (i, j + nj)),
                  pl.BlockSpec((TM, tn), lambda i, j: (i, j + 2 * nj)),
                  p_spec, p_spec, p_spec],
        out_specs=pl.BlockSpec((TM, tn), lambda i, j: (i, j)),
        out_shape=jax.ShapeDtypeStruct((m, d), BF16),
        compiler_params=_cparams(("arbitrary", "arbitrary")),
        name="branch_merge",
    )(o_a, o_b, o_c, gates, gates, gates, pa, pb, pc)


def _resid_mm_kernel(grp_ref, a_ref, w_ref, x_ref, gate_ref, o_ref):
    del grp_ref
    o_ref[...] = x_ref[...] + gate_ref[0] * _dot(a_ref[...], w_ref[...])


def resid_matmul(a, w, x, gate3, grp, tn=512):
    m, k = a.shape
    d = w.shape[1]
    grid_spec = pltpu.PrefetchScalarGridSpec(
        num_scalar_prefetch=1,
        grid=(m // TM, d // tn),
        in_specs=[pl.BlockSpec((TM, k), lambda i, j, grp: (i, 0)),
                  pl.BlockSpec((k, tn), lambda i, j, grp: (0, j)),
                  pl.BlockSpec((TM, tn), lambda i, j, grp: (i, j)),
                  pl.BlockSpec((1, 1, tn), lambda i, j, grp: (grp[i], 0, j))],
        out_specs=pl.BlockSpec((TM, tn), lambda i, j, grp: (i, j)),
    )
    return pl.pallas_call(
        _resid_mm_kernel,
        grid_spec=grid_spec,
        out_shape=jax.ShapeDtypeStruct((m, d), F32),
        compiler_params=_cparams(("arbitrary", "arbitrary")),
        name="resid_matmul",
    )(grp, a, w, x, gate3)


def _rmsnorm_kernel(x_ref, g_ref, o_ref):
    x = x_ref[...]
    y = x * lax.rsqrt(jnp.mean(x * x, axis=-1, keepdims=True) + NORM_EPS)
    o_ref[...] = y * g_ref[...]


def rmsnorm_rows(x, g):
    m, d = x.shape
    return pl.pallas_call(
        _rmsnorm_kernel,
        grid=(m // TM,),
        in_specs=[pl.BlockSpec((TM, d), lambda i: (i, 0)), pl.BlockSpec((1, d), lambda i: (0, 0))],
        out_specs=pl.BlockSpec((TM, d), lambda i: (i, 0)),
        out_shape=jax.ShapeDtypeStruct((m, d), F32),
        compiler_params=_cparams(("arbitrary",)),
        name="final_rmsnorm",
    )(x, g.reshape(1, d))


def _hyfilt_kernel(z_ref, w1_ref, b1_ref, w2_ref, b2_ref, fr_ref, w3_ref, dl_ref, h_ref, ss_ref):
    z = z_ref[...]
    hdn = jnp.sin(fr_ref[0:1, :] * (_dot(z, w1_ref[...], HI) + b1_ref[...]))
    hdn = jnp.sin(fr_ref[1:2, :] * (_dot(hdn, w2_ref[...], HI) + b2_ref[...]))
    h = _dot(hdn, w3_ref[...], HI)
    h = h * jnp.exp(-z[:, 0:1] * dl_ref[...])
    h_ref[...] = h

    @pl.when(pl.program_id(0) == 0)
    def _():
        ss_ref[...] = jnp.zeros_like(ss_ref)

    ss_ref[...] += jnp.sum(h * h, axis=0, keepdims=True)


def hyena_filters_raw(L, w1, b1, w2, b2, freq, w3):
    t = np.linspace(0.0, 1.0, L, dtype=np.float32)[:, None]
    omega = np.float32(2.0 * math.pi / L) * np.arange(L, dtype=np.float32)[:, None]
    bands = np.linspace(1e-4, HY_BANDS - 1, HY_BANDS, dtype=np.float32)[None, :]
    z = np.concatenate([t, np.cos(omega * bands), -np.sin(omega * bands),
                        np.zeros((L, HY_FILTER_ORDER - HY_EMB), np.float32)], axis=-1).astype(np.float32)
    w1p = jnp.concatenate([w1, jnp.zeros((HY_FILTER_ORDER - HY_EMB, HY_FILTER_ORDER), F32)], axis=0)
    deltas = np.abs(np.linspace(math.log(HY_DECAY_TARGET) / HY_SLOW_DECAY,
                                math.log(HY_DECAY_TARGET) / HY_FAST_DECAY, HY_WIDTH, dtype=np.float32))
    dl4 = np.tile(deltas, 4)[None, :]
    tl = min(L, 256)
    n = 4 * HY_WIDTH
    fo = HY_FILTER_ORDER
    full = lambda shape: pl.BlockSpec(shape, lambda i: (0, 0))
    return pl.pallas_call(
        _hyfilt_kernel,
        grid=(L // tl,),
        in_specs=[pl.BlockSpec((tl, fo), lambda i: (i, 0)), full((fo, fo)), full((1, fo)), full((fo, fo)),
                  full((1, fo)), full((2, fo)), full((fo, n)), full((1, n))],
        out_specs=[pl.BlockSpec((tl, n), lambda i: (i, 0)), full((1, n))],
        out_shape=[jax.ShapeDtypeStruct((L, n), F32), jax.ShapeDtypeStruct((1, n), F32)],
        compiler_params=_cparams(("arbitrary",)),
        name="hyena_filters",
    )(jnp.asarray(z), w1p, b1.reshape(1, fo), w2, b2.reshape(1, fo), freq, w3, jnp.asarray(dl4))


def _filter_scale(ss):
    s = ss.reshape(2, 2, HY_WIDTH)
    rs = lax.rsqrt(jnp.sum(s, axis=1, keepdims=True))
    return jnp.broadcast_to(rs, (2, 2, HY_WIDTH)).reshape(1, 4 * HY_WIDTH)


FFT_N1 = 128
FFT_N2 = 64


@functools.lru_cache(maxsize=None)
def _fft_tables():
    n1, n2 = FFT_N1, FFT_N2
    n = n1 * n2
    a = np.arange(n1 // 2)[None, None, :]
    k1 = np.arange(n1)[None, :, None]
    b = np.arange(n2)[:, None, None]
    theta = 2.0 * np.pi * ((a * k1 % n1) / n1 + (b * k1) / n)
    g = np.concatenate([np.cos(theta), -np.sin(theta)], axis=1)
    ig = np.concatenate([np.cos(theta), -np.sin(theta)], axis=1).transpose(0, 2, 1) / n
    k2 = np.arange(n2)[:, None]
    bb = np.arange(n2)[None, :]
    ph = 2.0 * np.pi * (k2 * bb % n2) / n2
    fr, fi = np.cos(ph), -np.sin(ph)
    f2 = np.block([[fr, -fi], [fi, fr]])
    if2 = np.block([[fr, fi], [-fi, fr]])
    return (jnp.asarray(g, BF16), jnp.asarray(f2, BF16), jnp.asarray(if2, BF16), jnp.asarray(ig, BF16))


def _fft_stage1(u_ref, g_ref, sr_ref, si_ref):
    n1, n2 = FFT_N1, FFT_N2

    def body(b, carry):
        xb = u_ref[pl.ds(b, n1 // 2, stride=n2), :].astype(BF16)
        a = _dot(g_ref[b], xb)
        sr_ref[pl.ds(b, n1, stride=n2), :] = a[:n1]
        si_ref[pl.ds(b, n1, stride=n2), :] = a[n1:]
        return carry

    lax.fori_loop(0, n2, body, 0, unroll=2)


def _hyconv_kernel(xm_ref, u_ref, hr_ref, hi_ref, bias_ref, g_ref, f2_ref, if2_ref, ig_ref, o_ref, sr_ref, si_ref):
    n1, n2 = FFT_N1, FFT_N2
    _fft_stage1(u_ref, g_ref, sr_ref, si_ref)

    def pair(ref, k):
        blk = ref[pl.ds(pl.multiple_of(k * 2 * n2, 2 * n2), 2 * n2), :]
        return jnp.concatenate([blk[:n2], blk[n2:]], axis=1)

    def unpair(ref, k, val):
        w = val.shape[1] // 2
        ref[pl.ds(pl.multiple_of(k * 2 * n2, 2 * n2), n2), :] = val[:, :w]
        ref[pl.ds(pl.multiple_of(k * 2 * n2 + n2, n2), n2), :] = val[:, w:]

    def stage2(k, carry):
        z = jnp.concatenate([pair(sr_ref, k), pair(si_ref, k)], axis=0).astype(BF16)
        x = _dot(f2_ref[...], z)
        xr, xi = x[:n2], x[n2:]
        hr, hi = pair(hr_ref, k), pair(hi_ref, k)
        y = jnp.concatenate([xr * hr - xi * hi, xr * hi + xi * hr], axis=0).astype(BF16)
        bb = _dot(if2_ref[...], y)
        unpair(sr_ref, k, bb[:n2])
        unpair(si_ref, k, bb[n2:])
        return carry

    lax.fori_loop(0, n1 // 2, stage2, 0, unroll=2)
    bias = bias_ref[...]

    def stage3(b, carry):
        st = jnp.concatenate([sr_ref[pl.ds(b, n1, stride=n2), :], si_ref[pl.ds(b, n1, stride=n2), :]], axis=0)
        yb = _dot(ig_ref[b], st.astype(BF16))
        rows = pl.ds(b, n1 // 2, stride=n2)
        ub = u_ref[rows, :]
        o_ref[rows, :] = xm_ref[rows, :] * (yb + ub * bias)
        return carry

    lax.fori_loop(0, n2, stage3, 0, unroll=2)


def hyena_conv(xm_arr, xm_col, u_arr, u_col, hr, hi, h_col, bias, n_batch, L):
    assert L == FFT_N1 * FFT_N2 // 2
    cb = LANES
    nct = HY_WIDTH // cb
    n = 2 * L
    g, f2, if2, ig = _fft_tables()
    const3 = lambda shape: pl.BlockSpec(shape, lambda b, c: (0, 0, 0))
    const2 = lambda shape: pl.BlockSpec(shape, lambda b, c: (0, 0))
    return pl.pallas_call(
        _hyconv_kernel,
        grid=(n_batch, nct),
        in_specs=[pl.BlockSpec((L, cb), lambda b, c: (b, xm_col * nct + c)),
                  pl.BlockSpec((L, cb), lambda b, c: (b, u_col * nct + c)),
                  pl.BlockSpec((n, cb), lambda b, c: (0, h_col * nct + c)),
                  pl.BlockSpec((n, cb), lambda b, c: (0, h_col * nct + c)),
                  pl.BlockSpec((1, cb), lambda b, c: (0, c)),
                  const3(g.shape), const2(f2.shape), const2(if2.shape), const3(ig.shape)],
        out_specs=pl.BlockSpec((L, cb), lambda b, c: (b, c)),
        out_shape=jax.ShapeDtypeStruct((n_batch * L, HY_WIDTH), F32),
        scratch_shapes=[pltpu.VMEM((n, cb), F32), pltpu.VMEM((n, cb), F32)],
        compiler_params=_cparams(("arbitrary", "arbitrary")),
        name="hyena_conv",
    )(xm_arr, u_arr, hr, hi, bias.reshape(1, HY_WIDTH), g, f2, if2, ig)


def _hyspec_kernel(h0_ref, h1_ref, rs_ref, g_ref, f2_ref, hr_ref, hi_ref, s0r, s0i, s1r, s1i):
    n1, n2 = FFT_N1, FFT_N2
    _fft_stage1(h0_ref, g_ref, s0r, s0i)
    _fft_stage1(h1_ref, g_ref, s1r, s1i)
    rs = rs_ref[...]
    h10 = h1_ref[0:1, :]

    def stage2(k1, carry):
        rows = pl.ds(pl.multiple_of(k1 * n2, n2), n2)
        z = jnp.concatenate([jnp.concatenate([s0r[rows, :], s1r[rows, :]], axis=1),
                             jnp.concatenate([s0i[rows, :], s1i[rows, :]], axis=1)], axis=0).astype(BF16)
        x = _dot(f2_ref[...], z)
        w = x.shape[1] // 2
        hr_ref[rows, :] = rs * (x[:n2, :w] + x[:n2, w:] - h10)
        hi_ref[rows, :] = rs * (x[n2:, :w] - x[n2:, w:])
        return carry

    lax.fori_loop(0, n1, stage2, 0, unroll=2)


def hyena_filter_spectrum(h_raw, rs, L):
    assert L == FFT_N1 * FFT_N2 // 2
    cb = LANES
    nct = HY_WIDTH // cb
    n = 2 * L
    g, f2, _, _ = _fft_tables()
    out_spec = pl.BlockSpec((n, cb), lambda o, c: (0, o * nct + c))
    scr = pltpu.VMEM((n, cb), F32)
    return pl.pallas_call(
        _hyspec_kernel,
        grid=(2, nct),
        in_specs=[pl.BlockSpec((L, cb), lambda o, c: (0, (2 * o) * nct + c)),
                  pl.BlockSpec((L, cb), lambda o, c: (0, (2 * o + 1) * nct + c)),
                  pl.BlockSpec((1, cb), lambda o, c: (0, (2 * o) * nct + c)),
                  pl.BlockSpec(g.shape, lambda o, c: (0, 0, 0)),
                  pl.BlockSpec(f2.shape, lambda o, c: (0, 0))],
        out_specs=[out_spec, out_spec],
        out_shape=[jax.ShapeDtypeStruct((n, 2 * HY_WIDTH), F32)] * 2,
        scratch_shapes=[scr, scr, scr, scr],
        compiler_params=_cparams(("arbitrary", "arbitrary")),
        name="hyena_filter_spectrum",
    )(h_raw, h_raw, rs, g, f2)


@functools.lru_cache(maxsize=None)
def _dense_dft_tables(L):
    n = 2 * L
    k = np.arange(n)[:, None]
    t = np.arange(L)[None, :]
    ph = 2.0 * np.pi * (k * t % n) / n
    fwd = np.concatenate([np.cos(ph), -np.sin(ph)], axis=0)
    inv = np.concatenate([np.cos(ph), -np.sin(ph)], axis=0).T / n
    return jnp.asarray(fwd, F32), jnp.asarray(inv, F32)


def _hyena_small_kernel(x1_ref, x2_ref, v_ref, h_ref0a, h_ref0b, h_ref1a, h_ref1b, rs0_ref, rs1_ref,
                        b0_ref, b1_ref, fwd_ref, inv_ref, o_ref, *, L):
    n = 2 * L
    fwd = fwd_ref[...]
    inv = inv_ref[...]

    def conv(u, ha_ref, hb_ref, rs_ref, bias_ref):
        ha, hb = ha_ref[...], hb_ref[...]
        ka = _dot(fwd, ha, HI)
        kb = _dot(fwd, hb, HI)
        rs = rs_ref[...]
        kr = rs * (ka[:n] + kb[:n] - hb[0:1, :])
        ki = rs * (ka[n:] - kb[n:])
        uf = _dot(fwd, u, HI)
        ur, ui = uf[:n], uf[n:]
        y = jnp.concatenate([ur * kr - ui * ki, ur * ki + ui * kr], axis=0)
        return _dot(inv, y, HI) + u * bias_ref[...]

    v = v_ref[...]
    z = x1_ref[...] * conv(v, h_ref0a, h_ref0b, rs0_ref, b0_ref)
    o_ref[...] = (x2_ref[...] * conv(z, h_ref1a, h_ref1b, rs1_ref, b1_ref)).astype(o_ref.dtype)


def hyena_small(u_arr, row0_blocks, n_batch, L, h_raw, rs, bias):
    cb = LANES
    nct = HY_WIDTH // cb
    fwd, inv = _dense_dft_tables(L)
    uspec = lambda col: pl.BlockSpec((L, cb), lambda b, c: (row0_blocks + b, col * nct + c))
    hspec = lambda col: pl.BlockSpec((L, cb), lambda b, c: (0, col * nct + c))
    rspec = lambda col: pl.BlockSpec((1, cb), lambda b, c: (0, col * nct + c))
    bspec = pl.BlockSpec((1, cb), lambda b, c: (0, c))
    bias_0 = bias[0].reshape(1, HY_WIDTH)
    bias_1 = bias[1].reshape(1, HY_WIDTH)
    return pl.pallas_call(
        functools.partial(_hyena_small_kernel, L=L),
        grid=(n_batch, nct),
        in_specs=[uspec(0), uspec(1), uspec(2), hspec(0), hspec(1), hspec(2), hspec(3), rspec(0), rspec(2),
                  bspec, bspec,
                  pl.BlockSpec(fwd.shape, lambda b, c: (0, 0)), pl.BlockSpec(inv.shape, lambda b, c: (0, 0))],
        out_specs=pl.BlockSpec((L, cb), lambda b, c: (b, c)),
        out_shape=jax.ShapeDtypeStruct((n_batch * L, HY_WIDTH), BF16),
        compiler_params=_cparams(("arbitrary", "arbitrary")),
        name="hyena_ctx",
    )(u_arr, u_arr, u_arr, h_raw, h_raw, h_raw, h_raw, rs, rs, bias_0, bias_1, fwd, inv)


def _na_bias_table(rpb):
    cols = np.arange(GRID_W)
    col_start = np.clip(cols - NA_COLS // 2, 0, GRID_W - NA_COLS)[:, None]
    in_win = (cols[None, :] >= col_start) & (cols[None, :] < col_start + NA_COLS)
    rel_col = np.clip(cols[None, :] - cols[:, None], 1 - NA_COLS, NA_COLS - 1) + NA_COLS - 1
    tbl = rpb.astype(F32)[:, :, rel_col]
    tbl = jnp.where(jnp.asarray(in_win)[None, None], tbl, -jnp.inf)
    return jnp.concatenate([tbl[:, :-1], tbl[:, 1:]], axis=-1)


def _na_kernel(*refs, n_rows):
    q_ref = refs[0]
    k_refs = refs[1:1 + NA_ROWS]
    v_refs = refs[1 + NA_ROWS:1 + 2 * NA_ROWS]
    kc_ref, vc_ref, tbl_ref, o_ref = refs[1 + 2 * NA_ROWS:]
    r = pl.program_id(1)
    start = jnp.clip(r - NA_ROWS // 2, 0, n_rows - NA_ROWS)
    d0 = start - r + NA_ROWS - 1
    dh = NA_HEAD_DIM
    q = q_ref[...] * (dh ** -0.5)
    for h in range(NA_HEADS):
        hs = slice(h * dh, (h + 1) * dh)
        qh = q[:, hs]
        s_tiles = []
        v_tiles = []
        for p in range(NA_ROWS // 2):
            kp = jnp.concatenate([k_refs[2 * p][:, hs], k_refs[2 * p + 1][:, hs]], axis=0)
            s_tiles.append(_dot_nt(qh, kp) + tbl_ref[h, d0 + 2 * p])
            v_tiles.append(jnp.concatenate([v_refs[2 * p][:, hs], v_refs[2 * p + 1][:, hs]], axis=0))
        s_tiles.append(_dot_nt(qh, kc_ref[:, hs]))
        v_tiles.append(vc_ref[:, hs])
        m = s_tiles[0].max(axis=-1, keepdims=True)
        for s in s_tiles[1:]:
            m = jnp.maximum(m, s.max(axis=-1, keepdims=True))
        l = jnp.zeros_like(m)
        acc = jnp.zeros((GRID_W, dh), F32)
        for s, vt in zip(s_tiles, v_tiles):
            p_ = jnp.exp(s - m)
            l = l + p_.sum(axis=-1, keepdims=True)
            acc = acc + _dot(p_.astype(BF16), vt)
        o_ref[:, hs] = (acc / l).astype(o_ref.dtype)


def na_latent(na, rpb, n_batch, L, Lc):
    n_rows = L // GRID_W
    assert n_rows >= NA_ROWS
    tbl = _na_bias_table(rpb)
    w = NA_WIDTH
    ctx_blk0 = n_batch * L // Lc

    def kv_spec(i, col):
        def imap(b, r):
            start = jnp.clip(r - NA_ROWS // 2, 0, n_rows - NA_ROWS)
            return (b * n_rows + start + i, col)
        return pl.BlockSpec((GRID_W, w), imap)

    in_specs = ([pl.BlockSpec((GRID_W, w), lambda b, r: (b * n_rows + r, 0))]
                + [kv_spec(i, 1) for i in range(NA_ROWS)] + [kv_spec(i, 2) for i in range(NA_ROWS)]
                + [pl.BlockSpec((Lc, w), lambda b, r: (ctx_blk0 + b, 1)),
                   pl.BlockSpec((Lc, w), lambda b, r: (ctx_blk0 + b, 2)),
                   pl.BlockSpec(tbl.shape, lambda b, r: (0, 0, 0, 0))])
    return pl.pallas_call(
        functools.partial(_na_kernel, n_rows=n_rows),
        grid=(n_batch, n_rows),
        in_specs=in_specs,
        out_specs=pl.BlockSpec((GRID_W, w), lambda b, r: (b * n_rows + r, 0)),
        out_shape=jax.ShapeDtypeStruct((n_batch * L, w), BF16),
        compiler_params=_cparams(("arbitrary", "arbitrary")),
        name="na_latent",
    )(*([na] * (3 + 2 * NA_ROWS)), tbl)


def _ctx_attn_kernel(q_ref, k_ref, v_ref, o_ref):
    dh = NA_HEAD_DIM
    q = q_ref[...] * (dh ** -0.5)
    for h in range(NA_HEADS):
        hs = slice(h * dh, (h + 1) * dh)
        s = _dot_nt(q[:, hs], k_ref[:, hs])
        p_ = jnp.exp(s - s.max(axis=-1, keepdims=True))
        acc = _dot(p_.astype(BF16), v_ref[:, hs])
        o_ref[:, hs] = (acc / p_.sum(axis=-1, keepdims=True)).astype(o_ref.dtype)


def ctx_attn(na, n_batch, L, Lc):
    w = NA_WIDTH
    blk0 = n_batch * L // Lc
    spec = lambda col: pl.BlockSpec((Lc, w), lambda b: (blk0 + b, col))
    return pl.pallas_call(
        _ctx_attn_kernel,
        grid=(n_batch,),
        in_specs=[spec(0), spec(1), spec(2)],
        out_specs=pl.BlockSpec((Lc, w), lambda b: (b, 0)),
        out_shape=jax.ShapeDtypeStruct((n_batch * Lc, w), BF16),
        compiler_params=_cparams(("arbitrary",)),
        name="ctx_attn",
    )(na, na, na)


@functools.lru_cache(maxsize=None)
def _wkv_masks():
    c, g = WKV_CHUNK, WKV_GROUP
    t = np.arange(c)[:, None]
    s = np.arange(c)[None, :]
    tinc = np.stack([(s <= t), (s >= t)]).astype(np.float32)
    strict = np.stack([(s < t), (s > t)]).astype(np.float32)
    tile = lambda m: np.tile(m, (1,) * (m.ndim - 1) + (g,))
    blk = lambda n: (t // n == s // n)
    blk16 = tile(blk(16).astype(np.float32))
    off32 = tile((blk(32) & ~blk(16)).astype(np.float32))
    off64 = tile((~blk(32)).astype(np.float32))
    eye = tile((t == s).astype(np.float32))
    rr = np.arange(g * c)
    hm = (rr[:, None] // c == np.arange(g * RW_HEAD_DIM)[None, :] // RW_HEAD_DIM).astype(np.float32)
    masks = tuple(jnp.asarray(m) for m in (tinc, tile(strict), tile(tinc), blk16, off32, off64, eye, hm))
    return masks + (jnp.asarray(hm, BF16),)


def _wkv_kernel(r_ref, v_ref, kk_ref, lw_ref, av_ref, kd_ref, tinc_ref, strict_ref, incl_ref, blk16_ref, off32_ref,
                off64_ref, eye_ref, hm_ref, hmb_ref, y_ref, state_ref):
    c, g = WKV_CHUNK, WKV_GROUP
    gw = g * RW_HEAD_DIM

    @pl.when(pl.program_id(2) == 0)
    def _():
        state_ref[...] = jnp.zeros_like(state_ref)

    lw = lw_ref[0]
    cum = _dot(tinc_ref[0], lw, HI)
    tot = jnp.sum(lw, axis=0, keepdims=True)
    e_in = jnp.exp(cum)
    e_ex = jnp.exp(cum - lw)
    e_neg = jnp.exp(-cum)
    e_rem = jnp.exp(tot - cum)
    e_tot = jnp.exp(tot)
    kk = kk_ref[...]
    b_vec = kk * av_ref[0]
    kd = kd_ref[0]
    at_all = -kk * e_ex
    rt_all = r_ref[...] * e_in
    bt_all = b_vec * e_neg
    kt_all = kd * e_neg
    bp_all = b_vec * e_rem
    kp_all = kd * e_rem
    v_all = v_ref[...]
    hm = hm_ref[...]
    hm_bf = hmb_ref[...]
    strict, incl = strict_ref[0], incl_ref[0]
    blk16, off32, off64, eye = blk16_ref[...], off32_ref[...], off64_ref[...], eye_ref[...]

    def bdiag(z):
        return jnp.concatenate([z.astype(BF16)] * g, axis=0) * hm_bf

    def pm(x4, zd):
        return _dot(x4.astype(BF16), zd)

    for gi in range(RW_HEADS // g):
        sl = slice(gi * gw, (gi + 1) * gw)
        ar = jnp.concatenate([at_all[:, sl], rt_all[:, sl]], axis=0).astype(BF16)
        pb = _dot_nt(ar, bdiag(bt_all[:, sl]))
        pk = _dot_nt(ar, bdiag(kt_all[:, sl]))
        a_ab, a_rb = pb[:c] * strict, pb[c:] * incl
        a_ak, a_rk = pk[:c] * strict, pk[c:] * incl
        ad = a_ab * blk16
        a2 = pm(ad, bdiag(ad))
        a4 = pm(a2, bdiag(a2))
        a8 = pm(a4, bdiag(a4))
        tinv = eye + ad
        tinv = tinv + pm(tinv, bdiag(a2))
        tinv = tinv + pm(tinv, bdiag(a4))
        tinv = tinv + pm(tinv, bdiag(a8))
        tinv = tinv + pm(pm(tinv, bdiag(a_ab * off32)), bdiag(tinv))
        tinv = tinv + pm(pm(tinv, bdiag(a_ab * off64)), bdiag(tinv))
        s0 = state_ref[gi]
        vv = v_all[:, sl]
        vd = bdiag(vv)
        ars = _dot_nt(ar, s0.astype(BF16))
        u = pm(tinv, bdiag(ars[:c] + pm(a_ak, vd)))
        y_ref[0, :, sl] = ars[c:] + pm(a_rb, bdiag(u)) + pm(a_rk, vd)
        uv = jnp.concatenate([u, vv], axis=0).astype(BF16)
        bk = jnp.concatenate([bp_all[:, sl], kp_all[:, sl]], axis=0).astype(BF16)
        state_ref[gi] = s0 * e_tot[:, sl] + hm * _dot_tn(uv, bk)


def wkv_scan(r, v, kk, lw, av, kd, n_batch, L, Lc):
    c = WKV_CHUNK
    rows, w = r.shape
    nc, nl = Lc // c, L // c
    masks = _wkv_masks()

    def blk(d, b, s):
        j_ctx = jnp.where(d == 0, s, nc - 1 - s)
        j_lat = jnp.where(d == 0, s - nc, nl - 1 - (s - nc))
        return jnp.where(s < nc, (n_batch * L + b * Lc) // c + j_ctx, (b * L) // c + j_lat)

    shared = pl.BlockSpec((c, w), lambda d, b, s: (blk(d, b, s), 0))
    perdir = pl.BlockSpec((1, c, w), lambda d, b, s: (d, blk(d, b, s), 0))
    dmask = lambda m: pl.BlockSpec((1,) + m.shape[1:], lambda d, b, s: (d, 0, 0))
    cmask = lambda m: pl.BlockSpec(m.shape, lambda d, b, s: (0, 0))
    tinc, strict, incl, blk16, off32, off64, eye, hm, hm_bf = masks
    gw = WKV_GROUP * RW_HEAD_DIM
    return pl.pallas_call(
        _wkv_kernel,
        grid=(2, n_batch, nc + nl),
        in_specs=[shared, shared, shared, perdir, perdir, perdir, dmask(tinc), dmask(strict), dmask(incl),
                  cmask(blk16), cmask(off32), cmask(off64), cmask(eye), cmask(hm), cmask(hm_bf)],
        out_specs=perdir,
        out_shape=jax.ShapeDtypeStruct((2, rows, w), F32),
        scratch_shapes=[pltpu.VMEM((RW_HEADS // WKV_GROUP, gw, gw), F32)],
        compiler_params=_cparams(("arbitrary", "arbitrary", "arbitrary")),
        name="wkv_scan",
    )(r, v, kk, lw, av, kd, *masks)


def _moe_kernel(be_ref, nb_ref, x_ref, sw_ref, wg_ref, wu_ref, wd_ref, o_ref, wg_s, wu_s, wd_s):
    i = pl.program_id(0)
    prev = be_ref[jnp.maximum(i - 1, 0)]

    @pl.when((i == 0) | (be_ref[i] != prev))
    def _():
        wg_s[...] = wg_ref[0].astype(BF16)
        wu_s[...] = wu_ref[0].astype(BF16)
        wd_s[...] = wd_ref[0].astype(BF16)

    @pl.when(i < nb_ref[0])
    def _():
        x = x_ref[...]
        hmid = (jax.nn.silu(_dot(x, wg_s[...])) * _dot(x, wu_s[...])).astype(BF16)
        o_ref[...] = (_dot(hmid, wd_s[...]) * sw_ref[...]).astype(o_ref.dtype)

    @pl.when(i >= nb_ref[0])
    def _():
        o_ref[...] = jnp.zeros_like(o_ref)


def grouped_swiglu(xg, slot_w, block_e, n_used, w_gate, w_up, w_down):
    n_slots, d = xg.shape
    ff = w_gate.shape[-1]
    nb = n_slots // MOE_BLOCK
    grid_spec = pltpu.PrefetchScalarGridSpec(
        num_scalar_prefetch=2,
        grid=(nb,),
        in_specs=[pl.BlockSpec((MOE_BLOCK, d), lambda i, be, nu: (i, 0)),
                  pl.BlockSpec((MOE_BLOCK, 1), lambda i, be, nu: (i, 0)),
                  pl.BlockSpec((1, d, ff), lambda i, be, nu: (be[i], 0, 0)),
                  pl.BlockSpec((1, d, ff), lambda i, be, nu: (be[i], 0, 0)),
                  pl.BlockSpec((1, ff, d), lambda i, be, nu: (be[i], 0, 0))],
        out_specs=pl.BlockSpec((MOE_BLOCK, d), lambda i, be, nu: (i, 0)),
        scratch_shapes=[pltpu.VMEM((d, ff), BF16), pltpu.VMEM((d, ff), BF16), pltpu.VMEM((ff, d), BF16)],
    )
    return pl.pallas_call(
        _moe_kernel,
        grid_spec=grid_spec,
        out_shape=jax.ShapeDtypeStruct((n_slots, d), BF16),
        compiler_params=_cparams(("arbitrary",)),
        name="grouped_swiglu",
    )(block_e, n_used, xg, slot_w, w_gate, w_up, w_down)


def _combine_kernel(grp_ref, x_ref, y_ref, s_ref, gate_ref, o_ref):
    del grp_ref
    d = x_ref.shape[1]
    f = s_ref[...].astype(F32)
    for k in range(TOP_K):
        f = f + y_ref[:, k * d:(k + 1) * d].astype(F32)
    o_ref[...] = x_ref[...] + gate_ref[0] * f


def moe_combine(x, yg, shared, gate3, grp):
    m, d = x.shape
    tm = TM // 2
    grid_spec = pltpu.PrefetchScalarGridSpec(
        num_scalar_prefetch=1,
        grid=(m // tm,),
        in_specs=[pl.BlockSpec((tm, d), lambda i, grp: (i, 0)),
                  pl.BlockSpec((tm, TOP_K * d), lambda i, grp: (i, 0)),
                  pl.BlockSpec((tm, d), lambda i, grp: (i, 0)),
                  pl.BlockSpec((1, 1, d), lambda i, grp: (grp[i // 2], 0, 0))],
        out_specs=pl.BlockSpec((tm, d), lambda i, grp: (i, 0)),
    )
    return pl.pallas_call(
        _combine_kernel,
        grid_spec=grid_spec,
        out_shape=jax.ShapeDtypeStruct((m, d), F32),
        compiler_params=_cparams(("arbitrary",)),
        name="moe_combine",
    )(grp, x, yg, shared, gate3)


def _seq_edge_masks(n_batch, L, Lc):
    n_lat = n_batch * L
    starts = np.concatenate([np.arange(n_batch) * L, n_lat + np.arange(n_batch) * Lc])
    ends = np.concatenate([(np.arange(n_batch) + 1) * L, n_lat + (np.arange(n_batch) + 1) * Lc]) - 1
    first = np.ones((n_batch * (L + Lc), 1), np.float32)
    last = first.copy()
    first[starts] = 0.0
    last[ends] = 0.0
    return jnp.asarray(first), jnp.asarray(last)


def _short_conv(u, w, first, last):
    assert SHORT_CONV == 3
    zero = jnp.zeros_like(u[:1])
    prev = jnp.concatenate([zero, u[:-1]], axis=0) * first
    nxt = jnp.concatenate([u[1:], zero], axis=0) * last
    return prev * w[0] + u * w[1] + nxt * w[2]


def _rwkv_prep(cols, first, last, shift_w, w0, w2, a0, a2, g2, k_k, k_a):
    u = _short_conv(cols, shift_w, first, last)
    n = u.shape[0]
    r, k, v, wlo, alo, glo = jnp.split(u, RW_SPLITS, axis=-1)
    lw, av, kd = [], [], []
    for d in range(2):
        wl = wlo[:, d * RW_DECAY_LORA:(d + 1) * RW_DECAY_LORA]
        al = alo[:, d * RW_AAA_LORA:(d + 1) * RW_AAA_LORA]
        w = -jax.nn.softplus(-(w0[d] + jnp.tanh(wl) @ w2[d])) - 0.5
        lw.append(-jnp.exp(w))
        a = jax.nn.sigmoid(a0[d] + al @ a2[d])
        av.append(a)
        kd.append(k * (1.0 + (a - 1.0) * k_a))
    g = jax.nn.sigmoid(glo) @ g2
    kk = (k * k_k).reshape(n, RW_HEADS, RW_HEAD_DIM)
    kk = kk / jnp.maximum(jnp.sqrt(jnp.sum(kk * kk, axis=-1, keepdims=True)), 1e-12)
    return r, v, kk.reshape(n, RW_WIDTH), jnp.stack(lw), jnp.stack(av), jnp.stack(kd), g


def _rwkv_out(y, r, kd, v, g, r_k, ln_w, ln_b):
    n = y.shape[0]
    hh = lambda t: t.reshape(t.shape[:-1] + (RW_HEADS, RW_HEAD_DIM))
    yh = hh(y)
    mu = jnp.mean(yh, axis=-1, keepdims=True)
    var = jnp.mean(jnp.square(yh - mu), axis=-1, keepdims=True)
    yh = (yh - mu) * lax.rsqrt(var + RW_GN_EPS)
    rk = jnp.sum(hh(r)[None] * hh(kd) * r_k, axis=-1, keepdims=True)
    bonus = jnp.sum(rk * hh(v)[None], axis=0)
    out = yh.reshape(n, RW_WIDTH) * ln_w + ln_b + bonus.reshape(n, RW_WIDTH)
    return out * g


def _route(logits, router_b):
    T = logits.shape[0]
    scores = jax.nn.sigmoid(logits)
    biased = scores + router_b.astype(F32)
    grp = biased.reshape(T, N_GROUPS, N_EXPERTS // N_GROUPS)
    grp_score = jnp.sum(lax.top_k(grp, 2)[0], axis=-1)
    _, g_sel = lax.top_k(grp_score, TOPK_GROUPS)
    g_mask = jnp.any(g_sel[:, :, None] == jnp.arange(N_GROUPS)[None, None, :], axis=1)
    e_mask = jnp.repeat(g_mask, N_EXPERTS // N_GROUPS, axis=1)
    _, e_idx = lax.top_k(jnp.where(e_mask, biased, -jnp.inf), TOP_K)
    w = jnp.take_along_axis(scores, e_idx, axis=1)
    return e_idx, w / jnp.sum(w, axis=-1, keepdims=True) * ROUTE_SCALE


def _moe(h, logits, router_b, w_gate, w_up, w_down, sh_gate, sh_up, sh_down):
    T, D = h.shape
    e_idx, e_w = _route(logits, router_b)
    n = T * TOP_K
    chosen = jnp.any(e_idx[:, :, None] == jnp.arange(N_EXPERTS)[None, None, :], axis=1).astype(jnp.int32)
    before = jnp.cumsum(chosen, axis=0) - chosen
    counts = jnp.sum(chosen, axis=0)
    padded = (counts + MOE_BLOCK - 1) // MOE_BLOCK * MOE_BLOCK
    pad_end = jnp.cumsum(padded)
    pad_start = pad_end - padded
    dest = (pad_start[e_idx] + jnp.take_along_axis(before, e_idx, axis=1)).astype(jnp.int32)
    n_blocks = -(-n // MOE_BLOCK) + N_EXPERTS
    n_slots = n_blocks * MOE_BLOCK
    flat_dest = dest.reshape(-1)
    tok = jnp.repeat(jnp.arange(T, dtype=jnp.int32), TOP_K)
    slot_tok = jnp.zeros((n_slots,), jnp.int32).at[flat_dest].set(tok)
    slot_w = jnp.zeros((n_slots,), F32).at[flat_dest].set(e_w.reshape(-1))
    block_e = jnp.minimum(jnp.searchsorted(pad_end, jnp.arange(n_blocks) * MOE_BLOCK, side='right'),
                          N_EXPERTS - 1).astype(jnp.int32)
    n_used = (pad_end[-1] // MOE_BLOCK).astype(jnp.int32).reshape(1)
    xg = jnp.take(h, slot_tok, axis=0)
    y = grouped_swiglu(xg, slot_w.reshape(n_slots, 1), block_e, n_used, w_gate, w_up, w_down)
    yg = jnp.take(y, flat_dest, axis=0).reshape(T, TOP_K * D)
    nb_sh = T // MOE_BLOCK
    shared = grouped_swiglu(h, jnp.ones((T, 1), F32), jnp.zeros((nb_sh,), jnp.int32),
                            jnp.full((1,), nb_sh, jnp.int32), sh_gate[None], sh_up[None], sh_down[None])
    return yg, shared


def kernel(x, c, ctx, c_ctx, mod_w, mod_b, norm1_g, norm2_g, w_in, hy_conv, hy_w1, hy_b1, hy_w2, hy_b2, hy_freq,
           hy_w3, hy_bias, na_rpb, rw_shift, rw_w0, rw_w2, rw_a0, rw_a2, rw_g2, rw_kk, rw_ka, rw_rk, rw_ln_w,
           rw_ln_b, proj_a, proj_b, proj_c, w_out, router_w, router_b, exp_gate, exp_up, exp_down, sh_gate, sh_up,
           sh_down, final_g):
    B, L, D = x.shape
    Lc = ctx.shape[1]
    depth = mod_w.shape[0]
    n_lat, n_ctx = B * L, B * Lc
    assert L % TM == 0 and n_ctx % TM == 0 and L % WKV_CHUNK == 0 and Lc % WKV_CHUNK == 0
    col_hy = 3 * HY_WIDTH
    col_na = col_hy + 3 * NA_WIDTH
    col_rw = col_na + RW_COLS

    xs = jnp.concatenate([x.reshape(n_lat, D), ctx.reshape(n_ctx, D)], axis=0)
    grp_all = jnp.asarray(np.concatenate([np.repeat(np.arange(B), L // TM), np.full(n_ctx // TM, B)]), jnp.int32)
    s8 = jnp.zeros((8, D), F32).at[:B].set(jax.nn.silu(c)).at[B].set(jax.nn.silu(c_ctx))
    first, last = _seq_edge_masks(B, L, Lc)

    for i in range(depth):
        with_ctx = i < depth - 1
        mod = small_matmul_bias(s8, mod_w[i], mod_b[i])[:B + 1].reshape(B + 1, 1, N_MOD * D)
        sh1, sc1, g1, sh2, sc2, g2 = (mod[:, :, j * D:(j + 1) * D] for j in range(N_MOD))
        w_bf = w_in[i].astype(BF16)
        proj = functools.partial(normmod_matmul, xs, norm1_g[i], sh1, sc1, grp_all)
        hy = _short_conv(proj(w_bf[:, :col_hy], 512, F32), hy_conv[i], first, last)
        na = proj(w_bf[:, col_hy:col_na], 512, BF16)
        rw = proj(w_bf[:, col_na:col_rw], 384, F32)
        gates = proj(w_bf[:, col_rw:], 512, F32)

        hy_args = (hy_w1[i], hy_b1[i], hy_w2[i], hy_b2[i], hy_freq[i], hy_w3[i])
        h_raw, ss = hyena_filters_raw(L, *hy_args)
        hr, hi = hyena_filter_spectrum(h_raw, _filter_scale(ss), L)
        z = hyena_conv(hy, 0, hy, 2, hr, hi, 0, hy_bias[i][0], B, L)
        o_a = hyena_conv(hy, 1, z, 0, hr, hi, 1, hy_bias[i][1], B, L).astype(BF16)
        o_b = na_latent(na, na_rpb[i], B, L, Lc)
        r_, v_, kk_, lw_, av_, kd_, gg_ = _rwkv_prep(rw, first, last, rw_shift[i], rw_w0[i], rw_w2[i], rw_a0[i],
                                                     rw_a2[i], rw_g2[i], rw_kk[i], rw_ka[i])
        y = wkv_scan(r_, v_, kk_, lw_, av_, kd_, B, L, Lc)
        o_c = _rwkv_out(y[0] + y[1], r_, kd_, v_, gg_, rw_rk[i], rw_ln_w[i], rw_ln_b[i]).astype(BF16)

        if with_ctx:
            h_raw_c, ss_c = hyena_filters_raw(Lc, *hy_args)
            o_a_c = hyena_small(hy, n_lat // Lc, B, Lc, h_raw_c, _filter_scale(ss_c), hy_bias[i])
            o_a = jnp.concatenate([o_a, o_a_c], axis=0)
            o_b = jnp.concatenate([o_b, ctx_attn(na, B, L, Lc)], axis=0)
            m_rows = n_lat + n_ctx
        else:
            m_rows = n_lat
        grp = grp_all[:m_rows // TM]
        merged = branch_merge(m_rows, o_a, o_b, o_c, gates, proj_a[i].astype(BF16), proj_b[i].astype(BF16),
                              proj_c[i].astype(BF16))
        xs = resid_matmul(merged, w_out[i].astype(BF16), xs, g1, grp)

        router_pad = jnp.concatenate([router_w[i], jnp.zeros((D, LANES - N_EXPERTS), F32)], axis=1)
        logits = normmod_matmul(xs, norm2_g[i], sh2, sc2, grp, router_pad, LANES, F32, hi=True)[:, :N_EXPERTS]
        h2 = normmod(xs, norm2_g[i], sh2, sc2, grp, BF16)
        yg, shared = _moe(h2, logits, router_b[i], exp_gate[i], exp_up[i], exp_down[i], sh_gate[i], sh_up[i],
                          sh_down[i])
        xs = moe_combine(xs, yg, shared, g2, grp)

    return rmsnorm_rows(xs, final_g).reshape(B, L, D)
```

```python
import functools
import math

import jax
import jax.numpy as jnp
import numpy as np
from jax import lax
from jax.experimental import pallas as pl
from jax.experimental.pallas import tpu as pltpu

F32 = jnp.float32
BF16 = jnp.bfloat16
HI = lax.Precision.HIGHEST

GRID_W = 64
NORM_EPS = 1e-6
N_MOD = 6
SHORT_CONV = 3
HY_WIDTH = 1024
HY_BANDS = 16
HY_EMB = 2 * HY_BANDS + 1
HY_FILTER_ORDER = 64
HY_FAST_DECAY = 0.3
HY_SLOW_DECAY = 1.5
HY_DECAY_TARGET = 1e-2
NA_HEADS = 16
NA_HEAD_DIM = 64
NA_WIDTH = NA_HEADS * NA_HEAD_DIM
NA_ROWS = 8
NA_COLS = 16
RW_HEADS = 16
RW_HEAD_DIM = 64
RW_WIDTH = RW_HEADS * RW_HEAD_DIM
RW_DECAY_LORA = 64
RW_AAA_LORA = 64
RW_GATE_LORA = 128
RW_GN_EPS = 64e-5
RW_COLS = 3 * RW_WIDTH + 2 * RW_DECAY_LORA + 2 * RW_AAA_LORA + RW_GATE_LORA
RW_SPLITS = [RW_WIDTH, 2 * RW_WIDTH, 3 * RW_WIDTH, 3 * RW_WIDTH + 2 * RW_DECAY_LORA,
             3 * RW_WIDTH + 2 * RW_DECAY_LORA + 2 * RW_AAA_LORA]
N_BRANCH = 3
N_EXPERTS = 64
TOP_K = 6
N_GROUPS = 8
TOPK_GROUPS = 4
ROUTE_SCALE = 2.5

LANES = 128
VMEM_LIMIT = 56 * 1024 * 1024
TM = 512
MOE_BLOCK = 256
WKV_CHUNK = 64
WKV_GROUP = 4


def _cparams(sem):
    return pltpu.CompilerParams(dimension_semantics=sem, vmem_limit_bytes=VMEM_LIMIT)


def _dot(a, b, prec=None):
    return jnp.dot(a, b, preferred_element_type=F32, precision=prec)


def _dot_nt(a, b, prec=None):
    return lax.dot_general(a, b, (((1,), (1,)), ((), ())), preferred_element_type=F32, precision=prec)


def _dot_tn(a, b, prec=None):
    return lax.dot_general(a, b, (((0,), (0,)), ((), ())), preferred_element_type=F32, precision=prec)


def _small_mm_kernel(a_ref, w_ref, b_ref, o_ref):
    o_ref[...] = _dot(a_ref[...], w_ref[0], HI) + b_ref[0]


def small_matmul_bias(a, w, b, layer, tn=1536):
    m, k = a.shape
    n = w.shape[2]
    return pl.pallas_call(
        _small_mm_kernel,
        grid=(n // tn,),
        in_specs=[pl.BlockSpec((m, k), lambda j: (0, 0)),
                  pl.BlockSpec((1, k, tn), lambda j: (layer, 0, j)),
                  pl.BlockSpec((1, 1, tn), lambda j: (layer, 0, j))],
        out_specs=pl.BlockSpec((m, tn), lambda j: (0, j)),
        out_shape=jax.ShapeDtypeStruct((m, n), F32),
        compiler_params=_cparams(("arbitrary",)),
        name="mod_matmul",
    )(a, w, b.reshape(b.shape[0], 1, n))


def _normmod_mm_kernel(grp_ref, x_ref, g_ref, sh_ref, sc_ref, w_ref, o_ref, h_ref, *, hi):
    del grp_ref

    @pl.when(pl.program_id(1) == 0)
    def _():
        x = x_ref[...]
        y = x * lax.rsqrt(jnp.mean(x * x, axis=-1, keepdims=True) + NORM_EPS)
        y = y * g_ref[...]
        h_ref[...] = (y * (1.0 + sc_ref[0]) + sh_ref[0]).astype(h_ref.dtype)

    o_ref[...] = _dot(h_ref[...], w_ref[...], HI if hi else None).astype(o_ref.dtype)


def normmod_matmul(x, g, shift3, scale3, grp, w, tn, out_dtype, hi=False):
    m, d = x.shape
    n = w.shape[1]
    grid_spec = pltpu.PrefetchScalarGridSpec(
        num_scalar_prefetch=1,
        grid=(m // TM, n // tn),
        in_specs=[pl.BlockSpec((TM, d), lambda i, j, grp: (i, 0)),
                  pl.BlockSpec((1, d), lambda i, j, grp: (0, 0)),
                  pl.BlockSpec((1, 1, d), lambda i, j, grp: (grp[i], 0, 0)),
                  pl.BlockSpec((1, 1, d), lambda i, j, grp: (grp[i], 0, 0)),
                  pl.BlockSpec((d, tn), lambda i, j, grp: (0, j))],
        out_specs=pl.BlockSpec((TM, tn), lambda i, j, grp: (i, j)),
        scratch_shapes=[pltpu.VMEM((TM, d), F32 if hi else BF16)],
    )
    return pl.pallas_call(
        functools.partial(_normmod_mm_kernel, hi=hi),
        grid_spec=grid_spec,
        out_shape=jax.ShapeDtypeStruct((m, n), out_dtype),
        compiler_params=_cparams(("arbitrary", "arbitrary")),
        name="normmod_matmul",
    )(grp, x, g.reshape(1, d), shift3, scale3, w)


def _normmod_kernel(grp_ref, x_ref, g_ref, sh_ref, sc_ref, o_ref):
    del grp_ref
    x = x_ref[...]
    y = x * lax.rsqrt(jnp.mean(x * x, axis=-1, keepdims=True) + NORM_EPS)
    o_ref[...] = ((y * g_ref[...]) * (1.0 + sc_ref[0]) + sh_ref[0]).astype(o_ref.dtype)


def normmod(x, g, shift3, scale3, grp, out_dtype):
    m, d = x.shape
    grid_spec = pltpu.PrefetchScalarGridSpec(
        num_scalar_prefetch=1,
        grid=(m // TM,),
        in_specs=[pl.BlockSpec((TM, d), lambda i, grp: (i, 0)),
                  pl.BlockSpec((1, d), lambda i, grp: (0, 0)),
                  pl.BlockSpec((1, 1, d), lambda i, grp: (grp[i], 0, 0)),
                  pl.BlockSpec((1, 1, d), lambda i, grp: (grp[i], 0, 0))],
        out_specs=pl.BlockSpec((TM, d), lambda i, grp: (i, 0)),
    )
    return pl.pallas_call(
        _normmod_kernel,
        grid_spec=grid_spec,
        out_shape=jax.ShapeDtypeStruct((m, d), out_dtype),
        compiler_params=_cparams(("arbitrary",)),
        name="normmod",
    )(grp, x, g.reshape(1, d), shift3, scale3)


def _merge_kernel(oa_ref, ob_ref, oc_ref, ga_ref, gb_ref, gc_ref, pa_ref, pb_ref, pc_ref, o_ref):
    m = jax.nn.sigmoid(ga_ref[...]) * _dot(oa_ref[...], pa_ref[...])
    m = m + jax.nn.sigmoid(gb_ref[...]) * _dot(ob_ref[...], pb_ref[...])
    m = m + jax.nn.sigmoid(gc_ref[...]) * _dot(oc_ref[...], pc_ref[...])
    o_ref[...] = m.astype(o_ref.dtype)


def branch_merge(m, o_a, o_b, o_c, gates, pa, pb, pc, tn=512):
    k = o_a.shape[1]
    d = pa.shape[1]
    nj = d // tn
    o_spec = pl.BlockSpec((TM, k), lambda i, j: (i, 0))
    p_spec = pl.BlockSpec((k, tn), lambda i, j: (0, j))
    return pl.pallas_call(
        _merge_kernel,
        grid=(m // TM, nj),
        in_specs=[o_spec, o_spec, o_spec,
                  pl.BlockSpec((TM, tn), lambda i, j: (i, j)),
                  pl.BlockSpec((TM, tn), lambda i, j: (i, j + nj)),
                  pl.BlockSpec((TM, tn), lambda i, j: (i, j + 2 * nj)),
                  p_spec, p_spec, p_spec],
        out_specs=pl.BlockSpec((TM, tn), lambda i, j: (i, j)),
        out_shape=jax.ShapeDtypeStruct((m, d), BF16),
        compiler_params=_cparams(("arbitrary", "arbitrary")),
        name="branch_merge",
    )(o_a, o_b, o_c, gates, gates, gates, pa, pb, pc)


def _resid_mm_kernel(grp_ref, a_ref, w_ref, x_ref, gate_ref, o_ref):
    del grp_ref
    o_ref[...] = x_ref[...] + gate_ref[0] * _dot(a_ref[...], w_ref[...])


def resid_matmul(a, w, x, gate3, grp, tn=512):
    m, k = a.shape
    d = w.shape[1]
    grid_spec = pltpu.PrefetchScalarGridSpec(
        num_scalar_prefetch=1,
        grid=(m // TM, d // tn),
        in_specs=[pl.BlockSpec((TM, k), lambda i, j, grp: (i, 0)),
                  pl.BlockSpec((k, tn), lambda i, j, grp: (0, j)),
                  pl.BlockSpec((TM, tn), lambda i, j, grp: (i, j)),
                  pl.BlockSpec((1, 1, tn), lambda i, j, grp: (grp[i], 0, j))],
        out_specs=pl.BlockSpec((TM, tn), lambda i, j, grp: (i, j)),
    )
    return pl.pallas_call(
        _resid_mm_kernel,
        grid_spec=grid_spec,
        out_shape=jax.ShapeDtypeStruct((m, d), F32),
        compiler_params=_cparams(("arbitrary", "arbitrary")),
        name="resid_matmul",
    )(grp, a, w, x, gate3)


def _rmsnorm_kernel(x_ref, g_ref, o_ref):
    x = x_ref[...]
    y = x * lax.rsqrt(jnp.mean(x * x, axis=-1, keepdims=True) + NORM_EPS)
    o_ref[...] = y * g_ref[...]


def rmsnorm_rows(x, g):
    m, d = x.shape
    return pl.pallas_call(
        _rmsnorm_kernel,
        grid=(m // TM,),
        in_specs=[pl.BlockSpec((TM, d), lambda i: (i, 0)), pl.BlockSpec((1, d), lambda i: (0, 0))],
        out_specs=pl.BlockSpec((TM, d), lambda i: (i, 0)),
        out_shape=jax.ShapeDtypeStruct((m, d), F32),
        compiler_params=_cparams(("arbitrary",)),
        name="final_rmsnorm",
    )(x, g.reshape(1, d))


def _hyfilt_kernel(z_ref, w1_ref, b1_ref, w2_ref, b2_ref, fr_ref, w3_ref, dl_ref, h_ref, ss_ref):
    z = z_ref[...]
    hdn = jnp.sin(fr_ref[0:1, :] * (_dot(z, w1_ref[...], HI) + b1_ref[...]))
    hdn = jnp.sin(fr_ref[1:2, :] * (_dot(hdn, w2_ref[...], HI) + b2_ref[...]))
    h = _dot(hdn, w3_ref[...], HI)
    h = h * jnp.exp(-z[:, 0:1] * dl_ref[...])
    h_ref[...] = h

    @pl.when(pl.program_id(0) == 0)
    def _():
        ss_ref[...] = jnp.zeros_like(ss_ref)

    ss_ref[...] += jnp.sum(h * h, axis=0, keepdims=True)


def hyena_filters_raw(L, w1, b1, w2, b2, freq, w3):
    t = np.linspace(0.0, 1.0, L, dtype=np.float32)[:, None]
    omega = np.float32(2.0 * math.pi / L) * np.arange(L, dtype=np.float32)[:, None]
    bands = np.linspace(1e-4, HY_BANDS - 1, HY_BANDS, dtype=np.float32)[None, :]
    z = np.concatenate([t, np.cos(omega * bands), -np.sin(omega * bands),
                        np.zeros((L, HY_FILTER_ORDER - HY_EMB), np.float32)], axis=-1).astype(np.float32)
    w1p = jnp.concatenate([w1, jnp.zeros((HY_FILTER_ORDER - HY_EMB, HY_FILTER_ORDER), F32)], axis=0)
    deltas = np.abs(np.linspace(math.log(HY_DECAY_TARGET) / HY_SLOW_DECAY,
                                math.log(HY_DECAY_TARGET) / HY_FAST_DECAY, HY_WIDTH, dtype=np.float32))
    dl4 = np.tile(deltas, 4)[None, :]
    tl = min(L, 256)
    n = 4 * HY_WIDTH
    fo = HY_FILTER_ORDER
    full = lambda shape: pl.BlockSpec(shape, lambda i: (0, 0))
    return pl.pallas_call(
        _hyfilt_kernel,
        grid=(L // tl,),
        in_specs=[pl.BlockSpec((tl, fo), lambda i: (i, 0)), full((fo, fo)), full((1, fo)), full((fo, fo)),
                  full((1, fo)), full((2, fo)), full((fo, n)), full((1, n))],
        out_specs=[pl.BlockSpec((tl, n), lambda i: (i, 0)), full((1, n))],
        out_shape=[jax.ShapeDtypeStruct((L, n), F32), jax.ShapeDtypeStruct((1, n), F32)],
        compiler_params=_cparams(("arbitrary",)),
        name="hyena_filters",
    )(jnp.asarray(z), w1p, b1.reshape(1, fo), w2, b2.reshape(1, fo), freq, w3, jnp.asarray(dl4))


def _filter_scale(ss):
    s = ss.reshape(2, 2, HY_WIDTH)
    rs = lax.rsqrt(jnp.sum(s, axis=1, keepdims=True))
    return jnp.broadcast_to(rs, (2, 2, HY_WIDTH)).reshape(1, 4 * HY_WIDTH)


FFT_N1 = 128
FFT_N2 = 64


@functools.lru_cache(maxsize=None)
def _fft_tables():
    n1, n2 = FFT_N1, FFT_N2
    n = n1 * n2
    a = np.arange(n1 // 2)[None, None, :]
    k1 = np.arange(n1)[None, :, None]
    b = np.arange(n2)[:, None, None]
    theta = 2.0 * np.pi * ((a * k1 % n1) / n1 + (b * k1) / n)
    g = np.concatenate([np.cos(theta), -np.sin(theta)], axis=1)
    ig = np.concatenate([np.cos(theta), -np.sin(theta)], axis=1).transpose(0, 2, 1) / n
    k2 = np.arange(n2)[:, None]
    bb = np.arange(n2)[None, :]
    ph = 2.0 * np.pi * (k2 * bb % n2) / n2
    fr, fi = np.cos(ph), -np.sin(ph)
    f2 = np.block([[fr, -fi], [fi, fr]])
    if2 = np.block([[fr, fi], [-fi, fr]])
    return (jnp.asarray(g, BF16), jnp.asarray(f2, BF16), jnp.asarray(if2, BF16), jnp.asarray(ig, BF16))


def _fft_stage1(u_ref, g_ref, sr_ref, si_ref):
    n1, n2 = FFT_N1, FFT_N2

    def body(b, carry):
        xb = u_ref[pl.ds(b, n1 // 2, stride=n2), :].astype(BF16)
        a = _dot(g_ref[b], xb)
        sr_ref[pl.ds(b, n1, stride=n2), :] = a[:n1]
        si_ref[pl.ds(b, n1, stride=n2), :] = a[n1:]
        return carry

    lax.fori_loop(0, n2, body, 0, unroll=2)


def _hyconv_kernel(xm_ref, u_ref, hr_ref, hi_ref, bias_ref, g_ref, f2_ref, if2_ref, ig_ref, o_ref, sr_ref, si_ref):
    n1, n2 = FFT_N1, FFT_N2
    _fft_stage1(u_ref, g_ref, sr_ref, si_ref)

    def pair(ref, k):
        blk = ref[pl.ds(pl.multiple_of(k * 2 * n2, 2 * n2), 2 * n2), :]
        return jnp.concatenate([blk[:n2], blk[n2:]], axis=1)

    def unpair(ref, k, val):
        w = val.shape[1] // 2
        ref[pl.ds(pl.multiple_of(k * 2 * n2, 2 * n2), n2), :] = val[:, :w]
        ref[pl.ds(pl.multiple_of(k * 2 * n2 + n2, n2), n2), :] = val[:, w:]

    def stage2(k, carry):
        z = jnp.concatenate([pair(sr_ref, k), pair(si_ref, k)], axis=0).astype(BF16)
        x = _dot(f2_ref[...], z)
        xr, xi = x[:n2], x[n2:]
        hr, hi = pair(hr_ref, k), pair(hi_ref, k)
        y = jnp.concatenate([xr * hr - xi * hi, xr * hi + xi * hr], axis=0).astype(BF16)
        bb = _dot(if2_ref[...], y)
        unpair(sr_ref, k, bb[:n2])
        unpair(si_ref, k, bb[n2:])
        return carry

    lax.fori_loop(0, n1 // 2, stage2, 0, unroll=2)
    bias = bias_ref[...]

    def stage3(b, carry):
        st = jnp.concatenate([sr_ref[pl.ds(b, n1, stride=n2), :], si_ref[pl.ds(b, n1, stride=n2), :]], axis=0)
        yb = _dot(ig_ref[b], st.astype(BF16))
        rows = pl.ds(b, n1 // 2, stride=n2)
        ub = u_ref[rows, :]
        o_ref[rows, :] = xm_ref[rows, :] * (yb + ub * bias)
        return carry

    lax.fori_loop(0, n2, stage3, 0, unroll=2)


def hyena_conv(xm_arr, xm_col, u_arr, u_col, hr, hi, h_col, bias, n_batch, L):
    assert L == FFT_N1 * FFT_N2 // 2
    cb = LANES
    nct = HY_WIDTH // cb
    n = 2 * L
    g, f2, if2, ig = _fft_tables()
    const3 = lambda shape: pl.BlockSpec(shape, lambda b, c: (0, 0, 0))
    const2 = lambda shape: pl.BlockSpec(shape, lambda b, c: (0, 0))
    return pl.pallas_call(
        _hyconv_kernel,
        grid=(n_batch, nct),
        in_specs=[pl.BlockSpec((L, cb), lambda b, c: (b, xm_col * nct + c)),
                  pl.BlockSpec((L, cb), lambda b, c: (b, u_col * nct + c)),
                  pl.BlockSpec((n, cb), lambda b, c: (0, h_col * nct + c)),
                  pl.BlockSpec((n, cb), lambda b, c: (0, h_col * nct + c)),
                  pl.BlockSpec((1, cb), lambda b, c: (0, c)),
                  const3(g.shape), const2(f2.shape), const2(if2.shape), const3(ig.shape)],
        out_specs=pl.BlockSpec((L, cb), lambda b, c: (b, c)),
        out_shape=jax.ShapeDtypeStruct((n_batch * L, HY_WIDTH), F32),
        scratch_shapes=[pltpu.VMEM((n, cb), F32), pltpu.VMEM((n, cb), F32)],
        compiler_params=_cparams(("arbitrary", "arbitrary")),
        name="hyena_conv",
    )(xm_arr, u_arr, hr, hi, bias.reshape(1, HY_WIDTH), g, f2, if2, ig)


def _hyspec_kernel(h0_ref, h1_ref, rs_ref, g_ref, f2_ref, hr_ref, hi_ref, s0r, s0i, s1r, s1i):
    n1, n2 = FFT_N1, FFT_N2
    _fft_stage1(h0_ref, g_ref, s0r, s0i)
    _fft_stage1(h1_ref, g_ref, s1r, s1i)
    rs = rs_ref[...]
    h10 = h1_ref[0:1, :]

    def stage2(k1, carry):
        rows = pl.ds(pl.multiple_of(k1 * n2, n2), n2)
        z = jnp.concatenate([jnp.concatenate([s0r[rows, :], s1r[rows, :]], axis=1),
                             jnp.concatenate([s0i[rows, :], s1i[rows, :]], axis=1)], axis=0).astype(BF16)
        x = _dot(f2_ref[...], z)
        w = x.shape[1] // 2
        hr_ref[rows, :] = rs * (x[:n2, :w] + x[:n2, w:] - h10)
        hi_ref[rows, :] = rs * (x[n2:, :w] - x[n2:, w:])
        return carry

    lax.fori_loop(0, n1, stage2, 0, unroll=2)


def hyena_filter_spectrum(h_raw, rs, L):
    assert L == FFT_N1 * FFT_N2 // 2
    cb = LANES
    nct = HY_WIDTH // cb
    n = 2 * L
    g, f2, _, _ = _fft_tables()
    out_spec = pl.BlockSpec((n, cb), lambda o, c: (0, o * nct + c))
    scr = pltpu.VMEM((n, cb), F32)
    return pl.pallas_call(
        _hyspec_kernel,
        grid=(2, nct),
        in_specs=[pl.BlockSpec((L, cb), lambda o, c: (0, (2 * o) * nct + c)),
                  pl.BlockSpec((L, cb), lambda o, c: (0, (2 * o + 1) * nct + c)),
                  pl.BlockSpec((1, cb), lambda o, c: (0, (2 * o) * nct + c)),
                  pl.BlockSpec(g.shape, lambda o, c: (0, 0, 0)),
                  pl.BlockSpec(f2.shape, lambda o, c: (0, 0))],
        out_specs=[out_spec, out_spec],
        out_shape=[jax.ShapeDtypeStruct((n, 2 * HY_WIDTH), F32)] * 2,
        scratch_shapes=[scr, scr, scr, scr],
        compiler_params=_cparams(("arbitrary", "arbitrary")),
        name="hyena_filter_spectrum",
    )(h_raw, h_raw, rs, g, f2)


@functools.lru_cache(maxsize=None)
def _dense_dft_tables(L):
    n = 2 * L
    k = np.arange(n)[:, None]
    t = np.arange(L)[None, :]
    ph = 2.0 * np.pi * (k * t % n) / n
    fwd = np.concatenate([np.cos(ph), -np.sin(ph)], axis=0)
    inv = np.concatenate([np.cos(ph), -np.sin(ph)], axis=0).T / n
    return jnp.asarray(fwd, F32), jnp.asarray(inv, F32)


def _hyena_small_kernel(x1_ref, x2_ref, v_ref, h_ref0a, h_ref0b, h_ref1a, h_ref1b, rs0_ref, rs1_ref,
                        b0_ref, b1_ref, fwd_ref, inv_ref, o_ref, *, L):
    n = 2 * L
    fwd = fwd_ref[...]
    inv = inv_ref[...]

    def conv(u, ha_ref, hb_ref, rs_ref, bias_ref):
        ha, hb = ha_ref[...], hb_ref[...]
        ka = _dot(fwd, ha, HI)
        kb = _dot(fwd, hb, HI)
        rs = rs_ref[...]
        kr = rs * (ka[:n] + kb[:n] - hb[0:1, :])
        ki = rs * (ka[n:] - kb[n:])
        uf = _dot(fwd, u, HI)
        ur, ui = uf[:n], uf[n:]
        y = jnp.concatenate([ur * kr - ui * ki, ur * ki + ui * kr], axis=0)
        return _dot(inv, y, HI) + u * bias_ref[...]

    v = v_ref[...]
    z = x1_ref[...] * conv(v, h_ref0a, h_ref0b, rs0_ref, b0_ref)
    o_ref[...] = (x2_ref[...] * conv(z, h_ref1a, h_ref1b, rs1_ref, b1_ref)).astype(o_ref.dtype)


def hyena_small(u_arr, row0_blocks, n_batch, L, h_raw, rs, bias):
    cb = LANES
    nct = HY_WIDTH // cb
    fwd, inv = _dense_dft_tables(L)
    uspec = lambda col: pl.BlockSpec((L, cb), lambda b, c: (row0_blocks + b, col * nct + c))
    hspec = lambda col: pl.BlockSpec((L, cb), lambda b, c: (0, col * nct + c))
    rspec = lambda col: pl.BlockSpec((1, cb), lambda b, c: (0, col * nct + c))
    bspec = pl.BlockSpec((1, cb), lambda b, c: (0, c))
    bias_0 = bias[0].reshape(1, HY_WIDTH)
    bias_1 = bias[1].reshape(1, HY_WIDTH)
    return pl.pallas_call(
        functools.partial(_hyena_small_kernel, L=L),
        grid=(n_batch, nct),
        in_specs=[uspec(0), uspec(1), uspec(2), hspec(0), hspec(1), hspec(2), hspec(3), rspec(0), rspec(2),
                  bspec, bspec,
                  pl.BlockSpec(fwd.shape, lambda b, c: (0, 0)), pl.BlockSpec(inv.shape, lambda b, c: (0, 0))],
        out_specs=pl.BlockSpec((L, cb), lambda b, c: (b, c)),
        out_shape=jax.ShapeDtypeStruct((n_batch * L, HY_WIDTH), BF16),
        compiler_params=_cparams(("arbitrary", "arbitrary")),
        name="hyena_ctx",
    )(u_arr, u_arr, u_arr, h_raw, h_raw, h_raw, h_raw, rs, rs, bias_0, bias_1, fwd, inv)


def _na_bias_table(rpb):
    cols = np.arange(GRID_W)
    col_start = np.clip(cols - NA_COLS // 2, 0, GRID_W - NA_COLS)[:, None]
    in_win = (cols[None, :] >= col_start) & (cols[None, :] < col_start + NA_COLS)
    rel_col = np.clip(cols[None, :] - cols[:, None], 1 - NA_COLS, NA_COLS - 1) + NA_COLS - 1
    tbl = rpb.astype(F32)[:, :, rel_col]
    tbl = jnp.where(jnp.asarray(in_win)[None, None], tbl, -jnp.inf)
    return jnp.concatenate([tbl[:, :-1], tbl[:, 1:]], axis=-1)


def _na_kernel(*refs, n_rows):
    q_ref = refs[0]
    k_refs = refs[1:1 + NA_ROWS]
    v_refs = refs[1 + NA_ROWS:1 + 2 * NA_ROWS]
    kc_ref, vc_ref, tbl_ref, o_ref = refs[1 + 2 * NA_ROWS:]
    r = pl.program_id(1)
    start = jnp.clip(r - NA_ROWS // 2, 0, n_rows - NA_ROWS)
    d0 = start - r + NA_ROWS - 1
    dh = NA_HEAD_DIM
    q = q_ref[...] * (dh ** -0.5)
    for h in range(NA_HEADS):
        hs = slice(h * dh, (h + 1) * dh)
        qh = q[:, hs]
        s_tiles = []
        v_tiles = []
        for p in range(NA_ROWS // 2):
            kp = jnp.concatenate([k_refs[2 * p][:, hs], k_refs[2 * p + 1][:, hs]], axis=0)
            s_tiles.append(_dot_nt(qh, kp) + tbl_ref[h, d0 + 2 * p])
            v_tiles.append(jnp.concatenate([v_refs[2 * p][:, hs], v_refs[2 * p + 1][:, hs]], axis=0))
        s_tiles.append(_dot_nt(qh, kc_ref[:, hs]))
        v_tiles.append(vc_ref[:, hs])
        m = s_tiles[0].max(axis=-1, keepdims=True)
        for s in s_tiles[1:]:
            m = jnp.maximum(m, s.max(axis=-1, keepdims=True))
        l = jnp.zeros_like(m)
        acc = jnp.zeros((GRID_W, dh), F32)
        for s, vt in zip(s_tiles, v_tiles):
            p_ = jnp.exp(s - m)
            l = l + p_.sum(axis=-1, keepdims=True)
            acc = acc + _dot(p_.astype(BF16), vt)
        o_ref[:, hs] = (acc / l).astype(o_ref.dtype)


def na_latent(na, rpb, n_batch, L, Lc):
    n_rows = L // GRID_W
    assert n_rows >= NA_ROWS
    tbl = _na_bias_table(rpb)
    w = NA_WIDTH
    ctx_blk0 = n_batch * L // Lc

    def kv_spec(i, col):
        def imap(b, r):
            start = jnp.clip(r - NA_ROWS // 2, 0, n_rows - NA_ROWS)
            return (b * n_rows + start + i, col)
        return pl.BlockSpec((GRID_W, w), imap)

    in_specs = ([pl.BlockSpec((GRID_W, w), lambda b, r: (b * n_rows + r, 0))]
                + [kv_spec(i, 1) for i in range(NA_ROWS)] + [kv_spec(i, 2) for i in range(NA_ROWS)]
                + [pl.BlockSpec((Lc, w), lambda b, r: (ctx_blk0 + b, 1)),
                   pl.BlockSpec((Lc, w), lambda b, r: (ctx_blk0 + b, 2)),
                   pl.BlockSpec(tbl.shape, lambda b, r: (0, 0, 0, 0))])
    return pl.pallas_call(
        functools.partial(_na_kernel, n_rows=n_rows),
        grid=(n_batch, n_rows),
        in_specs=in_specs,
        out_specs=pl.BlockSpec((GRID_W, w), lambda b, r: (b * n_rows + r, 0)),
        out_shape=jax.ShapeDtypeStruct((n_batch * L, w), BF16),
        compiler_params=_cparams(("arbitrary", "arbitrary")),
        name="na_latent",
    )(*([na] * (3 + 2 * NA_ROWS)), tbl)


def _ctx_attn_kernel(q_ref, k_ref, v_ref, o_ref):
    dh = NA_HEAD_DIM
    q = q_ref[...] * (dh ** -0.5)
    for h in range(NA_HEADS):
        hs = slice(h * dh, (h + 1) * dh)
        s = _dot_nt(q[:, hs], k_ref[:, hs])
        p_ = jnp.exp(s - s.max(axis=-1, keepdims=True))
        acc = _dot(p_.astype(BF16), v_ref[:, hs])
        o_ref[:, hs] = (acc / p_.sum(axis=-1, keepdims=True)).astype(o_ref.dtype)


def ctx_attn(na, n_batch, L, Lc):
    w = NA_WIDTH
    blk0 = n_batch * L // Lc
    spec = lambda col: pl.BlockSpec((Lc, w), lambda b: (blk0 + b, col))
    return pl.pallas_call(
        _ctx_attn_kernel,
        grid=(n_batch,),
        in_specs=[spec(0), spec(1), spec(2)],
        out_specs=pl.BlockSpec((Lc, w), lambda b: (b, 0)),
        out_shape=jax.ShapeDtypeStruct((n_batch * Lc, w), BF16),
        compiler_params=_cparams(("arbitrary",)),
        name="ctx_attn",
    )(na, na, na)


@functools.lru_cache(maxsize=None)
def _wkv_masks():
    c, g = WKV_CHUNK, WKV_GROUP
    t = np.arange(c)[:, None]
    s = np.arange(c)[None, :]
    tinc = np.stack([(s <= t), (s >= t)]).astype(np.float32)
    strict = np.stack([(s < t), (s > t)]).astype(np.float32)
    tile = lambda m: np.tile(m, (1,) * (m.ndim - 1) + (g,))
    blk = lambda n: (t // n == s // n)
    blk16 = tile(blk(16).astype(np.float32))
    off32 = tile((blk(32) & ~blk(16)).astype(np.float32))
    off64 = tile((~blk(32)).astype(np.float32))
    eye = tile((t == s).astype(np.float32))
    rr = np.arange(g * c)
    hm = (rr[:, None] // c == np.arange(g * RW_HEAD_DIM)[None, :] // RW_HEAD_DIM).astype(np.float32)
    masks = tuple(jnp.asarray(m) for m in (tinc, tile(strict), tile(tinc), blk16, off32, off64, eye, hm))
    return masks + (jnp.asarray(hm, BF16),)


def _wkv_kernel(r_ref, v_ref, kk_ref, lw_ref, av_ref, kd_ref, tinc_ref, strict_ref, incl_ref, blk16_ref, off32_ref,
                off64_ref, eye_ref, hm_ref, hmb_ref, y_ref, state_ref):
    c, g = WKV_CHUNK, WKV_GROUP
    gw = g * RW_HEAD_DIM

    @pl.when(pl.program_id(2) == 0)
    def _():
        state_ref[...] = jnp.zeros_like(state_ref)

    lw = lw_ref[0]
    cum = _dot(tinc_ref[0], lw, HI)
    tot = jnp.sum(lw, axis=0, keepdims=True)
    e_in = jnp.exp(cum)
    e_ex = jnp.exp(cum - lw)
    e_neg = jnp.exp(-cum)
    e_rem = jnp.exp(tot - cum)
    e_tot = jnp.exp(tot)
    kk = kk_ref[...]
    b_vec = kk * av_ref[0]
    kd = kd_ref[0]
    at_all = -kk * e_ex
    rt_all = r_ref[...] * e_in
    bt_all = b_vec * e_neg
    kt_all = kd * e_neg
    bp_all = b_vec * e_rem
    kp_all = kd * e_rem
    v_all = v_ref[...]
    hm = hm_ref[...]
    hm_bf = hmb_ref[...]
    strict, incl = strict_ref[0], incl_ref[0]
    blk16, off32, off64, eye = blk16_ref[...], off32_ref[...], off64_ref[...], eye_ref[...]

    def bdiag(z):
        return jnp.concatenate([z.astype(BF16)] * g, axis=0) * hm_bf

    def pm(x4, zd):
        return _dot(x4.astype(BF16), zd)

    for gi in range(RW_HEADS // g):
        sl = slice(gi * gw, (gi + 1) * gw)
        ar = jnp.concatenate([at_all[:, sl], rt_all[:, sl]], axis=0).astype(BF16)
        pb = _dot_nt(ar, bdiag(bt_all[:, sl]))
        pk = _dot_nt(ar, bdiag(kt_all[:, sl]))
        a_ab, a_rb = pb[:c] * strict, pb[c:] * incl
        a_ak, a_rk = pk[:c] * strict, pk[c:] * incl
        ad = a_ab * blk16
        a2 = pm(ad, bdiag(ad))
        a4 = pm(a2, bdiag(a2))
        a8 = pm(a4, bdiag(a4))
        tinv = eye + ad
        tinv = tinv + pm(tinv, bdiag(a2))
        tinv = tinv + pm(tinv, bdiag(a4))
        tinv = tinv + pm(tinv, bdiag(a8))
        tinv = tinv + pm(pm(tinv, bdiag(a_ab * off32)), bdiag(tinv))
        tinv = tinv + pm(pm(tinv, bdiag(a_ab * off64)), bdiag(tinv))
        s0 = state_ref[gi]
        vv = v_all[:, sl]
        vd = bdiag(vv)
        ars = _dot_nt(ar, s0.astype(BF16))
        u = pm(tinv, bdiag(ars[:c] + pm(a_ak, vd)))
        y_ref[0, :, sl] = ars[c:] + pm(a_rb, bdiag(u)) + pm(a_rk, vd)
        uv = jnp.concatenate([u, vv], axis=0).astype(BF16)
        bk = jnp.concatenate([bp_all[:, sl], kp_all[:, sl]], axis=0).astype(BF16)
        state_ref[gi] = s0 * e_tot[:, sl] + hm * _dot_tn(uv, bk)


def wkv_scan(r, v, kk, lw, av, kd, n_batch, L, Lc):
    c = WKV_CHUNK
    rows, w = r.shape
    nc, nl = Lc // c, L // c
    masks = _wkv_masks()

    def blk(d, b, s):
        j_ctx = jnp.where(d == 0, s, nc - 1 - s)
        j_lat = jnp.where(d == 0, s - nc, nl - 1 - (s - nc))
        return jnp.where(s < nc, (n_batch * L + b * Lc) // c + j_ctx, (b * L) // c + j_lat)

    shared = pl.BlockSpec((c, w), lambda d, b, s: (blk(d, b, s), 0))
    perdir = pl.BlockSpec((1, c, w), lambda d, b, s: (d, blk(d, b, s), 0))
    dmask = lambda m: pl.BlockSpec((1,) + m.shape[1:], lambda d, b, s: (d, 0, 0))
    cmask = lambda m: pl.BlockSpec(m.shape, lambda d, b, s: (0, 0))
    tinc, strict, incl, blk16, off32, off64, eye, hm, hm_bf = masks
    gw = WKV_GROUP * RW_HEAD_DIM
    return pl.pallas_call(
        _wkv_kernel,
        grid=(2, n_batch, nc + nl),
        in_specs=[shared, shared, shared, perdir, perdir, perdir, dmask(tinc), dmask(strict), dmask(incl),
                  cmask(blk16), cmask(off32), cmask(off64), cmask(eye), cmask(hm), cmask(hm_bf)],
        out_specs=perdir,
        out_shape=jax.ShapeDtypeStruct((2, rows, w), F32),
        scratch_shapes=[pltpu.VMEM((RW_HEADS // WKV_GROUP, gw, gw), F32)],
        compiler_params=_cparams(("arbitrary", "arbitrary", "arbitrary")),
        name="wkv_scan",
    )(r, v, kk, lw, av, kd, *masks)


def _moe_kernel(be_ref, nb_ref, xa_ref, xb_ref, wg_ref, wu_ref, wd_ref, o_ref, wg_s, wu_s, wd_s, *, nb_a):
    i = pl.program_id(0)
    prev = be_ref[jnp.maximum(i - 1, 0)]

    @pl.when((i == 0) | (be_ref[i] != prev))
    def _():
        wg_s[...] = wg_ref[0, 0].astype(BF16)
        wu_s[...] = wu_ref[0, 0].astype(BF16)
        wd_s[...] = wd_ref[0, 0].astype(BF16)

    @pl.when(i < nb_ref[0])
    def _():
        x = jnp.where(i < nb_a, xa_ref[...], xb_ref[...])
        hmid = (jax.nn.silu(_dot(x, wg_s[...])) * _dot(x, wu_s[...])).astype(BF16)
        o_ref[...] = _dot(hmid, wd_s[...]).astype(o_ref.dtype)

    @pl.when(i >= nb_ref[0])
    def _():
        o_ref[...] = jnp.zeros_like(o_ref)


def grouped_swiglu(xa, xb, block_e, n_used, w_gate, w_up, w_down, layer):
    d = xa.shape[1]
    ff = w_gate.shape[-1]
    nb_a = xa.shape[0] // MOE_BLOCK
    nb = nb_a + (0 if xb is None else xb.shape[0] // MOE_BLOCK)
    xb = xa if xb is None else xb
    grid_spec = pltpu.PrefetchScalarGridSpec(
        num_scalar_prefetch=2,
        grid=(nb,),
        in_specs=[pl.BlockSpec((MOE_BLOCK, d), lambda i, be, nu: (jnp.minimum(i, nb_a - 1), 0)),
                  pl.BlockSpec((MOE_BLOCK, d), lambda i, be, nu: (jnp.maximum(i - nb_a, 0), 0)),
                  pl.BlockSpec((1, 1, d, ff), lambda i, be, nu: (layer, be[i], 0, 0)),
                  pl.BlockSpec((1, 1, d, ff), lambda i, be, nu: (layer, be[i], 0, 0)),
                  pl.BlockSpec((1, 1, ff, d), lambda i, be, nu: (layer, be[i], 0, 0))],
        out_specs=pl.BlockSpec((MOE_BLOCK, d), lambda i, be, nu: (i, 0)),
        scratch_shapes=[pltpu.VMEM((d, ff), BF16), pltpu.VMEM((d, ff), BF16), pltpu.VMEM((ff, d), BF16)],
    )
    return pl.pallas_call(
        functools.partial(_moe_kernel, nb_a=nb_a),
        grid_spec=grid_spec,
        out_shape=jax.ShapeDtypeStruct((nb * MOE_BLOCK, d), BF16),
        compiler_params=_cparams(("arbitrary",)),
        name="grouped_swiglu",
    )(block_e, n_used, xa, xb, w_gate, w_up, w_down)


def _combine_kernel(grp_ref, x_ref, y_ref, w_ref, s_ref, gate_ref, o_ref):
    del grp_ref
    f = s_ref[...].astype(F32)
    for k in range(TOP_K):
        f = f + w_ref[:, k:k + 1] * y_ref[k].astype(F32)
    o_ref[...] = x_ref[...] + gate_ref[0] * f


def moe_combine(x, yg, e_w, shared, gate3, grp):
    m, d = x.shape
    tm = TM // 2
    grid_spec = pltpu.PrefetchScalarGridSpec(
        num_scalar_prefetch=1,
        grid=(m // tm,),
        in_specs=[pl.BlockSpec((tm, d), lambda i, grp: (i, 0)),
                  pl.BlockSpec((TOP_K, tm, d), lambda i, grp: (0, i, 0)),
                  pl.BlockSpec((tm, e_w.shape[1]), lambda i, grp: (i, 0)),
                  pl.BlockSpec((tm, d), lambda i, grp: (i, 0)),
                  pl.BlockSpec((1, 1, d), lambda i, grp: (grp[i // 2], 0, 0))],
        out_specs=pl.BlockSpec((tm, d), lambda i, grp: (i, 0)),
    )
    return pl.pallas_call(
        _combine_kernel,
        grid_spec=grid_spec,
        out_shape=jax.ShapeDtypeStruct((m, d), F32),
        compiler_params=_cparams(("arbitrary",)),
        name="moe_combine",
    )(grp, x, yg, e_w, shared, gate3)


def _route_kernel(grp_ref, x_ref, g_ref, sh_ref, sc_ref, wt_ref, rb_ref, tri_ref, ones_ref,
                  h_ref, idx_ref, w_ref, rank_ref, cnt_ref, carry_ref):
    del grp_ref
    tm = x_ref.shape[0]
    gs = N_EXPERTS // N_GROUPS
    neg = -jnp.inf

    @pl.when(pl.program_id(0) == 0)
    def _():
        carry_ref[...] = jnp.zeros_like(carry_ref)

    x = x_ref[...]
    y = x * lax.rsqrt(jnp.mean(x * x, axis=-1, keepdims=True) + NORM_EPS)
    h = (y * g_ref[...]) * (1.0 + sc_ref[0]) + sh_ref[0]
    h_ref[...] = h.astype(h_ref.dtype)
    scores = jax.nn.sigmoid(_dot_nt(wt_ref[...], h, HI))
    biased = scores + rb_ref[...]

    def first_argmax(v, iota, n):
        m = jnp.max(v, axis=0, keepdims=True)
        return m, jnp.min(jnp.where(v == m, iota, float(n)), axis=0, keepdims=True)

    def stack_rows(rows):
        iota8 = lax.broadcasted_iota(jnp.int32, (8, tm), 0)
        out = jnp.zeros((8, tm), F32)
        for k, row in enumerate(rows):
            out = jnp.where(iota8 == k, row, out)
        return out

    assert gs == 8 and N_GROUPS == 8
    iota_g = lax.broadcasted_iota(jnp.int32, (gs, tm), 0).astype(F32)
    g_rows = []
    for g in range(N_GROUPS):
        bg = biased[g * gs:(g + 1) * gs]
        m1, i1 = first_argmax(bg, iota_g, gs)
        m2 = jnp.max(jnp.where(iota_g == i1, neg, bg), axis=0, keepdims=True)
        g_rows.append(m1 + m2)
    g_score = stack_rows(g_rows)
    g_sel = jnp.zeros((N_GROUPS, tm), F32)
    for _ in range(TOPK_GROUPS):
        _, ig = first_argmax(g_score, iota_g, N_GROUPS)
        hit = iota_g == ig
        g_sel = jnp.where(hit, 1.0, g_sel)
        g_score = jnp.where(hit, neg, g_score)
    e_sel = jnp.concatenate([jnp.broadcast_to(g_sel[g:g + 1], (gs, tm)) for g in range(N_GROUPS)], axis=0)
    masked = jnp.where(e_sel > 0.0, biased, neg)

    iota_e = lax.broadcasted_iota(jnp.int32, (N_EXPERTS, tm), 0).astype(F32)
    chosen = jnp.zeros((N_EXPERTS, tm), F32)
    hits, idx_rows, w_rows = [], [], []
    for _ in range(TOP_K):
        _, ie = first_argmax(masked, iota_e, N_EXPERTS)
        hit = iota_e == ie
        hits.append(hit)
        idx_rows.append(ie)
        w_rows.append(jnp.sum(jnp.where(hit, scores, 0.0), axis=0, keepdims=True))
        chosen = jnp.where(hit, 1.0, chosen)
        masked = jnp.where(hit, neg, masked)
    w_sum = w_rows[0]
    for wk in w_rows[1:]:
        w_sum = w_sum + wk
    idx_ref[...] = stack_rows(idx_rows).astype(jnp.int32)
    w_ref[...] = stack_rows([wk / w_sum * ROUTE_SCALE for wk in w_rows])

    chosen_b = chosen.astype(BF16)
    before = carry_ref[...] + _dot(chosen_b, tri_ref[...])
    rank_ref[...] = stack_rows([jnp.sum(jnp.where(hit, before, 0.0), axis=0, keepdims=True)
                                for hit in hits]).astype(jnp.int32)
    carry_ref[...] += _dot(chosen_b, ones_ref[...])
    cnt_ref[...] = carry_ref[:, :LANES].astype(jnp.int32)


def route(x, g, shift3, scale3, grp, router_w, router_b):
    m, d = x.shape
    ne = N_EXPERTS
    tri = jnp.asarray(np.triu(np.ones((TM, TM), np.float32), 1), BF16)
    ones = jnp.ones((TM, TM), BF16)
    rb = jnp.broadcast_to(router_b.astype(F32)[:, None], (ne, TM))
    row = lambda r: pl.BlockSpec((r, TM), lambda i, grp: (0, i))
    const = lambda shape: pl.BlockSpec(shape, lambda i, grp: (0, 0))
    grid_spec = pltpu.PrefetchScalarGridSpec(
        num_scalar_prefetch=1,
        grid=(m // TM,),
        in_specs=[pl.BlockSpec((TM, d), lambda i, grp: (i, 0)),
                  const((1, d)),
                  pl.BlockSpec((1, 1, d), lambda i, grp: (grp[i], 0, 0)),
                  pl.BlockSpec((1, 1, d), lambda i, grp: (grp[i], 0, 0)),
                  const((ne, d)), const((ne, TM)), const((TM, TM)), const((TM, TM))],
        out_specs=[pl.BlockSpec((TM, d), lambda i, grp: (i, 0)), row(8), row(8), row(8), const((ne, LANES))],
        scratch_shapes=[pltpu.VMEM((ne, TM), F32)],
    )
    return pl.pallas_call(
        _route_kernel,
        grid_spec=grid_spec,
        out_shape=[jax.ShapeDtypeStruct((m, d), BF16), jax.ShapeDtypeStruct((8, m), jnp.int32),
                   jax.ShapeDtypeStruct((8, m), F32), jax.ShapeDtypeStruct((8, m), jnp.int32),
                   jax.ShapeDtypeStruct((ne, LANES), jnp.int32)],
        compiler_params=_cparams(("arbitrary",)),
        name="route",
    )(grp, x, g.reshape(1, d), shift3, scale3, router_w.T, rb, tri, ones)


def _seq_edge_masks(n_batch, L, Lc):
    n_lat = n_batch * L
    starts = np.concatenate([np.arange(n_batch) * L, n_lat + np.arange(n_batch) * Lc])
    ends = np.concatenate([(np.arange(n_batch) + 1) * L, n_lat + (np.arange(n_batch) + 1) * Lc]) - 1
    first = np.ones((n_batch * (L + Lc), 1), np.float32)
    last = first.copy()
    first[starts] = 0.0
    last[ends] = 0.0
    return jnp.asarray(first), jnp.asarray(last)


def _short_conv(u, w, first, last):
    assert SHORT_CONV == 3
    zero = jnp.zeros_like(u[:1])
    prev = jnp.concatenate([zero, u[:-1]], axis=0) * first
    nxt = jnp.concatenate([u[1:], zero], axis=0) * last
    return prev * w[0] + u * w[1] + nxt * w[2]


def _rwkv_prep(cols, first, last, shift_w, w0, w2, a0, a2, g2, k_k, k_a):
    u = _short_conv(cols, shift_w, first, last)
    n = u.shape[0]
    r, k, v, wlo, alo, glo = jnp.split(u, RW_SPLITS, axis=-1)
    lw, av, kd = [], [], []
    for d in range(2):
        wl = wlo[:, d * RW_DECAY_LORA:(d + 1) * RW_DECAY_LORA]
        al = alo[:, d * RW_AAA_LORA:(d + 1) * RW_AAA_LORA]
        w = -jax.nn.softplus(-(w0[d] + jnp.tanh(wl) @ w2[d])) - 0.5
        lw.append(-jnp.exp(w))
        a = jax.nn.sigmoid(a0[d] + al @ a2[d])
        av.append(a)
        kd.append(k * (1.0 + (a - 1.0) * k_a))
    g = jax.nn.sigmoid(glo) @ g2
    kk = (k * k_k).reshape(n, RW_HEADS, RW_HEAD_DIM)
    kk = kk / jnp.maximum(jnp.sqrt(jnp.sum(kk * kk, axis=-1, keepdims=True)), 1e-12)
    return r, v, kk.reshape(n, RW_WIDTH), jnp.stack(lw), jnp.stack(av), jnp.stack(kd), g


def _rwkv_out(y, r, kd, v, g, r_k, ln_w, ln_b):
    n = y.shape[0]
    hh = lambda t: t.reshape(t.shape[:-1] + (RW_HEADS, RW_HEAD_DIM))
    yh = hh(y)
    mu = jnp.mean(yh, axis=-1, keepdims=True)
    var = jnp.mean(jnp.square(yh - mu), axis=-1, keepdims=True)
    yh = (yh - mu) * lax.rsqrt(var + RW_GN_EPS)
    rk = jnp.sum(hh(r)[None] * hh(kd) * r_k, axis=-1, keepdims=True)
    bonus = jnp.sum(rk * hh(v)[None], axis=0)
    out = yh.reshape(n, RW_WIDTH) * ln_w + ln_b + bonus.reshape(n, RW_WIDTH)
    return out * g


def _moe(h, e_idx, rank, counts, exp_gate, exp_up, exp_down, sh_gate, sh_up, sh_down, layer):
    T, D = h.shape
    n = T * TOP_K
    padded = (counts + MOE_BLOCK - 1) // MOE_BLOCK * MOE_BLOCK
    pad_end = jnp.cumsum(padded)
    pad_start = pad_end - padded
    experts = jnp.arange(N_EXPERTS, dtype=jnp.int32)
    dest = rank + jnp.sum(jnp.where(e_idx[:, :, None] == experts, pad_start.astype(jnp.int32), 0), axis=-1)
    n_blocks = -(-n // MOE_BLOCK) + N_EXPERTS
    n_blocks += n_blocks % 2
    n_slots = n_blocks * MOE_BLOCK
    flat_dest = dest.reshape(-1)
    tok = jnp.tile(jnp.arange(T, dtype=jnp.int32), TOP_K)
    slot_tok = jnp.zeros((n_slots,), jnp.int32).at[flat_dest].set(tok)
    block_start = jnp.arange(n_blocks, dtype=jnp.int32) * MOE_BLOCK
    block_e = jnp.minimum(jnp.sum(block_start[:, None] >= pad_end[None, :], axis=1), N_EXPERTS - 1).astype(jnp.int32)
    n_used = (pad_end[-1] // MOE_BLOCK).astype(jnp.int32).reshape(1)
    half = n_slots // 2
    xa = jnp.take(h, slot_tok[:half], axis=0, mode="clip")
    xb = jnp.take(h, slot_tok[half:], axis=0, mode="clip")
    y = grouped_swiglu(xa, xb, block_e, n_used, exp_gate, exp_up, exp_down, layer)
    yg = jnp.take(y, flat_dest, axis=0, mode="clip").reshape(TOP_K, T, D)
    nb_sh = T // MOE_BLOCK
    sh4 = lambda w: w.reshape((w.shape[0], 1) + w.shape[1:])
    shared = grouped_swiglu(h, None, jnp.zeros((nb_sh,), jnp.int32), jnp.full((1,), nb_sh, jnp.int32),
                            sh4(sh_gate), sh4(sh_up), sh4(sh_down), layer)
    return yg, shared


def kernel(x, c, ctx, c_ctx, mod_w, mod_b, norm1_g, norm2_g, w_in, hy_conv, hy_w1, hy_b1, hy_w2, hy_b2, hy_freq,
           hy_w3, hy_bias, na_rpb, rw_shift, rw_w0, rw_w2, rw_a0, rw_a2, rw_g2, rw_kk, rw_ka, rw_rk, rw_ln_w,
           rw_ln_b, proj_a, proj_b, proj_c, w_out, router_w, router_b, exp_gate, exp_up, exp_down, sh_gate, sh_up,
           sh_down, final_g):
    B, L, D = x.shape
    Lc = ctx.shape[1]
    depth = mod_w.shape[0]
    n_lat, n_ctx = B * L, B * Lc
    assert L % TM == 0 and n_ctx % TM == 0 and L % WKV_CHUNK == 0 and Lc % WKV_CHUNK == 0
    col_hy = 3 * HY_WIDTH
    col_na = col_hy + 3 * NA_WIDTH
    col_rw = col_na + RW_COLS

    xs = jnp.concatenate([x.reshape(n_lat, D), ctx.reshape(n_ctx, D)], axis=0)
    grp_all = jnp.asarray(np.concatenate([np.repeat(np.arange(B), L // TM), np.full(n_ctx // TM, B)]), jnp.int32)
    s8 = jnp.zeros((8, D), F32).at[:B].set(jax.nn.silu(c)).at[B].set(jax.nn.silu(c_ctx))
    first, last = _seq_edge_masks(B, L, Lc)

    for i in range(depth):
        with_ctx = i < depth - 1
        mod = small_matmul_bias(s8, mod_w, mod_b, i)[:B + 1].reshape(B + 1, 1, N_MOD * D)
        sh1, sc1, g1, sh2, sc2, g2 = (mod[:, :, j * D:(j + 1) * D] for j in range(N_MOD))
        w_bf = w_in[i].astype(BF16)
        proj = functools.partial(normmod_matmul, xs, norm1_g[i], sh1, sc1, grp_all)
        hy = _short_conv(proj(w_bf[:, :col_hy], 512, F32), hy_conv[i], first, last)
        na = proj(w_bf[:, col_hy:col_na], 512, BF16)
        rw = proj(w_bf[:, col_na:col_rw], 384, F32)
        gates = proj(w_bf[:, col_rw:], 512, F32)

        hy_args = (hy_w1[i], hy_b1[i], hy_w2[i], hy_b2[i], hy_freq[i], hy_w3[i])
        h_raw, ss = hyena_filters_raw(L, *hy_args)
        hr, hi = hyena_filter_spectrum(h_raw, _filter_scale(ss), L)
        z = hyena_conv(hy, 0, hy, 2, hr, hi, 0, hy_bias[i][0], B, L)
        o_a = hyena_conv(hy, 1, z, 0, hr, hi, 1, hy_bias[i][1], B, L).astype(BF16)
        o_b = na_latent(na, na_rpb[i], B, L, Lc)
        r_, v_, kk_, lw_, av_, kd_, gg_ = _rwkv_prep(rw, first, last, rw_shift[i], rw_w0[i], rw_w2[i], rw_a0[i],
                                                     rw_a2[i], rw_g2[i], rw_kk[i], rw_ka[i])
        y = wkv_scan(r_, v_, kk_, lw_, av_, kd_, B, L, Lc)
        o_c = _rwkv_out(y[0] + y[1], r_, kd_, v_, gg_, rw_rk[i], rw_ln_w[i], rw_ln_b[i]).astype(BF16)

        if with_ctx:
            h_raw_c, ss_c = hyena_filters_raw(Lc, *hy_args)
            o_a_c = hyena_small(hy, n_lat // Lc, B, Lc, h_raw_c, _filter_scale(ss_c), hy_bias[i])
            o_a = jnp.concatenate([o_a, o_a_c], axis=0)
            o_b = jnp.concatenate([o_b, ctx_attn(na, B, L, Lc)], axis=0)
            m_rows = n_lat + n_ctx
        else:
            m_rows = n_lat
        grp = grp_all[:m_rows // TM]
        merged = branch_merge(m_rows, o_a, o_b, o_c, gates, proj_a[i].astype(BF16), proj_b[i].astype(BF16),
                              proj_c[i].astype(BF16))
        xs = resid_matmul(merged, w_out[i].astype(BF16), xs, g1, grp)

        h2, e_idx, e_w, rank, counts = route(xs, norm2_g[i], sh2, sc2, grp, router_w[i], router_b[i])
        yg, shared = _moe(h2, e_idx[:TOP_K], rank[:TOP_K], counts[:, 0], exp_gate, exp_up, exp_down, sh_gate, sh_up,
                          sh_down, i)
        xs = moe_combine(xs, yg, e_w.T, shared, g2, grp)

    return rmsnorm_rows(xs, final_g).reshape(B, L, D)
```

```python
import functools
import math

import jax
import jax.numpy as jnp
import numpy as np
from jax import lax
from jax.experimental import pallas as pl
from jax.experimental.pallas import tpu as pltpu

F32 = jnp.float32
BF16 = jnp.bfloat16
HI = lax.Precision.HIGHEST

GRID_W = 64
NORM_EPS = 1e-6
N_MOD = 6
SHORT_CONV = 3
HY_WIDTH = 1024
HY_BANDS = 16
HY_EMB = 2 * HY_BANDS + 1
HY_FILTER_ORDER = 64
HY_FAST_DECAY = 0.3
HY_SLOW_DECAY = 1.5
HY_DECAY_TARGET = 1e-2
NA_HEADS = 16
NA_HEAD_DIM = 64
NA_WIDTH = NA_HEADS * NA_HEAD_DIM
NA_ROWS = 8
NA_COLS = 16
RW_HEADS = 16
RW_HEAD_DIM = 64
RW_WIDTH = RW_HEADS * RW_HEAD_DIM
RW_DECAY_LORA = 64
RW_AAA_LORA = 64
RW_GATE_LORA = 128
RW_GN_EPS = 64e-5
RW_COLS = 3 * RW_WIDTH + 2 * RW_DECAY_LORA + 2 * RW_AAA_LORA + RW_GATE_LORA
RW_SPLITS = [RW_WIDTH, 2 * RW_WIDTH, 3 * RW_WIDTH, 3 * RW_WIDTH + 2 * RW_DECAY_LORA,
             3 * RW_WIDTH + 2 * RW_DECAY_LORA + 2 * RW_AAA_LORA]
N_BRANCH = 3
N_EXPERTS = 64
TOP_K = 6
N_GROUPS = 8
TOPK_GROUPS = 4
ROUTE_SCALE = 2.5

LANES = 128
VMEM_LIMIT = 56 * 1024 * 1024
TM = 512
MOE_BLOCK = 256
WKV_CHUNK = 64
WKV_GROUP = 4


def _cparams(sem):
    return pltpu.CompilerParams(dimension_semantics=sem, vmem_limit_bytes=VMEM_LIMIT)


def _dot(a, b, prec=None):
    return jnp.dot(a, b, preferred_element_type=F32, precision=prec)


def _dot_nt(a, b, prec=None):
    return lax.dot_general(a, b, (((1,), (1,)), ((), ())), preferred_element_type=F32, precision=prec)


def _dot_tn(a, b, prec=None):
    return lax.dot_general(a, b, (((0,), (0,)), ((), ())), preferred_element_type=F32, precision=prec)


def _small_mm_kernel(a_ref, w_ref, b_ref, o_ref):
    o_ref[...] = _dot(a_ref[...], w_ref[0], HI) + b_ref[0]


def small_matmul_bias(a, w, b, layer, tn=1536):
    m, k = a.shape
    n = w.shape[2]
    return pl.pallas_call(
        _small_mm_kernel,
        grid=(n // tn,),
        in_specs=[pl.BlockSpec((m, k), lambda j: (0, 0)),
                  pl.BlockSpec((1, k, tn), lambda j: (layer, 0, j)),
                  pl.BlockSpec((1, 1, tn), lambda j: (layer, 0, j))],
        out_specs=pl.BlockSpec((m, tn), lambda j: (0, j)),
        out_shape=jax.ShapeDtypeStruct((m, n), F32),
        compiler_params=_cparams(("arbitrary",)),
        name="mod_matmul",
    )(a, w, b.reshape(b.shape[0], 1, n))


def _normmod_mm_kernel(grp_ref, x_ref, g_ref, sh_ref, sc_ref, w_ref, o_ref, h_ref, *, hi):
    del grp_ref

    @pl.when(pl.program_id(1) == 0)
    def _():
        x = x_ref[...]
        y = x * lax.rsqrt(jnp.mean(x * x, axis=-1, keepdims=True) + NORM_EPS)
        y = y * g_ref[...]
        h_ref[...] = (y * (1.0 + sc_ref[0]) + sh_ref[0]).astype(h_ref.dtype)

    o_ref[...] = _dot(h_ref[...], w_ref[...], HI if hi else None).astype(o_ref.dtype)


def normmod_matmul(x, g, shift3, scale3, grp, w, tn, out_dtype, hi=False):
    m, d = x.shape
    n = w.shape[1]
    grid_spec = pltpu.PrefetchScalarGridSpec(
        num_scalar_prefetch=1,
        grid=(m // TM, n // tn),
        in_specs=[pl.BlockSpec((TM, d), lambda i, j, grp: (i, 0)),
                  pl.BlockSpec((1, d), lambda i, j, grp: (0, 0)),
                  pl.BlockSpec((1, 1, d), lambda i, j, grp: (grp[i], 0, 0)),
                  pl.BlockSpec((1, 1, d), lambda i, j, grp: (grp[i], 0, 0)),
                  pl.BlockSpec((d, tn), lambda i, j, grp: (0, j))],
        out_specs=pl.BlockSpec((TM, tn), lambda i, j, grp: (i, j)),
        scratch_shapes=[pltpu.VMEM((TM, d), F32 if hi else BF16)],
    )
    return pl.pallas_call(
        functools.partial(_normmod_mm_kernel, hi=hi),
        grid_spec=grid_spec,
        out_shape=jax.ShapeDtypeStruct((m, n), out_dtype),
        compiler_params=_cparams(("arbitrary", "arbitrary")),
        name="normmod_matmul",
    )(grp, x, g.reshape(1, d), shift3, scale3, w)


def _normmod_kernel(grp_ref, x_ref, g_ref, sh_ref, sc_ref, o_ref):
    del grp_ref
    x = x_ref[...]
    y = x * lax.rsqrt(jnp.mean(x * x, axis=-1, keepdims=True) + NORM_EPS)
    o_ref[...] = ((y * g_ref[...]) * (1.0 + sc_ref[0]) + sh_ref[0]).astype(o_ref.dtype)


def normmod(x, g, shift3, scale3, grp, out_dtype):
    m, d = x.shape
    grid_spec = pltpu.PrefetchScalarGridSpec(
        num_scalar_prefetch=1,
        grid=(m // TM,),
        in_specs=[pl.BlockSpec((TM, d), lambda i, grp: (i, 0)),
                  pl.BlockSpec((1, d), lambda i, grp: (0, 0)),
                  pl.BlockSpec((1, 1, d), lambda i, grp: (grp[i], 0, 0)),
                  pl.BlockSpec((1, 1, d), lambda i, grp: (grp[i], 0, 0))],
        out_specs=pl.BlockSpec((TM, d), lambda i, grp: (i, 0)),
    )
    return pl.pallas_call(
        _normmod_kernel,
        grid_spec=grid_spec,
        out_shape=jax.ShapeDtypeStruct((m, d), out_dtype),
        compiler_params=_cparams(("arbitrary",)),
        name="normmod",
    )(grp, x, g.reshape(1, d), shift3, scale3)


def _merge_kernel(oa_ref, ob_ref, oc_ref, ga_ref, gb_ref, gc_ref, pa_ref, pb_ref, pc_ref, o_ref):
    m = jax.nn.sigmoid(ga_ref[...]) * _dot(oa_ref[...], pa_ref[...])
    m = m + jax.nn.sigmoid(gb_ref[...]) * _dot(ob_ref[...], pb_ref[...])
    m = m + jax.nn.sigmoid(gc_ref[...]) * _dot(oc_ref[...], pc_ref[...])
    o_ref[...] = m.astype(o_ref.dtype)


def branch_merge(m, o_a, o_b, o_c, gates, pa, pb, pc, tn=512):
    k = o_a.shape[1]
    d = pa.shape[1]
    nj = d // tn
    o_spec = pl.BlockSpec((TM, k), lambda i, j: (i, 0))
    p_spec = pl.BlockSpec((k, tn), lambda i, j: (0, j))
    return pl.pallas_call(
        _merge_kernel,
        grid=(m // TM, nj),
        in_specs=[o_spec, o_spec, o_spec,
                  pl.BlockSpec((TM, tn), lambda i, j: (i, j)),
                  pl.BlockSpec((TM, tn), lambda i, j: (i, j + nj)),
                  pl.BlockSpec((TM, tn), lambda i, j: (i, j + 2 * nj)),
                  p_spec, p_spec, p_spec],
        out_specs=pl.BlockSpec((TM, tn), lambda i, j: (i, j)),
        out_shape=jax.ShapeDtypeStruct((m, d), BF16),
        compiler_params=_cparams(("arbitrary", "arbitrary")),
        name="branch_merge",
    )(o_a, o_b, o_c, gates, gates, gates, pa, pb, pc)


def _resid_mm_kernel(grp_ref, a_ref, w_ref, x_ref, gate_ref, o_ref):
    del grp_ref
    o_ref[...] = x_ref[...] + gate_ref[0] * _dot(a_ref[...], w_ref[...])


def resid_matmul(a, w, x, gate3, grp, tn=512):
    m, k = a.shape
    d = w.shape[1]
    grid_spec = pltpu.PrefetchScalarGridSpec(
        num_scalar_prefetch=1,
        grid=(m // TM, d // tn),
        in_specs=[pl.BlockSpec((TM, k), lambda i, j, grp: (i, 0)),
                  pl.BlockSpec((k, tn), lambda i, j, grp: (0, j)),
                  pl.BlockSpec((TM, tn), lambda i, j, grp: (i, j)),
                  pl.BlockSpec((1, 1, tn), lambda i, j, grp: (grp[i], 0, j))],
        out_specs=pl.BlockSpec((TM, tn), lambda i, j, grp: (i, j)),
    )
    return pl.pallas_call(
        _resid_mm_kernel,
        grid_spec=grid_spec,
        out_shape=jax.ShapeDtypeStruct((m, d), F32),
        compiler_params=_cparams(("arbitrary", "arbitrary")),
        name="resid_matmul",
    )(grp, a, w, x, gate3)


def _short_conv_kernel(x_ref, prev_ref, next_ref, w_ref, first_ref, last_ref, o_ref):
    x = x_ref[...]
    tm = x.shape[0]
    row = lax.broadcasted_iota(jnp.int32, x.shape, 0)
    halo = prev_ref.shape[0]
    prev = jnp.where(row == 0, prev_ref[halo - 1:halo, :], pltpu.roll(x, 1, 0)) * first_ref[...]
    nxt = jnp.where(row == tm - 1, next_ref[0:1, :], pltpu.roll(x, tm - 1, 0)) * last_ref[...]
    o_ref[...] = prev * w_ref[0:1, :] + x * w_ref[1:2, :] + nxt * w_ref[2:3, :]


def short_conv(u, w, first, last, tn=384):
    assert SHORT_CONV == 3
    m, c = u.shape
    halo = 8
    per = TM // halo
    n_halo = m // halo
    return pl.pallas_call(
        _short_conv_kernel,
        grid=(m // TM, c // tn),
        in_specs=[pl.BlockSpec((TM, tn), lambda i, j: (i, j)),
                  pl.BlockSpec((halo, tn), lambda i, j: (jnp.maximum(i * per - 1, 0), j)),
                  pl.BlockSpec((halo, tn), lambda i, j: (jnp.minimum((i + 1) * per, n_halo - 1), j)),
                  pl.BlockSpec((SHORT_CONV, tn), lambda i, j: (0, j)),
                  pl.BlockSpec((TM, 1), lambda i, j: (i, 0)),
                  pl.BlockSpec((TM, 1), lambda i, j: (i, 0))],
        out_specs=pl.BlockSpec((TM, tn), lambda i, j: (i, j)),
        out_shape=jax.ShapeDtypeStruct((m, c), F32),
        compiler_params=_cparams(("arbitrary", "arbitrary")),
        name="short_conv",
    )(u, u, u, w, first, last)


def _rmsnorm_kernel(x_ref, g_ref, o_ref):
    x = x_ref[...]
    y = x * lax.rsqrt(jnp.mean(x * x, axis=-1, keepdims=True) + NORM_EPS)
    o_ref[...] = y * g_ref[...]


def rmsnorm_rows(x, g):
    m, d = x.shape
    return pl.pallas_call(
        _rmsnorm_kernel,
        grid=(m // TM,),
        in_specs=[pl.BlockSpec((TM, d), lambda i: (i, 0)), pl.BlockSpec((1, d), lambda i: (0, 0))],
        out_specs=pl.BlockSpec((TM, d), lambda i: (i, 0)),
        out_shape=jax.ShapeDtypeStruct((m, d), F32),
        compiler_params=_cparams(("arbitrary",)),
        name="final_rmsnorm",
    )(x, g.reshape(1, d))


def _hyfilt_kernel(z_ref, w1_ref, b1_ref, w2_ref, b2_ref, fr_ref, w3_ref, dl_ref, h_ref, ss_ref):
    z = z_ref[...]
    hdn = jnp.sin(fr_ref[0:1, :] * (_dot(z, w1_ref[...], HI) + b1_ref[...]))
    hdn = jnp.sin(fr_ref[1:2, :] * (_dot(hdn, w2_ref[...], HI) + b2_ref[...]))
    h = _dot(hdn, w3_ref[...], HI)
    h = h * jnp.exp(-z[:, 0:1] * dl_ref[...])
    h_ref[...] = h

    @pl.when(pl.program_id(0) == 0)
    def _():
        ss_ref[...] = jnp.zeros_like(ss_ref)

    ss_ref[...] += jnp.sum(h * h, axis=0, keepdims=True)


def hyena_filters_raw(L, w1, b1, w2, b2, freq, w3):
    t = np.linspace(0.0, 1.0, L, dtype=np.float32)[:, None]
    omega = np.float32(2.0 * math.pi / L) * np.arange(L, dtype=np.float32)[:, None]
    bands = np.linspace(1e-4, HY_BANDS - 1, HY_BANDS, dtype=np.float32)[None, :]
    z = np.concatenate([t, np.cos(omega * bands), -np.sin(omega * bands),
                        np.zeros((L, HY_FILTER_ORDER - HY_EMB), np.float32)], axis=-1).astype(np.float32)
    w1p = jnp.concatenate([w1, jnp.zeros((HY_FILTER_ORDER - HY_EMB, HY_FILTER_ORDER), F32)], axis=0)
    deltas = np.abs(np.linspace(math.log(HY_DECAY_TARGET) / HY_SLOW_DECAY,
                                math.log(HY_DECAY_TARGET) / HY_FAST_DECAY, HY_WIDTH, dtype=np.float32))
    dl4 = np.tile(deltas, 4)[None, :]
    tl = min(L, 256)
    n = 4 * HY_WIDTH
    fo = HY_FILTER_ORDER
    full = lambda shape: pl.BlockSpec(shape, lambda i: (0, 0))
    return pl.pallas_call(
        _hyfilt_kernel,
        grid=(L // tl,),
        in_specs=[pl.BlockSpec((tl, fo), lambda i: (i, 0)), full((fo, fo)), full((1, fo)), full((fo, fo)),
                  full((1, fo)), full((2, fo)), full((fo, n)), full((1, n))],
        out_specs=[pl.BlockSpec((tl, n), lambda i: (i, 0)), full((1, n))],
        out_shape=[jax.ShapeDtypeStruct((L, n), F32), jax.ShapeDtypeStruct((1, n), F32)],
        compiler_params=_cparams(("arbitrary",)),
        name="hyena_filters",
    )(jnp.asarray(z), w1p, b1.reshape(1, fo), w2, b2.reshape(1, fo), freq, w3, jnp.asarray(dl4))


def _filter_scale(ss):
    s = ss.reshape(2, 2, HY_WIDTH)
    rs = lax.rsqrt(jnp.sum(s, axis=1, keepdims=True))
    return jnp.broadcast_to(rs, (2, 2, HY_WIDTH)).reshape(1, 4 * HY_WIDTH)


FFT_N1 = 128
FFT_N2 = 64
FFT_PITCH = 72
FFT_BATCH = 4


@functools.lru_cache(maxsize=None)
def _fft_tables():
    n1, n2 = FFT_N1, FFT_N2
    n = n1 * n2
    a = np.arange(n1 // 2)[None, None, :]
    k1 = np.arange(n1)[None, :, None]
    b = np.arange(n2)[:, None, None]
    theta = 2.0 * np.pi * ((a * k1 % n1) / n1 + (b * k1) / n)
    g = np.concatenate([np.cos(theta), -np.sin(theta)], axis=1)
    ig = np.concatenate([np.cos(theta), -np.sin(theta)], axis=1).transpose(0, 2, 1) / n
    k2 = np.arange(n2)[:, None]
    bb = np.arange(n2)[None, :]
    ph = 2.0 * np.pi * (k2 * bb % n2) / n2
    fr, fi = np.cos(ph), -np.sin(ph)
    f2 = np.block([[fr, -fi], [fi, fr]])
    if2 = np.block([[fr, fi], [-fi, fr]])
    return (jnp.asarray(g, BF16), jnp.asarray(f2, BF16), jnp.asarray(if2, BF16), jnp.asarray(ig, BF16))


def _fft_stage1(u_ref, g_ref, sr_ref, si_ref):
    n1, n2, p = FFT_N1, FFT_N2, FFT_PITCH

    def body(i, carry):
        bs = [i * FFT_BATCH + j for j in range(FFT_BATCH)]
        xs = [u_ref[pl.ds(b, n1 // 2, stride=n2), :].astype(BF16) for b in bs]
        outs = [_dot(g_ref[b], x) for b, x in zip(bs, xs)]
        for b, a in zip(bs, outs):
            sr_ref[pl.ds(b, n1, stride=p), :] = a[:n1]
            si_ref[pl.ds(b, n1, stride=p), :] = a[n1:]
        return carry

    lax.fori_loop(0, n2 // FFT_BATCH, body, 0)


def _bin_rows(k1):
    return pl.ds(pl.multiple_of(k1 * FFT_PITCH, 8), FFT_N2)


def _hyconv_kernel(xm_ref, u_ref, hr_ref, hi_ref, bias_ref, g_ref, f2_ref, if2_ref, ig_ref, o_ref, sr_ref, si_ref):
    n1, n2, p = FFT_N1, FFT_N2, FFT_PITCH
    _fft_stage1(u_ref, g_ref, sr_ref, si_ref)

    def pair(ref, k):
        return jnp.concatenate([ref[_bin_rows(2 * k), :], ref[_bin_rows(2 * k + 1), :]], axis=1)

    def pair_h(ref, k):
        blk = ref[pl.ds(pl.multiple_of(k * 2 * n2, 2 * n2), 2 * n2), :]
        return jnp.concatenate([blk[:n2], blk[n2:]], axis=1)

    def unpair(ref, k, val):
        w = val.shape[1] // 2
        ref[_bin_rows(2 * k), :] = val[:, :w]
        ref[_bin_rows(2 * k + 1), :] = val[:, w:]

    def stage2(i, carry):
        ks = [i * FFT_BATCH + j for j in range(FFT_BATCH)]
        zs = [jnp.concatenate([pair(sr_ref, k), pair(si_ref, k)], axis=0).astype(BF16) for k in ks]
        xs = [_dot(f2_ref[...], z) for z in zs]
        ys = []
        for k, x in zip(ks, xs):
            xr, xi = x[:n2], x[n2:]
            hr, hi = pair_h(hr_ref, k), pair_h(hi_ref, k)
            ys.append(jnp.concatenate([xr * hr - xi * hi, xr * hi + xi * hr], axis=0).astype(BF16))
        bbs = [_dot(if2_ref[...], y) for y in ys]
        for k, bb in zip(ks, bbs):
            unpair(sr_ref, k, bb[:n2])
            unpair(si_ref, k, bb[n2:])
        return carry

    lax.fori_loop(0, n1 // 2 // FFT_BATCH, stage2, 0)

    def stage3(i, carry):
        bs = [i * FFT_BATCH + j for j in range(FFT_BATCH)]
        sts = [jnp.concatenate([sr_ref[pl.ds(b, n1, stride=p), :], si_ref[pl.ds(b, n1, stride=p), :]],
                               axis=0).astype(BF16) for b in bs]
        outs = [_dot(ig_ref[b], st) for b, st in zip(bs, sts)]
        for b, o in zip(bs, outs):
            o_ref[pl.ds(b, n1 // 2, stride=n2), :] = o
        return carry

    lax.fori_loop(0, n2 // FFT_BATCH, stage3, 0)
    bias = bias_ref[...]
    rows_per_pass = 512

    def finish(i, carry):
        rows = pl.ds(pl.multiple_of(i * rows_per_pass, rows_per_pass), rows_per_pass)
        o_ref[rows, :] = xm_ref[rows, :] * (o_ref[rows, :] + u_ref[rows, :] * bias)
        return carry

    lax.fori_loop(0, o_ref.shape[0] // rows_per_pass, finish, 0)


def hyena_conv(xm_arr, xm_col, u_arr, u_col, hr, hi, h_col, bias, n_batch, L):
    assert L == FFT_N1 * FFT_N2 // 2
    cb = LANES
    nct = HY_WIDTH // cb
    n = 2 * L
    g, f2, if2, ig = _fft_tables()
    const3 = lambda shape: pl.BlockSpec(shape, lambda b, c: (0, 0, 0))
    const2 = lambda shape: pl.BlockSpec(shape, lambda b, c: (0, 0))
    return pl.pallas_call(
        _hyconv_kernel,
        grid=(n_batch, nct),
        in_specs=[pl.BlockSpec((L, cb), lambda b, c: (b, xm_col * nct + c)),
                  pl.BlockSpec((L, cb), lambda b, c: (b, u_col * nct + c)),
                  pl.BlockSpec((n, cb), lambda b, c: (0, h_col * nct + c)),
                  pl.BlockSpec((n, cb), lambda b, c: (0, h_col * nct + c)),
                  pl.BlockSpec((1, cb), lambda b, c: (0, c)),
                  const3(g.shape), const2(f2.shape), const2(if2.shape), const3(ig.shape)],
        out_specs=pl.BlockSpec((L, cb), lambda b, c: (b, c)),
        out_shape=jax.ShapeDtypeStruct((n_batch * L, HY_WIDTH), F32),
        scratch_shapes=[pltpu.VMEM((FFT_N1 * FFT_PITCH, cb), F32)] * 2,
        compiler_params=_cparams(("arbitrary", "arbitrary")),
        name="hyena_conv",
    )(xm_arr, u_arr, hr, hi, bias.reshape(1, HY_WIDTH), g, f2, if2, ig)


def _hyspec_kernel(h0_ref, h1_ref, rs_ref, g_ref, f2_ref, hr_ref, hi_ref, s0r, s0i, s1r, s1i):
    n1, n2 = FFT_N1, FFT_N2
    _fft_stage1(h0_ref, g_ref, s0r, s0i)
    _fft_stage1(h1_ref, g_ref, s1r, s1i)
    rs = rs_ref[...]
    h10 = h1_ref[0:1, :]

    def stage2(i, carry):
        ks = [i * FFT_BATCH + j for j in range(FFT_BATCH)]
        zs = [jnp.concatenate([jnp.concatenate([s0r[_bin_rows(k), :], s1r[_bin_rows(k), :]], axis=1),
                               jnp.concatenate([s0i[_bin_rows(k), :], s1i[_bin_rows(k), :]], axis=1)],
                              axis=0).astype(BF16) for k in ks]
        xs = [_dot(f2_ref[...], z) for z in zs]
        for k, x in zip(ks, xs):
            rows = pl.ds(pl.multiple_of(k * n2, n2), n2)
            w = x.shape[1] // 2
            hr_ref[rows, :] = rs * (x[:n2, :w] + x[:n2, w:] - h10)
            hi_ref[rows, :] = rs * (x[n2:, :w] - x[n2:, w:])
        return carry

    lax.fori_loop(0, n1 // FFT_BATCH, stage2, 0)


def hyena_filter_spectrum(h_raw, rs, L):
    assert L == FFT_N1 * FFT_N2 // 2
    cb = LANES
    nct = HY_WIDTH // cb
    n = 2 * L
    g, f2, _, _ = _fft_tables()
    out_spec = pl.BlockSpec((n, cb), lambda o, c: (0, o * nct + c))
    scr = pltpu.VMEM((FFT_N1 * FFT_PITCH, cb), F32)
    return pl.pallas_call(
        _hyspec_kernel,
        grid=(2, nct),
        in_specs=[pl.BlockSpec((L, cb), lambda o, c: (0, (2 * o) * nct + c)),
                  pl.BlockSpec((L, cb), lambda o, c: (0, (2 * o + 1) * nct + c)),
                  pl.BlockSpec((1, cb), lambda o, c: (0, (2 * o) * nct + c)),
                  pl.BlockSpec(g.shape, lambda o, c: (0, 0, 0)),
                  pl.BlockSpec(f2.shape, lambda o, c: (0, 0))],
        out_specs=[out_spec, out_spec],
        out_shape=[jax.ShapeDtypeStruct((n, 2 * HY_WIDTH), F32)] * 2,
        scratch_shapes=[scr, scr, scr, scr],
        compiler_params=_cparams(("arbitrary", "arbitrary")),
        name="hyena_filter_spectrum",
    )(h_raw, h_raw, rs, g, f2)


@functools.lru_cache(maxsize=None)
def _dense_dft_tables(L):
    n = 2 * L
    k = np.arange(n)[:, None]
    t = np.arange(L)[None, :]
    ph = 2.0 * np.pi * (k * t % n) / n
    fwd = np.concatenate([np.cos(ph), -np.sin(ph)], axis=0)
    inv = np.concatenate([np.cos(ph), -np.sin(ph)], axis=0).T / n
    return jnp.asarray(fwd, F32), jnp.asarray(inv, F32)


def _hyena_small_kernel(x1_ref, x2_ref, v_ref, h_ref0a, h_ref0b, h_ref1a, h_ref1b, rs0_ref, rs1_ref,
                        b0_ref, b1_ref, fwd_ref, inv_ref, o_ref, *, L):
    n = 2 * L
    fwd = fwd_ref[...]
    inv = inv_ref[...]

    def conv(u, ha_ref, hb_ref, rs_ref, bias_ref):
        ha, hb = ha_ref[...], hb_ref[...]
        ka = _dot(fwd, ha, HI)
        kb = _dot(fwd, hb, HI)
        rs = rs_ref[...]
        kr = rs * (ka[:n] + kb[:n] - hb[0:1, :])
        ki = rs * (ka[n:] - kb[n:])
        uf = _dot(fwd, u, HI)
        ur, ui = uf[:n], uf[n:]
        y = jnp.concatenate([ur * kr - ui * ki, ur * ki + ui * kr], axis=0)
        return _dot(inv, y, HI) + u * bias_ref[...]

    v = v_ref[...]
    z = x1_ref[...] * conv(v, h_ref0a, h_ref0b, rs0_ref, b0_ref)
    o_ref[...] = (x2_ref[...] * conv(z, h_ref1a, h_ref1b, rs1_ref, b1_ref)).astype(o_ref.dtype)


def hyena_small(u_arr, row0_blocks, n_batch, L, h_raw, rs, bias):
    cb = LANES
    nct = HY_WIDTH // cb
    fwd, inv = _dense_dft_tables(L)
    uspec = lambda col: pl.BlockSpec((L, cb), lambda b, c: (row0_blocks + b, col * nct + c))
    hspec = lambda col: pl.BlockSpec((L, cb), lambda b, c: (0, col * nct + c))
    rspec = lambda col: pl.BlockSpec((1, cb), lambda b, c: (0, col * nct + c))
    bspec = pl.BlockSpec((1, cb), lambda b, c: (0, c))
    bias_0 = bias[0].reshape(1, HY_WIDTH)
    bias_1 = bias[1].reshape(1, HY_WIDTH)
    return pl.pallas_call(
        functools.partial(_hyena_small_kernel, L=L),
        grid=(n_batch, nct),
        in_specs=[uspec(0), uspec(1), uspec(2), hspec(0), hspec(1), hspec(2), hspec(3), rspec(0), rspec(2),
                  bspec, bspec,
                  pl.BlockSpec(fwd.shape, lambda b, c: (0, 0)), pl.BlockSpec(inv.shape, lambda b, c: (0, 0))],
        out_specs=pl.BlockSpec((L, cb), lambda b, c: (b, c)),
        out_shape=jax.ShapeDtypeStruct((n_batch * L, HY_WIDTH), BF16),
        compiler_params=_cparams(("arbitrary", "arbitrary")),
        name="hyena_ctx",
    )(u_arr, u_arr, u_arr, h_raw, h_raw, h_raw, h_raw, rs, rs, bias_0, bias_1, fwd, inv)


def _na_bias_table(rpb):
    cols = np.arange(GRID_W)
    col_start = np.clip(cols - NA_COLS // 2, 0, GRID_W - NA_COLS)[:, None]
    in_win = (cols[None, :] >= col_start) & (cols[None, :] < col_start + NA_COLS)
    rel_col = np.clip(cols[None, :] - cols[:, None], 1 - NA_COLS, NA_COLS - 1) + NA_COLS - 1
    tbl = rpb.astype(F32)[:, :, rel_col]
    tbl = jnp.where(jnp.asarray(in_win)[None, None], tbl, -jnp.inf)
    return jnp.concatenate([tbl[:, :-1], tbl[:, 1:]], axis=-1)


def _na_kernel(*refs, n_rows):
    q_ref = refs[0]
    k_refs = refs[1:1 + NA_ROWS]
    v_refs = refs[1 + NA_ROWS:1 + 2 * NA_ROWS]
    kc_ref, vc_ref, tbl_ref, o_ref = refs[1 + 2 * NA_ROWS:]
    r = pl.program_id(1)
    start = jnp.clip(r - NA_ROWS // 2, 0, n_rows - NA_ROWS)
    d0 = start - r + NA_ROWS - 1
    dh = NA_HEAD_DIM
    q = q_ref[...] * (dh ** -0.5)
    n_pairs = NA_ROWS // 2
    heads = [slice(h * dh, (h + 1) * dh) for h in range(NA_HEADS)]
    scores = []
    for h, hs in enumerate(heads):
        qh = q[:, hs]
        tiles = [_dot_nt(qh, jnp.concatenate([k_refs[2 * p][:, hs], k_refs[2 * p + 1][:, hs]], axis=0))
                 + tbl_ref[h, d0 + 2 * p] for p in range(n_pairs)]
        scores.append(tiles + [_dot_nt(qh, kc_ref[:, hs])])
    probs, denoms = [], []
    for tiles in scores:
        m = tiles[0].max(axis=-1, keepdims=True)
        for s in tiles[1:]:
            m = jnp.maximum(m, s.max(axis=-1, keepdims=True))
        ps = [jnp.exp(s - m) for s in tiles]
        l = ps[0].sum(axis=-1, keepdims=True)
        for p_ in ps[1:]:
            l = l + p_.sum(axis=-1, keepdims=True)
        probs.append([p_.astype(BF16) for p_ in ps])
        denoms.append(l)
    for hs, ps, l in zip(heads, probs, denoms):
        acc = _dot(ps[-1], vc_ref[:, hs])
        for p in range(n_pairs):
            acc = acc + _dot(ps[p], jnp.concatenate([v_refs[2 * p][:, hs], v_refs[2 * p + 1][:, hs]], axis=0))
        o_ref[:, hs] = (acc / l).astype(o_ref.dtype)


def na_latent(na, rpb, n_batch, L, Lc):
    n_rows = L // GRID_W
    assert n_rows >= NA_ROWS
    tbl = _na_bias_table(rpb)
    w = NA_WIDTH
    ctx_blk0 = n_batch * L // Lc

    def kv_spec(i, col):
        def imap(b, r):
            start = jnp.clip(r - NA_ROWS // 2, 0, n_rows - NA_ROWS)
            return (b * n_rows + start + i, col)
        return pl.BlockSpec((GRID_W, w), imap)

    in_specs = ([pl.BlockSpec((GRID_W, w), lambda b, r: (b * n_rows + r, 0))]
                + [kv_spec(i, 1) for i in range(NA_ROWS)] + [kv_spec(i, 2) for i in range(NA_ROWS)]
                + [pl.BlockSpec((Lc, w), lambda b, r: (ctx_blk0 + b, 1)),
                   pl.BlockSpec((Lc, w), lambda b, r: (ctx_blk0 + b, 2)),
                   pl.BlockSpec(tbl.shape, lambda b, r: (0, 0, 0, 0))])
    return pl.pallas_call(
        functools.partial(_na_kernel, n_rows=n_rows),
        grid=(n_batch, n_rows),
        in_specs=in_specs,
        out_specs=pl.BlockSpec((GRID_W, w), lambda b, r: (b * n_rows + r, 0)),
        out_shape=jax.ShapeDtypeStruct((n_batch * L, w), BF16),
        compiler_params=_cparams(("arbitrary", "arbitrary")),
        name="na_latent",
    )(*([na] * (3 + 2 * NA_ROWS)), tbl)


def _ctx_attn_kernel(q_ref, k_ref, v_ref, o_ref):
    dh = NA_HEAD_DIM
    q = q_ref[...] * (dh ** -0.5)
    for h in range(NA_HEADS):
        hs = slice(h * dh, (h + 1) * dh)
        s = _dot_nt(q[:, hs], k_ref[:, hs])
        p_ = jnp.exp(s - s.max(axis=-1, keepdims=True))
        acc = _dot(p_.astype(BF16), v_ref[:, hs])
        o_ref[:, hs] = (acc / p_.sum(axis=-1, keepdims=True)).astype(o_ref.dtype)


def ctx_attn(na, n_batch, L, Lc):
    w = NA_WIDTH
    blk0 = n_batch * L // Lc
    spec = lambda col: pl.BlockSpec((Lc, w), lambda b: (blk0 + b, col))
    return pl.pallas_call(
        _ctx_attn_kernel,
        grid=(n_batch,),
        in_specs=[spec(0), spec(1), spec(2)],
        out_specs=pl.BlockSpec((Lc, w), lambda b: (b, 0)),
        out_shape=jax.ShapeDtypeStruct((n_batch * Lc, w), BF16),
        compiler_params=_cparams(("arbitrary",)),
        name="ctx_attn",
    )(na, na, na)


@functools.lru_cache(maxsize=None)
def _wkv_masks():
    c, g = WKV_CHUNK, WKV_GROUP
    t = np.arange(c)[:, None]
    s = np.arange(c)[None, :]
    tinc = np.stack([(s <= t), (s >= t)]).astype(np.float32)
    strict = np.stack([(s < t), (s > t)]).astype(np.float32)
    tile = lambda m: np.tile(m, (1,) * (m.ndim - 1) + (g,))
    blk = lambda n: (t // n == s // n)
    blk16 = tile(blk(16).astype(np.float32))
    off32 = tile((blk(32) & ~blk(16)).astype(np.float32))
    off64 = tile((~blk(32)).astype(np.float32))
    eye = tile((t == s).astype(np.float32))
    rr = np.arange(g * c)
    hm = (rr[:, None] // c == np.arange(g * RW_HEAD_DIM)[None, :] // RW_HEAD_DIM).astype(np.float32)
    masks = tuple(jnp.asarray(m) for m in (tinc, tile(strict), tile(tinc), blk16, off32, off64, eye, hm))
    return masks + (jnp.asarray(hm, BF16),)


def _wkv_kernel(*refs):
    (r0_ref, v0_ref, kk0_ref, r1_ref, v1_ref, kk1_ref, lw0_ref, av0_ref, kd0_ref, lw1_ref, av1_ref, kd1_ref,
     tinc_ref, strict_ref, incl_ref, blk16_ref, off32_ref, off64_ref, eye_ref, hm_ref, hmb_ref,
     y0_ref, y1_ref, state_ref) = refs
    c, g = WKV_CHUNK, WKV_GROUP
    gw = g * RW_HEAD_DIM

    @pl.when(pl.program_id(1) == 0)
    def _():
        state_ref[...] = jnp.zeros_like(state_ref)

    hm = hm_ref[...]
    hm_bf = hmb_ref[...]
    blk16, off32, off64, eye = blk16_ref[...], off32_ref[...], off64_ref[...], eye_ref[...]

    def bdiag(z):
        return jnp.concatenate([z.astype(BF16)] * g, axis=0) * hm_bf

    def pm(x4, zd):
        return _dot(x4.astype(BF16), zd)

    def prepare(d, r_ref, kk_ref, lw_ref, av_ref, kd_ref):
        lw = lw_ref[0]
        cum = _dot(tinc_ref[d], lw, HI)
        tot = jnp.sum(lw, axis=0, keepdims=True)
        e_neg = jnp.exp(-cum)
        e_rem = jnp.exp(tot - cum)
        kk = kk_ref[...]
        b_vec = kk * av_ref[0]
        kd = kd_ref[0]
        return dict(at=-kk * jnp.exp(cum - lw), rt=r_ref[...] * jnp.exp(cum), bt=b_vec * e_neg, kt=kd * e_neg,
                    bp=b_vec * e_rem, kp=kd * e_rem, e_tot=jnp.exp(tot))

    qs = (prepare(0, r0_ref, kk0_ref, lw0_ref, av0_ref, kd0_ref), prepare(1, r1_ref, kk1_ref, lw1_ref, av1_ref, kd1_ref))
    v_refs, y_refs = (v0_ref, v1_ref), (y0_ref, y1_ref)
    chains = [(d, gi) for gi in range(RW_HEADS // g) for d in range(2)]
    sl = lambda gi: slice(gi * gw, (gi + 1) * gw)
    each = lambda f, *lists: [f(*args) for args in zip(*lists)]
    pm_all = lambda xs, zs: each(lambda x, z: pm(x, bdiag(z)), xs, zs)

    ar = [jnp.concatenate([qs[d]["at"][:, sl(gi)], qs[d]["rt"][:, sl(gi)]], axis=0).astype(BF16) for d, gi in chains]
    pb = [_dot_nt(a, bdiag(qs[d]["bt"][:, sl(gi)])) for a, (d, gi) in zip(ar, chains)]
    pk = [_dot_nt(a, bdiag(qs[d]["kt"][:, sl(gi)])) for a, (d, gi) in zip(ar, chains)]
    a_ab = [p[:c] * strict_ref[d] for p, (d, gi) in zip(pb, chains)]
    a_rb = [p[c:] * incl_ref[d] for p, (d, gi) in zip(pb, chains)]
    a_ak = [p[:c] * strict_ref[d] for p, (d, gi) in zip(pk, chains)]
    a_rk = [p[c:] * incl_ref[d] for p, (d, gi) in zip(pk, chains)]
    ad = [a * blk16 for a in a_ab]
    a2 = pm_all(ad, ad)
    a4 = pm_all(a2, a2)
    a8 = pm_all(a4, a4)
    tinv = [eye + a for a in ad]
    for powr in (a2, a4, a8):
        tinv = each(lambda t, p_: t + p_, tinv, pm_all(tinv, powr))
    for off in (off32, off64):
        mid = pm_all(tinv, [a * off for a in a_ab])
        tinv = each(lambda t, p_: t + p_, tinv, pm_all(mid, tinv))
    s0 = [state_ref[d, gi] for d, gi in chains]
    vv = [v_refs[d][:, sl(gi)] for d, gi in chains]
    vd = [bdiag(v_) for v_ in vv]
    ars = each(lambda a, s_: _dot_nt(a, s_.astype(BF16)), ar, s0)
    akv = each(pm, a_ak, vd)
    u = pm_all(tinv, each(lambda x, y_: x[:c] + y_, ars, akv))
    yu = pm_all(a_rb, u)
    yv = each(pm, a_rk, vd)
    for (d, gi), x, y1_, y2_ in zip(chains, ars, yu, yv):
        y_refs[d][:, sl(gi)] = x[c:] + y1_ + y2_
    upd = [_dot_tn(jnp.concatenate([u_, v_], axis=0).astype(BF16),
                   jnp.concatenate([qs[d]["bp"][:, sl(gi)], qs[d]["kp"][:, sl(gi)]], axis=0).astype(BF16))
           for u_, v_, (d, gi) in zip(u, vv, chains)]
    for (d, gi), s_, up in zip(chains, s0, upd):
        state_ref[d, gi] = s_ * qs[d]["e_tot"][:, sl(gi)] + hm * up


def wkv_scan(r, v, kk, lw, av, kd, n_batch, L, Lc):
    c = WKV_CHUNK
    rows, w = r.shape
    nc, nl = Lc // c, L // c
    masks = _wkv_masks()

    def blk(d, b, s):
        j_ctx = s if d == 0 else nc - 1 - s
        j_lat = s - nc if d == 0 else nl - 1 - (s - nc)
        return jnp.where(s < nc, (n_batch * L + b * Lc) // c + j_ctx, (b * L) // c + j_lat)

    shared = lambda d: pl.BlockSpec((c, w), lambda b, s: (blk(d, b, s), 0))
    perdir = lambda d: pl.BlockSpec((1, c, w), lambda b, s: (d, blk(d, b, s), 0))
    full = lambda m: pl.BlockSpec(m.shape, lambda b, s: (0,) * m.ndim)
    gw = WKV_GROUP * RW_HEAD_DIM
    return pl.pallas_call(
        _wkv_kernel,
        grid=(n_batch, nc + nl),
        in_specs=[shared(0)] * 3 + [shared(1)] * 3 + [perdir(0)] * 3 + [perdir(1)] * 3 + [full(m) for m in masks],
        out_specs=[shared(0), shared(1)],
        out_shape=[jax.ShapeDtypeStruct((rows, w), F32)] * 2,
        scratch_shapes=[pltpu.VMEM((2, RW_HEADS // WKV_GROUP, gw, gw), F32)],
        compiler_params=_cparams(("arbitrary", "arbitrary")),
        name="wkv_scan",
    )(r, v, kk, r, v, kk, lw, av, kd, lw, av, kd, *masks)


def _moe_kernel(be_ref, nb_ref, xa_ref, xb_ref, wg_ref, wu_ref, wd_ref, o_ref, wg_s, wu_s, wd_s, *, nb_a):
    i = pl.program_id(0)
    prev = be_ref[jnp.maximum(i - 1, 0)]

    @pl.when((i == 0) | (be_ref[i] != prev))
    def _():
        wg_s[...] = wg_ref[0, 0].astype(BF16)
        wu_s[...] = wu_ref[0, 0].astype(BF16)
        wd_s[...] = wd_ref[0, 0].astype(BF16)

    @pl.when(i < nb_ref[0])
    def _():
        x = jnp.where(i < nb_a, xa_ref[...], xb_ref[...])
        hmid = (jax.nn.silu(_dot(x, wg_s[...])) * _dot(x, wu_s[...])).astype(BF16)
        o_ref[...] = _dot(hmid, wd_s[...]).astype(o_ref.dtype)

    @pl.when(i >= nb_ref[0])
    def _():
        o_ref[...] = jnp.zeros_like(o_ref)


def grouped_swiglu(xa, xb, block_e, n_used, w_gate, w_up, w_down, layer):
    d = xa.shape[1]
    ff = w_gate.shape[-1]
    nb_a = xa.shape[0] // MOE_BLOCK
    nb = nb_a + (0 if xb is None else xb.shape[0] // MOE_BLOCK)
    xb = xa if xb is None else xb
    grid_spec = pltpu.PrefetchScalarGridSpec(
        num_scalar_prefetch=2,
        grid=(nb,),
        in_specs=[pl.BlockSpec((MOE_BLOCK, d), lambda i, be, nu: (jnp.minimum(i, nb_a - 1), 0)),
                  pl.BlockSpec((MOE_BLOCK, d), lambda i, be, nu: (jnp.maximum(i - nb_a, 0), 0)),
                  pl.BlockSpec((1, 1, d, ff), lambda i, be, nu: (layer, be[i], 0, 0)),
                  pl.BlockSpec((1, 1, d, ff), lambda i, be, nu: (layer, be[i], 0, 0)),
                  pl.BlockSpec((1, 1, ff, d), lambda i, be, nu: (layer, be[i], 0, 0))],
        out_specs=pl.BlockSpec((MOE_BLOCK, d), lambda i, be, nu: (i, 0)),
        scratch_shapes=[pltpu.VMEM((d, ff), BF16), pltpu.VMEM((d, ff), BF16), pltpu.VMEM((ff, d), BF16)],
    )
    return pl.pallas_call(
        functools.partial(_moe_kernel, nb_a=nb_a),
        grid_spec=grid_spec,
        out_shape=jax.ShapeDtypeStruct((nb * MOE_BLOCK, d), BF16),
        compiler_params=_cparams(("arbitrary",)),
        name="grouped_swiglu",
    )(block_e, n_used, xa, xb, w_gate, w_up, w_down)


def _combine_kernel(grp_ref, x_ref, y_ref, w_ref, s_ref, gate_ref, o_ref):
    del grp_ref
    f = s_ref[...].astype(F32)
    for k in range(TOP_K):
        f = f + w_ref[:, k:k + 1] * y_ref[k].astype(F32)
    o_ref[...] = x_ref[...] + gate_ref[0] * f


def moe_combine(x, yg, e_w, shared, gate3, grp):
    m, d = x.shape
    tm = TM // 2
    grid_spec = pltpu.PrefetchScalarGridSpec(
        num_scalar_prefetch=1,
        grid=(m // tm,),
        in_specs=[pl.BlockSpec((tm, d), lambda i, grp: (i, 0)),
                  pl.BlockSpec((TOP_K, tm, d), lambda i, grp: (0, i, 0)),
                  pl.BlockSpec((tm, e_w.shape[1]), lambda i, grp: (i, 0)),
                  pl.BlockSpec((tm, d), lambda i, grp: (i, 0)),
                  pl.BlockSpec((1, 1, d), lambda i, grp: (grp[i // 2], 0, 0))],
        out_specs=pl.BlockSpec((tm, d), lambda i, grp: (i, 0)),
    )
    return pl.pallas_call(
        _combine_kernel,
        grid_spec=grid_spec,
        out_shape=jax.ShapeDtypeStruct((m, d), F32),
        compiler_params=_cparams(("arbitrary",)),
        name="moe_combine",
    )(grp, x, yg, e_w, shared, gate3)


def _route_kernel(grp_ref, x_ref, g_ref, sh_ref, sc_ref, wt_ref, rb_ref, tri_ref, ones_ref,
                  h_ref, idx_ref, w_ref, rank_ref, cnt_ref, carry_ref):
    del grp_ref
    tm = x_ref.shape[0]
    gs = N_EXPERTS // N_GROUPS
    neg = -jnp.inf

    @pl.when(pl.program_id(0) == 0)
    def _():
        carry_ref[...] = jnp.zeros_like(carry_ref)

    x = x_ref[...]
    y = x * lax.rsqrt(jnp.mean(x * x, axis=-1, keepdims=True) + NORM_EPS)
    h = (y * g_ref[...]) * (1.0 + sc_ref[0]) + sh_ref[0]
    h_ref[...] = h.astype(h_ref.dtype)
    scores = jax.nn.sigmoid(_dot_nt(wt_ref[...], h, HI))
    biased = scores + rb_ref[...]

    def first_argmax(v, iota, n):
        m = jnp.max(v, axis=0, keepdims=True)
        return m, jnp.min(jnp.where(v == m, iota, float(n)), axis=0, keepdims=True)

    def stack_rows(rows):
        iota8 = lax.broadcasted_iota(jnp.int32, (8, tm), 0)
        out = jnp.zeros((8, tm), F32)
        for k, row in enumerate(rows):
            out = jnp.where(iota8 == k, row, out)
        return out

    assert gs == 8 and N_GROUPS == 8
    iota_g = lax.broadcasted_iota(jnp.int32, (gs, tm), 0).astype(F32)
    g_rows = []
    for g in range(N_GROUPS):
        bg = biased[g * gs:(g + 1) * gs]
        m1, i1 = first_argmax(bg, iota_g, gs)
        m2 = jnp.max(jnp.where(iota_g == i1, neg, bg), axis=0, keepdims=True)
        g_rows.append(m1 + m2)
    g_score = stack_rows(g_rows)
    g_sel = jnp.zeros((N_GROUPS, tm), F32)
    for _ in range(TOPK_GROUPS):
        _, ig = first_argmax(g_score, iota_g, N_GROUPS)
        hit = iota_g == ig
        g_sel = jnp.where(hit, 1.0, g_sel)
        g_score = jnp.where(hit, neg, g_score)
    e_sel = jnp.concatenate([jnp.broadcast_to(g_sel[g:g + 1], (gs, tm)) for g in range(N_GROUPS)], axis=0)
    masked = jnp.where(e_sel > 0.0, biased, neg)

    iota_e = lax.broadcasted_iota(jnp.int32, (N_EXPERTS, tm), 0).astype(F32)
    chosen = jnp.zeros((N_EXPERTS, tm), F32)
    hits, idx_rows, w_rows = [], [], []
    for _ in range(TOP_K):
        _, ie = first_argmax(masked, iota_e, N_EXPERTS)
        hit = iota_e == ie
        hits.append(hit)
        idx_rows.append(ie)
        w_rows.append(jnp.sum(jnp.where(hit, scores, 0.0), axis=0, keepdims=True))
        chosen = jnp.where(hit, 1.0, chosen)
        masked = jnp.where(hit, neg, masked)
    w_sum = w_rows[0]
    for wk in w_rows[1:]:
        w_sum = w_sum + wk
    idx_ref[...] = stack_rows(idx_rows).astype(jnp.int32)
    w_ref[...] = stack_rows([wk / w_sum * ROUTE_SCALE for wk in w_rows])

    chosen_b = chosen.astype(BF16)
    before = carry_ref[...] + _dot(chosen_b, tri_ref[...])
    rank_ref[...] = stack_rows([jnp.sum(jnp.where(hit, before, 0.0), axis=0, keepdims=True)
                                for hit in hits]).astype(jnp.int32)
    carry_ref[...] += _dot(chosen_b, ones_ref[...])
    cnt_ref[...] = carry_ref[:, :LANES].astype(jnp.int32)


def route(x, g, shift3, scale3, grp, router_w, router_b):
    m, d = x.shape
    ne = N_EXPERTS
    tri = jnp.asarray(np.triu(np.ones((TM, TM), np.float32), 1), BF16)
    ones = jnp.ones((TM, TM), BF16)
    rb = jnp.broadcast_to(router_b.astype(F32)[:, None], (ne, TM))
    row = lambda r: pl.BlockSpec((r, TM), lambda i, grp: (0, i))
    const = lambda shape: pl.BlockSpec(shape, lambda i, grp: (0, 0))
    grid_spec = pltpu.PrefetchScalarGridSpec(
        num_scalar_prefetch=1,
        grid=(m // TM,),
        in_specs=[pl.BlockSpec((TM, d), lambda i, grp: (i, 0)),
                  const((1, d)),
                  pl.BlockSpec((1, 1, d), lambda i, grp: (grp[i], 0, 0)),
                  pl.BlockSpec((1, 1, d), lambda i, grp: (grp[i], 0, 0)),
                  const((ne, d)), const((ne, TM)), const((TM, TM)), const((TM, TM))],
        out_specs=[pl.BlockSpec((TM, d), lambda i, grp: (i, 0)), row(8), row(8), row(8), const((ne, LANES))],
        scratch_shapes=[pltpu.VMEM((ne, TM), F32)],
    )
    return pl.pallas_call(
        _route_kernel,
        grid_spec=grid_spec,
        out_shape=[jax.ShapeDtypeStruct((m, d), BF16), jax.ShapeDtypeStruct((8, m), jnp.int32),
                   jax.ShapeDtypeStruct((8, m), F32), jax.ShapeDtypeStruct((8, m), jnp.int32),
                   jax.ShapeDtypeStruct((ne, LANES), jnp.int32)],
        compiler_params=_cparams(("arbitrary",)),
        name="route",
    )(grp, x, g.reshape(1, d), shift3, scale3, router_w.T, rb, tri, ones)


def _seq_edge_masks(n_batch, L, Lc):
    n_lat = n_batch * L
    starts = np.concatenate([np.arange(n_batch) * L, n_lat + np.arange(n_batch) * Lc])
    ends = np.concatenate([(np.arange(n_batch) + 1) * L, n_lat + (np.arange(n_batch) + 1) * Lc]) - 1
    first = np.ones((n_batch * (L + Lc), 1), np.float32)
    last = first.copy()
    first[starts] = 0.0
    last[ends] = 0.0
    return jnp.asarray(first), jnp.asarray(last)


def _rwkv_prep(cols, first, last, shift_w, w0, w2, a0, a2, g2, k_k, k_a):
    u = short_conv(cols, shift_w, first, last)
    n = u.shape[0]
    r, k, v, wlo, alo, glo = jnp.split(u, RW_SPLITS, axis=-1)
    lw, av, kd = [], [], []
    for d in range(2):
        wl = wlo[:, d * RW_DECAY_LORA:(d + 1) * RW_DECAY_LORA]
        al = alo[:, d * RW_AAA_LORA:(d + 1) * RW_AAA_LORA]
        w = -jax.nn.softplus(-(w0[d] + jnp.tanh(wl) @ w2[d])) - 0.5
        lw.append(-jnp.exp(w))
        a = jax.nn.sigmoid(a0[d] + al @ a2[d])
        av.append(a)
        kd.append(k * (1.0 + (a - 1.0) * k_a))
    g = jax.nn.sigmoid(glo) @ g2
    kk = (k * k_k).reshape(n, RW_HEADS, RW_HEAD_DIM)
    kk = kk / jnp.maximum(jnp.sqrt(jnp.sum(kk * kk, axis=-1, keepdims=True)), 1e-12)
    return r, v, kk.reshape(n, RW_WIDTH), jnp.stack(lw), jnp.stack(av), jnp.stack(kd), g


def _rwkv_out(y, r, kd, v, g, r_k, ln_w, ln_b):
    n = y.shape[0]
    hh = lambda t: t.reshape(t.shape[:-1] + (RW_HEADS, RW_HEAD_DIM))
    yh = hh(y)
    mu = jnp.mean(yh, axis=-1, keepdims=True)
    var = jnp.mean(jnp.square(yh - mu), axis=-1, keepdims=True)
    yh = (yh - mu) * lax.rsqrt(var + RW_GN_EPS)
    rk = jnp.sum(hh(r)[None] * hh(kd) * r_k, axis=-1, keepdims=True)
    bonus = jnp.sum(rk * hh(v)[None], axis=0)
    out = yh.reshape(n, RW_WIDTH) * ln_w + ln_b + bonus.reshape(n, RW_WIDTH)
    return out * g


def _moe(h, e_idx, rank, counts, exp_gate, exp_up, exp_down, sh_gate, sh_up, sh_down, layer):
    T, D = h.shape
    n = T * TOP_K
    padded = (counts + MOE_BLOCK - 1) // MOE_BLOCK * MOE_BLOCK
    pad_end = jnp.cumsum(padded)
    pad_start = pad_end - padded
    experts = jnp.arange(N_EXPERTS, dtype=jnp.int32)
    dest = rank + jnp.sum(jnp.where(e_idx[:, :, None] == experts, pad_start.astype(jnp.int32), 0), axis=-1)
    n_blocks = -(-n // MOE_BLOCK) + N_EXPERTS
    n_blocks += n_blocks % 2
    n_slots = n_blocks * MOE_BLOCK
    flat_dest = dest.reshape(-1)
    tok = jnp.tile(jnp.arange(T, dtype=jnp.int32), TOP_K)
    slot_tok = jnp.zeros((n_slots,), jnp.int32).at[flat_dest].set(tok)
    block_start = jnp.arange(n_blocks, dtype=jnp.int32) * MOE_BLOCK
    block_e = jnp.minimum(jnp.sum(block_start[:, None] >= pad_end[None, :], axis=1), N_EXPERTS - 1).astype(jnp.int32)
    n_used = (pad_end[-1] // MOE_BLOCK).astype(jnp.int32).reshape(1)
    half = n_slots // 2
    xa = jnp.take(h, slot_tok[:half], axis=0, mode="clip")
    xb = jnp.take(h, slot_tok[half:], axis=0, mode="clip")
    y = grouped_swiglu(xa, xb, block_e, n_used, exp_gate, exp_up, exp_down, layer)
    yg = jnp.take(y, flat_dest, axis=0, mode="clip").reshape(TOP_K, T, D)
    nb_sh = T // MOE_BLOCK
    sh4 = lambda w: w.reshape((w.shape[0], 1) + w.shape[1:])
    shared = grouped_swiglu(h, None, jnp.zeros((nb_sh,), jnp.int32), jnp.full((1,), nb_sh, jnp.int32),
                            sh4(sh_gate), sh4(sh_up), sh4(sh_down), layer)
    return yg, shared


def kernel(x, c, ctx, c_ctx, mod_w, mod_b, norm1_g, norm2_g, w_in, hy_conv, hy_w1, hy_b1, hy_w2, hy_b2, hy_freq,
           hy_w3, hy_bias, na_rpb, rw_shift, rw_w0, rw_w2, rw_a0, rw_a2, rw_g2, rw_kk, rw_ka, rw_rk, rw_ln_w,
           rw_ln_b, proj_a, proj_b, proj_c, w_out, router_w, router_b, exp_gate, exp_up, exp_down, sh_gate, sh_up,
           sh_down, final_g):
    B, L, D = x.shape
    Lc = ctx.shape[1]
    depth = mod_w.shape[0]
    n_lat, n_ctx = B * L, B * Lc
    assert L % TM == 0 and n_ctx % TM == 0 and L % WKV_CHUNK == 0 and Lc % WKV_CHUNK == 0
    col_hy = 3 * HY_WIDTH
    col_na = col_hy + 3 * NA_WIDTH
    col_rw = col_na + RW_COLS

    xs = jnp.concatenate([x.reshape(n_lat, D), ctx.reshape(n_ctx, D)], axis=0)
    grp_all = jnp.asarray(np.concatenate([np.repeat(np.arange(B), L // TM), np.full(n_ctx // TM, B)]), jnp.int32)
    s8 = jnp.zeros((8, D), F32).at[:B].set(jax.nn.silu(c)).at[B].set(jax.nn.silu(c_ctx))
    first, last = _seq_edge_masks(B, L, Lc)

    for i in range(depth):
        with_ctx = i < depth - 1
        mod = small_matmul_bias(s8, mod_w, mod_b, i)[:B + 1].reshape(B + 1, 1, N_MOD * D)
        sh1, sc1, g1, sh2, sc2, g2 = (mod[:, :, j * D:(j + 1) * D] for j in range(N_MOD))
        w_bf = w_in[i].astype(BF16)
        proj = functools.partial(normmod_matmul, xs, norm1_g[i], sh1, sc1, grp_all)
        hy = short_conv(proj(w_bf[:, :col_hy], 512, F32), hy_conv[i], first, last)
        na = proj(w_bf[:, col_hy:col_na], 512, BF16)
        rw = proj(w_bf[:, col_na:col_rw], 384, F32)
        gates = proj(w_bf[:, col_rw:], 512, F32)

        hy_args = (hy_w1[i], hy_b1[i], hy_w2[i], hy_b2[i], hy_freq[i], hy_w3[i])
        h_raw, ss = hyena_filters_raw(L, *hy_args)
        hr, hi = hyena_filter_spectrum(h_raw, _filter_scale(ss), L)
        z = hyena_conv(hy, 0, hy, 2, hr, hi, 0, hy_bias[i][0], B, L)
        o_a = hyena_conv(hy, 1, z, 0, hr, hi, 1, hy_bias[i][1], B, L).astype(BF16)
        o_b = na_latent(na, na_rpb[i], B, L, Lc)
        r_, v_, kk_, lw_, av_, kd_, gg_ = _rwkv_prep(rw, first, last, rw_shift[i], rw_w0[i], rw_w2[i], rw_a0[i],
                                                     rw_a2[i], rw_g2[i], rw_kk[i], rw_ka[i])
        y = wkv_scan(r_, v_, kk_, lw_, av_, kd_, B, L, Lc)
        o_c = _rwkv_out(y[0] + y[1], r_, kd_, v_, gg_, rw_rk[i], rw_ln_w[i], rw_ln_b[i]).astype(BF16)

        if with_ctx:
            h_raw_c, ss_c = hyena_filters_raw(Lc, *hy_args)
            o_a_c = hyena_small(hy, n_lat // Lc, B, Lc, h_raw_c, _filter_scale(ss_c), hy_bias[i])
            o_a = jnp.concatenate([o_a, o_a_c], axis=0)
            o_b = jnp.concatenate([o_b, ctx_attn(na, B, L, Lc)], axis=0)
            m_rows = n_lat + n_ctx
        else:
            m_rows = n_lat
        grp = grp_all[:m_rows // TM]
        merged = branch_merge(m_rows, o_a, o_b, o_c, gates, proj_a[i].astype(BF16), proj_b[i].astype(BF16),
                              proj_c[i].astype(BF16))
        xs = resid_matmul(merged, w_out[i].astype(BF16), xs, g1, grp)

        h2, e_idx, e_w, rank, counts = route(xs, norm2_g[i], sh2, sc2, grp, router_w[i], router_b[i])
        yg, shared = _moe(h2, e_idx[:TOP_K], rank[:TOP_K], counts[:, 0], exp_gate, exp_up, exp_down, sh_gate, sh_up,
                          sh_down, i)
        xs = moe_combine(xs, yg, e_w.T, shared, g2, grp)

    return rmsnorm_rows(xs, final_g).reshape(B, L, D)
```

```python
import functools
import math

import jax
import jax.numpy as jnp
import numpy as np
from jax import lax
from jax.experimental import pallas as pl
from jax.experimental.pallas import tpu as pltpu

F32 = jnp.float32
BF16 = jnp.bfloat16
HI = lax.Precision.HIGHEST

GRID_W = 64
NORM_EPS = 1e-6
N_MOD = 6
SHORT_CONV = 3
HY_WIDTH = 1024
HY_BANDS = 16
HY_EMB = 2 * HY_BANDS + 1
HY_FILTER_ORDER = 64
HY_FAST_DECAY = 0.3
HY_SLOW_DECAY = 1.5
HY_DECAY_TARGET = 1e-2
NA_HEADS = 16
NA_HEAD_DIM = 64
NA_WIDTH = NA_HEADS * NA_HEAD_DIM
NA_ROWS = 8
NA_COLS = 16
RW_HEADS = 16
RW_HEAD_DIM = 64
RW_WIDTH = RW_HEADS * RW_HEAD_DIM
RW_DECAY_LORA = 64
RW_AAA_LORA = 64
RW_GATE_LORA = 128
RW_GN_EPS = 64e-5
RW_COLS = 3 * RW_WIDTH + 2 * RW_DECAY_LORA + 2 * RW_AAA_LORA + RW_GATE_LORA
RW_SPLITS = [RW_WIDTH, 2 * RW_WIDTH, 3 * RW_WIDTH, 3 * RW_WIDTH + 2 * RW_DECAY_LORA,
             3 * RW_WIDTH + 2 * RW_DECAY_LORA + 2 * RW_AAA_LORA]
N_BRANCH = 3
N_EXPERTS = 64
TOP_K = 6
N_GROUPS = 8
TOPK_GROUPS = 4
ROUTE_SCALE = 2.5

LANES = 128
VMEM_LIMIT = 56 * 1024 * 1024
TM = 512
MOE_BLOCK = 256
WKV_CHUNK = 64
WKV_GROUP = 4


def _cparams(sem):
    return pltpu.CompilerParams(dimension_semantics=sem, vmem_limit_bytes=VMEM_LIMIT)


def _dot(a, b, prec=None):
    return jnp.dot(a, b, preferred_element_type=F32, precision=prec)


def _dot_nt(a, b, prec=None):
    return lax.dot_general(a, b, (((1,), (1,)), ((), ())), preferred_element_type=F32, precision=prec)


def _dot_tn(a, b, prec=None):
    return lax.dot_general(a, b, (((0,), (0,)), ((), ())), preferred_element_type=F32, precision=prec)


def _small_mm_kernel(a_ref, w_ref, b_ref, o_ref):
    o_ref[...] = _dot(a_ref[...], w_ref[0], HI) + b_ref[0]


def small_matmul_bias(a, w, b, layer, tn=1536):
    m, k = a.shape
    n = w.shape[2]
    return pl.pallas_call(
        _small_mm_kernel,
        grid=(n // tn,),
        in_specs=[pl.BlockSpec((m, k), lambda j: (0, 0)),
                  pl.BlockSpec((1, k, tn), lambda j: (layer, 0, j)),
                  pl.BlockSpec((1, 1, tn), lambda j: (layer, 0, j))],
        out_specs=pl.BlockSpec((m, tn), lambda j: (0, j)),
        out_shape=jax.ShapeDtypeStruct((m, n), F32),
        compiler_params=_cparams(("arbitrary",)),
        name="mod_matmul",
    )(a, w, b.reshape(b.shape[0], 1, n))


def _normmod_mm_kernel(grp_ref, x_ref, g_ref, sh_ref, sc_ref, w_ref, o_ref, h_ref, *, hi):
    del grp_ref

    @pl.when(pl.program_id(1) == 0)
    def _():
        x = x_ref[...]
        y = x * lax.rsqrt(jnp.mean(x * x, axis=-1, keepdims=True) + NORM_EPS)
        y = y * g_ref[...]
        h_ref[...] = (y * (1.0 + sc_ref[0]) + sh_ref[0]).astype(h_ref.dtype)

    o_ref[...] = _dot(h_ref[...], w_ref[...], HI if hi else None).astype(o_ref.dtype)


def normmod_matmul(x, g, shift3, scale3, grp, w, tn, out_dtype, hi=False):
    m, d = x.shape
    n = w.shape[1]
    grid_spec = pltpu.PrefetchScalarGridSpec(
        num_scalar_prefetch=1,
        grid=(m // TM, n // tn),
        in_specs=[pl.BlockSpec((TM, d), lambda i, j, grp: (i, 0)),
                  pl.BlockSpec((1, d), lambda i, j, grp: (0, 0)),
                  pl.BlockSpec((1, 1, d), lambda i, j, grp: (grp[i], 0, 0)),
                  pl.BlockSpec((1, 1, d), lambda i, j, grp: (grp[i], 0, 0)),
                  pl.BlockSpec((d, tn), lambda i, j, grp: (0, j))],
        out_specs=pl.BlockSpec((TM, tn), lambda i, j, grp: (i, j)),
        scratch_shapes=[pltpu.VMEM((TM, d), F32 if hi else BF16)],
    )
    return pl.pallas_call(
        functools.partial(_normmod_mm_kernel, hi=hi),
        grid_spec=grid_spec,
        out_shape=jax.ShapeDtypeStruct((m, n), out_dtype),
        compiler_params=_cparams(("arbitrary", "arbitrary")),
        name="normmod_matmul",
    )(grp, x, g.reshape(1, d), shift3, scale3, w)


def _normmod_kernel(grp_ref, x_ref, g_ref, sh_ref, sc_ref, o_ref):
    del grp_ref
    x = x_ref[...]
    y = x * lax.rsqrt(jnp.mean(x * x, axis=-1, keepdims=True) + NORM_EPS)
    o_ref[...] = ((y * g_ref[...]) * (1.0 + sc_ref[0]) + sh_ref[0]).astype(o_ref.dtype)


def normmod(x, g, shift3, scale3, grp, out_dtype):
    m, d = x.shape
    grid_spec = pltpu.PrefetchScalarGridSpec(
        num_scalar_prefetch=1,
        grid=(m // TM,),
        in_specs=[pl.BlockSpec((TM, d), lambda i, grp: (i, 0)),
                  pl.BlockSpec((1, d), lambda i, grp: (0, 0)),
                  pl.BlockSpec((1, 1, d), lambda i, grp: (grp[i], 0, 0)),
                  pl.BlockSpec((1, 1, d), lambda i, grp: (grp[i], 0, 0))],
        out_specs=pl.BlockSpec((TM, d), lambda i, grp: (i, 0)),
    )
    return pl.pallas_call(
        _normmod_kernel,
        grid_spec=grid_spec,
        out_shape=jax.ShapeDtypeStruct((m, d), out_dtype),
        compiler_params=_cparams(("arbitrary",)),
        name="normmod",
    )(grp, x, g.reshape(1, d), shift3, scale3)


def _merge_kernel(oa_ref, ob_ref, oc_ref, ga_ref, gb_ref, gc_ref, pa_ref, pb_ref, pc_ref, o_ref):
    gate = lambda ref: jax.nn.sigmoid(ref[...].astype(F32))
    m = gate(ga_ref) * _dot(oa_ref[...], pa_ref[...])
    m = m + gate(gb_ref) * _dot(ob_ref[...], pb_ref[...])
    m = m + gate(gc_ref) * _dot(oc_ref[...], pc_ref[...])
    o_ref[...] = m.astype(o_ref.dtype)


def branch_merge(m, o_a, o_b, o_c, gates, pa, pb, pc, tn=512):
    k = o_a.shape[1]
    d = pa.shape[1]
    nj = d // tn
    o_spec = pl.BlockSpec((TM, k), lambda i, j: (i, 0))
    p_spec = pl.BlockSpec((k, tn), lambda i, j: (0, j))
    return pl.pallas_call(
        _merge_kernel,
        grid=(m // TM, nj),
        in_specs=[o_spec, o_spec, o_spec,
                  pl.BlockSpec((TM, tn), lambda i, j: (i, j)),
                  pl.BlockSpec((TM, tn), lambda i, j: (i, j + nj)),
                  pl.BlockSpec((TM, tn), lambda i, j: (i, j + 2 * nj)),
                  p_spec, p_spec, p_spec],
        out_specs=pl.BlockSpec((TM, tn), lambda i, j: (i, j)),
        out_shape=jax.ShapeDtypeStruct((m, d), BF16),
        compiler_params=_cparams(("arbitrary", "arbitrary")),
        name="branch_merge",
    )(o_a, o_b, o_c, gates, gates, gates, pa, pb, pc)


def _resid_mm_kernel(grp_ref, a_ref, w_ref, x_ref, gate_ref, o_ref):
    del grp_ref
    o_ref[...] = x_ref[...] + gate_ref[0] * _dot(a_ref[...], w_ref[...])


def resid_matmul(a, w, x, gate3, grp, tn=512):
    m, k = a.shape
    d = w.shape[1]
    grid_spec = pltpu.PrefetchScalarGridSpec(
        num_scalar_prefetch=1,
        grid=(m // TM, d // tn),
        in_specs=[pl.BlockSpec((TM, k), lambda i, j, grp: (i, 0)),
                  pl.BlockSpec((k, tn), lambda i, j, grp: (0, j)),
                  pl.BlockSpec((TM, tn), lambda i, j, grp: (i, j)),
                  pl.BlockSpec((1, 1, tn), lambda i, j, grp: (grp[i], 0, j))],
        out_specs=pl.BlockSpec((TM, tn), lambda i, j, grp: (i, j)),
    )
    return pl.pallas_call(
        _resid_mm_kernel,
        grid_spec=grid_spec,
        out_shape=jax.ShapeDtypeStruct((m, d), F32),
        compiler_params=_cparams(("arbitrary", "arbitrary")),
        name="resid_matmul",
    )(grp, a, w, x, gate3)


def _short_conv_kernel(x_ref, prev_ref, next_ref, w_ref, first_ref, last_ref, o_ref):
    x = x_ref[...].astype(F32)
    tm = x.shape[0]
    row = lax.broadcasted_iota(jnp.int32, x.shape, 0)
    halo = prev_ref.shape[0]
    before = prev_ref[...].astype(F32)[halo - 1:halo, :]
    after = next_ref[...].astype(F32)[0:1, :]
    prev = jnp.where(row == 0, before, pltpu.roll(x, 1, 0)) * first_ref[...]
    nxt = jnp.where(row == tm - 1, after, pltpu.roll(x, tm - 1, 0)) * last_ref[...]
    o_ref[...] = prev * w_ref[0:1, :] + x * w_ref[1:2, :] + nxt * w_ref[2:3, :]


def short_conv(u, w, first, last, tn):
    assert SHORT_CONV == 3
    m, c = u.shape
    halo = 16
    per = TM // halo
    n_halo = m // halo
    return pl.pallas_call(
        _short_conv_kernel,
        grid=(m // TM, c // tn),
        in_specs=[pl.BlockSpec((TM, tn), lambda i, j: (i, j)),
                  pl.BlockSpec((halo, tn), lambda i, j: (jnp.maximum(i * per - 1, 0), j)),
                  pl.BlockSpec((halo, tn), lambda i, j: (jnp.minimum((i + 1) * per, n_halo - 1), j)),
                  pl.BlockSpec((SHORT_CONV, tn), lambda i, j: (0, j)),
                  pl.BlockSpec((TM, 1), lambda i, j: (i, 0)),
                  pl.BlockSpec((TM, 1), lambda i, j: (i, 0))],
        out_specs=pl.BlockSpec((TM, tn), lambda i, j: (i, j)),
        out_shape=jax.ShapeDtypeStruct((m, c), F32),
        compiler_params=_cparams(("arbitrary", "arbitrary")),
        name="short_conv",
    )(u, u, u, w, first, last)


def _rmsnorm_kernel(x_ref, g_ref, o_ref):
    x = x_ref[...]
    y = x * lax.rsqrt(jnp.mean(x * x, axis=-1, keepdims=True) + NORM_EPS)
    o_ref[...] = y * g_ref[...]


def rmsnorm_rows(x, g):
    m, d = x.shape
    return pl.pallas_call(
        _rmsnorm_kernel,
        grid=(m // TM,),
        in_specs=[pl.BlockSpec((TM, d), lambda i: (i, 0)), pl.BlockSpec((1, d), lambda i: (0, 0))],
        out_specs=pl.BlockSpec((TM, d), lambda i: (i, 0)),
        out_shape=jax.ShapeDtypeStruct((m, d), F32),
        compiler_params=_cparams(("arbitrary",)),
        name="final_rmsnorm",
    )(x, g.reshape(1, d))


def _hyfilt_kernel(z_ref, w1_ref, b1_ref, w2_ref, b2_ref, fr_ref, w3_ref, dl_ref, h_ref, ss_ref):
    z = z_ref[...]
    hdn = jnp.sin(fr_ref[0:1, :] * (_dot(z, w1_ref[...], HI) + b1_ref[...]))
    hdn = jnp.sin(fr_ref[1:2, :] * (_dot(hdn, w2_ref[...], HI) + b2_ref[...]))
    h = _dot(hdn, w3_ref[...], HI)
    h = h * jnp.exp(-z[:, 0:1] * dl_ref[...])
    h_ref[...] = h

    @pl.when(pl.program_id(0) == 0)
    def _():
        ss_ref[...] = jnp.zeros_like(ss_ref)

    ss_ref[...] += jnp.sum(h * h, axis=0, keepdims=True)


def hyena_filters_raw(L, w1, b1, w2, b2, freq, w3):
    t = np.linspace(0.0, 1.0, L, dtype=np.float32)[:, None]
    omega = np.float32(2.0 * math.pi / L) * np.arange(L, dtype=np.float32)[:, None]
    bands = np.linspace(1e-4, HY_BANDS - 1, HY_BANDS, dtype=np.float32)[None, :]
    z = np.concatenate([t, np.cos(omega * bands), -np.sin(omega * bands),
                        np.zeros((L, HY_FILTER_ORDER - HY_EMB), np.float32)], axis=-1).astype(np.float32)
    w1p = jnp.concatenate([w1, jnp.zeros((HY_FILTER_ORDER - HY_EMB, HY_FILTER_ORDER), F32)], axis=0)
    deltas = np.abs(np.linspace(math.log(HY_DECAY_TARGET) / HY_SLOW_DECAY,
                                math.log(HY_DECAY_TARGET) / HY_FAST_DECAY, HY_WIDTH, dtype=np.float32))
    dl4 = np.tile(deltas, 4)[None, :]
    tl = min(L, 256)
    n = 4 * HY_WIDTH
    fo = HY_FILTER_ORDER
    full = lambda shape: pl.BlockSpec(shape, lambda i: (0, 0))
    return pl.pallas_call(
        _hyfilt_kernel,
        grid=(L // tl,),
        in_specs=[pl.BlockSpec((tl, fo), lambda i: (i, 0)), full((fo, fo)), full((1, fo)), full((fo, fo)),
                  full((1, fo)), full((2, fo)), full((fo, n)), full((1, n))],
        out_specs=[pl.BlockSpec((tl, n), lambda i: (i, 0)), full((1, n))],
        out_shape=[jax.ShapeDtypeStruct((L, n), F32), jax.ShapeDtypeStruct((1, n), F32)],
        compiler_params=_cparams(("arbitrary",)),
        name="hyena_filters",
    )(jnp.asarray(z), w1p, b1.reshape(1, fo), w2, b2.reshape(1, fo), freq, w3, jnp.asarray(dl4))


def _filter_scale(ss):
    s = ss.reshape(2, 2, HY_WIDTH)
    rs = lax.rsqrt(jnp.sum(s, axis=1, keepdims=True))
    return jnp.broadcast_to(rs, (2, 2, HY_WIDTH)).reshape(1, 4 * HY_WIDTH)


FFT_N1 = 128
FFT_N2 = 64
FFT_PITCH = 72
FFT_BATCH = 4


@functools.lru_cache(maxsize=None)
def _fft_tables():
    n1, n2 = FFT_N1, FFT_N2
    n = n1 * n2
    a = np.arange(n1 // 2)[None, None, :]
    k1 = np.arange(n1)[None, :, None]
    b = np.arange(n2)[:, None, None]
    theta = 2.0 * np.pi * ((a * k1 % n1) / n1 + (b * k1) / n)
    g = np.concatenate([np.cos(theta), -np.sin(theta)], axis=1)
    ig = np.concatenate([np.cos(theta), -np.sin(theta)], axis=1).transpose(0, 2, 1) / n
    k2 = np.arange(n2)[:, None]
    bb = np.arange(n2)[None, :]
    ph = 2.0 * np.pi * (k2 * bb % n2) / n2
    fr, fi = np.cos(ph), -np.sin(ph)
    f2 = np.block([[fr, -fi], [fi, fr]])
    if2 = np.block([[fr, fi], [-fi, fr]])
    return (jnp.asarray(g, BF16), jnp.asarray(f2, BF16), jnp.asarray(if2, BF16), jnp.asarray(ig, BF16))


def _fft_stage1(u_ref, g_ref, sr_ref, si_ref):
    n1, n2, p = FFT_N1, FFT_N2, FFT_PITCH

    def body(i, carry):
        bs = [i * FFT_BATCH + j for j in range(FFT_BATCH)]
        xs = [u_ref[pl.ds(b, n1 // 2, stride=n2), :].astype(BF16) for b in bs]
        outs = [_dot(g_ref[b], x) for b, x in zip(bs, xs)]
        for b, a in zip(bs, outs):
            sr_ref[pl.ds(b, n1, stride=p), :] = a[:n1]
            si_ref[pl.ds(b, n1, stride=p), :] = a[n1:]
        return carry

    lax.fori_loop(0, n2 // FFT_BATCH, body, 0)


def _bin_rows(k1):
    return pl.ds(pl.multiple_of(k1 * FFT_PITCH, 8), FFT_N2)


def _hyconv_kernel(xm_ref, u_ref, hr_ref, hi_ref, bias_ref, g_ref, f2_ref, if2_ref, ig_ref, o_ref, sr_ref, si_ref):
    n1, n2, p = FFT_N1, FFT_N2, FFT_PITCH
    _fft_stage1(u_ref, g_ref, sr_ref, si_ref)

    def pair(ref, k):
        return jnp.concatenate([ref[_bin_rows(2 * k), :], ref[_bin_rows(2 * k + 1), :]], axis=1)

    def pair_h(ref, k):
        blk = ref[pl.ds(pl.multiple_of(k * 2 * n2, 2 * n2), 2 * n2), :]
        return jnp.concatenate([blk[:n2], blk[n2:]], axis=1)

    def unpair(ref, k, val):
        w = val.shape[1] // 2
        ref[_bin_rows(2 * k), :] = val[:, :w]
        ref[_bin_rows(2 * k + 1), :] = val[:, w:]

    def stage2(i, carry):
        ks = [i * FFT_BATCH + j for j in range(FFT_BATCH)]
        zs = [jnp.concatenate([pair(sr_ref, k), pair(si_ref, k)], axis=0).astype(BF16) for k in ks]
        xs = [_dot(f2_ref[...], z) for z in zs]
        ys = []
        for k, x in zip(ks, xs):
            xr, xi = x[:n2], x[n2:]
            hr, hi = pair_h(hr_ref, k), pair_h(hi_ref, k)
            ys.append(jnp.concatenate([xr * hr - xi * hi, xr * hi + xi * hr], axis=0).astype(BF16))
        bbs = [_dot(if2_ref[...], y) for y in ys]
        for k, bb in zip(ks, bbs):
            unpair(sr_ref, k, bb[:n2])
            unpair(si_ref, k, bb[n2:])
        return carry

    lax.fori_loop(0, n1 // 2 // FFT_BATCH, stage2, 0)

    def stage3(i, carry):
        bs = [i * FFT_BATCH + j for j in range(FFT_BATCH)]
        sts = [jnp.concatenate([sr_ref[pl.ds(b, n1, stride=p), :], si_ref[pl.ds(b, n1, stride=p), :]],
                               axis=0).astype(BF16) for b in bs]
        outs = [_dot(ig_ref[b], st) for b, st in zip(bs, sts)]
        for b, o in zip(bs, outs):
            o_ref[pl.ds(b, n1 // 2, stride=n2), :] = o
        return carry

    lax.fori_loop(0, n2 // FFT_BATCH, stage3, 0)
    bias = bias_ref[...]
    rows_per_pass = 512

    def finish(i, carry):
        rows = pl.ds(pl.multiple_of(i * rows_per_pass, rows_per_pass), rows_per_pass)
        o_ref[rows, :] = xm_ref[rows, :] * (o_ref[rows, :] + u_ref[rows, :] * bias)
        return carry

    lax.fori_loop(0, o_ref.shape[0] // rows_per_pass, finish, 0)


def hyena_conv(xm_arr, xm_col, u_arr, u_col, hr, hi, h_col, bias, n_batch, L):
    assert L == FFT_N1 * FFT_N2 // 2
    cb = LANES
    nct = HY_WIDTH // cb
    n = 2 * L
    g, f2, if2, ig = _fft_tables()
    const3 = lambda shape: pl.BlockSpec(shape, lambda b, c: (0, 0, 0))
    const2 = lambda shape: pl.BlockSpec(shape, lambda b, c: (0, 0))
    return pl.pallas_call(
        _hyconv_kernel,
        grid=(n_batch, nct),
        in_specs=[pl.BlockSpec((L, cb), lambda b, c: (b, xm_col * nct + c)),
                  pl.BlockSpec((L, cb), lambda b, c: (b, u_col * nct + c)),
                  pl.BlockSpec((n, cb), lambda b, c: (0, h_col * nct + c)),
                  pl.BlockSpec((n, cb), lambda b, c: (0, h_col * nct + c)),
                  pl.BlockSpec((1, cb), lambda b, c: (0, c)),
                  const3(g.shape), const2(f2.shape), const2(if2.shape), const3(ig.shape)],
        out_specs=pl.BlockSpec((L, cb), lambda b, c: (b, c)),
        out_shape=jax.ShapeDtypeStruct((n_batch * L, HY_WIDTH), F32),
        scratch_shapes=[pltpu.VMEM((FFT_N1 * FFT_PITCH, cb), F32)] * 2,
        compiler_params=_cparams(("arbitrary", "arbitrary")),
        name="hyena_conv",
    )(xm_arr, u_arr, hr, hi, bias.reshape(1, HY_WIDTH), g, f2, if2, ig)


def _hyspec_kernel(h0_ref, h1_ref, rs_ref, g_ref, f2_ref, hr_ref, hi_ref, s0r, s0i, s1r, s1i):
    n1, n2 = FFT_N1, FFT_N2
    _fft_stage1(h0_ref, g_ref, s0r, s0i)
    _fft_stage1(h1_ref, g_ref, s1r, s1i)
    rs = rs_ref[...]
    h10 = h1_ref[0:1, :]

    def stage2(i, carry):
        ks = [i * FFT_BATCH + j for j in range(FFT_BATCH)]
        zs = [jnp.concatenate([jnp.concatenate([s0r[_bin_rows(k), :], s1r[_bin_rows(k), :]], axis=1),
                               jnp.concatenate([s0i[_bin_rows(k), :], s1i[_bin_rows(k), :]], axis=1)],
                              axis=0).astype(BF16) for k in ks]
        xs = [_dot(f2_ref[...], z) for z in zs]
        for k, x in zip(ks, xs):
            rows = pl.ds(pl.multiple_of(k * n2, n2), n2)
            w = x.shape[1] // 2
            hr_ref[rows, :] = rs * (x[:n2, :w] + x[:n2, w:] - h10)
            hi_ref[rows, :] = rs * (x[n2:, :w] - x[n2:, w:])
        return carry

    lax.fori_loop(0, n1 // FFT_BATCH, stage2, 0)


def hyena_filter_spectrum(h_raw, rs, L):
    assert L == FFT_N1 * FFT_N2 // 2
    cb = LANES
    nct = HY_WIDTH // cb
    n = 2 * L
    g, f2, _, _ = _fft_tables()
    out_spec = pl.BlockSpec((n, cb), lambda o, c: (0, o * nct + c))
    scr = pltpu.VMEM((FFT_N1 * FFT_PITCH, cb), F32)
    return pl.pallas_call(
        _hyspec_kernel,
        grid=(2, nct),
        in_specs=[pl.BlockSpec((L, cb), lambda o, c: (0, (2 * o) * nct + c)),
                  pl.BlockSpec((L, cb), lambda o, c: (0, (2 * o + 1) * nct + c)),
                  pl.BlockSpec((1, cb), lambda o, c: (0, (2 * o) * nct + c)),
                  pl.BlockSpec(g.shape, lambda o, c: (0, 0, 0)),
                  pl.BlockSpec(f2.shape, lambda o, c: (0, 0))],
        out_specs=[out_spec, out_spec],
        out_shape=[jax.ShapeDtypeStruct((n, 2 * HY_WIDTH), F32)] * 2,
        scratch_shapes=[scr, scr, scr, scr],
        compiler_params=_cparams(("arbitrary", "arbitrary")),
        name="hyena_filter_spectrum",
    )(h_raw, h_raw, rs, g, f2)


@functools.lru_cache(maxsize=None)
def _dense_dft_tables(L):
    n = 2 * L
    k = np.arange(n)[:, None]
    t = np.arange(L)[None, :]
    ph = 2.0 * np.pi * (k * t % n) / n
    fwd = np.concatenate([np.cos(ph), -np.sin(ph)], axis=0)
    inv = np.concatenate([np.cos(ph), -np.sin(ph)], axis=0).T / n
    return jnp.asarray(fwd, F32), jnp.asarray(inv, F32)


def _hyena_small_kernel(x1_ref, x2_ref, v_ref, h_ref0a, h_ref0b, h_ref1a, h_ref1b, rs0_ref, rs1_ref,
                        b0_ref, b1_ref, fwd_ref, inv_ref, o_ref, *, L):
    n = 2 * L
    fwd = fwd_ref[...]
    inv = inv_ref[...]

    def conv(u, ha_ref, hb_ref, rs_ref, bias_ref):
        ha, hb = ha_ref[...], hb_ref[...]
        ka = _dot(fwd, ha, HI)
        kb = _dot(fwd, hb, HI)
        rs = rs_ref[...]
        kr = rs * (ka[:n] + kb[:n] - hb[0:1, :])
        ki = rs * (ka[n:] - kb[n:])
        uf = _dot(fwd, u, HI)
        ur, ui = uf[:n], uf[n:]
        y = jnp.concatenate([ur * kr - ui * ki, ur * ki + ui * kr], axis=0)
        return _dot(inv, y, HI) + u * bias_ref[...]

    v = v_ref[...]
    z = x1_ref[...] * conv(v, h_ref0a, h_ref0b, rs0_ref, b0_ref)
    o_ref[...] = (x2_ref[...] * conv(z, h_ref1a, h_ref1b, rs1_ref, b1_ref)).astype(o_ref.dtype)


def hyena_small(u_arr, row0_blocks, n_batch, L, h_raw, rs, bias):
    cb = LANES
    nct = HY_WIDTH // cb
    fwd, inv = _dense_dft_tables(L)
    uspec = lambda col: pl.BlockSpec((L, cb), lambda b, c: (row0_blocks + b, col * nct + c))
    hspec = lambda col: pl.BlockSpec((L, cb), lambda b, c: (0, col * nct + c))
    rspec = lambda col: pl.BlockSpec((1, cb), lambda b, c: (0, col * nct + c))
    bspec = pl.BlockSpec((1, cb), lambda b, c: (0, c))
    bias_0 = bias[0].reshape(1, HY_WIDTH)
    bias_1 = bias[1].reshape(1, HY_WIDTH)
    return pl.pallas_call(
        functools.partial(_hyena_small_kernel, L=L),
        grid=(n_batch, nct),
        in_specs=[uspec(0), uspec(1), uspec(2), hspec(0), hspec(1), hspec(2), hspec(3), rspec(0), rspec(2),
                  bspec, bspec,
                  pl.BlockSpec(fwd.shape, lambda b, c: (0, 0)), pl.BlockSpec(inv.shape, lambda b, c: (0, 0))],
        out_specs=pl.BlockSpec((L, cb), lambda b, c: (b, c)),
        out_shape=jax.ShapeDtypeStruct((n_batch * L, HY_WIDTH), BF16),
        compiler_params=_cparams(("arbitrary", "arbitrary")),
        name="hyena_ctx",
    )(u_arr, u_arr, u_arr, h_raw, h_raw, h_raw, h_raw, rs, rs, bias_0, bias_1, fwd, inv)


def _na_bias_table(rpb):
    cols = np.arange(GRID_W)
    col_start = np.clip(cols - NA_COLS // 2, 0, GRID_W - NA_COLS)[:, None]
    in_win = (cols[None, :] >= col_start) & (cols[None, :] < col_start + NA_COLS)
    rel_col = np.clip(cols[None, :] - cols[:, None], 1 - NA_COLS, NA_COLS - 1) + NA_COLS - 1
    tbl = rpb.astype(F32)[:, :, rel_col]
    tbl = jnp.where(jnp.asarray(in_win)[None, None], tbl, -jnp.inf)
    return jnp.concatenate([tbl[:, :-1], tbl[:, 1:]], axis=-1)


def _na_kernel(*refs, n_rows):
    q_ref = refs[0]
    k_refs = refs[1:1 + NA_ROWS]
    v_refs = refs[1 + NA_ROWS:1 + 2 * NA_ROWS]
    kc_ref, vc_ref, tbl_ref, o_ref = refs[1 + 2 * NA_ROWS:]
    r = pl.program_id(1)
    start = jnp.clip(r - NA_ROWS // 2, 0, n_rows - NA_ROWS)
    d0 = start - r + NA_ROWS - 1
    dh = NA_HEAD_DIM
    q = q_ref[...] * (dh ** -0.5)
    n_pairs = NA_ROWS // 2
    heads = [slice(h * dh, (h + 1) * dh) for h in range(NA_HEADS)]
    scores = []
    for h, hs in enumerate(heads):
        qh = q[:, hs]
        tiles = [_dot_nt(qh, jnp.concatenate([k_refs[2 * p][:, hs], k_refs[2 * p + 1][:, hs]], axis=0))
                 + tbl_ref[h, d0 + 2 * p] for p in range(n_pairs)]
        scores.append(tiles + [_dot_nt(qh, kc_ref[:, hs])])
    probs, denoms = [], []
    for tiles in scores:
        m = tiles[0].max(axis=-1, keepdims=True)
        for s in tiles[1:]:
            m = jnp.maximum(m, s.max(axis=-1, keepdims=True))
        ps = [jnp.exp(s - m) for s in tiles]
        l = ps[0].sum(axis=-1, keepdims=True)
        for p_ in ps[1:]:
            l = l + p_.sum(axis=-1, keepdims=True)
        probs.append([p_.astype(BF16) for p_ in ps])
        denoms.append(l)
    for hs, ps, l in zip(heads, probs, denoms):
        acc = _dot(ps[-1], vc_ref[:, hs])
        for p in range(n_pairs):
            acc = acc + _dot(ps[p], jnp.concatenate([v_refs[2 * p][:, hs], v_refs[2 * p + 1][:, hs]], axis=0))
        o_ref[:, hs] = (acc / l).astype(o_ref.dtype)


def na_latent(na, rpb, n_batch, L, Lc):
    n_rows = L // GRID_W
    assert n_rows >= NA_ROWS
    tbl = _na_bias_table(rpb)
    w = NA_WIDTH
    ctx_blk0 = n_batch * L // Lc

    def kv_spec(i, col):
        def imap(b, r):
            start = jnp.clip(r - NA_ROWS // 2, 0, n_rows - NA_ROWS)
            return (b * n_rows + start + i, col)
        return pl.BlockSpec((GRID_W, w), imap)

    in_specs = ([pl.BlockSpec((GRID_W, w), lambda b, r: (b * n_rows + r, 0))]
                + [kv_spec(i, 1) for i in range(NA_ROWS)] + [kv_spec(i, 2) for i in range(NA_ROWS)]
                + [pl.BlockSpec((Lc, w), lambda b, r: (ctx_blk0 + b, 1)),
                   pl.BlockSpec((Lc, w), lambda b, r: (ctx_blk0 + b, 2)),
                   pl.BlockSpec(tbl.shape, lambda b, r: (0, 0, 0, 0))])
    return pl.pallas_call(
        functools.partial(_na_kernel, n_rows=n_rows),
        grid=(n_batch, n_rows),
        in_specs=in_specs,
        out_specs=pl.BlockSpec((GRID_W, w), lambda b, r: (b * n_rows + r, 0)),
        out_shape=jax.ShapeDtypeStruct((n_batch * L, w), BF16),
        compiler_params=_cparams(("arbitrary", "arbitrary")),
        name="na_latent",
    )(*([na] * (3 + 2 * NA_ROWS)), tbl)


def _ctx_attn_kernel(q_ref, k_ref, v_ref, o_ref):
    dh = NA_HEAD_DIM
    q = q_ref[...] * (dh ** -0.5)
    for h in range(NA_HEADS):
        hs = slice(h * dh, (h + 1) * dh)
        s = _dot_nt(q[:, hs], k_ref[:, hs])
        p_ = jnp.exp(s - s.max(axis=-1, keepdims=True))
        acc = _dot(p_.astype(BF16), v_ref[:, hs])
        o_ref[:, hs] = (acc / p_.sum(axis=-1, keepdims=True)).astype(o_ref.dtype)


def ctx_attn(na, n_batch, L, Lc):
    w = NA_WIDTH
    blk0 = n_batch * L // Lc
    spec = lambda col: pl.BlockSpec((Lc, w), lambda b: (blk0 + b, col))
    return pl.pallas_call(
        _ctx_attn_kernel,
        grid=(n_batch,),
        in_specs=[spec(0), spec(1), spec(2)],
        out_specs=pl.BlockSpec((Lc, w), lambda b: (b, 0)),
        out_shape=jax.ShapeDtypeStruct((n_batch * Lc, w), BF16),
        compiler_params=_cparams(("arbitrary",)),
        name="ctx_attn",
    )(na, na, na)


@functools.lru_cache(maxsize=None)
def _wkv_masks():
    c, g = WKV_CHUNK, WKV_GROUP
    t = np.arange(c)[:, None]
    s = np.arange(c)[None, :]
    tinc = np.stack([(s <= t), (s >= t)]).astype(np.float32)
    strict = np.stack([(s < t), (s > t)]).astype(np.float32)
    tile = lambda m: np.tile(m, (1,) * (m.ndim - 1) + (g,))
    blk = lambda n: (t // n == s // n)
    blk16 = tile(blk(16).astype(np.float32))
    off32 = tile((blk(32) & ~blk(16)).astype(np.float32))
    off64 = tile((~blk(32)).astype(np.float32))
    eye = tile((t == s).astype(np.float32))
    rr = np.arange(g * c)
    hm = (rr[:, None] // c == np.arange(g * RW_HEAD_DIM)[None, :] // RW_HEAD_DIM).astype(np.float32)
    masks = tuple(jnp.asarray(m) for m in (tinc, tile(strict), tile(tinc), blk16, off32, off64, eye, hm))
    return masks + (jnp.asarray(hm, BF16),)


def _wkv_kernel(*refs):
    (r0_ref, v0_ref, kk0_ref, r1_ref, v1_ref, kk1_ref, lw0_ref, av0_ref, kd0_ref, lw1_ref, av1_ref, kd1_ref,
     tinc_ref, strict_ref, incl_ref, blk16_ref, off32_ref, off64_ref, eye_ref, hm_ref, hmb_ref,
     y0_ref, y1_ref, state_ref) = refs
    c, g = WKV_CHUNK, WKV_GROUP
    gw = g * RW_HEAD_DIM

    @pl.when(pl.program_id(1) == 0)
    def _():
        state_ref[...] = jnp.zeros_like(state_ref)

    hm = hm_ref[...]
    hm_bf = hmb_ref[...]
    blk16, off32, off64, eye = blk16_ref[...], off32_ref[...], off64_ref[...], eye_ref[...]

    def bdiag(z):
        return jnp.concatenate([z.astype(BF16)] * g, axis=0) * hm_bf

    def pm(x4, zd):
        return _dot(x4.astype(BF16), zd)

    def prepare(d, r_ref, kk_ref, lw_ref, av_ref, kd_ref):
        lw = lw_ref[0]
        cum = _dot(tinc_ref[d], lw, HI)
        tot = jnp.sum(lw, axis=0, keepdims=True)
        e_neg = jnp.exp(-cum)
        e_rem = jnp.exp(tot - cum)
        kk = kk_ref[...]
        b_vec = kk * av_ref[0]
        kd = kd_ref[0]
        return dict(at=-kk * jnp.exp(cum - lw), rt=r_ref[...] * jnp.exp(cum), bt=b_vec * e_neg, kt=kd * e_neg,
                    bp=b_vec * e_rem, kp=kd * e_rem, e_tot=jnp.exp(tot))

    qs = (prepare(0, r0_ref, kk0_ref, lw0_ref, av0_ref, kd0_ref), prepare(1, r1_ref, kk1_ref, lw1_ref, av1_ref, kd1_ref))
    v_refs, y_refs = (v0_ref, v1_ref), (y0_ref, y1_ref)
    chains = [(d, gi) for gi in range(RW_HEADS // g) for d in range(2)]
    sl = lambda gi: slice(gi * gw, (gi + 1) * gw)
    each = lambda f, *lists: [f(*args) for args in zip(*lists)]
    pm_all = lambda xs, zs: each(lambda x, z: pm(x, bdiag(z)), xs, zs)

    ar = [jnp.concatenate([qs[d]["at"][:, sl(gi)], qs[d]["rt"][:, sl(gi)]], axis=0).astype(BF16) for d, gi in chains]
    pb = [_dot_nt(a, bdiag(qs[d]["bt"][:, sl(gi)])) for a, (d, gi) in zip(ar, chains)]
    pk = [_dot_nt(a, bdiag(qs[d]["kt"][:, sl(gi)])) for a, (d, gi) in zip(ar, chains)]
    a_ab = [p[:c] * strict_ref[d] for p, (d, gi) in zip(pb, chains)]
    a_rb = [p[c:] * incl_ref[d] for p, (d, gi) in zip(pb, chains)]
    a_ak = [p[:c] * strict_ref[d] for p, (d, gi) in zip(pk, chains)]
    a_rk = [p[c:] * incl_ref[d] for p, (d, gi) in zip(pk, chains)]
    ad = [a * blk16 for a in a_ab]
    a2 = pm_all(ad, ad)
    a4 = pm_all(a2, a2)
    a8 = pm_all(a4, a4)
    tinv = [eye + a for a in ad]
    for powr in (a2, a4, a8):
        tinv = each(lambda t, p_: t + p_, tinv, pm_all(tinv, powr))
    for off in (off32, off64):
        mid = pm_all(tinv, [a * off for a in a_ab])
        tinv = each(lambda t, p_: t + p_, tinv, pm_all(mid, tinv))
    s0 = [state_ref[d, gi] for d, gi in chains]
    vv = [v_refs[d][:, sl(gi)] for d, gi in chains]
    vd = [bdiag(v_) for v_ in vv]
    ars = each(lambda a, s_: _dot_nt(a, s_.astype(BF16)), ar, s0)
    akv = each(pm, a_ak, vd)
    u = pm_all(tinv, each(lambda x, y_: x[:c] + y_, ars, akv))
    yu = pm_all(a_rb, u)
    yv = each(pm, a_rk, vd)
    for (d, gi), x, y1_, y2_ in zip(chains, ars, yu, yv):
        y_refs[d][:, sl(gi)] = x[c:] + y1_ + y2_
    upd = [_dot_tn(jnp.concatenate([u_, v_], axis=0).astype(BF16),
                   jnp.concatenate([qs[d]["bp"][:, sl(gi)], qs[d]["kp"][:, sl(gi)]], axis=0).astype(BF16))
           for u_, v_, (d, gi) in zip(u, vv, chains)]
    for (d, gi), s_, up in zip(chains, s0, upd):
        state_ref[d, gi] = s_ * qs[d]["e_tot"][:, sl(gi)] + hm * up


def wkv_scan(r, v, kk, lw, av, kd, n_batch, L, Lc):
    c = WKV_CHUNK
    rows, w = r.shape
    nc, nl = Lc // c, L // c
    masks = _wkv_masks()

    def blk(d, b, s):
        j_ctx = s if d == 0 else nc - 1 - s
        j_lat = s - nc if d == 0 else nl - 1 - (s - nc)
        return jnp.where(s < nc, (n_batch * L + b * Lc) // c + j_ctx, (b * L) // c + j_lat)

    shared = lambda d: pl.BlockSpec((c, w), lambda b, s: (blk(d, b, s), 0))
    perdir = lambda d: pl.BlockSpec((1, c, w), lambda b, s: (d, blk(d, b, s), 0))
    full = lambda m: pl.BlockSpec(m.shape, lambda b, s: (0,) * m.ndim)
    gw = WKV_GROUP * RW_HEAD_DIM
    return pl.pallas_call(
        _wkv_kernel,
        grid=(n_batch, nc + nl),
        in_specs=[shared(0)] * 3 + [shared(1)] * 3 + [perdir(0)] * 3 + [perdir(1)] * 3 + [full(m) for m in masks],
        out_specs=[shared(0), shared(1)],
        out_shape=[jax.ShapeDtypeStruct((rows, w), F32)] * 2,
        scratch_shapes=[pltpu.VMEM((2, RW_HEADS // WKV_GROUP, gw, gw), F32)],
        compiler_params=_cparams(("arbitrary", "arbitrary")),
        name="wkv_scan",
    )(r, v, kk, r, v, kk, lw, av, kd, lw, av, kd, *masks)


def _pack_bf16_pairs(h):
    half = h.shape[1] // 2
    bits = lambda t: lax.bitcast_convert_type(t.astype(BF16).astype(F32), jnp.uint32)
    return (bits(h[:, :half]) >> 16) | (bits(h[:, half:]) & jnp.uint32(0xFFFF0000))


def _unpack_bf16_pairs(p):
    lo = lax.bitcast_convert_type(p << 16, F32).astype(BF16)
    hi = lax.bitcast_convert_type(p & jnp.uint32(0xFFFF0000), F32).astype(BF16)
    return jnp.concatenate([lo, hi], axis=1)


GATHER_ROWS = 1024


def _gather_kernel(idx_ref, tab_ref, out_ref, sem):
    base = pl.program_id(0) * GATHER_ROWS

    def issue(r, carry):
        tok = idx_ref[0, 0, r]
        pltpu.make_async_copy(tab_ref.at[pl.ds(tok, 1)], out_ref.at[pl.ds(base + r, 1)], sem).start()
        return carry

    lax.fori_loop(0, GATHER_ROWS, issue, 0, unroll=8)
    pltpu.make_async_copy(tab_ref.at[pl.ds(0, GATHER_ROWS)], out_ref.at[pl.ds(base, GATHER_ROWS)], sem).wait()


def gather_rows(table, idx):
    n = idx.shape[0]
    w = table.shape[1]
    assert n % GATHER_ROWS == 0 and table.shape[0] >= GATHER_ROWS
    nb = n // GATHER_ROWS
    return pl.pallas_call(
        _gather_kernel,
        grid=(nb,),
        in_specs=[pl.BlockSpec((1, 1, GATHER_ROWS), lambda i: (i, 0, 0), memory_space=pltpu.SMEM),
                  pl.BlockSpec(memory_space=pl.ANY)],
        out_specs=pl.BlockSpec(memory_space=pl.ANY),
        out_shape=jax.ShapeDtypeStruct((n, w), table.dtype),
        scratch_shapes=[pltpu.SemaphoreType.DMA],
        compiler_params=pltpu.CompilerParams(dimension_semantics=("arbitrary",), disable_bounds_checks=True),
        name="dispatch_gather",
    )(idx.reshape(nb, 1, GATHER_ROWS), table)


def _moe_kernel(be_ref, nb_ref, x_ref, wg_ref, wu_ref, wd_ref, o_ref, wg_s, wu_s, wd_s):
    i = pl.program_id(0)
    prev = be_ref[jnp.maximum(i - 1, 0)]

    @pl.when((i == 0) | (be_ref[i] != prev))
    def _():
        wg_s[...] = wg_ref[0, 0].astype(BF16)
        wu_s[...] = wu_ref[0, 0].astype(BF16)
        wd_s[...] = wd_ref[0, 0].astype(BF16)

    @pl.when(i < nb_ref[0])
    def _():
        x = _unpack_bf16_pairs(x_ref[...])
        hmid = (jax.nn.silu(_dot(x, wg_s[...])) * _dot(x, wu_s[...])).astype(BF16)
        o_ref[...] = _dot(hmid, wd_s[...]).astype(o_ref.dtype)

    @pl.when(i >= nb_ref[0])
    def _():
        o_ref[...] = jnp.zeros_like(o_ref)


def grouped_swiglu(x, block_e, n_used, w_gate, w_up, w_down, layer):
    d, ff = w_gate.shape[-2:]
    nb = x.shape[0] // MOE_BLOCK
    grid_spec = pltpu.PrefetchScalarGridSpec(
        num_scalar_prefetch=2,
        grid=(nb,),
        in_specs=[pl.BlockSpec((MOE_BLOCK, d // 2), lambda i, be, nu: (i, 0)),
                  pl.BlockSpec((1, 1, d, ff), lambda i, be, nu: (layer, be[i], 0, 0)),
                  pl.BlockSpec((1, 1, d, ff), lambda i, be, nu: (layer, be[i], 0, 0)),
                  pl.BlockSpec((1, 1, ff, d), lambda i, be, nu: (layer, be[i], 0, 0))],
        out_specs=pl.BlockSpec((MOE_BLOCK, d), lambda i, be, nu: (i, 0)),
        scratch_shapes=[pltpu.VMEM((d, ff), BF16), pltpu.VMEM((d, ff), BF16), pltpu.VMEM((ff, d), BF16)],
    )
    return pl.pallas_call(
        _moe_kernel,
        grid_spec=grid_spec,
        out_shape=jax.ShapeDtypeStruct((nb * MOE_BLOCK, d), BF16),
        compiler_params=_cparams(("arbitrary",)),
        name="grouped_swiglu",
    )(block_e, n_used, x, w_gate, w_up, w_down)


def _combine_kernel(grp_ref, x_ref, y_ref, w_ref, s_ref, gate_ref, o_ref):
    del grp_ref
    f = s_ref[...].astype(F32)
    for k in range(TOP_K):
        f = f + w_ref[:, k:k + 1] * y_ref[k].astype(F32)
    o_ref[...] = x_ref[...] + gate_ref[0] * f


def moe_combine(x, yg, e_w, shared, gate3, grp):
    m, d = x.shape
    tm = TM // 2
    grid_spec = pltpu.PrefetchScalarGridSpec(
        num_scalar_prefetch=1,
        grid=(m // tm,),
        in_specs=[pl.BlockSpec((tm, d), lambda i, grp: (i, 0)),
                  pl.BlockSpec((TOP_K, tm, d), lambda i, grp: (0, i, 0)),
                  pl.BlockSpec((tm, e_w.shape[1]), lambda i, grp: (i, 0)),
                  pl.BlockSpec((tm, d), lambda i, grp: (i, 0)),
                  pl.BlockSpec((1, 1, d), lambda i, grp: (grp[i // 2], 0, 0))],
        out_specs=pl.BlockSpec((tm, d), lambda i, grp: (i, 0)),
    )
    return pl.pallas_call(
        _combine_kernel,
        grid_spec=grid_spec,
        out_shape=jax.ShapeDtypeStruct((m, d), F32),
        compiler_params=_cparams(("arbitrary",)),
        name="moe_combine",
    )(grp, x, yg, e_w, shared, gate3)


def _route_kernel(grp_ref, x_ref, g_ref, sh_ref, sc_ref, wt_ref, rb_ref, tri_ref, ones_ref,
                  h_ref, idx_ref, w_ref, rank_ref, cnt_ref, carry_ref):
    del grp_ref
    tm = x_ref.shape[0]
    gs = N_EXPERTS // N_GROUPS
    neg = -jnp.inf

    @pl.when(pl.program_id(0) == 0)
    def _():
        carry_ref[...] = jnp.zeros_like(carry_ref)

    x = x_ref[...]
    y = x * lax.rsqrt(jnp.mean(x * x, axis=-1, keepdims=True) + NORM_EPS)
    h = (y * g_ref[...]) * (1.0 + sc_ref[0]) + sh_ref[0]
    h_ref[...] = _pack_bf16_pairs(h)
    scores = jax.nn.sigmoid(_dot_nt(wt_ref[...], h, HI))
    biased = scores + rb_ref[...]

    def first_argmax(v, iota, n):
        m = jnp.max(v, axis=0, keepdims=True)
        return m, jnp.min(jnp.where(v == m, iota, float(n)), axis=0, keepdims=True)

    def stack_rows(rows):
        iota8 = lax.broadcasted_iota(jnp.int32, (8, tm), 0)
        out = jnp.zeros((8, tm), F32)
        for k, row in enumerate(rows):
            out = jnp.where(iota8 == k, row, out)
        return out

    assert gs == 8 and N_GROUPS == 8
    iota_g = lax.broadcasted_iota(jnp.int32, (gs, tm), 0).astype(F32)
    g_rows = []
    for g in range(N_GROUPS):
        bg = biased[g * gs:(g + 1) * gs]
        m1, i1 = first_argmax(bg, iota_g, gs)
        m2 = jnp.max(jnp.where(iota_g == i1, neg, bg), axis=0, keepdims=True)
        g_rows.append(m1 + m2)
    g_score = stack_rows(g_rows)
    g_sel = jnp.zeros((N_GROUPS, tm), F32)
    for _ in range(TOPK_GROUPS):
        _, ig = first_argmax(g_score, iota_g, N_GROUPS)
        hit = iota_g == ig
        g_sel = jnp.where(hit, 1.0, g_sel)
        g_score = jnp.where(hit, neg, g_score)
    e_sel = jnp.concatenate([jnp.broadcast_to(g_sel[g:g + 1], (gs, tm)) for g in range(N_GROUPS)], axis=0)
    masked = jnp.where(e_sel > 0.0, biased, neg)

    iota_e = lax.broadcasted_iota(jnp.int32, (N_EXPERTS, tm), 0).astype(F32)
    chosen = jnp.zeros((N_EXPERTS, tm), F32)
    hits, idx_rows, w_rows = [], [], []
    for _ in range(TOP_K):
        _, ie = first_argmax(masked, iota_e, N_EXPERTS)
        hit = iota_e == ie
        hits.append(hit)
        idx_rows.append(ie)
        w_rows.append(jnp.sum(jnp.where(hit, scores, 0.0), axis=0, keepdims=True))
        chosen = jnp.where(hit, 1.0, chosen)
        masked = jnp.where(hit, neg, masked)
    w_sum = w_rows[0]
    for wk in w_rows[1:]:
        w_sum = w_sum + wk
    idx_ref[...] = stack_rows(idx_rows).astype(jnp.int32)
    w_ref[...] = stack_rows([wk / w_sum * ROUTE_SCALE for wk in w_rows])

    chosen_b = chosen.astype(BF16)
    before = carry_ref[...] + _dot(chosen_b, tri_ref[...])
    rank_ref[...] = stack_rows([jnp.sum(jnp.where(hit, before, 0.0), axis=0, keepdims=True)
                                for hit in hits]).astype(jnp.int32)
    carry_ref[...] += _dot(chosen_b, ones_ref[...])
    cnt_ref[...] = carry_ref[:, :LANES].astype(jnp.int32)


def route(x, g, shift3, scale3, grp, router_w, router_b):
    m, d = x.shape
    ne = N_EXPERTS
    tri = jnp.asarray(np.triu(np.ones((TM, TM), np.float32), 1), BF16)
    ones = jnp.ones((TM, TM), BF16)
    rb = jnp.broadcast_to(router_b.astype(F32)[:, None], (ne, TM))
    row = lambda r: pl.BlockSpec((r, TM), lambda i, grp: (0, i))
    const = lambda shape: pl.BlockSpec(shape, lambda i, grp: (0, 0))
    grid_spec = pltpu.PrefetchScalarGridSpec(
        num_scalar_prefetch=1,
        grid=(m // TM,),
        in_specs=[pl.BlockSpec((TM, d), lambda i, grp: (i, 0)),
                  const((1, d)),
                  pl.BlockSpec((1, 1, d), lambda i, grp: (grp[i], 0, 0)),
                  pl.BlockSpec((1, 1, d), lambda i, grp: (grp[i], 0, 0)),
                  const((ne, d)), const((ne, TM)), const((TM, TM)), const((TM, TM))],
        out_specs=[pl.BlockSpec((TM, d // 2), lambda i, grp: (i, 0)), row(8), row(8), row(8), const((ne, LANES))],
        scratch_shapes=[pltpu.VMEM((ne, TM), F32)],
    )
    return pl.pallas_call(
        _route_kernel,
        grid_spec=grid_spec,
        out_shape=[jax.ShapeDtypeStruct((m, d // 2), jnp.uint32), jax.ShapeDtypeStruct((8, m), jnp.int32),
                   jax.ShapeDtypeStruct((8, m), F32), jax.ShapeDtypeStruct((8, m), jnp.int32),
                   jax.ShapeDtypeStruct((ne, LANES), jnp.int32)],
        compiler_params=_cparams(("arbitrary",)),
        name="route",
    )(grp, x, g.reshape(1, d), shift3, scale3, router_w.T, rb, tri, ones)


def _seq_edge_masks(n_batch, L, Lc):
    n_lat = n_batch * L
    starts = np.concatenate([np.arange(n_batch) * L, n_lat + np.arange(n_batch) * Lc])
    ends = np.concatenate([(np.arange(n_batch) + 1) * L, n_lat + (np.arange(n_batch) + 1) * Lc]) - 1
    first = np.ones((n_batch * (L + Lc), 1), np.float32)
    last = first.copy()
    first[starts] = 0.0
    last[ends] = 0.0
    return jnp.asarray(first), jnp.asarray(last)


def _rwkv_prep(cols, first, last, shift_w, w0, w2, a0, a2, g2, k_k, k_a):
    u = short_conv(cols, shift_w, first, last, RW_COLS // 3)
    n = u.shape[0]
    r, k, v, wlo, alo, glo = jnp.split(u, RW_SPLITS, axis=-1)
    lw, av, kd = [], [], []
    for d in range(2):
        wl = wlo[:, d * RW_DECAY_LORA:(d + 1) * RW_DECAY_LORA]
        al = alo[:, d * RW_AAA_LORA:(d + 1) * RW_AAA_LORA]
        w = -jax.nn.softplus(-(w0[d] + jnp.tanh(wl) @ w2[d])) - 0.5
        lw.append(-jnp.exp(w))
        a = jax.nn.sigmoid(a0[d] + al @ a2[d])
        av.append(a)
        kd.append(k * (1.0 + (a - 1.0) * k_a))
    g = jax.nn.sigmoid(glo) @ g2
    kk = (k * k_k).reshape(n, RW_HEADS, RW_HEAD_DIM)
    kk = kk / jnp.maximum(jnp.sqrt(jnp.sum(kk * kk, axis=-1, keepdims=True)), 1e-12)
    return r, v, kk.reshape(n, RW_WIDTH), jnp.stack(lw), jnp.stack(av), jnp.stack(kd), g


def _rwkv_out(y, r, kd, v, g, r_k, ln_w, ln_b):
    n = y.shape[0]
    hh = lambda t: t.reshape(t.shape[:-1] + (RW_HEADS, RW_HEAD_DIM))
    yh = hh(y)
    mu = jnp.mean(yh, axis=-1, keepdims=True)
    var = jnp.mean(jnp.square(yh - mu), axis=-1, keepdims=True)
    yh = (yh - mu) * lax.rsqrt(var + RW_GN_EPS)
    rk = jnp.sum(hh(r)[None] * hh(kd) * r_k, axis=-1, keepdims=True)
    bonus = jnp.sum(rk * hh(v)[None], axis=0)
    out = yh.reshape(n, RW_WIDTH) * ln_w + ln_b + bonus.reshape(n, RW_WIDTH)
    return out * g


def _moe(h, e_idx, rank, counts, exp_gate, exp_up, exp_down, sh_gate, sh_up, sh_down, layer):
    T = h.shape[0]
    D = exp_gate.shape[-2]
    n = T * TOP_K
    padded = (counts + MOE_BLOCK - 1) // MOE_BLOCK * MOE_BLOCK
    pad_end = jnp.cumsum(padded)
    pad_start = pad_end - padded
    experts = jnp.arange(N_EXPERTS, dtype=jnp.int32)
    dest = rank + jnp.sum(jnp.where(e_idx[:, :, None] == experts, pad_start.astype(jnp.int32), 0), axis=-1)
    per_step = GATHER_ROWS // MOE_BLOCK
    n_blocks = -(-(-(-n // MOE_BLOCK) + N_EXPERTS) // per_step) * per_step
    n_slots = n_blocks * MOE_BLOCK
    flat_dest = dest.reshape(-1)
    tok = jnp.tile(jnp.arange(T, dtype=jnp.int32), TOP_K)
    slot_tok = jnp.zeros((n_slots,), jnp.int32).at[flat_dest].set(tok)
    block_start = jnp.arange(n_blocks, dtype=jnp.int32) * MOE_BLOCK
    block_e = jnp.minimum(jnp.sum(block_start[:, None] >= pad_end[None, :], axis=1), N_EXPERTS - 1).astype(jnp.int32)
    n_used = (pad_end[-1] // MOE_BLOCK).astype(jnp.int32).reshape(1)
    y = grouped_swiglu(gather_rows(h, slot_tok), block_e, n_used, exp_gate, exp_up, exp_down, layer)
    yg = jnp.take(y, flat_dest, axis=0, mode="clip").reshape(TOP_K, T, D)
    nb_sh = T // MOE_BLOCK
    sh4 = lambda w: w.reshape((w.shape[0], 1) + w.shape[1:])
    shared = grouped_swiglu(h, jnp.zeros((nb_sh,), jnp.int32), jnp.full((1,), nb_sh, jnp.int32),
                            sh4(sh_gate), sh4(sh_up), sh4(sh_down), layer)
    return yg, shared


def kernel(x, c, ctx, c_ctx, mod_w, mod_b, norm1_g, norm2_g, w_in, hy_conv, hy_w1, hy_b1, hy_w2, hy_b2, hy_freq,
           hy_w3, hy_bias, na_rpb, rw_shift, rw_w0, rw_w2, rw_a0, rw_a2, rw_g2, rw_kk, rw_ka, rw_rk, rw_ln_w,
           rw_ln_b, proj_a, proj_b, proj_c, w_out, router_w, router_b, exp_gate, exp_up, exp_down, sh_gate, sh_up,
           sh_down, final_g):
    B, L, D = x.shape
    Lc = ctx.shape[1]
    depth = mod_w.shape[0]
    n_lat, n_ctx = B * L, B * Lc
    assert L % TM == 0 and n_ctx % TM == 0 and L % WKV_CHUNK == 0 and Lc % WKV_CHUNK == 0
    col_hy = 3 * HY_WIDTH
    col_na = col_hy + 3 * NA_WIDTH
    col_rw = col_na + RW_COLS

    xs = jnp.concatenate([x.reshape(n_lat, D), ctx.reshape(n_ctx, D)], axis=0)
    grp_all = jnp.asarray(np.concatenate([np.repeat(np.arange(B), L // TM), np.full(n_ctx // TM, B)]), jnp.int32)
    s8 = jnp.zeros((8, D), F32).at[:B].set(jax.nn.silu(c)).at[B].set(jax.nn.silu(c_ctx))
    first, last = _seq_edge_masks(B, L, Lc)

    for i in range(depth):
        with_ctx = i < depth - 1
        mod = small_matmul_bias(s8, mod_w, mod_b, i)[:B + 1].reshape(B + 1, 1, N_MOD * D)
        sh1, sc1, g1, sh2, sc2, g2 = (mod[:, :, j * D:(j + 1) * D] for j in range(N_MOD))
        w_bf = w_in[i].astype(BF16)
        proj = functools.partial(normmod_matmul, xs, norm1_g[i], sh1, sc1, grp_all)
        hy = short_conv(proj(w_bf[:, :col_hy], 512, BF16), hy_conv[i], first, last, HY_WIDTH)
        na = proj(w_bf[:, col_hy:col_na], 512, BF16)
        rw = proj(w_bf[:, col_na:col_rw], 384, F32)
        gates = proj(w_bf[:, col_rw:], 512, BF16)

        hy_args = (hy_w1[i], hy_b1[i], hy_w2[i], hy_b2[i], hy_freq[i], hy_w3[i])
        h_raw, ss = hyena_filters_raw(L, *hy_args)
        hr, hi = hyena_filter_spectrum(h_raw, _filter_scale(ss), L)
        z = hyena_conv(hy, 0, hy, 2, hr, hi, 0, hy_bias[i][0], B, L)
        o_a = hyena_conv(hy, 1, z, 0, hr, hi, 1, hy_bias[i][1], B, L).astype(BF16)
        o_b = na_latent(na, na_rpb[i], B, L, Lc)
        r_, v_, kk_, lw_, av_, kd_, gg_ = _rwkv_prep(rw, first, last, rw_shift[i], rw_w0[i], rw_w2[i], rw_a0[i],
                                                     rw_a2[i], rw_g2[i], rw_kk[i], rw_ka[i])
        y = wkv_scan(r_, v_, kk_, lw_, av_, kd_, B, L, Lc)
        o_c = _rwkv_out(y[0] + y[1], r_, kd_, v_, gg_, rw_rk[i], rw_ln_w[i], rw_ln_b[i]).astype(BF16)

        if with_ctx:
            h_raw_c, ss_c = hyena_filters_raw(Lc, *hy_args)
            o_a_c = hyena_small(hy, n_lat // Lc, B, Lc, h_raw_c, _filter_scale(ss_c), hy_bias[i])
            o_a = jnp.concatenate([o_a, o_a_c], axis=0)
            o_b = jnp.concatenate([o_b, ctx_attn(na, B, L, Lc)], axis=0)
            m_rows = n_lat + n_ctx
        else:
            m_rows = n_lat
        grp = grp_all[:m_rows // TM]
        merged = branch_merge(m_rows, o_a, o_b, o_c, gates, proj_a[i].astype(BF16), proj_b[i].astype(BF16),
                              proj_c[i].astype(BF16))
        xs = resid_matmul(merged, w_out[i].astype(BF16), xs, g1, grp)

        h2, e_idx, e_w, rank, counts = route(xs, norm2_g[i], sh2, sc2, grp, router_w[i], router_b[i])
        yg, shared = _moe(h2, e_idx[:TOP_K], rank[:TOP_K], counts[:, 0], exp_gate, exp_up, exp_down, sh_gate, sh_up,
                          sh_down, i)
        xs = moe_combine(xs, yg, e_w.T, shared, g2, grp)

    return rmsnorm_rows(xs, final_g).reshape(B, L, D)
```

```python
import functools
import math

import jax
import jax.numpy as jnp
import numpy as np
from jax import lax
from jax.experimental import pallas as pl
from jax.experimental.pallas import tpu as pltpu

F32 = jnp.float32
BF16 = jnp.bfloat16
HI = lax.Precision.HIGHEST

GRID_W = 64
NORM_EPS = 1e-6
N_MOD = 6
SHORT_CONV = 3
HY_WIDTH = 1024
HY_BANDS = 16
HY_EMB = 2 * HY_BANDS + 1
HY_FILTER_ORDER = 64
HY_FAST_DECAY = 0.3
HY_SLOW_DECAY = 1.5
HY_DECAY_TARGET = 1e-2
NA_HEADS = 16
NA_HEAD_DIM = 64
NA_WIDTH = NA_HEADS * NA_HEAD_DIM
NA_ROWS = 8
NA_COLS = 16
RW_HEADS = 16
RW_HEAD_DIM = 64
RW_WIDTH = RW_HEADS * RW_HEAD_DIM
RW_DECAY_LORA = 64
RW_AAA_LORA = 64
RW_GATE_LORA = 128
RW_GN_EPS = 64e-5
RW_COLS = 3 * RW_WIDTH + 2 * RW_DECAY_LORA + 2 * RW_AAA_LORA + RW_GATE_LORA
RW_SPLITS = [RW_WIDTH, 2 * RW_WIDTH, 3 * RW_WIDTH, 3 * RW_WIDTH + 2 * RW_DECAY_LORA,
             3 * RW_WIDTH + 2 * RW_DECAY_LORA + 2 * RW_AAA_LORA]
N_BRANCH = 3
N_EXPERTS = 64
TOP_K = 6
N_GROUPS = 8
TOPK_GROUPS = 4
ROUTE_SCALE = 2.5

LANES = 128
VMEM_LIMIT = 56 * 1024 * 1024
TM = 512
MOE_BLOCK = 256
WKV_CHUNK = 64
WKV_GROUP = 4


def _cparams(sem):
    return pltpu.CompilerParams(dimension_semantics=sem, vmem_limit_bytes=VMEM_LIMIT)


def _dot(a, b, prec=None):
    return jnp.dot(a, b, preferred_element_type=F32, precision=prec)


def _dot_nt(a, b, prec=None):
    return lax.dot_general(a, b, (((1,), (1,)), ((), ())), preferred_element_type=F32, precision=prec)


def _dot_tn(a, b, prec=None):
    return lax.dot_general(a, b, (((0,), (0,)), ((), ())), preferred_element_type=F32, precision=prec)


def _small_mm_kernel(a_ref, w_ref, b_ref, o_ref):
    o_ref[...] = _dot(a_ref[...], w_ref[0], HI) + b_ref[0]


def small_matmul_bias(a, w, b, layer, tn=1536):
    m, k = a.shape
    n = w.shape[2]
    return pl.pallas_call(
        _small_mm_kernel,
        grid=(n // tn,),
        in_specs=[pl.BlockSpec((m, k), lambda j: (0, 0)),
                  pl.BlockSpec((1, k, tn), lambda j: (layer, 0, j)),
                  pl.BlockSpec((1, 1, tn), lambda j: (layer, 0, j))],
        out_specs=pl.BlockSpec((m, tn), lambda j: (0, j)),
        out_shape=jax.ShapeDtypeStruct((m, n), F32),
        compiler_params=_cparams(("arbitrary",)),
        name="mod_matmul",
    )(a, w, b.reshape(b.shape[0], 1, n))


def _normmod_mm_kernel(grp_ref, x_ref, g_ref, sh_ref, sc_ref, w_ref, o_ref, h_ref, *, hi):
    del grp_ref

    @pl.when(pl.program_id(1) == 0)
    def _():
        x = x_ref[...]
        y = x * lax.rsqrt(jnp.mean(x * x, axis=-1, keepdims=True) + NORM_EPS)
        y = y * g_ref[...]
        h_ref[...] = (y * (1.0 + sc_ref[0]) + sh_ref[0]).astype(h_ref.dtype)

    o_ref[...] = _dot(h_ref[...], w_ref[...], HI if hi else None).astype(o_ref.dtype)


def normmod_matmul(x, g, shift3, scale3, grp, w, tn, out_dtype, hi=False):
    m, d = x.shape
    n = w.shape[1]
    grid_spec = pltpu.PrefetchScalarGridSpec(
        num_scalar_prefetch=1,
        grid=(m // TM, n // tn),
        in_specs=[pl.BlockSpec((TM, d), lambda i, j, grp: (i, 0)),
                  pl.BlockSpec((1, d), lambda i, j, grp: (0, 0)),
                  pl.BlockSpec((1, 1, d), lambda i, j, grp: (grp[i], 0, 0)),
                  pl.BlockSpec((1, 1, d), lambda i, j, grp: (grp[i], 0, 0)),
                  pl.BlockSpec((d, tn), lambda i, j, grp: (0, j))],
        out_specs=pl.BlockSpec((TM, tn), lambda i, j, grp: (i, j)),
        scratch_shapes=[pltpu.VMEM((TM, d), F32 if hi else BF16)],
    )
    return pl.pallas_call(
        functools.partial(_normmod_mm_kernel, hi=hi),
        grid_spec=grid_spec,
        out_shape=jax.ShapeDtypeStruct((m, n), out_dtype),
        compiler_params=_cparams(("arbitrary", "arbitrary")),
        name="normmod_matmul",
    )(grp, x, g.reshape(1, d), shift3, scale3, w)


def _normmod_kernel(grp_ref, x_ref, g_ref, sh_ref, sc_ref, o_ref):
    del grp_ref
    x = x_ref[...]
    y = x * lax.rsqrt(jnp.mean(x * x, axis=-1, keepdims=True) + NORM_EPS)
    o_ref[...] = ((y * g_ref[...]) * (1.0 + sc_ref[0]) + sh_ref[0]).astype(o_ref.dtype)


def normmod(x, g, shift3, scale3, grp, out_dtype):
    m, d = x.shape
    grid_spec = pltpu.PrefetchScalarGridSpec(
        num_scalar_prefetch=1,
        grid=(m // TM,),
        in_specs=[pl.BlockSpec((TM, d), lambda i, grp: (i, 0)),
                  pl.BlockSpec((1, d), lambda i, grp: (0, 0)),
                  pl.BlockSpec((1, 1, d), lambda i, grp: (grp[i], 0, 0)),
                  pl.BlockSpec((1, 1, d), lambda i, grp: (grp[i], 0, 0))],
        out_specs=pl.BlockSpec((TM, d), lambda i, grp: (i, 0)),
    )
    return pl.pallas_call(
        _normmod_kernel,
        grid_spec=grid_spec,
        out_shape=jax.ShapeDtypeStruct((m, d), out_dtype),
        compiler_params=_cparams(("arbitrary",)),
        name="normmod",
    )(grp, x, g.reshape(1, d), shift3, scale3)


def _merge_kernel(oa_ref, ob_ref, oc_ref, ga_ref, gb_ref, gc_ref, pa_ref, pb_ref, pc_ref, o_ref):
    gate = lambda ref: jax.nn.sigmoid(ref[...].astype(F32))
    m = gate(ga_ref) * _dot(oa_ref[...], pa_ref[...])
    m = m + gate(gb_ref) * _dot(ob_ref[...], pb_ref[...])
    m = m + gate(gc_ref) * _dot(oc_ref[...], pc_ref[...])
    o_ref[...] = m.astype(o_ref.dtype)


def branch_merge(m, o_a, o_b, o_c, gates, pa, pb, pc, tn=512):
    k = o_a.shape[1]
    d = pa.shape[1]
    nj = d // tn
    o_spec = pl.BlockSpec((TM, k), lambda i, j: (i, 0))
    p_spec = pl.BlockSpec((k, tn), lambda i, j: (0, j))
    return pl.pallas_call(
        _merge_kernel,
        grid=(m // TM, nj),
        in_specs=[o_spec, o_spec, o_spec,
                  pl.BlockSpec((TM, tn), lambda i, j: (i, j)),
                  pl.BlockSpec((TM, tn), lambda i, j: (i, j + nj)),
                  pl.BlockSpec((TM, tn), lambda i, j: (i, j + 2 * nj)),
                  p_spec, p_spec, p_spec],
        out_specs=pl.BlockSpec((TM, tn), lambda i, j: (i, j)),
        out_shape=jax.ShapeDtypeStruct((m, d), BF16),
        compiler_params=_cparams(("arbitrary", "arbitrary")),
        name="branch_merge",
    )(o_a, o_b, o_c, gates, gates, gates, pa, pb, pc)


def _resid_mm_kernel(grp_ref, a_ref, w_ref, x_ref, gate_ref, o_ref):
    del grp_ref
    o_ref[...] = x_ref[...] + gate_ref[0] * _dot(a_ref[...], w_ref[...])


def resid_matmul(a, w, x, gate3, grp, tn=512):
    m, k = a.shape
    d = w.shape[1]
    grid_spec = pltpu.PrefetchScalarGridSpec(
        num_scalar_prefetch=1,
        grid=(m // TM, d // tn),
        in_specs=[pl.BlockSpec((TM, k), lambda i, j, grp: (i, 0)),
                  pl.BlockSpec((k, tn), lambda i, j, grp: (0, j)),
                  pl.BlockSpec((TM, tn), lambda i, j, grp: (i, j)),
                  pl.BlockSpec((1, 1, tn), lambda i, j, grp: (grp[i], 0, j))],
        out_specs=pl.BlockSpec((TM, tn), lambda i, j, grp: (i, j)),
    )
    return pl.pallas_call(
        _resid_mm_kernel,
        grid_spec=grid_spec,
        out_shape=jax.ShapeDtypeStruct((m, d), F32),
        compiler_params=_cparams(("arbitrary", "arbitrary")),
        name="resid_matmul",
    )(grp, a, w, x, gate3)


def _short_conv_kernel(x_ref, prev_ref, next_ref, w_ref, first_ref, last_ref, o_ref):
    x = x_ref[...].astype(F32)
    tm = x.shape[0]
    row = lax.broadcasted_iota(jnp.int32, x.shape, 0)
    halo = prev_ref.shape[0]
    before = prev_ref[...].astype(F32)[halo - 1:halo, :]
    after = next_ref[...].astype(F32)[0:1, :]
    prev = jnp.where(row == 0, before, pltpu.roll(x, 1, 0)) * first_ref[...]
    nxt = jnp.where(row == tm - 1, after, pltpu.roll(x, tm - 1, 0)) * last_ref[...]
    o_ref[...] = prev * w_ref[0:1, :] + x * w_ref[1:2, :] + nxt * w_ref[2:3, :]


def short_conv(u, w, first, last, tn):
    assert SHORT_CONV == 3
    m, c = u.shape
    halo = 16
    per = TM // halo
    n_halo = m // halo
    return pl.pallas_call(
        _short_conv_kernel,
        grid=(m // TM, c // tn),
        in_specs=[pl.BlockSpec((TM, tn), lambda i, j: (i, j)),
                  pl.BlockSpec((halo, tn), lambda i, j: (jnp.maximum(i * per - 1, 0), j)),
                  pl.BlockSpec((halo, tn), lambda i, j: (jnp.minimum((i + 1) * per, n_halo - 1), j)),
                  pl.BlockSpec((SHORT_CONV, tn), lambda i, j: (0, j)),
                  pl.BlockSpec((TM, 1), lambda i, j: (i, 0)),
                  pl.BlockSpec((TM, 1), lambda i, j: (i, 0))],
        out_specs=pl.BlockSpec((TM, tn), lambda i, j: (i, j)),
        out_shape=jax.ShapeDtypeStruct((m, c), F32),
        compiler_params=_cparams(("arbitrary", "arbitrary")),
        name="short_conv",
    )(u, u, u, w, first, last)


def _rmsnorm_kernel(x_ref, g_ref, o_ref):
    x = x_ref[...]
    y = x * lax.rsqrt(jnp.mean(x * x, axis=-1, keepdims=True) + NORM_EPS)
    o_ref[...] = y * g_ref[...]


def rmsnorm_rows(x, g):
    m, d = x.shape
    return pl.pallas_call(
        _rmsnorm_kernel,
        grid=(m // TM,),
        in_specs=[pl.BlockSpec((TM, d), lambda i: (i, 0)), pl.BlockSpec((1, d), lambda i: (0, 0))],
        out_specs=pl.BlockSpec((TM, d), lambda i: (i, 0)),
        out_shape=jax.ShapeDtypeStruct((m, d), F32),
        compiler_params=_cparams(("arbitrary",)),
        name="final_rmsnorm",
    )(x, g.reshape(1, d))


def _hyfilt_kernel(z_ref, w1_ref, b1_ref, w2_ref, b2_ref, fr_ref, w3_ref, dl_ref, h_ref, ss_ref):
    z = z_ref[...]
    hdn = jnp.sin(fr_ref[0:1, :] * (_dot(z, w1_ref[...], HI) + b1_ref[...]))
    hdn = jnp.sin(fr_ref[1:2, :] * (_dot(hdn, w2_ref[...], HI) + b2_ref[...]))
    h = _dot(hdn, w3_ref[...], HI)
    h = h * jnp.exp(-z[:, 0:1] * dl_ref[...])
    h_ref[...] = h

    @pl.when(pl.program_id(0) == 0)
    def _():
        ss_ref[...] = jnp.zeros_like(ss_ref)

    ss_ref[...] += jnp.sum(h * h, axis=0, keepdims=True)


def hyena_filters_raw(L, w1, b1, w2, b2, freq, w3):
    t = np.linspace(0.0, 1.0, L, dtype=np.float32)[:, None]
    omega = np.float32(2.0 * math.pi / L) * np.arange(L, dtype=np.float32)[:, None]
    bands = np.linspace(1e-4, HY_BANDS - 1, HY_BANDS, dtype=np.float32)[None, :]
    z = np.concatenate([t, np.cos(omega * bands), -np.sin(omega * bands),
                        np.zeros((L, HY_FILTER_ORDER - HY_EMB), np.float32)], axis=-1).astype(np.float32)
    w1p = jnp.concatenate([w1, jnp.zeros((HY_FILTER_ORDER - HY_EMB, HY_FILTER_ORDER), F32)], axis=0)
    deltas = np.abs(np.linspace(math.log(HY_DECAY_TARGET) / HY_SLOW_DECAY,
                                math.log(HY_DECAY_TARGET) / HY_FAST_DECAY, HY_WIDTH, dtype=np.float32))
    dl4 = np.tile(deltas, 4)[None, :]
    tl = min(L, 256)
    n = 4 * HY_WIDTH
    fo = HY_FILTER_ORDER
    full = lambda shape: pl.BlockSpec(shape, lambda i: (0, 0))
    return pl.pallas_call(
        _hyfilt_kernel,
        grid=(L // tl,),
        in_specs=[pl.BlockSpec((tl, fo), lambda i: (i, 0)), full((fo, fo)), full((1, fo)), full((fo, fo)),
                  full((1, fo)), full((2, fo)), full((fo, n)), full((1, n))],
        out_specs=[pl.BlockSpec((tl, n), lambda i: (i, 0)), full((1, n))],
        out_shape=[jax.ShapeDtypeStruct((L, n), F32), jax.ShapeDtypeStruct((1, n), F32)],
        compiler_params=_cparams(("arbitrary",)),
        name="hyena_filters",
    )(jnp.asarray(z), w1p, b1.reshape(1, fo), w2, b2.reshape(1, fo), freq, w3, jnp.asarray(dl4))


def _filter_scale(ss):
    s = ss.reshape(2, 2, HY_WIDTH)
    rs = lax.rsqrt(jnp.sum(s, axis=1, keepdims=True))
    return jnp.broadcast_to(rs, (2, 2, HY_WIDTH)).reshape(1, 4 * HY_WIDTH)


FFT_N1 = 128
FFT_N2 = 64
FFT_PITCH = 72
FFT_BATCH = 4


@functools.lru_cache(maxsize=None)
def _fft_tables():
    n1, n2 = FFT_N1, FFT_N2
    n = n1 * n2
    a = np.arange(n1 // 2)[None, None, :]
    k1 = np.arange(n1)[None, :, None]
    b = np.arange(n2)[:, None, None]
    theta = 2.0 * np.pi * ((a * k1 % n1) / n1 + (b * k1) / n)
    g = np.concatenate([np.cos(theta), -np.sin(theta)], axis=1)
    ig = np.concatenate([np.cos(theta), -np.sin(theta)], axis=1).transpose(0, 2, 1) / n
    k2 = np.arange(n2)[:, None]
    bb = np.arange(n2)[None, :]
    ph = 2.0 * np.pi * (k2 * bb % n2) / n2
    fr, fi = np.cos(ph), -np.sin(ph)
    f2 = np.block([[fr, -fi], [fi, fr]])
    if2 = np.block([[fr, fi], [-fi, fr]])
    return (jnp.asarray(g, BF16), jnp.asarray(f2, BF16), jnp.asarray(if2, BF16), jnp.asarray(ig, BF16))


def _fft_stage1(u_ref, g_ref, sr_ref, si_ref):
    n1, n2, p = FFT_N1, FFT_N2, FFT_PITCH

    def body(i, carry):
        bs = [i * FFT_BATCH + j for j in range(FFT_BATCH)]
        xs = [u_ref[pl.ds(b, n1 // 2, stride=n2), :].astype(BF16) for b in bs]
        outs = [_dot(g_ref[b], x) for b, x in zip(bs, xs)]
        for b, a in zip(bs, outs):
            sr_ref[pl.ds(b, n1, stride=p), :] = a[:n1]
            si_ref[pl.ds(b, n1, stride=p), :] = a[n1:]
        return carry

    lax.fori_loop(0, n2 // FFT_BATCH, body, 0)


def _bin_rows(k1):
    return pl.ds(pl.multiple_of(k1 * FFT_PITCH, 8), FFT_N2)


def _hyconv_kernel(xm_ref, u_ref, hr_ref, hi_ref, bias_ref, g_ref, f2_ref, if2_ref, ig_ref, o_ref, sr_ref, si_ref):
    n1, n2, p = FFT_N1, FFT_N2, FFT_PITCH
    _fft_stage1(u_ref, g_ref, sr_ref, si_ref)

    def pair(ref, k):
        return jnp.concatenate([ref[_bin_rows(2 * k), :], ref[_bin_rows(2 * k + 1), :]], axis=1)

    def pair_h(ref, k):
        blk = ref[pl.ds(pl.multiple_of(k * 2 * n2, 2 * n2), 2 * n2), :]
        return jnp.concatenate([blk[:n2], blk[n2:]], axis=1)

    def unpair(ref, k, val):
        w = val.shape[1] // 2
        ref[_bin_rows(2 * k), :] = val[:, :w]
        ref[_bin_rows(2 * k + 1), :] = val[:, w:]

    def stage2(i, carry):
        ks = [i * FFT_BATCH + j for j in range(FFT_BATCH)]
        zs = [jnp.concatenate([pair(sr_ref, k), pair(si_ref, k)], axis=0).astype(BF16) for k in ks]
        xs = [_dot(f2_ref[...], z) for z in zs]
        ys = []
        for k, x in zip(ks, xs):
            xr, xi = x[:n2], x[n2:]
            hr, hi = pair_h(hr_ref, k), pair_h(hi_ref, k)
            ys.append(jnp.concatenate([xr * hr - xi * hi, xr * hi + xi * hr], axis=0).astype(BF16))
        bbs = [_dot(if2_ref[...], y) for y in ys]
        for k, bb in zip(ks, bbs):
            unpair(sr_ref, k, bb[:n2])
            unpair(si_ref, k, bb[n2:])
        return carry

    lax.fori_loop(0, n1 // 2 // FFT_BATCH, stage2, 0)

    def stage3(i, carry):
        bs = [i * FFT_BATCH + j for j in range(FFT_BATCH)]
        sts = [jnp.concatenate([sr_ref[pl.ds(b, n1, stride=p), :], si_ref[pl.ds(b, n1, stride=p), :]],
                               axis=0).astype(BF16) for b in bs]
        outs = [_dot(ig_ref[b], st) for b, st in zip(bs, sts)]
        for b, o in zip(bs, outs):
            o_ref[pl.ds(b, n1 // 2, stride=n2), :] = o
        return carry

    lax.fori_loop(0, n2 // FFT_BATCH, stage3, 0)
    bias = bias_ref[...]
    rows_per_pass = 512

    def finish(i, carry):
        rows = pl.ds(pl.multiple_of(i * rows_per_pass, rows_per_pass), rows_per_pass)
        o_ref[rows, :] = xm_ref[rows, :] * (o_ref[rows, :] + u_ref[rows, :] * bias)
        return carry

    lax.fori_loop(0, o_ref.shape[0] // rows_per_pass, finish, 0)


def hyena_conv(xm_arr, xm_col, u_arr, u_col, hr, hi, h_col, bias, n_batch, L):
    assert L == FFT_N1 * FFT_N2 // 2
    cb = LANES
    nct = HY_WIDTH // cb
    n = 2 * L
    g, f2, if2, ig = _fft_tables()
    const3 = lambda shape: pl.BlockSpec(shape, lambda b, c: (0, 0, 0))
    const2 = lambda shape: pl.BlockSpec(shape, lambda b, c: (0, 0))
    return pl.pallas_call(
        _hyconv_kernel,
        grid=(n_batch, nct),
        in_specs=[pl.BlockSpec((L, cb), lambda b, c: (b, xm_col * nct + c)),
                  pl.BlockSpec((L, cb), lambda b, c: (b, u_col * nct + c)),
                  pl.BlockSpec((n, cb), lambda b, c: (0, h_col * nct + c)),
                  pl.BlockSpec((n, cb), lambda b, c: (0, h_col * nct + c)),
                  pl.BlockSpec((1, cb), lambda b, c: (0, c)),
                  const3(g.shape), const2(f2.shape), const2(if2.shape), const3(ig.shape)],
        out_specs=pl.BlockSpec((L, cb), lambda b, c: (b, c)),
        out_shape=jax.ShapeDtypeStruct((n_batch * L, HY_WIDTH), F32),
        scratch_shapes=[pltpu.VMEM((FFT_N1 * FFT_PITCH, cb), F32)] * 2,
        compiler_params=_cparams(("arbitrary", "arbitrary")),
        name="hyena_conv",
    )(xm_arr, u_arr, hr, hi, bias.reshape(1, HY_WIDTH), g, f2, if2, ig)


def _hyspec_kernel(h0_ref, h1_ref, rs_ref, g_ref, f2_ref, hr_ref, hi_ref, s0r, s0i, s1r, s1i):
    n1, n2 = FFT_N1, FFT_N2
    _fft_stage1(h0_ref, g_ref, s0r, s0i)
    _fft_stage1(h1_ref, g_ref, s1r, s1i)
    rs = rs_ref[...]
    h10 = h1_ref[0:1, :]

    def stage2(i, carry):
        ks = [i * FFT_BATCH + j for j in range(FFT_BATCH)]
        zs = [jnp.concatenate([jnp.concatenate([s0r[_bin_rows(k), :], s1r[_bin_rows(k), :]], axis=1),
                               jnp.concatenate([s0i[_bin_rows(k), :], s1i[_bin_rows(k), :]], axis=1)],
                              axis=0).astype(BF16) for k in ks]
        xs = [_dot(f2_ref[...], z) for z in zs]
        for k, x in zip(ks, xs):
            rows = pl.ds(pl.multiple_of(k * n2, n2), n2)
            w = x.shape[1] // 2
            hr_ref[rows, :] = rs * (x[:n2, :w] + x[:n2, w:] - h10)
            hi_ref[rows, :] = rs * (x[n2:, :w] - x[n2:, w:])
        return carry

    lax.fori_loop(0, n1 // FFT_BATCH, stage2, 0)


def hyena_filter_spectrum(h_raw, rs, L):
    assert L == FFT_N1 * FFT_N2 // 2
    cb = LANES
    nct = HY_WIDTH // cb
    n = 2 * L
    g, f2, _, _ = _fft_tables()
    out_spec = pl.BlockSpec((n, cb), lambda o, c: (0, o * nct + c))
    scr = pltpu.VMEM((FFT_N1 * FFT_PITCH, cb), F32)
    return pl.pallas_call(
        _hyspec_kernel,
        grid=(2, nct),
        in_specs=[pl.BlockSpec((L, cb), lambda o, c: (0, (2 * o) * nct + c)),
                  pl.BlockSpec((L, cb), lambda o, c: (0, (2 * o + 1) * nct + c)),
                  pl.BlockSpec((1, cb), lambda o, c: (0, (2 * o) * nct + c)),
                  pl.BlockSpec(g.shape, lambda o, c: (0, 0, 0)),
                  pl.BlockSpec(f2.shape, lambda o, c: (0, 0))],
        out_specs=[out_spec, out_spec],
        out_shape=[jax.ShapeDtypeStruct((n, 2 * HY_WIDTH), F32)] * 2,
        scratch_shapes=[scr, scr, scr, scr],
        compiler_params=_cparams(("arbitrary", "arbitrary")),
        name="hyena_filter_spectrum",
    )(h_raw, h_raw, rs, g, f2)


@functools.lru_cache(maxsize=None)
def _dense_dft_tables(L):
    n = 2 * L
    k = np.arange(n)[:, None]
    t = np.arange(L)[None, :]
    ph = 2.0 * np.pi * (k * t % n) / n
    fwd = np.concatenate([np.cos(ph), -np.sin(ph)], axis=0)
    inv = np.concatenate([np.cos(ph), -np.sin(ph)], axis=0).T / n
    return jnp.asarray(fwd, F32), jnp.asarray(inv, F32)


def _hyena_small_kernel(x1_ref, x2_ref, v_ref, h_ref0a, h_ref0b, h_ref1a, h_ref1b, rs0_ref, rs1_ref,
                        b0_ref, b1_ref, fwd_ref, inv_ref, o_ref, *, L):
    n = 2 * L
    fwd = fwd_ref[...]
    inv = inv_ref[...]

    def conv(u, ha_ref, hb_ref, rs_ref, bias_ref):
        ha, hb = ha_ref[...], hb_ref[...]
        ka = _dot(fwd, ha, HI)
        kb = _dot(fwd, hb, HI)
        rs = rs_ref[...]
        kr = rs * (ka[:n] + kb[:n] - hb[0:1, :])
        ki = rs * (ka[n:] - kb[n:])
        uf = _dot(fwd, u, HI)
        ur, ui = uf[:n], uf[n:]
        y = jnp.concatenate([ur * kr - ui * ki, ur * ki + ui * kr], axis=0)
        return _dot(inv, y, HI) + u * bias_ref[...]

    v = v_ref[...]
    z = x1_ref[...] * conv(v, h_ref0a, h_ref0b, rs0_ref, b0_ref)
    o_ref[...] = (x2_ref[...] * conv(z, h_ref1a, h_ref1b, rs1_ref, b1_ref)).astype(o_ref.dtype)


def hyena_small(u_arr, row0_blocks, n_batch, L, h_raw, rs, bias):
    cb = LANES
    nct = HY_WIDTH // cb
    fwd, inv = _dense_dft_tables(L)
    uspec = lambda col: pl.BlockSpec((L, cb), lambda b, c: (row0_blocks + b, col * nct + c))
    hspec = lambda col: pl.BlockSpec((L, cb), lambda b, c: (0, col * nct + c))
    rspec = lambda col: pl.BlockSpec((1, cb), lambda b, c: (0, col * nct + c))
    bspec = pl.BlockSpec((1, cb), lambda b, c: (0, c))
    bias_0 = bias[0].reshape(1, HY_WIDTH)
    bias_1 = bias[1].reshape(1, HY_WIDTH)
    return pl.pallas_call(
        functools.partial(_hyena_small_kernel, L=L),
        grid=(n_batch, nct),
        in_specs=[uspec(0), uspec(1), uspec(2), hspec(0), hspec(1), hspec(2), hspec(3), rspec(0), rspec(2),
                  bspec, bspec,
                  pl.BlockSpec(fwd.shape, lambda b, c: (0, 0)), pl.BlockSpec(inv.shape, lambda b, c: (0, 0))],
        out_specs=pl.BlockSpec((L, cb), lambda b, c: (b, c)),
        out_shape=jax.ShapeDtypeStruct((n_batch * L, HY_WIDTH), BF16),
        compiler_params=_cparams(("arbitrary", "arbitrary")),
        name="hyena_ctx",
    )(u_arr, u_arr, u_arr, h_raw, h_raw, h_raw, h_raw, rs, rs, bias_0, bias_1, fwd, inv)


def _na_bias_table(rpb):
    cols = np.arange(GRID_W)
    col_start = np.clip(cols - NA_COLS // 2, 0, GRID_W - NA_COLS)[:, None]
    in_win = (cols[None, :] >= col_start) & (cols[None, :] < col_start + NA_COLS)
    rel_col = np.clip(cols[None, :] - cols[:, None], 1 - NA_COLS, NA_COLS - 1) + NA_COLS - 1
    tbl = rpb.astype(F32)[:, :, rel_col]
    tbl = jnp.where(jnp.asarray(in_win)[None, None], tbl, -jnp.inf)
    return jnp.concatenate([tbl[:, :-1], tbl[:, 1:]], axis=-1)


def _na_kernel(*refs, n_rows):
    q_ref = refs[0]
    k_refs = refs[1:1 + NA_ROWS]
    v_refs = refs[1 + NA_ROWS:1 + 2 * NA_ROWS]
    kc_ref, vc_ref, tbl_ref, o_ref = refs[1 + 2 * NA_ROWS:]
    r = pl.program_id(1)
    start = jnp.clip(r - NA_ROWS // 2, 0, n_rows - NA_ROWS)
    d0 = start - r + NA_ROWS - 1
    dh = NA_HEAD_DIM
    q = q_ref[...] * (dh ** -0.5)
    n_pairs = NA_ROWS // 2
    heads = [slice(h * dh, (h + 1) * dh) for h in range(NA_HEADS)]
    scores = []
    for h, hs in enumerate(heads):
        qh = q[:, hs]
        tiles = [_dot_nt(qh, jnp.concatenate([k_refs[2 * p][:, hs], k_refs[2 * p + 1][:, hs]], axis=0))
                 + tbl_ref[h, d0 + 2 * p] for p in range(n_pairs)]
        scores.append(tiles + [_dot_nt(qh, kc_ref[:, hs])])
    probs, denoms = [], []
    for tiles in scores:
        m = tiles[0].max(axis=-1, keepdims=True)
        for s in tiles[1:]:
            m = jnp.maximum(m, s.max(axis=-1, keepdims=True))
        ps = [jnp.exp(s - m) for s in tiles]
        l = ps[0].sum(axis=-1, keepdims=True)
        for p_ in ps[1:]:
            l = l + p_.sum(axis=-1, keepdims=True)
        probs.append([p_.astype(BF16) for p_ in ps])
        denoms.append(l)
    for hs, ps, l in zip(heads, probs, denoms):
        acc = _dot(ps[-1], vc_ref[:, hs])
        for p in range(n_pairs):
            acc = acc + _dot(ps[p], jnp.concatenate([v_refs[2 * p][:, hs], v_refs[2 * p + 1][:, hs]], axis=0))
        o_ref[:, hs] = (acc / l).astype(o_ref.dtype)


def na_latent(na, rpb, n_batch, L, Lc):
    n_rows = L // GRID_W
    assert n_rows >= NA_ROWS
    tbl = _na_bias_table(rpb)
    w = NA_WIDTH
    ctx_blk0 = n_batch * L // Lc

    def kv_spec(i, col):
        def imap(b, r):
            start = jnp.clip(r - NA_ROWS // 2, 0, n_rows - NA_ROWS)
            return (b * n_rows + start + i, col)
        return pl.BlockSpec((GRID_W, w), imap)

    in_specs = ([pl.BlockSpec((GRID_W, w), lambda b, r: (b * n_rows + r, 0))]
                + [kv_spec(i, 1) for i in range(NA_ROWS)] + [kv_spec(i, 2) for i in range(NA_ROWS)]
                + [pl.BlockSpec((Lc, w), lambda b, r: (ctx_blk0 + b, 1)),
                   pl.BlockSpec((Lc, w), lambda b, r: (ctx_blk0 + b, 2)),
                   pl.BlockSpec(tbl.shape, lambda b, r: (0, 0, 0, 0))])
    return pl.pallas_call(
        functools.partial(_na_kernel, n_rows=n_rows),
        grid=(n_batch, n_rows),
        in_specs=in_specs,
        out_specs=pl.BlockSpec((GRID_W, w), lambda b, r: (b * n_rows + r, 0)),
        out_shape=jax.ShapeDtypeStruct((n_batch * L, w), BF16),
        compiler_params=_cparams(("arbitrary", "arbitrary")),
        name="na_latent",
    )(*([na] * (3 + 2 * NA_ROWS)), tbl)


def _ctx_attn_kernel(q_ref, k_ref, v_ref, o_ref):
    dh = NA_HEAD_DIM
    q = q_ref[...] * (dh ** -0.5)
    for h in range(NA_HEADS):
        hs = slice(h * dh, (h + 1) * dh)
        s = _dot_nt(q[:, hs], k_ref[:, hs])
        p_ = jnp.exp(s - s.max(axis=-1, keepdims=True))
        acc = _dot(p_.astype(BF16), v_ref[:, hs])
        o_ref[:, hs] = (acc / p_.sum(axis=-1, keepdims=True)).astype(o_ref.dtype)


def ctx_attn(na, n_batch, L, Lc):
    w = NA_WIDTH
    blk0 = n_batch * L // Lc
    spec = lambda col: pl.BlockSpec((Lc, w), lambda b: (blk0 + b, col))
    return pl.pallas_call(
        _ctx_attn_kernel,
        grid=(n_batch,),
        in_specs=[spec(0), spec(1), spec(2)],
        out_specs=pl.BlockSpec((Lc, w), lambda b: (b, 0)),
        out_shape=jax.ShapeDtypeStruct((n_batch * Lc, w), BF16),
        compiler_params=_cparams(("arbitrary",)),
        name="ctx_attn",
    )(na, na, na)


@functools.lru_cache(maxsize=None)
def _wkv_masks():
    c, g = WKV_CHUNK, WKV_GROUP
    t = np.arange(c)[:, None]
    s = np.arange(c)[None, :]
    tinc = np.stack([(s <= t), (s >= t)]).astype(np.float32)
    strict = np.stack([(s < t), (s > t)]).astype(np.float32)
    tile = lambda m: np.tile(m, (1,) * (m.ndim - 1) + (g,))
    blk = lambda n: (t // n == s // n)
    blk16 = tile(blk(16).astype(np.float32))
    off32 = tile((blk(32) & ~blk(16)).astype(np.float32))
    off64 = tile((~blk(32)).astype(np.float32))
    eye = tile((t == s).astype(np.float32))
    rr = np.arange(g * c)
    hm = (rr[:, None] // c == np.arange(g * RW_HEAD_DIM)[None, :] // RW_HEAD_DIM).astype(np.float32)
    masks = tuple(jnp.asarray(m) for m in (tinc, tile(strict), tile(tinc), blk16, off32, off64, eye, hm))
    return masks + (jnp.asarray(hm, BF16),)


def _wkv_kernel(*refs):
    (r0_ref, v0_ref, kk0_ref, r1_ref, v1_ref, kk1_ref, lw0_ref, av0_ref, kd0_ref, lw1_ref, av1_ref, kd1_ref,
     tinc_ref, strict_ref, incl_ref, blk16_ref, off32_ref, off64_ref, eye_ref, hm_ref, hmb_ref,
     y0_ref, y1_ref, state_ref) = refs
    c, g = WKV_CHUNK, WKV_GROUP
    gw = g * RW_HEAD_DIM

    @pl.when(pl.program_id(1) == 0)
    def _():
        state_ref[...] = jnp.zeros_like(state_ref)

    hm = hm_ref[...]
    hm_bf = hmb_ref[...]
    blk16, off32, off64, eye = blk16_ref[...], off32_ref[...], off64_ref[...], eye_ref[...]

    def bdiag(z):
        return jnp.concatenate([z.astype(BF16)] * g, axis=0) * hm_bf

    def pm(x4, zd):
        return _dot(x4.astype(BF16), zd)

    def prepare(d, r_ref, kk_ref, lw_ref, av_ref, kd_ref):
        lw = lw_ref[0]
        cum = _dot(tinc_ref[d], lw, HI)
        tot = jnp.sum(lw, axis=0, keepdims=True)
        e_neg = jnp.exp(-cum)
        e_rem = jnp.exp(tot - cum)
        kk = kk_ref[...]
        b_vec = kk * av_ref[0]
        kd = kd_ref[0]
        return dict(at=-kk * jnp.exp(cum - lw), rt=r_ref[...] * jnp.exp(cum), bt=b_vec * e_neg, kt=kd * e_neg,
                    bp=b_vec * e_rem, kp=kd * e_rem, e_tot=jnp.exp(tot))

    qs = (prepare(0, r0_ref, kk0_ref, lw0_ref, av0_ref, kd0_ref), prepare(1, r1_ref, kk1_ref, lw1_ref, av1_ref, kd1_ref))
    v_refs, y_refs = (v0_ref, v1_ref), (y0_ref, y1_ref)
    chains = [(d, gi) for gi in range(RW_HEADS // g) for d in range(2)]
    sl = lambda gi: slice(gi * gw, (gi + 1) * gw)
    each = lambda f, *lists: [f(*args) for args in zip(*lists)]
    pm_all = lambda xs, zs: each(lambda x, z: pm(x, bdiag(z)), xs, zs)

    ar = [jnp.concatenate([qs[d]["at"][:, sl(gi)], qs[d]["rt"][:, sl(gi)]], axis=0).astype(BF16) for d, gi in chains]
    pb = [_dot_nt(a, bdiag(qs[d]["bt"][:, sl(gi)])) for a, (d, gi) in zip(ar, chains)]
    pk = [_dot_nt(a, bdiag(qs[d]["kt"][:, sl(gi)])) for a, (d, gi) in zip(ar, chains)]
    a_ab = [p[:c] * strict_ref[d] for p, (d, gi) in zip(pb, chains)]
    a_rb = [p[c:] * incl_ref[d] for p, (d, gi) in zip(pb, chains)]
    a_ak = [p[:c] * strict_ref[d] for p, (d, gi) in zip(pk, chains)]
    a_rk = [p[c:] * incl_ref[d] for p, (d, gi) in zip(pk, chains)]
    ad = [a * blk16 for a in a_ab]
    a2 = pm_all(ad, ad)
    a4 = pm_all(a2, a2)
    a8 = pm_all(a4, a4)
    tinv = [eye + a for a in ad]
    for powr in (a2, a4, a8):
        tinv = each(lambda t, p_: t + p_, tinv, pm_all(tinv, powr))
    for off in (off32, off64):
        mid = pm_all(tinv, [a * off for a in a_ab])
        tinv = each(lambda t, p_: t + p_, tinv, pm_all(mid, tinv))
    s0 = [state_ref[d, gi] for d, gi in chains]
    vv = [v_refs[d][:, sl(gi)] for d, gi in chains]
    vd = [bdiag(v_) for v_ in vv]
    ars = each(lambda a, s_: _dot_nt(a, s_.astype(BF16)), ar, s0)
    akv = each(pm, a_ak, vd)
    u = pm_all(tinv, each(lambda x, y_: x[:c] + y_, ars, akv))
    yu = pm_all(a_rb, u)
    yv = each(pm, a_rk, vd)
    for (d, gi), x, y1_, y2_ in zip(chains, ars, yu, yv):
        y_refs[d][:, sl(gi)] = x[c:] + y1_ + y2_
    upd = [_dot_tn(jnp.concatenate([u_, v_], axis=0).astype(BF16),
                   jnp.concatenate([qs[d]["bp"][:, sl(gi)], qs[d]["kp"][:, sl(gi)]], axis=0).astype(BF16))
           for u_, v_, (d, gi) in zip(u, vv, chains)]
    for (d, gi), s_, up in zip(chains, s0, upd):
        state_ref[d, gi] = s_ * qs[d]["e_tot"][:, sl(gi)] + hm * up


def wkv_scan(r, v, kk, lw, av, kd, n_batch, L, Lc):
    c = WKV_CHUNK
    rows, w = r.shape
    nc, nl = Lc // c, L // c
    masks = _wkv_masks()

    def blk(d, b, s):
        j_ctx = s if d == 0 else nc - 1 - s
        j_lat = s - nc if d == 0 else nl - 1 - (s - nc)
        return jnp.where(s < nc, (n_batch * L + b * Lc) // c + j_ctx, (b * L) // c + j_lat)

    shared = lambda d: pl.BlockSpec((c, w), lambda b, s: (blk(d, b, s), 0))
    perdir = lambda d: pl.BlockSpec((1, c, w), lambda b, s: (d, blk(d, b, s), 0))
    full = lambda m: pl.BlockSpec(m.shape, lambda b, s: (0,) * m.ndim)
    gw = WKV_GROUP * RW_HEAD_DIM
    return pl.pallas_call(
        _wkv_kernel,
        grid=(n_batch, nc + nl),
        in_specs=[shared(0)] * 3 + [shared(1)] * 3 + [perdir(0)] * 3 + [perdir(1)] * 3 + [full(m) for m in masks],
        out_specs=[shared(0), shared(1)],
        out_shape=[jax.ShapeDtypeStruct((rows, w), F32)] * 2,
        scratch_shapes=[pltpu.VMEM((2, RW_HEADS // WKV_GROUP, gw, gw), F32)],
        compiler_params=_cparams(("arbitrary", "arbitrary")),
        name="wkv_scan",
    )(r, v, kk, r, v, kk, lw, av, kd, lw, av, kd, *masks)


def _pack_bf16_pairs(h):
    half = h.shape[1] // 2
    bits = lambda t: lax.bitcast_convert_type(t.astype(BF16).astype(F32), jnp.uint32)
    return (bits(h[:, :half]) >> 16) | (bits(h[:, half:]) & jnp.uint32(0xFFFF0000))


def _unpack_bf16_pairs(p):
    lo = lax.bitcast_convert_type(p << 16, F32).astype(BF16)
    hi = lax.bitcast_convert_type(p & jnp.uint32(0xFFFF0000), F32).astype(BF16)
    return jnp.concatenate([lo, hi], axis=1)


GATHER_ROWS = 1024


def _gather_kernel(idx_ref, tab_ref, out_ref, sem):
    base = pl.program_id(0) * GATHER_ROWS

    def issue(r, carry):
        tok = idx_ref[0, 0, r]
        pltpu.make_async_copy(tab_ref.at[pl.ds(tok, 1)], out_ref.at[pl.ds(base + r, 1)], sem).start()
        return carry

    lax.fori_loop(0, GATHER_ROWS, issue, 0, unroll=8)
    pltpu.make_async_copy(tab_ref.at[pl.ds(0, GATHER_ROWS)], out_ref.at[pl.ds(base, GATHER_ROWS)], sem).wait()


def gather_rows(table, idx):
    n = idx.shape[0]
    assert n % GATHER_ROWS == 0 and table.shape[0] >= GATHER_ROWS
    nb = n // GATHER_ROWS
    return pl.pallas_call(
        _gather_kernel,
        grid=(nb,),
        in_specs=[pl.BlockSpec((1, 1, GATHER_ROWS), lambda i: (i, 0, 0), memory_space=pltpu.SMEM),
                  pl.BlockSpec(memory_space=pl.ANY)],
        out_specs=pl.BlockSpec(memory_space=pl.ANY),
        out_shape=jax.ShapeDtypeStruct((n,) + table.shape[1:], table.dtype),
        scratch_shapes=[pltpu.SemaphoreType.DMA],
        compiler_params=pltpu.CompilerParams(dimension_semantics=("arbitrary",), disable_bounds_checks=True),
        name="dispatch_gather",
    )(idx.reshape(nb, 1, GATHER_ROWS), table)


def _moe_kernel(be_ref, nb_ref, x_ref, wg_ref, wu_ref, wd_ref, o_ref, wg_s, wu_s, wd_s):
    i = pl.program_id(0)
    prev = be_ref[jnp.maximum(i - 1, 0)]

    @pl.when((i == 0) | (be_ref[i] != prev))
    def _():
        wg_s[...] = wg_ref[0, 0].astype(BF16)
        wu_s[...] = wu_ref[0, 0].astype(BF16)
        wd_s[...] = wd_ref[0, 0].astype(BF16)

    @pl.when(i < nb_ref[0])
    def _():
        x = _unpack_bf16_pairs(jnp.concatenate([x_ref[:, s, :] for s in range(x_ref.shape[1])], axis=1))
        hmid = (jax.nn.silu(_dot(x, wg_s[...])) * _dot(x, wu_s[...])).astype(BF16)
        o_ref[...] = _dot(hmid, wd_s[...]).astype(o_ref.dtype)

    @pl.when(i >= nb_ref[0])
    def _():
        o_ref[...] = jnp.zeros_like(o_ref)


def grouped_swiglu(x, block_e, n_used, w_gate, w_up, w_down, layer):
    d, ff = w_gate.shape[-2:]
    nb = x.shape[0] // MOE_BLOCK
    grid_spec = pltpu.PrefetchScalarGridSpec(
        num_scalar_prefetch=2,
        grid=(nb,),
        in_specs=[pl.BlockSpec((MOE_BLOCK,) + x.shape[1:], lambda i, be, nu: (i, 0, 0)),
                  pl.BlockSpec((1, 1, d, ff), lambda i, be, nu: (layer, be[i], 0, 0)),
                  pl.BlockSpec((1, 1, d, ff), lambda i, be, nu: (layer, be[i], 0, 0)),
                  pl.BlockSpec((1, 1, ff, d), lambda i, be, nu: (layer, be[i], 0, 0))],
        out_specs=pl.BlockSpec((MOE_BLOCK, d), lambda i, be, nu: (i, 0)),
        scratch_shapes=[pltpu.VMEM((d, ff), BF16), pltpu.VMEM((d, ff), BF16), pltpu.VMEM((ff, d), BF16)],
    )
    return pl.pallas_call(
        _moe_kernel,
        grid_spec=grid_spec,
        out_shape=jax.ShapeDtypeStruct((nb * MOE_BLOCK, d), BF16),
        compiler_params=_cparams(("arbitrary",)),
        name="grouped_swiglu",
    )(block_e, n_used, x, w_gate, w_up, w_down)


def _combine_kernel(grp_ref, x_ref, y_ref, w_ref, s_ref, gate_ref, o_ref):
    del grp_ref
    f = s_ref[...].astype(F32)
    for k in range(TOP_K):
        f = f + w_ref[:, k:k + 1] * y_ref[k].astype(F32)
    o_ref[...] = x_ref[...] + gate_ref[0] * f


def moe_combine(x, yg, e_w, shared, gate3, grp):
    m, d = x.shape
    tm = TM // 2
    grid_spec = pltpu.PrefetchScalarGridSpec(
        num_scalar_prefetch=1,
        grid=(m // tm,),
        in_specs=[pl.BlockSpec((tm, d), lambda i, grp: (i, 0)),
                  pl.BlockSpec((TOP_K, tm, d), lambda i, grp: (0, i, 0)),
                  pl.BlockSpec((tm, e_w.shape[1]), lambda i, grp: (i, 0)),
                  pl.BlockSpec((tm, d), lambda i, grp: (i, 0)),
                  pl.BlockSpec((1, 1, d), lambda i, grp: (grp[i // 2], 0, 0))],
        out_specs=pl.BlockSpec((tm, d), lambda i, grp: (i, 0)),
    )
    return pl.pallas_call(
        _combine_kernel,
        grid_spec=grid_spec,
        out_shape=jax.ShapeDtypeStruct((m, d), F32),
        compiler_params=_cparams(("arbitrary",)),
        name="moe_combine",
    )(grp, x, yg, e_w, shared, gate3)


def _route_kernel(grp_ref, x_ref, g_ref, sh_ref, sc_ref, wt_ref, rb_ref, tri_ref, ones_ref,
                  h_ref, idx_ref, w_ref, rank_ref, cnt_ref, carry_ref):
    del grp_ref
    tm = x_ref.shape[0]
    gs = N_EXPERTS // N_GROUPS
    neg = -jnp.inf

    @pl.when(pl.program_id(0) == 0)
    def _():
        carry_ref[...] = jnp.zeros_like(carry_ref)

    x = x_ref[...]
    y = x * lax.rsqrt(jnp.mean(x * x, axis=-1, keepdims=True) + NORM_EPS)
    h = (y * g_ref[...]) * (1.0 + sc_ref[0]) + sh_ref[0]
    h_ref[...] = _pack_bf16_pairs(h)
    scores = jax.nn.sigmoid(_dot_nt(wt_ref[...], h, HI))
    biased = scores + rb_ref[...]

    def first_argmax(v, iota, n):
        m = jnp.max(v, axis=0, keepdims=True)
        return m, jnp.min(jnp.where(v == m, iota, float(n)), axis=0, keepdims=True)

    def stack_rows(rows):
        iota8 = lax.broadcasted_iota(jnp.int32, (8, tm), 0)
        out = jnp.zeros((8, tm), F32)
        for k, row in enumerate(rows):
            out = jnp.where(iota8 == k, row, out)
        return out

    assert gs == 8 and N_GROUPS == 8
    iota_g = lax.broadcasted_iota(jnp.int32, (gs, tm), 0).astype(F32)
    g_rows = []
    for g in range(N_GROUPS):
        bg = biased[g * gs:(g + 1) * gs]
        m1, i1 = first_argmax(bg, iota_g, gs)
        m2 = jnp.max(jnp.where(iota_g == i1, neg, bg), axis=0, keepdims=True)
        g_rows.append(m1 + m2)
    g_score = stack_rows(g_rows)
    g_sel = jnp.zeros((N_GROUPS, tm), F32)
    for _ in range(TOPK_GROUPS):
        _, ig = first_argmax(g_score, iota_g, N_GROUPS)
        hit = iota_g == ig
        g_sel = jnp.where(hit, 1.0, g_sel)
        g_score = jnp.where(hit, neg, g_score)
    e_sel = jnp.concatenate([jnp.broadcast_to(g_sel[g:g + 1], (gs, tm)) for g in range(N_GROUPS)], axis=0)
    masked = jnp.where(e_sel > 0.0, biased, neg)

    iota_e = lax.broadcasted_iota(jnp.int32, (N_EXPERTS, tm), 0).astype(F32)
    chosen = jnp.zeros((N_EXPERTS, tm), F32)
    hits, idx_rows, w_rows = [], [], []
    for _ in range(TOP_K):
        _, ie = first_argmax(masked, iota_e, N_EXPERTS)
        hit = iota_e == ie
        hits.append(hit)
        idx_rows.append(ie)
        w_rows.append(jnp.sum(jnp.where(hit, scores, 0.0), axis=0, keepdims=True))
        chosen = jnp.where(hit, 1.0, chosen)
        masked = jnp.where(hit, neg, masked)
    w_sum = w_rows[0]
    for wk in w_rows[1:]:
        w_sum = w_sum + wk
    idx_ref[...] = stack_rows(idx_rows).astype(jnp.int32)
    w_ref[...] = stack_rows([wk / w_sum * ROUTE_SCALE for wk in w_rows])

    chosen_b = chosen.astype(BF16)
    before = carry_ref[...] + _dot(chosen_b, tri_ref[...])
    rank_ref[...] = stack_rows([jnp.sum(jnp.where(hit, before, 0.0), axis=0, keepdims=True)
                                for hit in hits]).astype(jnp.int32)
    carry_ref[...] += _dot(chosen_b, ones_ref[...])
    cnt_ref[...] = carry_ref[:, :LANES].astype(jnp.int32)


def route(x, g, shift3, scale3, grp, router_w, router_b):
    m, d = x.shape
    ne = N_EXPERTS
    tri = jnp.asarray(np.triu(np.ones((TM, TM), np.float32), 1), BF16)
    ones = jnp.ones((TM, TM), BF16)
    rb = jnp.broadcast_to(router_b.astype(F32)[:, None], (ne, TM))
    row = lambda r: pl.BlockSpec((r, TM), lambda i, grp: (0, i))
    const = lambda shape: pl.BlockSpec(shape, lambda i, grp: (0, 0))
    grid_spec = pltpu.PrefetchScalarGridSpec(
        num_scalar_prefetch=1,
        grid=(m // TM,),
        in_specs=[pl.BlockSpec((TM, d), lambda i, grp: (i, 0)),
                  const((1, d)),
                  pl.BlockSpec((1, 1, d), lambda i, grp: (grp[i], 0, 0)),
                  pl.BlockSpec((1, 1, d), lambda i, grp: (grp[i], 0, 0)),
                  const((ne, d)), const((ne, TM)), const((TM, TM)), const((TM, TM))],
        out_specs=[pl.BlockSpec((TM, d // 2), lambda i, grp: (i, 0)), row(8), row(8), row(8), const((ne, LANES))],
        scratch_shapes=[pltpu.VMEM((ne, TM), F32)],
    )
    return pl.pallas_call(
        _route_kernel,
        grid_spec=grid_spec,
        out_shape=[jax.ShapeDtypeStruct((m, d // 2), jnp.uint32), jax.ShapeDtypeStruct((8, m), jnp.int32),
                   jax.ShapeDtypeStruct((8, m), F32), jax.ShapeDtypeStruct((8, m), jnp.int32),
                   jax.ShapeDtypeStruct((ne, LANES), jnp.int32)],
        compiler_params=_cparams(("arbitrary",)),
        name="route",
    )(grp, x, g.reshape(1, d), shift3, scale3, router_w.T, rb, tri, ones)


def _seq_edge_masks(n_batch, L, Lc):
    n_lat = n_batch * L
    starts = np.concatenate([np.arange(n_batch) * L, n_lat + np.arange(n_batch) * Lc])
    ends = np.concatenate([(np.arange(n_batch) + 1) * L, n_lat + (np.arange(n_batch) + 1) * Lc]) - 1
    first = np.ones((n_batch * (L + Lc), 1), np.float32)
    last = first.copy()
    first[starts] = 0.0
    last[ends] = 0.0
    return jnp.asarray(first), jnp.asarray(last)


RW_TM = 256


@functools.lru_cache(maxsize=None)
def _head_ones():
    h = np.arange(RW_WIDTH) // RW_HEAD_DIM
    return jnp.asarray(h[:, None] == h[None, :], BF16)


def _head_sum(x, ones):
    return _dot(x.astype(BF16), ones)


def _rwkv_prep_kernel(u_ref, w0_ref, w2_ref, a0_ref, a2_ref, g2_ref, kk_w_ref, ka_ref, ones_ref,
                      r_ref, v_ref, kk_ref, g_ref, lw_ref, av_ref, kd_ref):
    w = RW_WIDTH
    lo = 3 * w
    r_ref[...] = u_ref[:, 0:w]
    k = u_ref[:, w:2 * w]
    v_ref[...] = u_ref[:, 2 * w:lo]
    wl = jnp.tanh(u_ref[:, lo:lo + 2 * RW_DECAY_LORA]).astype(BF16)
    al = u_ref[:, lo + 2 * RW_DECAY_LORA:lo + 2 * RW_DECAY_LORA + 2 * RW_AAA_LORA].astype(BF16)
    gl = jax.nn.sigmoid(u_ref[:, lo + 2 * RW_DECAY_LORA + 2 * RW_AAA_LORA:]).astype(BF16)
    for d in range(2):
        z = -(w0_ref[d:d + 1, :] + _dot(wl, w2_ref[d]))
        softplus = jnp.maximum(z, 0.0) + jnp.log(1.0 + jnp.exp(-jnp.abs(z)))
        lw_ref[d] = -jnp.exp(-softplus - 0.5)
        a = jax.nn.sigmoid(a0_ref[d:d + 1, :] + _dot(al, a2_ref[d]))
        av_ref[d] = a
        kd_ref[d] = k * (1.0 + (a - 1.0) * ka_ref[...])
    g_ref[...] = _dot(gl, g2_ref[...])
    kk = k * kk_w_ref[...]
    norm = jnp.sqrt(_head_sum(kk * kk, ones_ref[...]))
    kk_ref[...] = kk / jnp.maximum(norm, 1e-12)


def rwkv_prep(u, w0, w2, a0, a2, g2, k_k, k_a):
    m = u.shape[0]
    w = RW_WIDTH
    zeros = jnp.zeros((RW_DECAY_LORA, w), F32)
    pad2 = lambda t: jnp.stack([jnp.concatenate([t[0], zeros], axis=0),
                                jnp.concatenate([zeros, t[1]], axis=0)]).astype(BF16)
    assert RW_DECAY_LORA == RW_AAA_LORA and 2 * RW_DECAY_LORA == LANES and RW_GATE_LORA == LANES
    full = lambda shape: pl.BlockSpec(shape, lambda i: (0,) * len(shape))
    row = pl.BlockSpec((RW_TM, w), lambda i: (i, 0))
    row2 = pl.BlockSpec((2, RW_TM, w), lambda i: (0, i, 0))
    one = jax.ShapeDtypeStruct((m, w), F32)
    two = jax.ShapeDtypeStruct((2, m, w), F32)
    return pl.pallas_call(
        _rwkv_prep_kernel,
        grid=(m // RW_TM,),
        in_specs=[pl.BlockSpec((RW_TM, RW_COLS), lambda i: (i, 0)), full((2, w)), full((2, LANES, w)), full((2, w)),
                  full((2, LANES, w)), full((LANES, w)), full((1, w)), full((1, w)), full((w, w))],
        out_specs=[row, row, row, row, row2, row2, row2],
        out_shape=[one, one, one, one, two, two, two],
        compiler_params=_cparams(("arbitrary",)),
        name="rwkv_prep",
    )(u, w0, pad2(w2), a0, pad2(a2), g2.astype(BF16), k_k.reshape(1, w), k_a.reshape(1, w), _head_ones())


def _rwkv_out_kernel(y0_ref, y1_ref, r_ref, v_ref, g_ref, kd_ref, rk_ref, lnw_ref, lnb_ref, ones_ref, o_ref):
    ones = ones_ref[...]
    inv_n = 1.0 / RW_HEAD_DIM
    y = y0_ref[...] + y1_ref[...]
    yc = y - _head_sum(y, ones) * inv_n
    var = _head_sum(yc * yc, ones) * inv_n
    yn = yc * lax.rsqrt(var + RW_GN_EPS)
    bonus = _head_sum(r_ref[...] * (kd_ref[0] + kd_ref[1]) * rk_ref[...], ones) * v_ref[...]
    o_ref[...] = ((yn * lnw_ref[...] + lnb_ref[...] + bonus) * g_ref[...]).astype(o_ref.dtype)


def rwkv_out(y0, y1, r, v, g, kd, r_k, ln_w, ln_b):
    m, w = r.shape
    full = lambda shape: pl.BlockSpec(shape, lambda i: (0,) * len(shape))
    row = pl.BlockSpec((RW_TM, w), lambda i: (i, 0))
    vec = lambda t: t.reshape(1, w)
    return pl.pallas_call(
        _rwkv_out_kernel,
        grid=(m // RW_TM,),
        in_specs=[row, row, row, row, row, pl.BlockSpec((2, RW_TM, w), lambda i: (0, i, 0)),
                  full((1, w)), full((1, w)), full((1, w)), full((w, w))],
        out_specs=row,
        out_shape=jax.ShapeDtypeStruct((m, w), BF16),
        compiler_params=_cparams(("arbitrary",)),
        name="rwkv_out",
    )(y0, y1, r, v, g, kd, vec(r_k), vec(ln_w), vec(ln_b), _head_ones())


def _moe(h, e_idx, rank, counts, exp_gate, exp_up, exp_down, sh_gate, sh_up, sh_down, layer):
    T = h.shape[0]
    D = exp_gate.shape[-2]
    n = T * TOP_K
    padded = (counts + MOE_BLOCK - 1) // MOE_BLOCK * MOE_BLOCK
    pad_end = jnp.cumsum(padded)
    pad_start = pad_end - padded
    experts = jnp.arange(N_EXPERTS, dtype=jnp.int32)
    dest = rank + jnp.sum(jnp.where(e_idx[:, :, None] == experts, pad_start.astype(jnp.int32), 0), axis=-1)
    per_step = GATHER_ROWS // MOE_BLOCK
    n_blocks = -(-(-(-n // MOE_BLOCK) + N_EXPERTS) // per_step) * per_step
    n_slots = n_blocks * MOE_BLOCK
    flat_dest = dest.reshape(-1)
    tok = jnp.tile(jnp.arange(T, dtype=jnp.int32), TOP_K)
    slot_tok = jnp.zeros((n_slots,), jnp.int32).at[flat_dest].set(tok)
    block_start = jnp.arange(n_blocks, dtype=jnp.int32) * MOE_BLOCK
    block_e = jnp.minimum(jnp.sum(block_start[:, None] >= pad_end[None, :], axis=1), N_EXPERTS - 1).astype(jnp.int32)
    n_used = (pad_end[-1] // MOE_BLOCK).astype(jnp.int32).reshape(1)
    h = h.reshape(T, -1, LANES)
    y = grouped_swiglu(gather_rows(h, slot_tok), block_e, n_used, exp_gate, exp_up, exp_down, layer)
    yg = jnp.take(y, flat_dest, axis=0, mode="clip").reshape(TOP_K, T, D)
    nb_sh = T // MOE_BLOCK
    sh4 = lambda w: w.reshape((w.shape[0], 1) + w.shape[1:])
    shared = grouped_swiglu(h, jnp.zeros((nb_sh,), jnp.int32), jnp.full((1,), nb_sh, jnp.int32),
                            sh4(sh_gate), sh4(sh_up), sh4(sh_down), layer)
    return yg, shared


def kernel(x, c, ctx, c_ctx, mod_w, mod_b, norm1_g, norm2_g, w_in, hy_conv, hy_w1, hy_b1, hy_w2, hy_b2, hy_freq,
           hy_w3, hy_bias, na_rpb, rw_shift, rw_w0, rw_w2, rw_a0, rw_a2, rw_g2, rw_kk, rw_ka, rw_rk, rw_ln_w,
           rw_ln_b, proj_a, proj_b, proj_c, w_out, router_w, router_b, exp_gate, exp_up, exp_down, sh_gate, sh_up,
           sh_down, final_g):
    B, L, D = x.shape
    Lc = ctx.shape[1]
    depth = mod_w.shape[0]
    n_lat, n_ctx = B * L, B * Lc
    assert L % TM == 0 and n_ctx % TM == 0 and L % WKV_CHUNK == 0 and Lc % WKV_CHUNK == 0
    col_hy = 3 * HY_WIDTH
    col_na = col_hy + 3 * NA_WIDTH
    col_rw = col_na + RW_COLS

    xs = jnp.concatenate([x.reshape(n_lat, D), ctx.reshape(n_ctx, D)], axis=0)
    grp_all = jnp.asarray(np.concatenate([np.repeat(np.arange(B), L // TM), np.full(n_ctx // TM, B)]), jnp.int32)
    s8 = jnp.zeros((8, D), F32).at[:B].set(jax.nn.silu(c)).at[B].set(jax.nn.silu(c_ctx))
    first, last = _seq_edge_masks(B, L, Lc)

    for i in range(depth):
        with_ctx = i < depth - 1
        mod = small_matmul_bias(s8, mod_w, mod_b, i)[:B + 1].reshape(B + 1, 1, N_MOD * D)
        sh1, sc1, g1, sh2, sc2, g2 = (mod[:, :, j * D:(j + 1) * D] for j in range(N_MOD))
        w_bf = w_in[i].astype(BF16)
        proj = functools.partial(normmod_matmul, xs, norm1_g[i], sh1, sc1, grp_all)
        hy = short_conv(proj(w_bf[:, :col_hy], 512, BF16), hy_conv[i], first, last, HY_WIDTH)
        na = proj(w_bf[:, col_hy:col_na], 512, BF16)
        rw = proj(w_bf[:, col_na:col_rw], 384, F32)
        gates = proj(w_bf[:, col_rw:], 512, BF16)

        hy_args = (hy_w1[i], hy_b1[i], hy_w2[i], hy_b2[i], hy_freq[i], hy_w3[i])
        h_raw, ss = hyena_filters_raw(L, *hy_args)
        hr, hi = hyena_filter_spectrum(h_raw, _filter_scale(ss), L)
        z = hyena_conv(hy, 0, hy, 2, hr, hi, 0, hy_bias[i][0], B, L)
        o_a = hyena_conv(hy, 1, z, 0, hr, hi, 1, hy_bias[i][1], B, L).astype(BF16)
        o_b = na_latent(na, na_rpb[i], B, L, Lc)
        rw = short_conv(rw, rw_shift[i], first, last, RW_COLS // 3)
        r_, v_, kk_, gg_, lw_, av_, kd_ = rwkv_prep(rw, rw_w0[i], rw_w2[i], rw_a0[i], rw_a2[i], rw_g2[i], rw_kk[i],
                                                    rw_ka[i])
        y_f, y_b = wkv_scan(r_, v_, kk_, lw_, av_, kd_, B, L, Lc)
        o_c = rwkv_out(y_f, y_b, r_, v_, gg_, kd_, rw_rk[i], rw_ln_w[i], rw_ln_b[i])

        if with_ctx:
            h_raw_c, ss_c = hyena_filters_raw(Lc, *hy_args)
            o_a_c = hyena_small(hy, n_lat // Lc, B, Lc, h_raw_c, _filter_scale(ss_c), hy_bias[i])
            o_a = jnp.concatenate([o_a, o_a_c], axis=0)
            o_b = jnp.concatenate([o_b, ctx_attn(na, B, L, Lc)], axis=0)
            m_rows = n_lat + n_ctx
        else:
            m_rows = n_lat
        grp = grp_all[:m_rows // TM]
        merged = branch_merge(m_rows, o_a, o_b, o_c, gates, proj_a[i].astype(BF16), proj_b[i].astype(BF16),
                              proj_c[i].astype(BF16))
        xs = resid_matmul(merged, w_out[i].astype(BF16), xs, g1, grp)

        h2, e_idx, e_w, rank, counts = route(xs, norm2_g[i], sh2, sc2, grp, router_w[i], router_b[i])
        yg, shared = _moe(h2, e_idx[:TOP_K], rank[:TOP_K], counts[:, 0], exp_gate, exp_up, exp_down, sh_gate, sh_up,
                          sh_down, i)
        xs = moe_combine(xs, yg, e_w.T, shared, g2, grp)

    return rmsnorm_rows(xs, final_g).reshape(B, L, D)
```

```python
import functools
import math

import jax
import jax.numpy as jnp
import numpy as np
from jax import lax
from jax.experimental import pallas as pl
from jax.experimental.pallas import tpu as pltpu

F32 = jnp.float32
BF16 = jnp.bfloat16
HI = lax.Precision.HIGHEST

GRID_W = 64
NORM_EPS = 1e-6
N_MOD = 6
SHORT_CONV = 3
HY_WIDTH = 1024
HY_BANDS = 16
HY_EMB = 2 * HY_BANDS + 1
HY_FILTER_ORDER = 64
HY_FAST_DECAY = 0.3
HY_SLOW_DECAY = 1.5
HY_DECAY_TARGET = 1e-2
NA_HEADS = 16
NA_HEAD_DIM = 64
NA_WIDTH = NA_HEADS * NA_HEAD_DIM
NA_ROWS = 8
NA_COLS = 16
RW_HEADS = 16
RW_HEAD_DIM = 64
RW_WIDTH = RW_HEADS * RW_HEAD_DIM
RW_DECAY_LORA = 64
RW_AAA_LORA = 64
RW_GATE_LORA = 128
RW_GN_EPS = 64e-5
RW_COLS = 3 * RW_WIDTH + 2 * RW_DECAY_LORA + 2 * RW_AAA_LORA + RW_GATE_LORA
RW_SPLITS = [RW_WIDTH, 2 * RW_WIDTH, 3 * RW_WIDTH, 3 * RW_WIDTH + 2 * RW_DECAY_LORA,
             3 * RW_WIDTH + 2 * RW_DECAY_LORA + 2 * RW_AAA_LORA]
N_BRANCH = 3
N_EXPERTS = 64
TOP_K = 6
N_GROUPS = 8
TOPK_GROUPS = 4
ROUTE_SCALE = 2.5

LANES = 128
VMEM_LIMIT = 56 * 1024 * 1024
TM = 512
MOE_BLOCK = 256
WKV_CHUNK = 64
WKV_GROUP = 4


def _cparams(sem):
    return pltpu.CompilerParams(dimension_semantics=sem, vmem_limit_bytes=VMEM_LIMIT)


def _dot(a, b, prec=None):
    return jnp.dot(a, b, preferred_element_type=F32, precision=prec)


def _dot_nt(a, b, prec=None):
    return lax.dot_general(a, b, (((1,), (1,)), ((), ())), preferred_element_type=F32, precision=prec)


def _dot_tn(a, b, prec=None):
    return lax.dot_general(a, b, (((0,), (0,)), ((), ())), preferred_element_type=F32, precision=prec)


def _small_mm_kernel(a_ref, w_ref, b_ref, o_ref):
    o_ref[...] = _dot(a_ref[...], w_ref[0], HI) + b_ref[0]


def small_matmul_bias(a, w, b, layer, tn=1536):
    m, k = a.shape
    n = w.shape[2]
    return pl.pallas_call(
        _small_mm_kernel,
        grid=(n // tn,),
        in_specs=[pl.BlockSpec((m, k), lambda j: (0, 0)),
                  pl.BlockSpec((1, k, tn), lambda j: (layer, 0, j)),
                  pl.BlockSpec((1, 1, tn), lambda j: (layer, 0, j))],
        out_specs=pl.BlockSpec((m, tn), lambda j: (0, j)),
        out_shape=jax.ShapeDtypeStruct((m, n), F32),
        compiler_params=_cparams(("arbitrary",)),
        name="mod_matmul",
    )(a, w, b.reshape(b.shape[0], 1, n))


def _normmod_mm_kernel(grp_ref, x_ref, g_ref, sh_ref, sc_ref, w_ref, o_ref, h_ref, *, hi):
    del grp_ref

    @pl.when(pl.program_id(1) == 0)
    def _():
        x = x_ref[...]
        y = x * lax.rsqrt(jnp.mean(x * x, axis=-1, keepdims=True) + NORM_EPS)
        y = y * g_ref[...]
        h_ref[...] = (y * (1.0 + sc_ref[0]) + sh_ref[0]).astype(h_ref.dtype)

    o_ref[...] = _dot(h_ref[...], w_ref[...], HI if hi else None).astype(o_ref.dtype)


def normmod_matmul(x, g, shift3, scale3, grp, w, tn, out_dtype, hi=False):
    m, d = x.shape
    n = w.shape[1]
    grid_spec = pltpu.PrefetchScalarGridSpec(
        num_scalar_prefetch=1,
        grid=(m // TM, n // tn),
        in_specs=[pl.BlockSpec((TM, d), lambda i, j, grp: (i, 0)),
                  pl.BlockSpec((1, d), lambda i, j, grp: (0, 0)),
                  pl.BlockSpec((1, 1, d), lambda i, j, grp: (grp[i], 0, 0)),
                  pl.BlockSpec((1, 1, d), lambda i, j, grp: (grp[i], 0, 0)),
                  pl.BlockSpec((d, tn), lambda i, j, grp: (0, j))],
        out_specs=pl.BlockSpec((TM, tn), lambda i, j, grp: (i, j)),
        scratch_shapes=[pltpu.VMEM((TM, d), F32 if hi else BF16)],
    )
    return pl.pallas_call(
        functools.partial(_normmod_mm_kernel, hi=hi),
        grid_spec=grid_spec,
        out_shape=jax.ShapeDtypeStruct((m, n), out_dtype),
        compiler_params=_cparams(("arbitrary", "arbitrary")),
        name="normmod_matmul",
    )(grp, x, g.reshape(1, d), shift3, scale3, w)


def _normmod_kernel(grp_ref, x_ref, g_ref, sh_ref, sc_ref, o_ref):
    del grp_ref
    x = x_ref[...]
    y = x * lax.rsqrt(jnp.mean(x * x, axis=-1, keepdims=True) + NORM_EPS)
    o_ref[...] = ((y * g_ref[...]) * (1.0 + sc_ref[0]) + sh_ref[0]).astype(o_ref.dtype)


def normmod(x, g, shift3, scale3, grp, out_dtype):
    m, d = x.shape
    grid_spec = pltpu.PrefetchScalarGridSpec(
        num_scalar_prefetch=1,
        grid=(m // TM,),
        in_specs=[pl.BlockSpec((TM, d), lambda i, grp: (i, 0)),
                  pl.BlockSpec((1, d), lambda i, grp: (0, 0)),
                  pl.BlockSpec((1, 1, d), lambda i, grp: (grp[i], 0, 0)),
                  pl.BlockSpec((1, 1, d), lambda i, grp: (grp[i], 0, 0))],
        out_specs=pl.BlockSpec((TM, d), lambda i, grp: (i, 0)),
    )
    return pl.pallas_call(
        _normmod_kernel,
        grid_spec=grid_spec,
        out_shape=jax.ShapeDtypeStruct((m, d), out_dtype),
        compiler_params=_cparams(("arbitrary",)),
        name="normmod",
    )(grp, x, g.reshape(1, d), shift3, scale3)


def _merge_kernel(oa_ref, ob_ref, oc_ref, ga_ref, gb_ref, gc_ref, pa_ref, pb_ref, pc_ref, o_ref):
    gate = lambda ref: jax.nn.sigmoid(ref[...].astype(F32))
    m = gate(ga_ref) * _dot(oa_ref[...], pa_ref[...])
    m = m + gate(gb_ref) * _dot(ob_ref[...], pb_ref[...])
    m = m + gate(gc_ref) * _dot(oc_ref[...], pc_ref[...])
    o_ref[...] = m.astype(o_ref.dtype)


def branch_merge(m, o_a, o_b, o_c, gates, pa, pb, pc, tn=512):
    k = o_a.shape[1]
    d = pa.shape[1]
    nj = d // tn
    o_spec = pl.BlockSpec((TM, k), lambda i, j: (i, 0))
    p_spec = pl.BlockSpec((k, tn), lambda i, j: (0, j))
    return pl.pallas_call(
        _merge_kernel,
        grid=(m // TM, nj),
        in_specs=[o_spec, o_spec, o_spec,
                  pl.BlockSpec((TM, tn), lambda i, j: (i, j)),
                  pl.BlockSpec((TM, tn), lambda i, j: (i, j + nj)),
                  pl.BlockSpec((TM, tn), lambda i, j: (i, j + 2 * nj)),
                  p_spec, p_spec, p_spec],
        out_specs=pl.BlockSpec((TM, tn), lambda i, j: (i, j)),
        out_shape=jax.ShapeDtypeStruct((m, d), BF16),
        compiler_params=_cparams(("arbitrary", "arbitrary")),
        name="branch_merge",
    )(o_a, o_b, o_c, gates, gates, gates, pa, pb, pc)


def _resid_mm_kernel(grp_ref, a_ref, w_ref, x_ref, gate_ref, o_ref):
    del grp_ref
    o_ref[...] = x_ref[...] + gate_ref[0] * _dot(a_ref[...], w_ref[...])


def resid_matmul(a, w, x, gate3, grp, tn=512):
    m, k = a.shape
    d = w.shape[1]
    grid_spec = pltpu.PrefetchScalarGridSpec(
        num_scalar_prefetch=1,
        grid=(m // TM, d // tn),
        in_specs=[pl.BlockSpec((TM, k), lambda i, j, grp: (i, 0)),
                  pl.BlockSpec((k, tn), lambda i, j, grp: (0, j)),
                  pl.BlockSpec((TM, tn), lambda i, j, grp: (i, j)),
                  pl.BlockSpec((1, 1, tn), lambda i, j, grp: (grp[i], 0, j))],
        out_specs=pl.BlockSpec((TM, tn), lambda i, j, grp: (i, j)),
    )
    return pl.pallas_call(
        _resid_mm_kernel,
        grid_spec=grid_spec,
        out_shape=jax.ShapeDtypeStruct((m, d), F32),
        compiler_params=_cparams(("arbitrary", "arbitrary")),
        name="resid_matmul",
    )(grp, a, w, x, gate3)


def _short_conv_kernel(x_ref, prev_ref, next_ref, w_ref, first_ref, last_ref, o_ref):
    x = x_ref[...].astype(F32)
    tm = x.shape[0]
    row = lax.broadcasted_iota(jnp.int32, x.shape, 0)
    halo = prev_ref.shape[0]
    before = prev_ref[...].astype(F32)[halo - 1:halo, :]
    after = next_ref[...].astype(F32)[0:1, :]
    prev = jnp.where(row == 0, before, pltpu.roll(x, 1, 0)) * first_ref[...]
    nxt = jnp.where(row == tm - 1, after, pltpu.roll(x, tm - 1, 0)) * last_ref[...]
    o_ref[...] = prev * w_ref[0:1, :] + x * w_ref[1:2, :] + nxt * w_ref[2:3, :]


def short_conv(u, w, first, last, tn):
    assert SHORT_CONV == 3
    m, c = u.shape
    halo = 16
    per = TM // halo
    n_halo = m // halo
    return pl.pallas_call(
        _short_conv_kernel,
        grid=(m // TM, c // tn),
        in_specs=[pl.BlockSpec((TM, tn), lambda i, j: (i, j)),
                  pl.BlockSpec((halo, tn), lambda i, j: (jnp.maximum(i * per - 1, 0), j)),
                  pl.BlockSpec((halo, tn), lambda i, j: (jnp.minimum((i + 1) * per, n_halo - 1), j)),
                  pl.BlockSpec((SHORT_CONV, tn), lambda i, j: (0, j)),
                  pl.BlockSpec((TM, 1), lambda i, j: (i, 0)),
                  pl.BlockSpec((TM, 1), lambda i, j: (i, 0))],
        out_specs=pl.BlockSpec((TM, tn), lambda i, j: (i, j)),
        out_shape=jax.ShapeDtypeStruct((m, c), F32),
        compiler_params=_cparams(("arbitrary", "arbitrary")),
        name="short_conv",
    )(u, u, u, w, first, last)


def _rmsnorm_kernel(x_ref, g_ref, o_ref):
    x = x_ref[...]
    y = x * lax.rsqrt(jnp.mean(x * x, axis=-1, keepdims=True) + NORM_EPS)
    o_ref[...] = y * g_ref[...]


def rmsnorm_rows(x, g):
    m, d = x.shape
    return pl.pallas_call(
        _rmsnorm_kernel,
        grid=(m // TM,),
        in_specs=[pl.BlockSpec((TM, d), lambda i: (i, 0)), pl.BlockSpec((1, d), lambda i: (0, 0))],
        out_specs=pl.BlockSpec((TM, d), lambda i: (i, 0)),
        out_shape=jax.ShapeDtypeStruct((m, d), F32),
        compiler_params=_cparams(("arbitrary",)),
        name="final_rmsnorm",
    )(x, g.reshape(1, d))


def _hyfilt_kernel(z_ref, w1_ref, b1_ref, w2_ref, b2_ref, fr_ref, w3_ref, dl_ref, h_ref, ss_ref):
    z = z_ref[...]
    hdn = jnp.sin(fr_ref[0:1, :] * (_dot(z, w1_ref[...], HI) + b1_ref[...]))
    hdn = jnp.sin(fr_ref[1:2, :] * (_dot(hdn, w2_ref[...], HI) + b2_ref[...]))
    h = _dot(hdn, w3_ref[...], HI)
    h = h * jnp.exp(-z[:, 0:1] * dl_ref[...])
    h_ref[...] = h

    @pl.when(pl.program_id(0) == 0)
    def _():
        ss_ref[...] = jnp.zeros_like(ss_ref)

    ss_ref[...] += jnp.sum(h * h, axis=0, keepdims=True)


def hyena_filters_raw(L, w1, b1, w2, b2, freq, w3):
    t = np.linspace(0.0, 1.0, L, dtype=np.float32)[:, None]
    omega = np.float32(2.0 * math.pi / L) * np.arange(L, dtype=np.float32)[:, None]
    bands = np.linspace(1e-4, HY_BANDS - 1, HY_BANDS, dtype=np.float32)[None, :]
    z = np.concatenate([t, np.cos(omega * bands), -np.sin(omega * bands),
                        np.zeros((L, HY_FILTER_ORDER - HY_EMB), np.float32)], axis=-1).astype(np.float32)
    w1p = jnp.concatenate([w1, jnp.zeros((HY_FILTER_ORDER - HY_EMB, HY_FILTER_ORDER), F32)], axis=0)
    deltas = np.abs(np.linspace(math.log(HY_DECAY_TARGET) / HY_SLOW_DECAY,
                                math.log(HY_DECAY_TARGET) / HY_FAST_DECAY, HY_WIDTH, dtype=np.float32))
    dl4 = np.tile(deltas, 4)[None, :]
    tl = min(L, 256)
    n = 4 * HY_WIDTH
    fo = HY_FILTER_ORDER
    full = lambda shape: pl.BlockSpec(shape, lambda i: (0, 0))
    return pl.pallas_call(
        _hyfilt_kernel,
        grid=(L // tl,),
        in_specs=[pl.BlockSpec((tl, fo), lambda i: (i, 0)), full((fo, fo)), full((1, fo)), full((fo, fo)),
                  full((1, fo)), full((2, fo)), full((fo, n)), full((1, n))],
        out_specs=[pl.BlockSpec((tl, n), lambda i: (i, 0)), full((1, n))],
        out_shape=[jax.ShapeDtypeStruct((L, n), F32), jax.ShapeDtypeStruct((1, n), F32)],
        compiler_params=_cparams(("arbitrary",)),
        name="hyena_filters",
    )(jnp.asarray(z), w1p, b1.reshape(1, fo), w2, b2.reshape(1, fo), freq, w3, jnp.asarray(dl4))


def _filter_scale(ss):
    s = ss.reshape(2, 2, HY_WIDTH)
    rs = lax.rsqrt(jnp.sum(s, axis=1, keepdims=True))
    return jnp.broadcast_to(rs, (2, 2, HY_WIDTH)).reshape(1, 4 * HY_WIDTH)


FFT_N1 = 128
FFT_N2 = 64
FFT_PITCH = 72
FFT_BATCH = 4


@functools.lru_cache(maxsize=None)
def _fft_tables():
    n1, n2 = FFT_N1, FFT_N2
    n = n1 * n2
    a = np.arange(n1 // 2)[None, None, :]
    k1 = np.arange(n1)[None, :, None]
    b = np.arange(n2)[:, None, None]
    theta = 2.0 * np.pi * ((a * k1 % n1) / n1 + (b * k1) / n)
    g = np.concatenate([np.cos(theta), -np.sin(theta)], axis=1)
    ig = np.concatenate([np.cos(theta), -np.sin(theta)], axis=1).transpose(0, 2, 1) / n
    k2 = np.arange(n2)[:, None]
    bb = np.arange(n2)[None, :]
    ph = 2.0 * np.pi * (k2 * bb % n2) / n2
    fr, fi = np.cos(ph), -np.sin(ph)
    f2 = np.block([[fr, -fi], [fi, fr]])
    if2 = np.block([[fr, fi], [-fi, fr]])
    return (jnp.asarray(g, BF16), jnp.asarray(f2, BF16), jnp.asarray(if2, BF16), jnp.asarray(ig, BF16))


def _fft_stage1(u_ref, g_ref, sr_ref, si_ref):
    n1, n2, p = FFT_N1, FFT_N2, FFT_PITCH

    def body(i, carry):
        bs = [i * FFT_BATCH + j for j in range(FFT_BATCH)]
        xs = [u_ref[pl.ds(b, n1 // 2, stride=n2), :].astype(BF16) for b in bs]
        outs = [_dot(g_ref[b], x) for b, x in zip(bs, xs)]
        for b, a in zip(bs, outs):
            sr_ref[pl.ds(b, n1, stride=p), :] = a[:n1]
            si_ref[pl.ds(b, n1, stride=p), :] = a[n1:]
        return carry

    lax.fori_loop(0, n2 // FFT_BATCH, body, 0)


def _bin_rows(k1):
    return pl.ds(pl.multiple_of(k1 * FFT_PITCH, 8), FFT_N2)


def _hyconv_kernel(xm_ref, u_ref, hr_ref, hi_ref, bias_ref, g_ref, f2_ref, if2_ref, ig_ref, o_ref, sr_ref, si_ref):
    n1, n2, p = FFT_N1, FFT_N2, FFT_PITCH
    _fft_stage1(u_ref, g_ref, sr_ref, si_ref)

    def pair(ref, k):
        return jnp.concatenate([ref[_bin_rows(2 * k), :], ref[_bin_rows(2 * k + 1), :]], axis=1)

    def pair_h(ref, k):
        blk = ref[pl.ds(pl.multiple_of(k * 2 * n2, 2 * n2), 2 * n2), :]
        return jnp.concatenate([blk[:n2], blk[n2:]], axis=1)

    def unpair(ref, k, val):
        w = val.shape[1] // 2
        ref[_bin_rows(2 * k), :] = val[:, :w]
        ref[_bin_rows(2 * k + 1), :] = val[:, w:]

    def stage2(i, carry):
        ks = [i * FFT_BATCH + j for j in range(FFT_BATCH)]
        zs = [jnp.concatenate([pair(sr_ref, k), pair(si_ref, k)], axis=0).astype(BF16) for k in ks]
        xs = [_dot(f2_ref[...], z) for z in zs]
        ys = []
        for k, x in zip(ks, xs):
            xr, xi = x[:n2], x[n2:]
            hr, hi = pair_h(hr_ref, k), pair_h(hi_ref, k)
            ys.append(jnp.concatenate([xr * hr - xi * hi, xr * hi + xi * hr], axis=0).astype(BF16))
        bbs = [_dot(if2_ref[...], y) for y in ys]
        for k, bb in zip(ks, bbs):
            unpair(sr_ref, k, bb[:n2])
            unpair(si_ref, k, bb[n2:])
        return carry

    lax.fori_loop(0, n1 // 2 // FFT_BATCH, stage2, 0)

    def stage3(i, carry):
        bs = [i * FFT_BATCH + j for j in range(FFT_BATCH)]
        sts = [jnp.concatenate([sr_ref[pl.ds(b, n1, stride=p), :], si_ref[pl.ds(b, n1, stride=p), :]],
                               axis=0).astype(BF16) for b in bs]
        outs = [_dot(ig_ref[b], st) for b, st in zip(bs, sts)]
        for b, o in zip(bs, outs):
            o_ref[pl.ds(b, n1 // 2, stride=n2), :] = o
        return carry

    lax.fori_loop(0, n2 // FFT_BATCH, stage3, 0)
    bias = bias_ref[...]
    rows_per_pass = 512

    def finish(i, carry):
        rows = pl.ds(pl.multiple_of(i * rows_per_pass, rows_per_pass), rows_per_pass)
        o_ref[rows, :] = xm_ref[rows, :] * (o_ref[rows, :] + u_ref[rows, :] * bias)
        return carry

    lax.fori_loop(0, o_ref.shape[0] // rows_per_pass, finish, 0)


def hyena_conv(xm_arr, xm_col, u_arr, u_col, hr, hi, h_col, bias, n_batch, L):
    assert L == FFT_N1 * FFT_N2 // 2
    cb = LANES
    nct = HY_WIDTH // cb
    n = 2 * L
    g, f2, if2, ig = _fft_tables()
    const3 = lambda shape: pl.BlockSpec(shape, lambda b, c: (0, 0, 0))
    const2 = lambda shape: pl.BlockSpec(shape, lambda b, c: (0, 0))
    return pl.pallas_call(
        _hyconv_kernel,
        grid=(n_batch, nct),
        in_specs=[pl.BlockSpec((L, cb), lambda b, c: (b, xm_col * nct + c)),
                  pl.BlockSpec((L, cb), lambda b, c: (b, u_col * nct + c)),
                  pl.BlockSpec((n, cb), lambda b, c: (0, h_col * nct + c)),
                  pl.BlockSpec((n, cb), lambda b, c: (0, h_col * nct + c)),
                  pl.BlockSpec((1, cb), lambda b, c: (0, c)),
                  const3(g.shape), const2(f2.shape), const2(if2.shape), const3(ig.shape)],
        out_specs=pl.BlockSpec((L, cb), lambda b, c: (b, c)),
        out_shape=jax.ShapeDtypeStruct((n_batch * L, HY_WIDTH), F32),
        scratch_shapes=[pltpu.VMEM((FFT_N1 * FFT_PITCH, cb), F32)] * 2,
        compiler_params=_cparams(("arbitrary", "arbitrary")),
        name="hyena_conv",
    )(xm_arr, u_arr, hr, hi, bias.reshape(1, HY_WIDTH), g, f2, if2, ig)


def _hyspec_kernel(h0_ref, h1_ref, rs_ref, g_ref, f2_ref, hr_ref, hi_ref, s0r, s0i, s1r, s1i):
    n1, n2 = FFT_N1, FFT_N2
    _fft_stage1(h0_ref, g_ref, s0r, s0i)
    _fft_stage1(h1_ref, g_ref, s1r, s1i)
    rs = rs_ref[...]
    h10 = h1_ref[0:1, :]

    def stage2(i, carry):
        ks = [i * FFT_BATCH + j for j in range(FFT_BATCH)]
        zs = [jnp.concatenate([jnp.concatenate([s0r[_bin_rows(k), :], s1r[_bin_rows(k), :]], axis=1),
                               jnp.concatenate([s0i[_bin_rows(k), :], s1i[_bin_rows(k), :]], axis=1)],
                              axis=0).astype(BF16) for k in ks]
        xs = [_dot(f2_ref[...], z) for z in zs]
        for k, x in zip(ks, xs):
            rows = pl.ds(pl.multiple_of(k * n2, n2), n2)
            w = x.shape[1] // 2
            hr_ref[rows, :] = rs * (x[:n2, :w] + x[:n2, w:] - h10)
            hi_ref[rows, :] = rs * (x[n2:, :w] - x[n2:, w:])
        return carry

    lax.fori_loop(0, n1 // FFT_BATCH, stage2, 0)


def hyena_filter_spectrum(h_raw, rs, L):
    assert L == FFT_N1 * FFT_N2 // 2
    cb = LANES
    nct = HY_WIDTH // cb
    n = 2 * L
    g, f2, _, _ = _fft_tables()
    out_spec = pl.BlockSpec((n, cb), lambda o, c: (0, o * nct + c))
    scr = pltpu.VMEM((FFT_N1 * FFT_PITCH, cb), F32)
    return pl.pallas_call(
        _hyspec_kernel,
        grid=(2, nct),
        in_specs=[pl.BlockSpec((L, cb), lambda o, c: (0, (2 * o) * nct + c)),
                  pl.BlockSpec((L, cb), lambda o, c: (0, (2 * o + 1) * nct + c)),
                  pl.BlockSpec((1, cb), lambda o, c: (0, (2 * o) * nct + c)),
                  pl.BlockSpec(g.shape, lambda o, c: (0, 0, 0)),
                  pl.BlockSpec(f2.shape, lambda o, c: (0, 0))],
        out_specs=[out_spec, out_spec],
        out_shape=[jax.ShapeDtypeStruct((n, 2 * HY_WIDTH), F32)] * 2,
        scratch_shapes=[scr, scr, scr, scr],
        compiler_params=_cparams(("arbitrary", "arbitrary")),
        name="hyena_filter_spectrum",
    )(h_raw, h_raw, rs, g, f2)


@functools.lru_cache(maxsize=None)
def _dense_dft_tables(L):
    n = 2 * L
    k = np.arange(n)[:, None]
    t = np.arange(L)[None, :]
    ph = 2.0 * np.pi * (k * t % n) / n
    fwd = np.concatenate([np.cos(ph), -np.sin(ph)], axis=0)
    inv = np.concatenate([np.cos(ph), -np.sin(ph)], axis=0).T / n
    return jnp.asarray(fwd, F32), jnp.asarray(inv, F32)


def _hyena_small_kernel(x1_ref, x2_ref, v_ref, h_ref0a, h_ref0b, h_ref1a, h_ref1b, rs0_ref, rs1_ref,
                        b0_ref, b1_ref, fwd_ref, inv_ref, o_ref, *, L):
    n = 2 * L
    fwd = fwd_ref[...]
    inv = inv_ref[...]

    def conv(u, ha_ref, hb_ref, rs_ref, bias_ref):
        ha, hb = ha_ref[...], hb_ref[...]
        ka = _dot(fwd, ha, HI)
        kb = _dot(fwd, hb, HI)
        rs = rs_ref[...]
        kr = rs * (ka[:n] + kb[:n] - hb[0:1, :])
        ki = rs * (ka[n:] - kb[n:])
        uf = _dot(fwd, u, HI)
        ur, ui = uf[:n], uf[n:]
        y = jnp.concatenate([ur * kr - ui * ki, ur * ki + ui * kr], axis=0)
        return _dot(inv, y, HI) + u * bias_ref[...]

    v = v_ref[...]
    z = x1_ref[...] * conv(v, h_ref0a, h_ref0b, rs0_ref, b0_ref)
    o_ref[...] = (x2_ref[...] * conv(z, h_ref1a, h_ref1b, rs1_ref, b1_ref)).astype(o_ref.dtype)


def hyena_small(u_arr, row0_blocks, n_batch, L, h_raw, rs, bias):
    cb = LANES
    nct = HY_WIDTH // cb
    fwd, inv = _dense_dft_tables(L)
    uspec = lambda col: pl.BlockSpec((L, cb), lambda b, c: (row0_blocks + b, col * nct + c))
    hspec = lambda col: pl.BlockSpec((L, cb), lambda b, c: (0, col * nct + c))
    rspec = lambda col: pl.BlockSpec((1, cb), lambda b, c: (0, col * nct + c))
    bspec = pl.BlockSpec((1, cb), lambda b, c: (0, c))
    bias_0 = bias[0].reshape(1, HY_WIDTH)
    bias_1 = bias[1].reshape(1, HY_WIDTH)
    return pl.pallas_call(
        functools.partial(_hyena_small_kernel, L=L),
        grid=(n_batch, nct),
        in_specs=[uspec(0), uspec(1), uspec(2), hspec(0), hspec(1), hspec(2), hspec(3), rspec(0), rspec(2),
                  bspec, bspec,
                  pl.BlockSpec(fwd.shape, lambda b, c: (0, 0)), pl.BlockSpec(inv.shape, lambda b, c: (0, 0))],
        out_specs=pl.BlockSpec((L, cb), lambda b, c: (b, c)),
        out_shape=jax.ShapeDtypeStruct((n_batch * L, HY_WIDTH), BF16),
        compiler_params=_cparams(("arbitrary", "arbitrary")),
        name="hyena_ctx",
    )(u_arr, u_arr, u_arr, h_raw, h_raw, h_raw, h_raw, rs, rs, bias_0, bias_1, fwd, inv)


def _na_bias_table(rpb):
    cols = np.arange(GRID_W)
    col_start = np.clip(cols - NA_COLS // 2, 0, GRID_W - NA_COLS)[:, None]
    in_win = (cols[None, :] >= col_start) & (cols[None, :] < col_start + NA_COLS)
    rel_col = np.clip(cols[None, :] - cols[:, None], 1 - NA_COLS, NA_COLS - 1) + NA_COLS - 1
    tbl = rpb.astype(F32)[:, :, rel_col]
    tbl = jnp.where(jnp.asarray(in_win)[None, None], tbl, -jnp.inf)
    return jnp.concatenate([tbl[:, :-1], tbl[:, 1:]], axis=-1)


def _na_kernel(*refs, n_rows):
    q_ref = refs[0]
    k_refs = refs[1:1 + NA_ROWS]
    v_refs = refs[1 + NA_ROWS:1 + 2 * NA_ROWS]
    kc_ref, vc_ref, tbl_ref, o_ref = refs[1 + 2 * NA_ROWS:]
    r = pl.program_id(1)
    start = jnp.clip(r - NA_ROWS // 2, 0, n_rows - NA_ROWS)
    d0 = start - r + NA_ROWS - 1
    dh = NA_HEAD_DIM
    q = q_ref[...] * (dh ** -0.5)
    n_pairs = NA_ROWS // 2
    heads = [slice(h * dh, (h + 1) * dh) for h in range(NA_HEADS)]
    scores = []
    for h, hs in enumerate(heads):
        qh = q[:, hs]
        tiles = [_dot_nt(qh, jnp.concatenate([k_refs[2 * p][:, hs], k_refs[2 * p + 1][:, hs]], axis=0))
                 + tbl_ref[h, d0 + 2 * p] for p in range(n_pairs)]
        scores.append(tiles + [_dot_nt(qh, kc_ref[:, hs])])
    probs, denoms = [], []
    for tiles in scores:
        m = tiles[0].max(axis=-1, keepdims=True)
        for s in tiles[1:]:
            m = jnp.maximum(m, s.max(axis=-1, keepdims=True))
        ps = [jnp.exp(s - m) for s in tiles]
        l = ps[0].sum(axis=-1, keepdims=True)
        for p_ in ps[1:]:
            l = l + p_.sum(axis=-1, keepdims=True)
        probs.append([p_.astype(BF16) for p_ in ps])
        denoms.append(l)
    for hs, ps, l in zip(heads, probs, denoms):
        acc = _dot(ps[-1], vc_ref[:, hs])
        for p in range(n_pairs):
            acc = acc + _dot(ps[p], jnp.concatenate([v_refs[2 * p][:, hs], v_refs[2 * p + 1][:, hs]], axis=0))
        o_ref[:, hs] = (acc / l).astype(o_ref.dtype)


def na_latent(na, rpb, n_batch, L, Lc):
    n_rows = L // GRID_W
    assert n_rows >= NA_ROWS
    tbl = _na_bias_table(rpb)
    w = NA_WIDTH
    ctx_blk0 = n_batch * L // Lc

    def kv_spec(i, col):
        def imap(b, r):
            start = jnp.clip(r - NA_ROWS // 2, 0, n_rows - NA_ROWS)
            return (b * n_rows + start + i, col)
        return pl.BlockSpec((GRID_W, w), imap)

    in_specs = ([pl.BlockSpec((GRID_W, w), lambda b, r: (b * n_rows + r, 0))]
                + [kv_spec(i, 1) for i in range(NA_ROWS)] + [kv_spec(i, 2) for i in range(NA_ROWS)]
                + [pl.BlockSpec((Lc, w), lambda b, r: (ctx_blk0 + b, 1)),
                   pl.BlockSpec((Lc, w), lambda b, r: (ctx_blk0 + b, 2)),
                   pl.BlockSpec(tbl.shape, lambda b, r: (0, 0, 0, 0))])
    return pl.pallas_call(
        functools.partial(_na_kernel, n_rows=n_rows),
        grid=(n_batch, n_rows),
        in_specs=in_specs,
        out_specs=pl.BlockSpec((GRID_W, w), lambda b, r: (b * n_rows + r, 0)),
        out_shape=jax.ShapeDtypeStruct((n_batch * L, w), BF16),
        compiler_params=_cparams(("arbitrary", "arbitrary")),
        name="na_latent",
    )(*([na] * (3 + 2 * NA_ROWS)), tbl)


def _ctx_attn_kernel(q_ref, k_ref, v_ref, o_ref):
    dh = NA_HEAD_DIM
    q = q_ref[...] * (dh ** -0.5)
    for h in range(NA_HEADS):
        hs = slice(h * dh, (h + 1) * dh)
        s = _dot_nt(q[:, hs], k_ref[:, hs])
        p_ = jnp.exp(s - s.max(axis=-1, keepdims=True))
        acc = _dot(p_.astype(BF16), v_ref[:, hs])
        o_ref[:, hs] = (acc / p_.sum(axis=-1, keepdims=True)).astype(o_ref.dtype)


def ctx_attn(na, n_batch, L, Lc):
    w = NA_WIDTH
    blk0 = n_batch * L // Lc
    spec = lambda col: pl.BlockSpec((Lc, w), lambda b: (blk0 + b, col))
    return pl.pallas_call(
        _ctx_attn_kernel,
        grid=(n_batch,),
        in_specs=[spec(0), spec(1), spec(2)],
        out_specs=pl.BlockSpec((Lc, w), lambda b: (b, 0)),
        out_shape=jax.ShapeDtypeStruct((n_batch * Lc, w), BF16),
        compiler_params=_cparams(("arbitrary",)),
        name="ctx_attn",
    )(na, na, na)


@functools.lru_cache(maxsize=None)
def _wkv_masks():
    c, g = WKV_CHUNK, WKV_GROUP
    t = np.arange(c)[:, None]
    s = np.arange(c)[None, :]
    tinc = np.stack([(s <= t), (s >= t)]).astype(np.float32)
    strict = np.stack([(s < t), (s > t)]).astype(np.float32)
    tile = lambda m: np.tile(m, (1,) * (m.ndim - 1) + (g,))
    blk = lambda n: (t // n == s // n)
    blk16 = tile(blk(16).astype(np.float32))
    off32 = tile((blk(32) & ~blk(16)).astype(np.float32))
    off64 = tile((~blk(32)).astype(np.float32))
    eye = tile((t == s).astype(np.float32))
    rr = np.arange(g * c)
    hm = (rr[:, None] // c == np.arange(g * RW_HEAD_DIM)[None, :] // RW_HEAD_DIM).astype(np.float32)
    masks = tuple(jnp.asarray(m) for m in (tinc, tile(strict), tile(tinc), blk16, off32, off64, eye, hm))
    return masks + (jnp.asarray(hm, BF16),)


def _wkv_kernel(*refs):
    (r0_ref, v0_ref, kk0_ref, r1_ref, v1_ref, kk1_ref, lw0_ref, av0_ref, kd0_ref, lw1_ref, av1_ref, kd1_ref,
     tinc_ref, strict_ref, incl_ref, blk16_ref, off32_ref, off64_ref, eye_ref, hm_ref, hmb_ref,
     y0_ref, y1_ref, state_ref) = refs
    c, g = WKV_CHUNK, WKV_GROUP
    gw = g * RW_HEAD_DIM

    @pl.when(pl.program_id(1) == 0)
    def _():
        state_ref[...] = jnp.zeros_like(state_ref)

    hm = hm_ref[...]
    hm_bf = hmb_ref[...]
    blk16, off32, off64, eye = blk16_ref[...], off32_ref[...], off64_ref[...], eye_ref[...]

    def bdiag(z):
        return jnp.concatenate([z.astype(BF16)] * g, axis=0) * hm_bf

    def pm(x4, zd):
        return _dot(x4.astype(BF16), zd)

    def prepare(d, r_ref, kk_ref, lw_ref, av_ref, kd_ref):
        lw = lw_ref[0]
        cum = _dot(tinc_ref[d], lw, HI)
        tot = jnp.sum(lw, axis=0, keepdims=True)
        e_neg = jnp.exp(-cum)
        e_rem = jnp.exp(tot - cum)
        kk = kk_ref[...]
        b_vec = kk * av_ref[0]
        kd = kd_ref[0]
        return dict(at=-kk * jnp.exp(cum - lw), rt=r_ref[...] * jnp.exp(cum), bt=b_vec * e_neg, kt=kd * e_neg,
                    bp=b_vec * e_rem, kp=kd * e_rem, e_tot=jnp.exp(tot))

    qs = (prepare(0, r0_ref, kk0_ref, lw0_ref, av0_ref, kd0_ref), prepare(1, r1_ref, kk1_ref, lw1_ref, av1_ref, kd1_ref))
    v_refs, y_refs = (v0_ref, v1_ref), (y0_ref, y1_ref)
    chains = [(d, gi) for gi in range(RW_HEADS // g) for d in range(2)]
    sl = lambda gi: slice(gi * gw, (gi + 1) * gw)
    each = lambda f, *lists: [f(*args) for args in zip(*lists)]
    pm_all = lambda xs, zs: each(lambda x, z: pm(x, bdiag(z)), xs, zs)

    ar = [jnp.concatenate([qs[d]["at"][:, sl(gi)], qs[d]["rt"][:, sl(gi)]], axis=0).astype(BF16) for d, gi in chains]
    pb = [_dot_nt(a, bdiag(qs[d]["bt"][:, sl(gi)])) for a, (d, gi) in zip(ar, chains)]
    pk = [_dot_nt(a, bdiag(qs[d]["kt"][:, sl(gi)])) for a, (d, gi) in zip(ar, chains)]
    a_ab = [p[:c] * strict_ref[d] for p, (d, gi) in zip(pb, chains)]
    a_rb = [p[c:] * incl_ref[d] for p, (d, gi) in zip(pb, chains)]
    a_ak = [p[:c] * strict_ref[d] for p, (d, gi) in zip(pk, chains)]
    a_rk = [p[c:] * incl_ref[d] for p, (d, gi) in zip(pk, chains)]
    ad = [a * blk16 for a in a_ab]
    a2 = pm_all(ad, ad)
    a4 = pm_all(a2, a2)
    a8 = pm_all(a4, a4)
    tinv = [eye + a for a in ad]
    for powr in (a2, a4, a8):
        tinv = each(lambda t, p_: t + p_, tinv, pm_all(tinv, powr))
    for off in (off32, off64):
        mid = pm_all(tinv, [a * off for a in a_ab])
        tinv = each(lambda t, p_: t + p_, tinv, pm_all(mid, tinv))
    s0 = [state_ref[d, gi] for d, gi in chains]
    vv = [v_refs[d][:, sl(gi)] for d, gi in chains]
    vd = [bdiag(v_) for v_ in vv]
    ars = each(lambda a, s_: _dot_nt(a, s_.astype(BF16)), ar, s0)
    akv = each(pm, a_ak, vd)
    u = pm_all(tinv, each(lambda x, y_: x[:c] + y_, ars, akv))
    yu = pm_all(a_rb, u)
    yv = each(pm, a_rk, vd)
    for (d, gi), x, y1_, y2_ in zip(chains, ars, yu, yv):
        y_refs[d][:, sl(gi)] = x[c:] + y1_ + y2_
    upd = [_dot_tn(jnp.concatenate([u_, v_], axis=0).astype(BF16),
                   jnp.concatenate([qs[d]["bp"][:, sl(gi)], qs[d]["kp"][:, sl(gi)]], axis=0).astype(BF16))
           for u_, v_, (d, gi) in zip(u, vv, chains)]
    for (d, gi), s_, up in zip(chains, s0, upd):
        state_ref[d, gi] = s_ * qs[d]["e_tot"][:, sl(gi)] + hm * up


def wkv_scan(r, v, kk, lw, av, kd, n_batch, L, Lc):
    c = WKV_CHUNK
    rows, w = r.shape
    nc, nl = Lc // c, L // c
    masks = _wkv_masks()

    def blk(d, b, s):
        j_ctx = s if d == 0 else nc - 1 - s
        j_lat = s - nc if d == 0 else nl - 1 - (s - nc)
        return jnp.where(s < nc, (n_batch * L + b * Lc) // c + j_ctx, (b * L) // c + j_lat)

    shared = lambda d: pl.BlockSpec((c, w), lambda b, s: (blk(d, b, s), 0))
    perdir = lambda d: pl.BlockSpec((1, c, w), lambda b, s: (d, blk(d, b, s), 0))
    full = lambda m: pl.BlockSpec(m.shape, lambda b, s: (0,) * m.ndim)
    gw = WKV_GROUP * RW_HEAD_DIM
    return pl.pallas_call(
        _wkv_kernel,
        grid=(n_batch, nc + nl),
        in_specs=[shared(0)] * 3 + [shared(1)] * 3 + [perdir(0)] * 3 + [perdir(1)] * 3 + [full(m) for m in masks],
        out_specs=[shared(0), shared(1)],
        out_shape=[jax.ShapeDtypeStruct((rows, w), F32)] * 2,
        scratch_shapes=[pltpu.VMEM((2, RW_HEADS // WKV_GROUP, gw, gw), F32)],
        compiler_params=_cparams(("arbitrary", "arbitrary")),
        name="wkv_scan",
    )(r, v, kk, r, v, kk, lw, av, kd, lw, av, kd, *masks)


def _pack_bf16_pairs(h):
    half = h.shape[1] // 2
    bits = lambda t: lax.bitcast_convert_type(t.astype(BF16).astype(F32), jnp.uint32)
    return (bits(h[:, :half]) >> 16) | (bits(h[:, half:]) & jnp.uint32(0xFFFF0000))


def _unpack_bf16_pairs(p):
    lo = lax.bitcast_convert_type(p << 16, F32).astype(BF16)
    hi = lax.bitcast_convert_type(p & jnp.uint32(0xFFFF0000), F32).astype(BF16)
    return jnp.concatenate([lo, hi], axis=1)


GATHER_ROWS = 1024


def _gather_kernel(idx_ref, tab_ref, out_ref, sem):
    def issue(r, carry):
        tok = idx_ref[0, 0, r]
        pltpu.make_async_copy(tab_ref.at[pl.ds(tok, 1)], out_ref.at[pl.ds(r, 1)], sem).start()
        return carry

    lax.fori_loop(0, GATHER_ROWS, issue, 0, unroll=8)
    pltpu.make_async_copy(tab_ref.at[pl.ds(0, GATHER_ROWS)], out_ref, sem).wait()


def gather_rows(table, idx):
    n = idx.shape[0]
    assert n % GATHER_ROWS == 0 and table.shape[0] >= GATHER_ROWS
    nb = n // GATHER_ROWS
    return pl.pallas_call(
        _gather_kernel,
        grid=(nb,),
        in_specs=[pl.BlockSpec((1, 1, GATHER_ROWS), lambda i: (i, 0, 0), memory_space=pltpu.SMEM),
                  pl.BlockSpec(memory_space=pl.ANY)],
        out_specs=pl.BlockSpec((GATHER_ROWS,) + table.shape[1:], lambda i: (i, 0, 0)),
        out_shape=jax.ShapeDtypeStruct((n,) + table.shape[1:], table.dtype),
        scratch_shapes=[pltpu.SemaphoreType.DMA],
        compiler_params=pltpu.CompilerParams(dimension_semantics=("arbitrary",), disable_bounds_checks=True,
                                             vmem_limit_bytes=VMEM_LIMIT),
        name="dispatch_gather",
    )(idx.reshape(nb, 1, GATHER_ROWS), table)


def _moe_kernel(be_ref, nb_ref, x_ref, wg_ref, wu_ref, wd_ref, o_ref, wg_s, wu_s, wd_s):
    i = pl.program_id(0)
    prev = be_ref[jnp.maximum(i - 1, 0)]

    @pl.when((i == 0) | (be_ref[i] != prev))
    def _():
        wg_s[...] = wg_ref[0, 0].astype(BF16)
        wu_s[...] = wu_ref[0, 0].astype(BF16)
        wd_s[...] = wd_ref[0, 0].astype(BF16)

    @pl.when(i < nb_ref[0])
    def _():
        x = _unpack_bf16_pairs(jnp.concatenate([x_ref[:, s, :] for s in range(x_ref.shape[1])], axis=1))
        hmid = (jax.nn.silu(_dot(x, wg_s[...])) * _dot(x, wu_s[...])).astype(BF16)
        o_ref[...] = _dot(hmid, wd_s[...]).astype(o_ref.dtype)

    @pl.when(i >= nb_ref[0])
    def _():
        o_ref[...] = jnp.zeros_like(o_ref)


def grouped_swiglu(x, block_e, n_used, w_gate, w_up, w_down, layer):
    d, ff = w_gate.shape[-2:]
    nb = x.shape[0] // MOE_BLOCK
    grid_spec = pltpu.PrefetchScalarGridSpec(
        num_scalar_prefetch=2,
        grid=(nb,),
        in_specs=[pl.BlockSpec((MOE_BLOCK,) + x.shape[1:], lambda i, be, nu: (i, 0, 0)),
                  pl.BlockSpec((1, 1, d, ff), lambda i, be, nu: (layer, be[i], 0, 0)),
                  pl.BlockSpec((1, 1, d, ff), lambda i, be, nu: (layer, be[i], 0, 0)),
                  pl.BlockSpec((1, 1, ff, d), lambda i, be, nu: (layer, be[i], 0, 0))],
        out_specs=pl.BlockSpec((MOE_BLOCK, d), lambda i, be, nu: (i, 0)),
        scratch_shapes=[pltpu.VMEM((d, ff), BF16), pltpu.VMEM((d, ff), BF16), pltpu.VMEM((ff, d), BF16)],
    )
    return pl.pallas_call(
        _moe_kernel,
        grid_spec=grid_spec,
        out_shape=jax.ShapeDtypeStruct((nb * MOE_BLOCK, d), BF16),
        compiler_params=_cparams(("arbitrary",)),
        name="grouped_swiglu",
    )(block_e, n_used, x, w_gate, w_up, w_down)


def _combine_kernel(grp_ref, x_ref, y_ref, w_ref, s_ref, gate_ref, o_ref):
    del grp_ref
    f = s_ref[...].astype(F32)
    for k in range(TOP_K):
        f = f + w_ref[:, k:k + 1] * y_ref[k].astype(F32)
    o_ref[...] = x_ref[...] + gate_ref[0] * f


def moe_combine(x, yg, e_w, shared, gate3, grp):
    m, d = x.shape
    tm = TM // 2
    grid_spec = pltpu.PrefetchScalarGridSpec(
        num_scalar_prefetch=1,
        grid=(m // tm,),
        in_specs=[pl.BlockSpec((tm, d), lambda i, grp: (i, 0)),
                  pl.BlockSpec((TOP_K, tm, d), lambda i, grp: (0, i, 0)),
                  pl.BlockSpec((tm, e_w.shape[1]), lambda i, grp: (i, 0)),
                  pl.BlockSpec((tm, d), lambda i, grp: (i, 0)),
                  pl.BlockSpec((1, 1, d), lambda i, grp: (grp[i // 2], 0, 0))],
        out_specs=pl.BlockSpec((tm, d), lambda i, grp: (i, 0)),
    )
    return pl.pallas_call(
        _combine_kernel,
        grid_spec=grid_spec,
        out_shape=jax.ShapeDtypeStruct((m, d), F32),
        compiler_params=_cparams(("arbitrary",)),
        name="moe_combine",
    )(grp, x, yg, e_w, shared, gate3)


def _route_kernel(grp_ref, x_ref, g_ref, sh_ref, sc_ref, wt_ref, rb_ref, tri_ref, ones_ref,
                  h_ref, idx_ref, w_ref, rank_ref, cnt_ref, carry_ref):
    del grp_ref
    tm = x_ref.shape[0]
    gs = N_EXPERTS // N_GROUPS
    neg = -jnp.inf

    @pl.when(pl.program_id(0) == 0)
    def _():
        carry_ref[...] = jnp.zeros_like(carry_ref)

    x = x_ref[...]
    y = x * lax.rsqrt(jnp.mean(x * x, axis=-1, keepdims=True) + NORM_EPS)
    h = (y * g_ref[...]) * (1.0 + sc_ref[0]) + sh_ref[0]
    h_ref[...] = _pack_bf16_pairs(h)
    scores = jax.nn.sigmoid(_dot_nt(wt_ref[...], h, HI))
    biased = scores + rb_ref[...]

    def first_argmax(v, iota, n):
        m = jnp.max(v, axis=0, keepdims=True)
        return m, jnp.min(jnp.where(v == m, iota, float(n)), axis=0, keepdims=True)

    def stack_rows(rows):
        iota8 = lax.broadcasted_iota(jnp.int32, (8, tm), 0)
        out = jnp.zeros((8, tm), F32)
        for k, row in enumerate(rows):
            out = jnp.where(iota8 == k, row, out)
        return out

    assert gs == 8 and N_GROUPS == 8
    iota_g = lax.broadcasted_iota(jnp.int32, (gs, tm), 0).astype(F32)
    g_rows = []
    for g in range(N_GROUPS):
        bg = biased[g * gs:(g + 1) * gs]
        m1, i1 = first_argmax(bg, iota_g, gs)
        m2 = jnp.max(jnp.where(iota_g == i1, neg, bg), axis=0, keepdims=True)
        g_rows.append(m1 + m2)
    g_score = stack_rows(g_rows)
    g_sel = jnp.zeros((N_GROUPS, tm), F32)
    for _ in range(TOPK_GROUPS):
        _, ig = first_argmax(g_score, iota_g, N_GROUPS)
        hit = iota_g == ig
        g_sel = jnp.where(hit, 1.0, g_sel)
        g_score = jnp.where(hit, neg, g_score)
    e_sel = jnp.concatenate([jnp.broadcast_to(g_sel[g:g + 1], (gs, tm)) for g in range(N_GROUPS)], axis=0)
    masked = jnp.where(e_sel > 0.0, biased, neg)

    iota_e = lax.broadcasted_iota(jnp.int32, (N_EXPERTS, tm), 0).astype(F32)
    chosen = jnp.zeros((N_EXPERTS, tm), F32)
    hits, idx_rows, w_rows = [], [], []
    for _ in range(TOP_K):
        _, ie = first_argmax(masked, iota_e, N_EXPERTS)
        hit = iota_e == ie
        hits.append(hit)
        idx_rows.append(ie)
        w_rows.append(jnp.sum(jnp.where(hit, scores, 0.0), axis=0, keepdims=True))
        chosen = jnp.where(hit, 1.0, chosen)
        masked = jnp.where(hit, neg, masked)
    w_sum = w_rows[0]
    for wk in w_rows[1:]:
        w_sum = w_sum + wk
    idx_ref[...] = stack_rows(idx_rows).astype(jnp.int32)
    w_ref[...] = stack_rows([wk / w_sum * ROUTE_SCALE for wk in w_rows])

    chosen_b = chosen.astype(BF16)
    before = carry_ref[...] + _dot(chosen_b, tri_ref[...])
    rank_ref[...] = stack_rows([jnp.sum(jnp.where(hit, before, 0.0), axis=0, keepdims=True)
                                for hit in hits]).astype(jnp.int32)
    carry_ref[...] += _dot(chosen_b, ones_ref[...])
    cnt_ref[...] = carry_ref[:, :LANES].astype(jnp.int32)


def route(x, g, shift3, scale3, grp, router_w, router_b):
    m, d = x.shape
    ne = N_EXPERTS
    tri = jnp.asarray(np.triu(np.ones((TM, TM), np.float32), 1), BF16)
    ones = jnp.ones((TM, TM), BF16)
    rb = jnp.broadcast_to(router_b.astype(F32)[:, None], (ne, TM))
    row = lambda r: pl.BlockSpec((r, TM), lambda i, grp: (0, i))
    const = lambda shape: pl.BlockSpec(shape, lambda i, grp: (0, 0))
    grid_spec = pltpu.PrefetchScalarGridSpec(
        num_scalar_prefetch=1,
        grid=(m // TM,),
        in_specs=[pl.BlockSpec((TM, d), lambda i, grp: (i, 0)),
                  const((1, d)),
                  pl.BlockSpec((1, 1, d), lambda i, grp: (grp[i], 0, 0)),
                  pl.BlockSpec((1, 1, d), lambda i, grp: (grp[i], 0, 0)),
                  const((ne, d)), const((ne, TM)), const((TM, TM)), const((TM, TM))],
        out_specs=[pl.BlockSpec((TM, d // 2), lambda i, grp: (i, 0)), row(8), row(8), row(8), const((ne, LANES))],
        scratch_shapes=[pltpu.VMEM((ne, TM), F32)],
    )
    return pl.pallas_call(
        _route_kernel,
        grid_spec=grid_spec,
        out_shape=[jax.ShapeDtypeStruct((m, d // 2), jnp.uint32), jax.ShapeDtypeStruct((8, m), jnp.int32),
                   jax.ShapeDtypeStruct((8, m), F32), jax.ShapeDtypeStruct((8, m), jnp.int32),
                   jax.ShapeDtypeStruct((ne, LANES), jnp.int32)],
        compiler_params=_cparams(("arbitrary",)),
        name="route",
    )(grp, x, g.reshape(1, d), shift3, scale3, router_w.T, rb, tri, ones)


def _seq_edge_masks(n_batch, L, Lc):
    n_lat = n_batch * L
    starts = np.concatenate([np.arange(n_batch) * L, n_lat + np.arange(n_batch) * Lc])
    ends = np.concatenate([(np.arange(n_batch) + 1) * L, n_lat + (np.arange(n_batch) + 1) * Lc]) - 1
    first = np.ones((n_batch * (L + Lc), 1), np.float32)
    last = first.copy()
    first[starts] = 0.0
    last[ends] = 0.0
    return jnp.asarray(first), jnp.asarray(last)


RW_TM = 256


@functools.lru_cache(maxsize=None)
def _head_ones():
    h = np.arange(RW_WIDTH) // RW_HEAD_DIM
    return jnp.asarray(h[:, None] == h[None, :], BF16)


def _head_sum(x, ones):
    return _dot(x.astype(BF16), ones)


def _rwkv_prep_kernel(u_ref, w0_ref, w2_ref, a0_ref, a2_ref, g2_ref, kk_w_ref, ka_ref, ones_ref,
                      r_ref, v_ref, kk_ref, g_ref, lw_ref, av_ref, kd_ref):
    w = RW_WIDTH
    lo = 3 * w
    r_ref[...] = u_ref[:, 0:w]
    k = u_ref[:, w:2 * w]
    v_ref[...] = u_ref[:, 2 * w:lo]
    wl = jnp.tanh(u_ref[:, lo:lo + 2 * RW_DECAY_LORA]).astype(BF16)
    al = u_ref[:, lo + 2 * RW_DECAY_LORA:lo + 2 * RW_DECAY_LORA + 2 * RW_AAA_LORA].astype(BF16)
    gl = jax.nn.sigmoid(u_ref[:, lo + 2 * RW_DECAY_LORA + 2 * RW_AAA_LORA:]).astype(BF16)
    for d in range(2):
        z = -(w0_ref[d:d + 1, :] + _dot(wl, w2_ref[d]))
        softplus = jnp.maximum(z, 0.0) + jnp.log(1.0 + jnp.exp(-jnp.abs(z)))
        lw_ref[d] = -jnp.exp(-softplus - 0.5)
        a = jax.nn.sigmoid(a0_ref[d:d + 1, :] + _dot(al, a2_ref[d]))
        av_ref[d] = a
        kd_ref[d] = k * (1.0 + (a - 1.0) * ka_ref[...])
    g_ref[...] = _dot(gl, g2_ref[...])
    kk = k * kk_w_ref[...]
    norm = jnp.sqrt(_head_sum(kk * kk, ones_ref[...]))
    kk_ref[...] = kk / jnp.maximum(norm, 1e-12)


def rwkv_prep(u, w0, w2, a0, a2, g2, k_k, k_a):
    m = u.shape[0]
    w = RW_WIDTH
    zeros = jnp.zeros((RW_DECAY_LORA, w), F32)
    pad2 = lambda t: jnp.stack([jnp.concatenate([t[0], zeros], axis=0),
                                jnp.concatenate([zeros, t[1]], axis=0)]).astype(BF16)
    assert RW_DECAY_LORA == RW_AAA_LORA and 2 * RW_DECAY_LORA == LANES and RW_GATE_LORA == LANES
    full = lambda shape: pl.BlockSpec(shape, lambda i: (0,) * len(shape))
    row = pl.BlockSpec((RW_TM, w), lambda i: (i, 0))
    row2 = pl.BlockSpec((2, RW_TM, w), lambda i: (0, i, 0))
    one = jax.ShapeDtypeStruct((m, w), F32)
    two = jax.ShapeDtypeStruct((2, m, w), F32)
    return pl.pallas_call(
        _rwkv_prep_kernel,
        grid=(m // RW_TM,),
        in_specs=[pl.BlockSpec((RW_TM, RW_COLS), lambda i: (i, 0)), full((2, w)), full((2, LANES, w)), full((2, w)),
                  full((2, LANES, w)), full((LANES, w)), full((1, w)), full((1, w)), full((w, w))],
        out_specs=[row, row, row, row, row2, row2, row2],
        out_shape=[one, one, one, one, two, two, two],
        compiler_params=_cparams(("arbitrary",)),
        name="rwkv_prep",
    )(u, w0, pad2(w2), a0, pad2(a2), g2.astype(BF16), k_k.reshape(1, w), k_a.reshape(1, w), _head_ones())


def _rwkv_out_kernel(y0_ref, y1_ref, r_ref, v_ref, g_ref, kd_ref, rk_ref, lnw_ref, lnb_ref, ones_ref, o_ref):
    ones = ones_ref[...]
    inv_n = 1.0 / RW_HEAD_DIM
    y = y0_ref[...] + y1_ref[...]
    yc = y - _head_sum(y, ones) * inv_n
    var = _head_sum(yc * yc, ones) * inv_n
    yn = yc * lax.rsqrt(var + RW_GN_EPS)
    bonus = _head_sum(r_ref[...] * (kd_ref[0] + kd_ref[1]) * rk_ref[...], ones) * v_ref[...]
    o_ref[...] = ((yn * lnw_ref[...] + lnb_ref[...] + bonus) * g_ref[...]).astype(o_ref.dtype)


def rwkv_out(y0, y1, r, v, g, kd, r_k, ln_w, ln_b):
    m, w = r.shape
    full = lambda shape: pl.BlockSpec(shape, lambda i: (0,) * len(shape))
    row = pl.BlockSpec((RW_TM, w), lambda i: (i, 0))
    vec = lambda t: t.reshape(1, w)
    return pl.pallas_call(
        _rwkv_out_kernel,
        grid=(m // RW_TM,),
        in_specs=[row, row, row, row, row, pl.BlockSpec((2, RW_TM, w), lambda i: (0, i, 0)),
                  full((1, w)), full((1, w)), full((1, w)), full((w, w))],
        out_specs=row,
        out_shape=jax.ShapeDtypeStruct((m, w), BF16),
        compiler_params=_cparams(("arbitrary",)),
        name="rwkv_out",
    )(y0, y1, r, v, g, kd, vec(r_k), vec(ln_w), vec(ln_b), _head_ones())


def _moe(h, e_idx, rank, counts, exp_gate, exp_up, exp_down, sh_gate, sh_up, sh_down, layer):
    T = h.shape[0]
    D = exp_gate.shape[-2]
    n = T * TOP_K
    padded = (counts + MOE_BLOCK - 1) // MOE_BLOCK * MOE_BLOCK
    pad_end = jnp.cumsum(padded)
    pad_start = pad_end - padded
    experts = jnp.arange(N_EXPERTS, dtype=jnp.int32)
    dest = rank + jnp.sum(jnp.where(e_idx[:, :, None] == experts, pad_start.astype(jnp.int32), 0), axis=-1)
    per_step = GATHER_ROWS // MOE_BLOCK
    n_blocks = -(-(-(-n // MOE_BLOCK) + N_EXPERTS) // per_step) * per_step
    n_slots = n_blocks * MOE_BLOCK
    flat_dest = dest.reshape(-1)
    tok = jnp.tile(jnp.arange(T, dtype=jnp.int32), TOP_K)
    slot_tok = jnp.zeros((n_slots,), jnp.int32).at[flat_dest].set(tok)
    block_start = jnp.arange(n_blocks, dtype=jnp.int32) * MOE_BLOCK
    block_e = jnp.minimum(jnp.sum(block_start[:, None] >= pad_end[None, :], axis=1), N_EXPERTS - 1).astype(jnp.int32)
    n_used = (pad_end[-1] // MOE_BLOCK).astype(jnp.int32).reshape(1)
    h = h.reshape(T, -1, LANES)
    y = grouped_swiglu(gather_rows(h, slot_tok), block_e, n_used, exp_gate, exp_up, exp_down, layer)
    yg = jnp.take(y, flat_dest, axis=0, mode="clip").reshape(TOP_K, T, D)
    nb_sh = T // MOE_BLOCK
    sh4 = lambda w: w.reshape((w.shape[0], 1) + w.shape[1:])
    shared = grouped_swiglu(h, jnp.zeros((nb_sh,), jnp.int32), jnp.full((1,), nb_sh, jnp.int32),
                            sh4(sh_gate), sh4(sh_up), sh4(sh_down), layer)
    return yg, shared


def kernel(x, c, ctx, c_ctx, mod_w, mod_b, norm1_g, norm2_g, w_in, hy_conv, hy_w1, hy_b1, hy_w2, hy_b2, hy_freq,
           hy_w3, hy_bias, na_rpb, rw_shift, rw_w0, rw_w2, rw_a0, rw_a2, rw_g2, rw_kk, rw_ka, rw_rk, rw_ln_w,
           rw_ln_b, proj_a, proj_b, proj_c, w_out, router_w, router_b, exp_gate, exp_up, exp_down, sh_gate, sh_up,
           sh_down, final_g):
    B, L, D = x.shape
    Lc = ctx.shape[1]
    depth = mod_w.shape[0]
    n_lat, n_ctx = B * L, B * Lc
    assert L % TM == 0 and n_ctx % TM == 0 and L % WKV_CHUNK == 0 and Lc % WKV_CHUNK == 0
    col_hy = 3 * HY_WIDTH
    col_na = col_hy + 3 * NA_WIDTH
    col_rw = col_na + RW_COLS

    xs = jnp.concatenate([x.reshape(n_lat, D), ctx.reshape(n_ctx, D)], axis=0)
    grp_all = jnp.asarray(np.concatenate([np.repeat(np.arange(B), L // TM), np.full(n_ctx // TM, B)]), jnp.int32)
    s8 = jnp.zeros((8, D), F32).at[:B].set(jax.nn.silu(c)).at[B].set(jax.nn.silu(c_ctx))
    first, last = _seq_edge_masks(B, L, Lc)

    for i in range(depth):
        with_ctx = i < depth - 1
        mod = small_matmul_bias(s8, mod_w, mod_b, i)[:B + 1].reshape(B + 1, 1, N_MOD * D)
        sh1, sc1, g1, sh2, sc2, g2 = (mod[:, :, j * D:(j + 1) * D] for j in range(N_MOD))
        w_bf = w_in[i].astype(BF16)
        proj = functools.partial(normmod_matmul, xs, norm1_g[i], sh1, sc1, grp_all)
        hy = short_conv(proj(w_bf[:, :col_hy], 512, BF16), hy_conv[i], first, last, HY_WIDTH)
        na = proj(w_bf[:, col_hy:col_na], 512, BF16)
        rw = proj(w_bf[:, col_na:col_rw], 384, F32)
        gates = proj(w_bf[:, col_rw:], 512, BF16)

        hy_args = (hy_w1[i], hy_b1[i], hy_w2[i], hy_b2[i], hy_freq[i], hy_w3[i])
        h_raw, ss = hyena_filters_raw(L, *hy_args)
        hr, hi = hyena_filter_spectrum(h_raw, _filter_scale(ss), L)
        z = hyena_conv(hy, 0, hy, 2, hr, hi, 0, hy_bias[i][0], B, L)
        o_a = hyena_conv(hy, 1, z, 0, hr, hi, 1, hy_bias[i][1], B, L).astype(BF16)
        o_b = na_latent(na, na_rpb[i], B, L, Lc)
        rw = short_conv(rw, rw_shift[i], first, last, RW_COLS // 3)
        r_, v_, kk_, gg_, lw_, av_, kd_ = rwkv_prep(rw, rw_w0[i], rw_w2[i], rw_a0[i], rw_a2[i], rw_g2[i], rw_kk[i],
                                                    rw_ka[i])
        y_f, y_b = wkv_scan(r_, v_, kk_, lw_, av_, kd_, B, L, Lc)
        o_c = rwkv_out(y_f, y_b, r_, v_, gg_, kd_, rw_rk[i], rw_ln_w[i], rw_ln_b[i])

        if with_ctx:
            h_raw_c, ss_c = hyena_filters_raw(Lc, *hy_args)
            o_a_c = hyena_small(hy, n_lat // Lc, B, Lc, h_raw_c, _filter_scale(ss_c), hy_bias[i])
            o_a = jnp.concatenate([o_a, o_a_c], axis=0)
            o_b = jnp.concatenate([o_b, ctx_attn(na, B, L, Lc)], axis=0)
            m_rows = n_lat + n_ctx
        else:
            m_rows = n_lat
        grp = grp_all[:m_rows // TM]
        merged = branch_merge(m_rows, o_a, o_b, o_c, gates, proj_a[i].astype(BF16), proj_b[i].astype(BF16),
                              proj_c[i].astype(BF16))
        xs = resid_matmul(merged, w_out[i].astype(BF16), xs, g1, grp)

        h2, e_idx, e_w, rank, counts = route(xs, norm2_g[i], sh2, sc2, grp, router_w[i], router_b[i])
        yg, shared = _moe(h2, e_idx[:TOP_K], rank[:TOP_K], counts[:, 0], exp_gate, exp_up, exp_down, sh_gate, sh_up,
                          sh_down, i)
        xs = moe_combine(xs, yg, e_w.T, shared, g2, grp)

    return rmsnorm_rows(xs, final_g).reshape(B, L, D)
```

```python
import functools
import math

import jax
import jax.numpy as jnp
import numpy as np
from jax import lax
from jax.experimental import pallas as pl
from jax.experimental.pallas import tpu as pltpu

F32 = jnp.float32
BF16 = jnp.bfloat16
HI = lax.Precision.HIGHEST

GRID_W = 64
NORM_EPS = 1e-6
N_MOD = 6
SHORT_CONV = 3
HY_WIDTH = 1024
HY_BANDS = 16
HY_EMB = 2 * HY_BANDS + 1
HY_FILTER_ORDER = 64
HY_FAST_DECAY = 0.3
HY_SLOW_DECAY = 1.5
HY_DECAY_TARGET = 1e-2
NA_HEADS = 16
NA_HEAD_DIM = 64
NA_WIDTH = NA_HEADS * NA_HEAD_DIM
NA_ROWS = 8
NA_COLS = 16
RW_HEADS = 16
RW_HEAD_DIM = 64
RW_WIDTH = RW_HEADS * RW_HEAD_DIM
RW_DECAY_LORA = 64
RW_AAA_LORA = 64
RW_GATE_LORA = 128
RW_GN_EPS = 64e-5
RW_COLS = 3 * RW_WIDTH + 2 * RW_DECAY_LORA + 2 * RW_AAA_LORA + RW_GATE_LORA
RW_SPLITS = [RW_WIDTH, 2 * RW_WIDTH, 3 * RW_WIDTH, 3 * RW_WIDTH + 2 * RW_DECAY_LORA,
             3 * RW_WIDTH + 2 * RW_DECAY_LORA + 2 * RW_AAA_LORA]
N_BRANCH = 3
N_EXPERTS = 64
TOP_K = 6
N_GROUPS = 8
TOPK_GROUPS = 4
ROUTE_SCALE = 2.5

LANES = 128
VMEM_LIMIT = 56 * 1024 * 1024
TM = 512
MOE_BLOCK = 256
WKV_CHUNK = 64
WKV_GROUP = 4


def _cparams(sem):
    return pltpu.CompilerParams(dimension_semantics=sem, vmem_limit_bytes=VMEM_LIMIT)


def _dot(a, b, prec=None):
    return jnp.dot(a, b, preferred_element_type=F32, precision=prec)


def _dot_nt(a, b, prec=None):
    return lax.dot_general(a, b, (((1,), (1,)), ((), ())), preferred_element_type=F32, precision=prec)


def _dot_tn(a, b, prec=None):
    return lax.dot_general(a, b, (((0,), (0,)), ((), ())), preferred_element_type=F32, precision=prec)


def _small_mm_kernel(a_ref, w_ref, b_ref, o_ref):
    o_ref[...] = _dot(a_ref[...], w_ref[0], HI) + b_ref[0]


def small_matmul_bias(a, w, b, layer, tn=1536):
    m, k = a.shape
    n = w.shape[2]
    return pl.pallas_call(
        _small_mm_kernel,
        grid=(n // tn,),
        in_specs=[pl.BlockSpec((m, k), lambda j: (0, 0)),
                  pl.BlockSpec((1, k, tn), lambda j: (layer, 0, j)),
                  pl.BlockSpec((1, 1, tn), lambda j: (layer, 0, j))],
        out_specs=pl.BlockSpec((m, tn), lambda j: (0, j)),
        out_shape=jax.ShapeDtypeStruct((m, n), F32),
        compiler_params=_cparams(("arbitrary",)),
        name="mod_matmul",
    )(a, w, b.reshape(b.shape[0], 1, n))


def _normmod_mm_kernel(grp_ref, x_ref, g_ref, sh_ref, sc_ref, w_ref, o_ref, h_ref, *, hi):
    del grp_ref

    @pl.when(pl.program_id(1) == 0)
    def _():
        x = x_ref[...]
        y = x * lax.rsqrt(jnp.mean(x * x, axis=-1, keepdims=True) + NORM_EPS)
        y = y * g_ref[...]
        h_ref[...] = (y * (1.0 + sc_ref[0]) + sh_ref[0]).astype(h_ref.dtype)

    o_ref[...] = _dot(h_ref[...], w_ref[...], HI if hi else None).astype(o_ref.dtype)


def normmod_matmul(x, g, shift3, scale3, grp, w, tn, out_dtype, hi=False):
    m, d = x.shape
    n = w.shape[1]
    grid_spec = pltpu.PrefetchScalarGridSpec(
        num_scalar_prefetch=1,
        grid=(m // TM, n // tn),
        in_specs=[pl.BlockSpec((TM, d), lambda i, j, grp: (i, 0)),
                  pl.BlockSpec((1, d), lambda i, j, grp: (0, 0)),
                  pl.BlockSpec((1, 1, d), lambda i, j, grp: (grp[i], 0, 0)),
                  pl.BlockSpec((1, 1, d), lambda i, j, grp: (grp[i], 0, 0)),
                  pl.BlockSpec((d, tn), lambda i, j, grp: (0, j))],
        out_specs=pl.BlockSpec((TM, tn), lambda i, j, grp: (i, j)),
        scratch_shapes=[pltpu.VMEM((TM, d), F32 if hi else BF16)],
    )
    return pl.pallas_call(
        functools.partial(_normmod_mm_kernel, hi=hi),
        grid_spec=grid_spec,
        out_shape=jax.ShapeDtypeStruct((m, n), out_dtype),
        compiler_params=_cparams(("arbitrary", "arbitrary")),
        name="normmod_matmul",
    )(grp, x, g.reshape(1, d), shift3, scale3, w)


def _normmod_kernel(grp_ref, x_ref, g_ref, sh_ref, sc_ref, o_ref):
    del grp_ref
    x = x_ref[...]
    y = x * lax.rsqrt(jnp.mean(x * x, axis=-1, keepdims=True) + NORM_EPS)
    o_ref[...] = ((y * g_ref[...]) * (1.0 + sc_ref[0]) + sh_ref[0]).astype(o_ref.dtype)


def normmod(x, g, shift3, scale3, grp, out_dtype):
    m, d = x.shape
    grid_spec = pltpu.PrefetchScalarGridSpec(
        num_scalar_prefetch=1,
        grid=(m // TM,),
        in_specs=[pl.BlockSpec((TM, d), lambda i, grp: (i, 0)),
                  pl.BlockSpec((1, d), lambda i, grp: (0, 0)),
                  pl.BlockSpec((1, 1, d), lambda i, grp: (grp[i], 0, 0)),
                  pl.BlockSpec((1, 1, d), lambda i, grp: (grp[i], 0, 0))],
        out_specs=pl.BlockSpec((TM, d), lambda i, grp: (i, 0)),
    )
    return pl.pallas_call(
        _normmod_kernel,
        grid_spec=grid_spec,
        out_shape=jax.ShapeDtypeStruct((m, d), out_dtype),
        compiler_params=_cparams(("arbitrary",)),
        name="normmod",
    )(grp, x, g.reshape(1, d), shift3, scale3)


def _merge_kernel(oa_ref, ob_ref, oc_ref, ga_ref, gb_ref, gc_ref, pa_ref, pb_ref, pc_ref, o_ref):
    gate = lambda ref: jax.nn.sigmoid(ref[...].astype(F32))
    m = gate(ga_ref) * _dot(oa_ref[...], pa_ref[...])
    m = m + gate(gb_ref) * _dot(ob_ref[...], pb_ref[...])
    m = m + gate(gc_ref) * _dot(oc_ref[...], pc_ref[...])
    o_ref[...] = m.astype(o_ref.dtype)


def branch_merge(m, o_a, o_b, o_c, gates, pa, pb, pc, tn=1024):
    k = o_a.shape[1]
    d = pa.shape[1]
    nj = d // tn
    o_spec = pl.BlockSpec((TM, k), lambda i, j: (i, 0))
    p_spec = pl.BlockSpec((k, tn), lambda i, j: (0, j))
    return pl.pallas_call(
        _merge_kernel,
        grid=(m // TM, nj),
        in_specs=[o_spec, o_spec, o_spec,
                  pl.BlockSpec((TM, tn), lambda i, j: (i, j)),
                  pl.BlockSpec((TM, tn), lambda i, j: (i, j + nj)),
                  pl.BlockSpec((TM, tn), lambda i, j: (i, j + 2 * nj)),
                  p_spec, p_spec, p_spec],
        out_specs=pl.BlockSpec((TM, tn), lambda i, j: (i, j)),
        out_shape=jax.ShapeDtypeStruct((m, d), BF16),
        compiler_params=_cparams(("arbitrary", "arbitrary")),
        name="branch_merge",
    )(o_a, o_b, o_c, gates, gates, gates, pa, pb, pc)


def _resid_mm_kernel(grp_ref, a_ref, w_ref, x_ref, gate_ref, o_ref):
    del grp_ref
    o_ref[...] = x_ref[...] + gate_ref[0] * _dot(a_ref[...], w_ref[...])


def resid_matmul(a, w, x, gate3, grp, tn=1024):
    m, k = a.shape
    d = w.shape[1]
    grid_spec = pltpu.PrefetchScalarGridSpec(
        num_scalar_prefetch=1,
        grid=(m // TM, d // tn),
        in_specs=[pl.BlockSpec((TM, k), lambda i, j, grp: (i, 0)),
                  pl.BlockSpec((k, tn), lambda i, j, grp: (0, j)),
                  pl.BlockSpec((TM, tn), lambda i, j, grp: (i, j)),
                  pl.BlockSpec((1, 1, tn), lambda i, j, grp: (grp[i], 0, j))],
        out_specs=pl.BlockSpec((TM, tn), lambda i, j, grp: (i, j)),
    )
    return pl.pallas_call(
        _resid_mm_kernel,
        grid_spec=grid_spec,
        out_shape=jax.ShapeDtypeStruct((m, d), F32),
        compiler_params=_cparams(("arbitrary", "arbitrary")),
        name="resid_matmul",
    )(grp, a, w, x, gate3)


def _short_conv_kernel(x_ref, prev_ref, next_ref, w_ref, first_ref, last_ref, o_ref):
    x = x_ref[...].astype(F32)
    tm = x.shape[0]
    row = lax.broadcasted_iota(jnp.int32, x.shape, 0)
    halo = prev_ref.shape[0]
    before = prev_ref[...].astype(F32)[halo - 1:halo, :]
    after = next_ref[...].astype(F32)[0:1, :]
    prev = jnp.where(row == 0, before, pltpu.roll(x, 1, 0)) * first_ref[...]
    nxt = jnp.where(row == tm - 1, after, pltpu.roll(x, tm - 1, 0)) * last_ref[...]
    o_ref[...] = prev * w_ref[0:1, :] + x * w_ref[1:2, :] + nxt * w_ref[2:3, :]


def short_conv(u, w, first, last, tn):
    assert SHORT_CONV == 3
    m, c = u.shape
    halo = 16
    per = TM // halo
    n_halo = m // halo
    return pl.pallas_call(
        _short_conv_kernel,
        grid=(m // TM, c // tn),
        in_specs=[pl.BlockSpec((TM, tn), lambda i, j: (i, j)),
                  pl.BlockSpec((halo, tn), lambda i, j: (jnp.maximum(i * per - 1, 0), j)),
                  pl.BlockSpec((halo, tn), lambda i, j: (jnp.minimum((i + 1) * per, n_halo - 1), j)),
                  pl.BlockSpec((SHORT_CONV, tn), lambda i, j: (0, j)),
                  pl.BlockSpec((TM, 1), lambda i, j: (i, 0)),
                  pl.BlockSpec((TM, 1), lambda i, j: (i, 0))],
        out_specs=pl.BlockSpec((TM, tn), lambda i, j: (i, j)),
        out_shape=jax.ShapeDtypeStruct((m, c), F32),
        compiler_params=_cparams(("arbitrary", "arbitrary")),
        name="short_conv",
    )(u, u, u, w, first, last)


def _rmsnorm_kernel(x_ref, g_ref, o_ref):
    x = x_ref[...]
    y = x * lax.rsqrt(jnp.mean(x * x, axis=-1, keepdims=True) + NORM_EPS)
    o_ref[...] = y * g_ref[...]


def rmsnorm_rows(x, g):
    m, d = x.shape
    return pl.pallas_call(
        _rmsnorm_kernel,
        grid=(m // TM,),
        in_specs=[pl.BlockSpec((TM, d), lambda i: (i, 0)), pl.BlockSpec((1, d), lambda i: (0, 0))],
        out_specs=pl.BlockSpec((TM, d), lambda i: (i, 0)),
        out_shape=jax.ShapeDtypeStruct((m, d), F32),
        compiler_params=_cparams(("arbitrary",)),
        name="final_rmsnorm",
    )(x, g.reshape(1, d))


def _hyfilt_kernel(z_ref, w1_ref, b1_ref, w2_ref, b2_ref, fr_ref, w3_ref, dl_ref, h_ref, ss_ref):
    z = z_ref[...]
    hdn = jnp.sin(fr_ref[0:1, :] * (_dot(z, w1_ref[...], HI) + b1_ref[...]))
    hdn = jnp.sin(fr_ref[1:2, :] * (_dot(hdn, w2_ref[...], HI) + b2_ref[...]))
    h = _dot(hdn, w3_ref[...], HI)
    h = h * jnp.exp(-z[:, 0:1] * dl_ref[...])
    h_ref[...] = h

    @pl.when(pl.program_id(0) == 0)
    def _():
        ss_ref[...] = jnp.zeros_like(ss_ref)

    ss_ref[...] += jnp.sum(h * h, axis=0, keepdims=True)


def hyena_filters_raw(L, w1, b1, w2, b2, freq, w3):
    t = np.linspace(0.0, 1.0, L, dtype=np.float32)[:, None]
    omega = np.float32(2.0 * math.pi / L) * np.arange(L, dtype=np.float32)[:, None]
    bands = np.linspace(1e-4, HY_BANDS - 1, HY_BANDS, dtype=np.float32)[None, :]
    z = np.concatenate([t, np.cos(omega * bands), -np.sin(omega * bands),
                        np.zeros((L, HY_FILTER_ORDER - HY_EMB), np.float32)], axis=-1).astype(np.float32)
    w1p = jnp.concatenate([w1, jnp.zeros((HY_FILTER_ORDER - HY_EMB, HY_FILTER_ORDER), F32)], axis=0)
    deltas = np.abs(np.linspace(math.log(HY_DECAY_TARGET) / HY_SLOW_DECAY,
                                math.log(HY_DECAY_TARGET) / HY_FAST_DECAY, HY_WIDTH, dtype=np.float32))
    dl4 = np.tile(deltas, 4)[None, :]
    tl = min(L, 256)
    n = 4 * HY_WIDTH
    fo = HY_FILTER_ORDER
    full = lambda shape: pl.BlockSpec(shape, lambda i: (0, 0))
    return pl.pallas_call(
        _hyfilt_kernel,
        grid=(L // tl,),
        in_specs=[pl.BlockSpec((tl, fo), lambda i: (i, 0)), full((fo, fo)), full((1, fo)), full((fo, fo)),
                  full((1, fo)), full((2, fo)), full((fo, n)), full((1, n))],
        out_specs=[pl.BlockSpec((tl, n), lambda i: (i, 0)), full((1, n))],
        out_shape=[jax.ShapeDtypeStruct((L, n), F32), jax.ShapeDtypeStruct((1, n), F32)],
        compiler_params=_cparams(("arbitrary",)),
        name="hyena_filters",
    )(jnp.asarray(z), w1p, b1.reshape(1, fo), w2, b2.reshape(1, fo), freq, w3, jnp.asarray(dl4))


def _filter_scale(ss):
    s = ss.reshape(2, 2, HY_WIDTH)
    rs = lax.rsqrt(jnp.sum(s, axis=1, keepdims=True))
    return jnp.broadcast_to(rs, (2, 2, HY_WIDTH)).reshape(1, 4 * HY_WIDTH)


FFT_N1 = 128
FFT_N2 = 64
FFT_PITCH = 72
FFT_BATCH = 4


@functools.lru_cache(maxsize=None)
def _fft_tables():
    n1, n2 = FFT_N1, FFT_N2
    n = n1 * n2
    a = np.arange(n1 // 2)[None, None, :]
    k1 = np.arange(n1)[None, :, None]
    b = np.arange(n2)[:, None, None]
    theta = 2.0 * np.pi * ((a * k1 % n1) / n1 + (b * k1) / n)
    g = np.concatenate([np.cos(theta), -np.sin(theta)], axis=1)
    ig = np.concatenate([np.cos(theta), -np.sin(theta)], axis=1).transpose(0, 2, 1) / n
    k2 = np.arange(n2)[:, None]
    bb = np.arange(n2)[None, :]
    ph = 2.0 * np.pi * (k2 * bb % n2) / n2
    fr, fi = np.cos(ph), -np.sin(ph)
    f2 = np.block([[fr, -fi], [fi, fr]])
    if2 = np.block([[fr, fi], [-fi, fr]])
    return (jnp.asarray(g, BF16), jnp.asarray(f2, BF16), jnp.asarray(if2, BF16), jnp.asarray(ig, BF16))


def _fft_stage1(u_ref, g_ref, sr_ref, si_ref):
    n1, n2, p = FFT_N1, FFT_N2, FFT_PITCH

    def body(i, carry):
        bs = [i * FFT_BATCH + j for j in range(FFT_BATCH)]
        xs = [u_ref[pl.ds(b, n1 // 2, stride=n2), :].astype(BF16) for b in bs]
        outs = [_dot(g_ref[b], x) for b, x in zip(bs, xs)]
        for b, a in zip(bs, outs):
            sr_ref[pl.ds(b, n1, stride=p), :] = a[:n1]
            si_ref[pl.ds(b, n1, stride=p), :] = a[n1:]
        return carry

    lax.fori_loop(0, n2 // FFT_BATCH, body, 0)


def _bin_rows(k1):
    return pl.ds(pl.multiple_of(k1 * FFT_PITCH, 8), FFT_N2)


def _hyconv_kernel(xm_ref, u_ref, hr_ref, hi_ref, bias_ref, g_ref, f2_ref, if2_ref, ig_ref, o_ref, sr_ref, si_ref):
    n1, n2, p = FFT_N1, FFT_N2, FFT_PITCH
    _fft_stage1(u_ref, g_ref, sr_ref, si_ref)

    def pair(ref, k):
        return jnp.concatenate([ref[_bin_rows(2 * k), :], ref[_bin_rows(2 * k + 1), :]], axis=1)

    def pair_h(ref, k):
        blk = ref[pl.ds(pl.multiple_of(k * 2 * n2, 2 * n2), 2 * n2), :]
        return jnp.concatenate([blk[:n2], blk[n2:]], axis=1)

    def unpair(ref, k, val):
        w = val.shape[1] // 2
        ref[_bin_rows(2 * k), :] = val[:, :w]
        ref[_bin_rows(2 * k + 1), :] = val[:, w:]

    def stage2(i, carry):
        ks = [i * FFT_BATCH + j for j in range(FFT_BATCH)]
        zs = [jnp.concatenate([pair(sr_ref, k), pair(si_ref, k)], axis=0).astype(BF16) for k in ks]
        xs = [_dot(f2_ref[...], z) for z in zs]
        ys = []
        for k, x in zip(ks, xs):
            xr, xi = x[:n2], x[n2:]
            hr, hi = pair_h(hr_ref, k), pair_h(hi_ref, k)
            ys.append(jnp.concatenate([xr * hr - xi * hi, xr * hi + xi * hr], axis=0).astype(BF16))
        bbs = [_dot(if2_ref[...], y) for y in ys]
        for k, bb in zip(ks, bbs):
            unpair(sr_ref, k, bb[:n2])
            unpair(si_ref, k, bb[n2:])
        return carry

    lax.fori_loop(0, n1 // 2 // FFT_BATCH, stage2, 0)

    def stage3(i, carry):
        bs = [i * FFT_BATCH + j for j in range(FFT_BATCH)]
        sts = [jnp.concatenate([sr_ref[pl.ds(b, n1, stride=p), :], si_ref[pl.ds(b, n1, stride=p), :]],
                               axis=0).astype(BF16) for b in bs]
        outs = [_dot(ig_ref[b], st) for b, st in zip(bs, sts)]
        for b, o in zip(bs, outs):
            o_ref[pl.ds(b, n1 // 2, stride=n2), :] = o
        return carry

    lax.fori_loop(0, n2 // FFT_BATCH, stage3, 0)
    bias = bias_ref[...]
    rows_per_pass = 512

    def finish(i, carry):
        rows = pl.ds(pl.multiple_of(i * rows_per_pass, rows_per_pass), rows_per_pass)
        o_ref[rows, :] = xm_ref[rows, :] * (o_ref[rows, :] + u_ref[rows, :] * bias)
        return carry

    lax.fori_loop(0, o_ref.shape[0] // rows_per_pass, finish, 0)


def hyena_conv(xm_arr, xm_col, u_arr, u_col, hr, hi, h_col, bias, n_batch, L):
    assert L == FFT_N1 * FFT_N2 // 2
    cb = LANES
    nct = HY_WIDTH // cb
    n = 2 * L
    g, f2, if2, ig = _fft_tables()
    const3 = lambda shape: pl.BlockSpec(shape, lambda b, c: (0, 0, 0))
    const2 = lambda shape: pl.BlockSpec(shape, lambda b, c: (0, 0))
    return pl.pallas_call(
        _hyconv_kernel,
        grid=(n_batch, nct),
        in_specs=[pl.BlockSpec((L, cb), lambda b, c: (b, xm_col * nct + c)),
                  pl.BlockSpec((L, cb), lambda b, c: (b, u_col * nct + c)),
                  pl.BlockSpec((n, cb), lambda b, c: (0, h_col * nct + c)),
                  pl.BlockSpec((n, cb), lambda b, c: (0, h_col * nct + c)),
                  pl.BlockSpec((1, cb), lambda b, c: (0, c)),
                  const3(g.shape), const2(f2.shape), const2(if2.shape), const3(ig.shape)],
        out_specs=pl.BlockSpec((L, cb), lambda b, c: (b, c)),
        out_shape=jax.ShapeDtypeStruct((n_batch * L, HY_WIDTH), F32),
        scratch_shapes=[pltpu.VMEM((FFT_N1 * FFT_PITCH, cb), F32)] * 2,
        compiler_params=_cparams(("arbitrary", "arbitrary")),
        name="hyena_conv",
    )(xm_arr, u_arr, hr, hi, bias.reshape(1, HY_WIDTH), g, f2, if2, ig)


def _hyspec_kernel(h0_ref, h1_ref, rs_ref, g_ref, f2_ref, hr_ref, hi_ref, s0r, s0i, s1r, s1i):
    n1, n2 = FFT_N1, FFT_N2
    _fft_stage1(h0_ref, g_ref, s0r, s0i)
    _fft_stage1(h1_ref, g_ref, s1r, s1i)
    rs = rs_ref[...]
    h10 = h1_ref[0:1, :]

    def stage2(i, carry):
        ks = [i * FFT_BATCH + j for j in range(FFT_BATCH)]
        zs = [jnp.concatenate([jnp.concatenate([s0r[_bin_rows(k), :], s1r[_bin_rows(k), :]], axis=1),
                               jnp.concatenate([s0i[_bin_rows(k), :], s1i[_bin_rows(k), :]], axis=1)],
                              axis=0).astype(BF16) for k in ks]
        xs = [_dot(f2_ref[...], z) for z in zs]
        for k, x in zip(ks, xs):
            rows = pl.ds(pl.multiple_of(k * n2, n2), n2)
            w = x.shape[1] // 2
            hr_ref[rows, :] = rs * (x[:n2, :w] + x[:n2, w:] - h10)
            hi_ref[rows, :] = rs * (x[n2:, :w] - x[n2:, w:])
        return carry

    lax.fori_loop(0, n1 // FFT_BATCH, stage2, 0)


def hyena_filter_spectrum(h_raw, rs, L):
    assert L == FFT_N1 * FFT_N2 // 2
    cb = LANES
    nct = HY_WIDTH // cb
    n = 2 * L
    g, f2, _, _ = _fft_tables()
    out_spec = pl.BlockSpec((n, cb), lambda o, c: (0, o * nct + c))
    scr = pltpu.VMEM((FFT_N1 * FFT_PITCH, cb), F32)
    return pl.pallas_call(
        _hyspec_kernel,
        grid=(2, nct),
        in_specs=[pl.BlockSpec((L, cb), lambda o, c: (0, (2 * o) * nct + c)),
                  pl.BlockSpec((L, cb), lambda o, c: (0, (2 * o + 1) * nct + c)),
                  pl.BlockSpec((1, cb), lambda o, c: (0, (2 * o) * nct + c)),
                  pl.BlockSpec(g.shape, lambda o, c: (0, 0, 0)),
                  pl.BlockSpec(f2.shape, lambda o, c: (0, 0))],
        out_specs=[out_spec, out_spec],
        out_shape=[jax.ShapeDtypeStruct((n, 2 * HY_WIDTH), F32)] * 2,
        scratch_shapes=[scr, scr, scr, scr],
        compiler_params=_cparams(("arbitrary", "arbitrary")),
        name="hyena_filter_spectrum",
    )(h_raw, h_raw, rs, g, f2)


@functools.lru_cache(maxsize=None)
def _dense_dft_tables(L):
    n = 2 * L
    k = np.arange(n)[:, None]
    t = np.arange(L)[None, :]
    ph = 2.0 * np.pi * (k * t % n) / n
    fwd = np.concatenate([np.cos(ph), -np.sin(ph)], axis=0)
    inv = np.concatenate([np.cos(ph), -np.sin(ph)], axis=0).T / n
    return jnp.asarray(fwd, F32), jnp.asarray(inv, F32)


def _hyena_small_kernel(x1_ref, x2_ref, v_ref, h_ref0a, h_ref0b, h_ref1a, h_ref1b, rs0_ref, rs1_ref,
                        b0_ref, b1_ref, fwd_ref, inv_ref, o_ref, *, L):
    n = 2 * L
    fwd = fwd_ref[...]
    inv = inv_ref[...]

    def conv(u, ha_ref, hb_ref, rs_ref, bias_ref):
        ha, hb = ha_ref[...], hb_ref[...]
        ka = _dot(fwd, ha, HI)
        kb = _dot(fwd, hb, HI)
        rs = rs_ref[...]
        kr = rs * (ka[:n] + kb[:n] - hb[0:1, :])
        ki = rs * (ka[n:] - kb[n:])
        uf = _dot(fwd, u, HI)
        ur, ui = uf[:n], uf[n:]
        y = jnp.concatenate([ur * kr - ui * ki, ur * ki + ui * kr], axis=0)
        return _dot(inv, y, HI) + u * bias_ref[...]

    v = v_ref[...]
    z = x1_ref[...] * conv(v, h_ref0a, h_ref0b, rs0_ref, b0_ref)
    o_ref[...] = (x2_ref[...] * conv(z, h_ref1a, h_ref1b, rs1_ref, b1_ref)).astype(o_ref.dtype)


def hyena_small(u_arr, row0_blocks, n_batch, L, h_raw, rs, bias):
    cb = LANES
    nct = HY_WIDTH // cb
    fwd, inv = _dense_dft_tables(L)
    uspec = lambda col: pl.BlockSpec((L, cb), lambda b, c: (row0_blocks + b, col * nct + c))
    hspec = lambda col: pl.BlockSpec((L, cb), lambda b, c: (0, col * nct + c))
    rspec = lambda col: pl.BlockSpec((1, cb), lambda b, c: (0, col * nct + c))
    bspec = pl.BlockSpec((1, cb), lambda b, c: (0, c))
    bias_0 = bias[0].reshape(1, HY_WIDTH)
    bias_1 = bias[1].reshape(1, HY_WIDTH)
    return pl.pallas_call(
        functools.partial(_hyena_small_kernel, L=L),
        grid=(n_batch, nct),
        in_specs=[uspec(0), uspec(1), uspec(2), hspec(0), hspec(1), hspec(2), hspec(3), rspec(0), rspec(2),
                  bspec, bspec,
                  pl.BlockSpec(fwd.shape, lambda b, c: (0, 0)), pl.BlockSpec(inv.shape, lambda b, c: (0, 0))],
        out_specs=pl.BlockSpec((L, cb), lambda b, c: (b, c)),
        out_shape=jax.ShapeDtypeStruct((n_batch * L, HY_WIDTH), BF16),
        compiler_params=_cparams(("arbitrary", "arbitrary")),
        name="hyena_ctx",
    )(u_arr, u_arr, u_arr, h_raw, h_raw, h_raw, h_raw, rs, rs, bias_0, bias_1, fwd, inv)


def _na_bias_table(rpb):
    cols = np.arange(GRID_W)
    col_start = np.clip(cols - NA_COLS // 2, 0, GRID_W - NA_COLS)[:, None]
    in_win = (cols[None, :] >= col_start) & (cols[None, :] < col_start + NA_COLS)
    rel_col = np.clip(cols[None, :] - cols[:, None], 1 - NA_COLS, NA_COLS - 1) + NA_COLS - 1
    tbl = rpb.astype(F32)[:, :, rel_col]
    tbl = jnp.where(jnp.asarray(in_win)[None, None], tbl, -jnp.inf)
    return jnp.concatenate([tbl[:, :-1], tbl[:, 1:]], axis=-1)


def _na_kernel(*refs, n_rows):
    q_ref = refs[0]
    k_refs = refs[1:1 + NA_ROWS]
    v_refs = refs[1 + NA_ROWS:1 + 2 * NA_ROWS]
    kc_ref, vc_ref, tbl_ref, o_ref = refs[1 + 2 * NA_ROWS:]
    r = pl.program_id(1)
    start = jnp.clip(r - NA_ROWS // 2, 0, n_rows - NA_ROWS)
    d0 = start - r + NA_ROWS - 1
    dh = NA_HEAD_DIM
    q = q_ref[...] * (dh ** -0.5)
    n_pairs = NA_ROWS // 2
    heads = [slice(h * dh, (h + 1) * dh) for h in range(NA_HEADS)]
    scores = []
    for h, hs in enumerate(heads):
        qh = q[:, hs]
        tiles = [_dot_nt(qh, jnp.concatenate([k_refs[2 * p][:, hs], k_refs[2 * p + 1][:, hs]], axis=0))
                 + tbl_ref[h, d0 + 2 * p] for p in range(n_pairs)]
        scores.append(tiles + [_dot_nt(qh, kc_ref[:, hs])])
    probs, denoms = [], []
    for tiles in scores:
        m = tiles[0].max(axis=-1, keepdims=True)
        for s in tiles[1:]:
            m = jnp.maximum(m, s.max(axis=-1, keepdims=True))
        ps = [jnp.exp(s - m) for s in tiles]
        l = ps[0].sum(axis=-1, keepdims=True)
        for p_ in ps[1:]:
            l = l + p_.sum(axis=-1, keepdims=True)
        probs.append([p_.astype(BF16) for p_ in ps])
        denoms.append(l)
    for hs, ps, l in zip(heads, probs, denoms):
        acc = _dot(ps[-1], vc_ref[:, hs])
        for p in range(n_pairs):
            acc = acc + _dot(ps[p], jnp.concatenate([v_refs[2 * p][:, hs], v_refs[2 * p + 1][:, hs]], axis=0))
        o_ref[:, hs] = (acc / l).astype(o_ref.dtype)


def na_latent(na, rpb, n_batch, L, Lc):
    n_rows = L // GRID_W
    assert n_rows >= NA_ROWS
    tbl = _na_bias_table(rpb)
    w = NA_WIDTH
    ctx_blk0 = n_batch * L // Lc

    def kv_spec(i, col):
        def imap(b, r):
            start = jnp.clip(r - NA_ROWS // 2, 0, n_rows - NA_ROWS)
            return (b * n_rows + start + i, col)
        return pl.BlockSpec((GRID_W, w), imap)

    in_specs = ([pl.BlockSpec((GRID_W, w), lambda b, r: (b * n_rows + r, 0))]
                + [kv_spec(i, 1) for i in range(NA_ROWS)] + [kv_spec(i, 2) for i in range(NA_ROWS)]
                + [pl.BlockSpec((Lc, w), lambda b, r: (ctx_blk0 + b, 1)),
                   pl.BlockSpec((Lc, w), lambda b, r: (ctx_blk0 + b, 2)),
                   pl.BlockSpec(tbl.shape, lambda b, r: (0, 0, 0, 0))])
    return pl.pallas_call(
        functools.partial(_na_kernel, n_rows=n_rows),
        grid=(n_batch, n_rows),
        in_specs=in_specs,
        out_specs=pl.BlockSpec((GRID_W, w), lambda b, r: (b * n_rows + r, 0)),
        out_shape=jax.ShapeDtypeStruct((n_batch * L, w), BF16),
        compiler_params=_cparams(("arbitrary", "arbitrary")),
        name="na_latent",
    )(*([na] * (3 + 2 * NA_ROWS)), tbl)


def _ctx_attn_kernel(q_ref, k_ref, v_ref, o_ref):
    dh = NA_HEAD_DIM
    q = q_ref[...] * (dh ** -0.5)
    for h in range(NA_HEADS):
        hs = slice(h * dh, (h + 1) * dh)
        s = _dot_nt(q[:, hs], k_ref[:, hs])
        p_ = jnp.exp(s - s.max(axis=-1, keepdims=True))
        acc = _dot(p_.astype(BF16), v_ref[:, hs])
        o_ref[:, hs] = (acc / p_.sum(axis=-1, keepdims=True)).astype(o_ref.dtype)


def ctx_attn(na, n_batch, L, Lc):
    w = NA_WIDTH
    blk0 = n_batch * L // Lc
    spec = lambda col: pl.BlockSpec((Lc, w), lambda b: (blk0 + b, col))
    return pl.pallas_call(
        _ctx_attn_kernel,
        grid=(n_batch,),
        in_specs=[spec(0), spec(1), spec(2)],
        out_specs=pl.BlockSpec((Lc, w), lambda b: (b, 0)),
        out_shape=jax.ShapeDtypeStruct((n_batch * Lc, w), BF16),
        compiler_params=_cparams(("arbitrary",)),
        name="ctx_attn",
    )(na, na, na)


@functools.lru_cache(maxsize=None)
def _wkv_masks():
    c, g = WKV_CHUNK, WKV_GROUP
    t = np.arange(c)[:, None]
    s = np.arange(c)[None, :]
    tinc = np.stack([(s <= t), (s >= t)]).astype(np.float32)
    strict = np.stack([(s < t), (s > t)]).astype(np.float32)
    tile = lambda m: np.tile(m, (1,) * (m.ndim - 1) + (g,))
    blk = lambda n: (t // n == s // n)
    blk16 = tile(blk(16).astype(np.float32))
    off32 = tile((blk(32) & ~blk(16)).astype(np.float32))
    off64 = tile((~blk(32)).astype(np.float32))
    eye = tile((t == s).astype(np.float32))
    rr = np.arange(g * c)
    hm = (rr[:, None] // c == np.arange(g * RW_HEAD_DIM)[None, :] // RW_HEAD_DIM).astype(np.float32)
    masks = tuple(jnp.asarray(m) for m in (tinc, tile(strict), tile(tinc), blk16, off32, off64, eye, hm))
    return masks + (jnp.asarray(hm, BF16),)


def _wkv_kernel(*refs):
    (r0_ref, v0_ref, kk0_ref, r1_ref, v1_ref, kk1_ref, lw0_ref, av0_ref, kd0_ref, lw1_ref, av1_ref, kd1_ref,
     tinc_ref, strict_ref, incl_ref, blk16_ref, off32_ref, off64_ref, eye_ref, hm_ref, hmb_ref,
     y0_ref, y1_ref, state_ref) = refs
    c, g = WKV_CHUNK, WKV_GROUP
    gw = g * RW_HEAD_DIM

    @pl.when(pl.program_id(1) == 0)
    def _():
        state_ref[...] = jnp.zeros_like(state_ref)

    hm = hm_ref[...]
    hm_bf = hmb_ref[...]
    blk16, off32, off64, eye = blk16_ref[...], off32_ref[...], off64_ref[...], eye_ref[...]

    def bdiag(z):
        return jnp.concatenate([z.astype(BF16)] * g, axis=0) * hm_bf

    def pm(x4, zd):
        return _dot(x4.astype(BF16), zd)

    def prepare(d, r_ref, kk_ref, lw_ref, av_ref, kd_ref):
        lw = lw_ref[0]
        cum = _dot(tinc_ref[d], lw, HI)
        tot = jnp.sum(lw, axis=0, keepdims=True)
        e_neg = jnp.exp(-cum)
        e_rem = jnp.exp(tot - cum)
        kk = kk_ref[...]
        b_vec = kk * av_ref[0]
        kd = kd_ref[0]
        return dict(at=-kk * jnp.exp(cum - lw), rt=r_ref[...] * jnp.exp(cum), bt=b_vec * e_neg, kt=kd * e_neg,
                    bp=b_vec * e_rem, kp=kd * e_rem, e_tot=jnp.exp(tot))

    qs = (prepare(0, r0_ref, kk0_ref, lw0_ref, av0_ref, kd0_ref), prepare(1, r1_ref, kk1_ref, lw1_ref, av1_ref, kd1_ref))
    v_refs, y_refs = (v0_ref, v1_ref), (y0_ref, y1_ref)
    chains = [(d, gi) for gi in range(RW_HEADS // g) for d in range(2)]
    sl = lambda gi: slice(gi * gw, (gi + 1) * gw)
    each = lambda f, *lists: [f(*args) for args in zip(*lists)]
    pm_all = lambda xs, zs: each(lambda x, z: pm(x, bdiag(z)), xs, zs)

    ar = [jnp.concatenate([qs[d]["at"][:, sl(gi)], qs[d]["rt"][:, sl(gi)]], axis=0).astype(BF16) for d, gi in chains]
    pb = [_dot_nt(a, bdiag(qs[d]["bt"][:, sl(gi)])) for a, (d, gi) in zip(ar, chains)]
    pk = [_dot_nt(a, bdiag(qs[d]["kt"][:, sl(gi)])) for a, (d, gi) in zip(ar, chains)]
    a_ab = [p[:c] * strict_ref[d] for p, (d, gi) in zip(pb, chains)]
    a_rb = [p[c:] * incl_ref[d] for p, (d, gi) in zip(pb, chains)]
    a_ak = [p[:c] * strict_ref[d] for p, (d, gi) in zip(pk, chains)]
    a_rk = [p[c:] * incl_ref[d] for p, (d, gi) in zip(pk, chains)]
    ad = [a * blk16 for a in a_ab]
    a2 = pm_all(ad, ad)
    a4 = pm_all(a2, a2)
    a8 = pm_all(a4, a4)
    tinv = [eye + a for a in ad]
    for powr in (a2, a4, a8):
        tinv = each(lambda t, p_: t + p_, tinv, pm_all(tinv, powr))
    for off in (off32, off64):
        mid = pm_all(tinv, [a * off for a in a_ab])
        tinv = each(lambda t, p_: t + p_, tinv, pm_all(mid, tinv))
    s0 = [state_ref[d, gi] for d, gi in chains]
    vv = [v_refs[d][:, sl(gi)] for d, gi in chains]
    vd = [bdiag(v_) for v_ in vv]
    ars = each(lambda a, s_: _dot_nt(a, s_.astype(BF16)), ar, s0)
    akv = each(pm, a_ak, vd)
    u = pm_all(tinv, each(lambda x, y_: x[:c] + y_, ars, akv))
    yu = pm_all(a_rb, u)
    yv = each(pm, a_rk, vd)
    for (d, gi), x, y1_, y2_ in zip(chains, ars, yu, yv):
        y_refs[d][:, sl(gi)] = x[c:] + y1_ + y2_
    upd = [_dot_tn(jnp.concatenate([u_, v_], axis=0).astype(BF16),
                   jnp.concatenate([qs[d]["bp"][:, sl(gi)], qs[d]["kp"][:, sl(gi)]], axis=0).astype(BF16))
           for u_, v_, (d, gi) in zip(u, vv, chains)]
    for (d, gi), s_, up in zip(chains, s0, upd):
        state_ref[d, gi] = s_ * qs[d]["e_tot"][:, sl(gi)] + hm * up


def wkv_scan(r, v, kk, lw, av, kd, n_batch, L, Lc):
    c = WKV_CHUNK
    rows, w = r.shape
    nc, nl = Lc // c, L // c
    masks = _wkv_masks()

    def blk(d, b, s):
        j_ctx = s if d == 0 else nc - 1 - s
        j_lat = s - nc if d == 0 else nl - 1 - (s - nc)
        return jnp.where(s < nc, (n_batch * L + b * Lc) // c + j_ctx, (b * L) // c + j_lat)

    shared = lambda d: pl.BlockSpec((c, w), lambda b, s: (blk(d, b, s), 0))
    perdir = lambda d: pl.BlockSpec((1, c, w), lambda b, s: (d, blk(d, b, s), 0))
    full = lambda m: pl.BlockSpec(m.shape, lambda b, s: (0,) * m.ndim)
    gw = WKV_GROUP * RW_HEAD_DIM
    return pl.pallas_call(
        _wkv_kernel,
        grid=(n_batch, nc + nl),
        in_specs=[shared(0)] * 3 + [shared(1)] * 3 + [perdir(0)] * 3 + [perdir(1)] * 3 + [full(m) for m in masks],
        out_specs=[shared(0), shared(1)],
        out_shape=[jax.ShapeDtypeStruct((rows, w), F32)] * 2,
        scratch_shapes=[pltpu.VMEM((2, RW_HEADS // WKV_GROUP, gw, gw), F32)],
        compiler_params=_cparams(("arbitrary", "arbitrary")),
        name="wkv_scan",
    )(r, v, kk, r, v, kk, lw, av, kd, lw, av, kd, *masks)


def _pack_bf16_pairs(h):
    half = h.shape[1] // 2
    bits = lambda t: lax.bitcast_convert_type(t.astype(BF16).astype(F32), jnp.uint32)
    return (bits(h[:, :half]) >> 16) | (bits(h[:, half:]) & jnp.uint32(0xFFFF0000))


def _unpack_bf16_pairs(p):
    lo = lax.bitcast_convert_type(p << 16, F32).astype(BF16)
    hi = lax.bitcast_convert_type(p & jnp.uint32(0xFFFF0000), F32).astype(BF16)
    return jnp.concatenate([lo, hi], axis=1)


GATHER_ROWS = 1024


def _gather_kernel(idx_ref, tab_ref, out_ref, sem):
    batch = 8

    def issue(i, carry):
        for j in range(batch):
            r = i * batch + j
            tok = idx_ref[0, 0, r]
            pltpu.async_copy(tab_ref.at[pl.ds(tok, 1)], out_ref.at[pl.ds(r, 1)], sem, priority=j % 2)
        return carry

    lax.fori_loop(0, GATHER_ROWS // batch, issue, 0)
    pltpu.make_async_copy(tab_ref.at[pl.ds(0, GATHER_ROWS)], out_ref, sem).wait()


def gather_rows(table, idx):
    n = idx.shape[0]
    assert n % GATHER_ROWS == 0 and table.shape[0] >= GATHER_ROWS
    nb = n // GATHER_ROWS
    return pl.pallas_call(
        _gather_kernel,
        grid=(nb,),
        in_specs=[pl.BlockSpec((1, 1, GATHER_ROWS), lambda i: (i, 0, 0), memory_space=pltpu.SMEM),
                  pl.BlockSpec(memory_space=pl.ANY)],
        out_specs=pl.BlockSpec((GATHER_ROWS,) + table.shape[1:], lambda i: (i, 0, 0)),
        out_shape=jax.ShapeDtypeStruct((n,) + table.shape[1:], table.dtype),
        scratch_shapes=[pltpu.SemaphoreType.DMA],
        compiler_params=pltpu.CompilerParams(dimension_semantics=("arbitrary",), disable_bounds_checks=True,
                                             vmem_limit_bytes=VMEM_LIMIT),
        name="dispatch_gather",
    )(idx.reshape(nb, 1, GATHER_ROWS), table)


def _moe_kernel(be_ref, nb_ref, x_ref, wg_ref, wu_ref, wd_ref, o_ref, wg_s, wu_s, wd_s):
    i = pl.program_id(0)
    prev = be_ref[jnp.maximum(i - 1, 0)]

    @pl.when((i == 0) | (be_ref[i] != prev))
    def _():
        wg_s[...] = wg_ref[0, 0].astype(BF16)
        wu_s[...] = wu_ref[0, 0].astype(BF16)
        wd_s[...] = wd_ref[0, 0].astype(BF16)

    @pl.when(i < nb_ref[0])
    def _():
        x = _unpack_bf16_pairs(jnp.concatenate([x_ref[:, s, :] for s in range(x_ref.shape[1])], axis=1))
        hmid = (jax.nn.silu(_dot(x, wg_s[...])) * _dot(x, wu_s[...])).astype(BF16)
        o_ref[...] = _dot(hmid, wd_s[...]).astype(o_ref.dtype)

    @pl.when(i >= nb_ref[0])
    def _():
        o_ref[...] = jnp.zeros_like(o_ref)


def grouped_swiglu(x, block_e, n_used, w_gate, w_up, w_down, layer):
    d, ff = w_gate.shape[-2:]
    nb = x.shape[0] // MOE_BLOCK
    grid_spec = pltpu.PrefetchScalarGridSpec(
        num_scalar_prefetch=2,
        grid=(nb,),
        in_specs=[pl.BlockSpec((MOE_BLOCK,) + x.shape[1:], lambda i, be, nu: (i, 0, 0)),
                  pl.BlockSpec((1, 1, d, ff), lambda i, be, nu: (layer, be[i], 0, 0)),
                  pl.BlockSpec((1, 1, d, ff), lambda i, be, nu: (layer, be[i], 0, 0)),
                  pl.BlockSpec((1, 1, ff, d), lambda i, be, nu: (layer, be[i], 0, 0))],
        out_specs=pl.BlockSpec((MOE_BLOCK, d), lambda i, be, nu: (i, 0)),
        scratch_shapes=[pltpu.VMEM((d, ff), BF16), pltpu.VMEM((d, ff), BF16), pltpu.VMEM((ff, d), BF16)],
    )
    return pl.pallas_call(
        _moe_kernel,
        grid_spec=grid_spec,
        out_shape=jax.ShapeDtypeStruct((nb * MOE_BLOCK, d), BF16),
        compiler_params=_cparams(("arbitrary",)),
        name="grouped_swiglu",
    )(block_e, n_used, x, w_gate, w_up, w_down)


def _combine_kernel(grp_ref, x_ref, y_ref, w_ref, s_ref, gate_ref, o_ref):
    del grp_ref
    f = s_ref[...].astype(F32)
    for k in range(TOP_K):
        f = f + w_ref[:, k:k + 1] * y_ref[k].astype(F32)
    o_ref[...] = x_ref[...] + gate_ref[0] * f


def moe_combine(x, yg, e_w, shared, gate3, grp):
    m, d = x.shape
    tm = TM // 2
    grid_spec = pltpu.PrefetchScalarGridSpec(
        num_scalar_prefetch=1,
        grid=(m // tm,),
        in_specs=[pl.BlockSpec((tm, d), lambda i, grp: (i, 0)),
                  pl.BlockSpec((TOP_K, tm, d), lambda i, grp: (0, i, 0)),
                  pl.BlockSpec((tm, e_w.shape[1]), lambda i, grp: (i, 0)),
                  pl.BlockSpec((tm, d), lambda i, grp: (i, 0)),
                  pl.BlockSpec((1, 1, d), lambda i, grp: (grp[i // 2], 0, 0))],
        out_specs=pl.BlockSpec((tm, d), lambda i, grp: (i, 0)),
    )
    return pl.pallas_call(
        _combine_kernel,
        grid_spec=grid_spec,
        out_shape=jax.ShapeDtypeStruct((m, d), F32),
        compiler_params=_cparams(("arbitrary",)),
        name="moe_combine",
    )(grp, x, yg, e_w, shared, gate3)


def _route_kernel(grp_ref, x_ref, g_ref, sh_ref, sc_ref, wt_ref, rb_ref, tri_ref, ones_ref,
                  h_ref, idx_ref, w_ref, rank_ref, cnt_ref, carry_ref):
    del grp_ref
    tm = x_ref.shape[0]
    gs = N_EXPERTS // N_GROUPS
    neg = -jnp.inf

    @pl.when(pl.program_id(0) == 0)
    def _():
        carry_ref[...] = jnp.zeros_like(carry_ref)

    x = x_ref[...]
    y = x * lax.rsqrt(jnp.mean(x * x, axis=-1, keepdims=True) + NORM_EPS)
    h = (y * g_ref[...]) * (1.0 + sc_ref[0]) + sh_ref[0]
    h_ref[...] = _pack_bf16_pairs(h)
    scores = jax.nn.sigmoid(_dot_nt(wt_ref[...], h, HI))
    biased = scores + rb_ref[...]

    def first_argmax(v, iota, n):
        m = jnp.max(v, axis=0, keepdims=True)
        return m, jnp.min(jnp.where(v == m, iota, float(n)), axis=0, keepdims=True)

    def stack_rows(rows):
        iota8 = lax.broadcasted_iota(jnp.int32, (8, tm), 0)
        out = jnp.zeros((8, tm), F32)
        for k, row in enumerate(rows):
            out = jnp.where(iota8 == k, row, out)
        return out

    assert gs == 8 and N_GROUPS == 8
    iota_g = lax.broadcasted_iota(jnp.int32, (gs, tm), 0).astype(F32)
    g_rows = []
    for g in range(N_GROUPS):
        bg = biased[g * gs:(g + 1) * gs]
        m1, i1 = first_argmax(bg, iota_g, gs)
        m2 = jnp.max(jnp.where(iota_g == i1, neg, bg), axis=0, keepdims=True)
        g_rows.append(m1 + m2)
    g_score = stack_rows(g_rows)
    g_sel = jnp.zeros((N_GROUPS, tm), F32)
    for _ in range(TOPK_GROUPS):
        _, ig = first_argmax(g_score, iota_g, N_GROUPS)
        hit = iota_g == ig
        g_sel = jnp.where(hit, 1.0, g_sel)
        g_score = jnp.where(hit, neg, g_score)
    e_sel = jnp.concatenate([jnp.broadcast_to(g_sel[g:g + 1], (gs, tm)) for g in range(N_GROUPS)], axis=0)
    masked = jnp.where(e_sel > 0.0, biased, neg)

    iota_e = lax.broadcasted_iota(jnp.int32, (N_EXPERTS, tm), 0).astype(F32)
    chosen = jnp.zeros((N_EXPERTS, tm), F32)
    hits, idx_rows, w_rows = [], [], []
    for _ in range(TOP_K):
        _, ie = first_argmax(masked, iota_e, N_EXPERTS)
        hit = iota_e == ie
        hits.append(hit)
        idx_rows.append(ie)
        w_rows.append(jnp.sum(jnp.where(hit, scores, 0.0), axis=0, keepdims=True))
        chosen = jnp.where(hit, 1.0, chosen)
        masked = jnp.where(hit, neg, masked)
    w_sum = w_rows[0]
    for wk in w_rows[1:]:
        w_sum = w_sum + wk
    idx_ref[...] = stack_rows(idx_rows).astype(jnp.int32)
    w_ref[...] = stack_rows([wk / w_sum * ROUTE_SCALE for wk in w_rows])

    chosen_b = chosen.astype(BF16)
    before = carry_ref[...] + _dot(chosen_b, tri_ref[...])
    rank_ref[...] = stack_rows([jnp.sum(jnp.where(hit, before, 0.0), axis=0, keepdims=True)
                                for hit in hits]).astype(jnp.int32)
    carry_ref[...] += _dot(chosen_b, ones_ref[...])
    cnt_ref[...] = carry_ref[:, :LANES].astype(jnp.int32)


def route(x, g, shift3, scale3, grp, router_w, router_b):
    m, d = x.shape
    ne = N_EXPERTS
    tri = jnp.asarray(np.triu(np.ones((TM, TM), np.float32), 1), BF16)
    ones = jnp.ones((TM, TM), BF16)
    rb = jnp.broadcast_to(router_b.astype(F32)[:, None], (ne, TM))
    row = lambda r: pl.BlockSpec((r, TM), lambda i, grp: (0, i))
    const = lambda shape: pl.BlockSpec(shape, lambda i, grp: (0, 0))
    grid_spec = pltpu.PrefetchScalarGridSpec(
        num_scalar_prefetch=1,
        grid=(m // TM,),
        in_specs=[pl.BlockSpec((TM, d), lambda i, grp: (i, 0)),
                  const((1, d)),
                  pl.BlockSpec((1, 1, d), lambda i, grp: (grp[i], 0, 0)),
                  pl.BlockSpec((1, 1, d), lambda i, grp: (grp[i], 0, 0)),
                  const((ne, d)), const((ne, TM)), const((TM, TM)), const((TM, TM))],
        out_specs=[pl.BlockSpec((TM, d // 2), lambda i, grp: (i, 0)), row(8), row(8), row(8), const((ne, LANES))],
        scratch_shapes=[pltpu.VMEM((ne, TM), F32)],
    )
    return pl.pallas_call(
        _route_kernel,
        grid_spec=grid_spec,
        out_shape=[jax.ShapeDtypeStruct((m, d // 2), jnp.uint32), jax.ShapeDtypeStruct((8, m), jnp.int32),
                   jax.ShapeDtypeStruct((8, m), F32), jax.ShapeDtypeStruct((8, m), jnp.int32),
                   jax.ShapeDtypeStruct((ne, LANES), jnp.int32)],
        compiler_params=_cparams(("arbitrary",)),
        name="route",
    )(grp, x, g.reshape(1, d), shift3, scale3, router_w.T, rb, tri, ones)


def _seq_edge_masks(n_batch, L, Lc):
    n_lat = n_batch * L
    starts = np.concatenate([np.arange(n_batch) * L, n_lat + np.arange(n_batch) * Lc])
    ends = np.concatenate([(np.arange(n_batch) + 1) * L, n_lat + (np.arange(n_batch) + 1) * Lc]) - 1
    first = np.ones((n_batch * (L + Lc), 1), np.float32)
    last = first.copy()
    first[starts] = 0.0
    last[ends] = 0.0
    return jnp.asarray(first), jnp.asarray(last)


RW_TM = 256


@functools.lru_cache(maxsize=None)
def _head_ones():
    h = np.arange(RW_WIDTH) // RW_HEAD_DIM
    return jnp.asarray(h[:, None] == h[None, :], BF16)


def _head_sum(x, ones):
    return _dot(x.astype(BF16), ones)


def _rwkv_prep_kernel(u_ref, w0_ref, w2_ref, a0_ref, a2_ref, g2_ref, kk_w_ref, ka_ref, ones_ref,
                      r_ref, v_ref, kk_ref, g_ref, lw_ref, av_ref, kd_ref):
    w = RW_WIDTH
    lo = 3 * w
    r_ref[...] = u_ref[:, 0:w]
    k = u_ref[:, w:2 * w]
    v_ref[...] = u_ref[:, 2 * w:lo]
    wl = jnp.tanh(u_ref[:, lo:lo + 2 * RW_DECAY_LORA]).astype(BF16)
    al = u_ref[:, lo + 2 * RW_DECAY_LORA:lo + 2 * RW_DECAY_LORA + 2 * RW_AAA_LORA].astype(BF16)
    gl = jax.nn.sigmoid(u_ref[:, lo + 2 * RW_DECAY_LORA + 2 * RW_AAA_LORA:]).astype(BF16)
    for d in range(2):
        z = -(w0_ref[d:d + 1, :] + _dot(wl, w2_ref[d]))
        softplus = jnp.maximum(z, 0.0) + jnp.log(1.0 + jnp.exp(-jnp.abs(z)))
        lw_ref[d] = -jnp.exp(-softplus - 0.5)
        a = jax.nn.sigmoid(a0_ref[d:d + 1, :] + _dot(al, a2_ref[d]))
        av_ref[d] = a
        kd_ref[d] = k * (1.0 + (a - 1.0) * ka_ref[...])
    g_ref[...] = _dot(gl, g2_ref[...])
    kk = k * kk_w_ref[...]
    norm = jnp.sqrt(_head_sum(kk * kk, ones_ref[...]))
    kk_ref[...] = kk / jnp.maximum(norm, 1e-12)


def rwkv_prep(u, w0, w2, a0, a2, g2, k_k, k_a):
    m = u.shape[0]
    w = RW_WIDTH
    zeros = jnp.zeros((RW_DECAY_LORA, w), F32)
    pad2 = lambda t: jnp.stack([jnp.concatenate([t[0], zeros], axis=0),
                                jnp.concatenate([zeros, t[1]], axis=0)]).astype(BF16)
    assert RW_DECAY_LORA == RW_AAA_LORA and 2 * RW_DECAY_LORA == LANES and RW_GATE_LORA == LANES
    full = lambda shape: pl.BlockSpec(shape, lambda i: (0,) * len(shape))
    row = pl.BlockSpec((RW_TM, w), lambda i: (i, 0))
    row2 = pl.BlockSpec((2, RW_TM, w), lambda i: (0, i, 0))
    one = jax.ShapeDtypeStruct((m, w), F32)
    two = jax.ShapeDtypeStruct((2, m, w), F32)
    return pl.pallas_call(
        _rwkv_prep_kernel,
        grid=(m // RW_TM,),
        in_specs=[pl.BlockSpec((RW_TM, RW_COLS), lambda i: (i, 0)), full((2, w)), full((2, LANES, w)), full((2, w)),
                  full((2, LANES, w)), full((LANES, w)), full((1, w)), full((1, w)), full((w, w))],
        out_specs=[row, row, row, row, row2, row2, row2],
        out_shape=[one, one, one, one, two, two, two],
        compiler_params=_cparams(("arbitrary",)),
        name="rwkv_prep",
    )(u, w0, pad2(w2), a0, pad2(a2), g2.astype(BF16), k_k.reshape(1, w), k_a.reshape(1, w), _head_ones())


def _rwkv_out_kernel(y0_ref, y1_ref, r_ref, v_ref, g_ref, kd_ref, rk_ref, lnw_ref, lnb_ref, ones_ref, o_ref):
    ones = ones_ref[...]
    inv_n = 1.0 / RW_HEAD_DIM
    y = y0_ref[...] + y1_ref[...]
    yc = y - _head_sum(y, ones) * inv_n
    var = _head_sum(yc * yc, ones) * inv_n
    yn = yc * lax.rsqrt(var + RW_GN_EPS)
    bonus = _head_sum(r_ref[...] * (kd_ref[0] + kd_ref[1]) * rk_ref[...], ones) * v_ref[...]
    o_ref[...] = ((yn * lnw_ref[...] + lnb_ref[...] + bonus) * g_ref[...]).astype(o_ref.dtype)


def rwkv_out(y0, y1, r, v, g, kd, r_k, ln_w, ln_b):
    m, w = r.shape
    full = lambda shape: pl.BlockSpec(shape, lambda i: (0,) * len(shape))
    row = pl.BlockSpec((RW_TM, w), lambda i: (i, 0))
    vec = lambda t: t.reshape(1, w)
    return pl.pallas_call(
        _rwkv_out_kernel,
        grid=(m // RW_TM,),
        in_specs=[row, row, row, row, row, pl.BlockSpec((2, RW_TM, w), lambda i: (0, i, 0)),
                  full((1, w)), full((1, w)), full((1, w)), full((w, w))],
        out_specs=row,
        out_shape=jax.ShapeDtypeStruct((m, w), BF16),
        compiler_params=_cparams(("arbitrary",)),
        name="rwkv_out",
    )(y0, y1, r, v, g, kd, vec(r_k), vec(ln_w), vec(ln_b), _head_ones())


def _moe(h, e_idx, rank, counts, exp_gate, exp_up, exp_down, sh_gate, sh_up, sh_down, layer):
    T = h.shape[0]
    D = exp_gate.shape[-2]
    n = T * TOP_K
    padded = (counts + MOE_BLOCK - 1) // MOE_BLOCK * MOE_BLOCK
    pad_end = jnp.cumsum(padded)
    pad_start = pad_end - padded
    experts = jnp.arange(N_EXPERTS, dtype=jnp.int32)
    dest = rank + jnp.sum(jnp.where(e_idx[:, :, None] == experts, pad_start.astype(jnp.int32), 0), axis=-1)
    per_step = GATHER_ROWS // MOE_BLOCK
    n_blocks = -(-(-(-n // MOE_BLOCK) + N_EXPERTS) // per_step) * per_step
    n_slots = n_blocks * MOE_BLOCK
    flat_dest = dest.reshape(-1)
    tok = jnp.tile(jnp.arange(T, dtype=jnp.int32), TOP_K)
    slot_tok = jnp.zeros((n_slots,), jnp.int32).at[flat_dest].set(tok)
    block_start = jnp.arange(n_blocks, dtype=jnp.int32) * MOE_BLOCK
    block_e = jnp.minimum(jnp.sum(block_start[:, None] >= pad_end[None, :], axis=1), N_EXPERTS - 1).astype(jnp.int32)
    n_used = (pad_end[-1] // MOE_BLOCK).astype(jnp.int32).reshape(1)
    h = h.reshape(T, -1, LANES)
    y = grouped_swiglu(gather_rows(h, slot_tok), block_e, n_used, exp_gate, exp_up, exp_down, layer)
    yg = jnp.take(y, flat_dest, axis=0, mode="clip").reshape(TOP_K, T, D)
    nb_sh = T // MOE_BLOCK
    sh4 = lambda w: w.reshape((w.shape[0], 1) + w.shape[1:])
    shared = grouped_swiglu(h, jnp.zeros((nb_sh,), jnp.int32), jnp.full((1,), nb_sh, jnp.int32),
                            sh4(sh_gate), sh4(sh_up), sh4(sh_down), layer)
    return yg, shared


def kernel(x, c, ctx, c_ctx, mod_w, mod_b, norm1_g, norm2_g, w_in, hy_conv, hy_w1, hy_b1, hy_w2, hy_b2, hy_freq,
           hy_w3, hy_bias, na_rpb, rw_shift, rw_w0, rw_w2, rw_a0, rw_a2, rw_g2, rw_kk, rw_ka, rw_rk, rw_ln_w,
           rw_ln_b, proj_a, proj_b, proj_c, w_out, router_w, router_b, exp_gate, exp_up, exp_down, sh_gate, sh_up,
           sh_down, final_g):
    B, L, D = x.shape
    Lc = ctx.shape[1]
    depth = mod_w.shape[0]
    n_lat, n_ctx = B * L, B * Lc
    assert L % TM == 0 and n_ctx % TM == 0 and L % WKV_CHUNK == 0 and Lc % WKV_CHUNK == 0
    col_hy = 3 * HY_WIDTH
    col_na = col_hy + 3 * NA_WIDTH
    col_rw = col_na + RW_COLS

    xs = jnp.concatenate([x.reshape(n_lat, D), ctx.reshape(n_ctx, D)], axis=0)
    grp_all = jnp.asarray(np.concatenate([np.repeat(np.arange(B), L // TM), np.full(n_ctx // TM, B)]), jnp.int32)
    s8 = jnp.zeros((8, D), F32).at[:B].set(jax.nn.silu(c)).at[B].set(jax.nn.silu(c_ctx))
    first, last = _seq_edge_masks(B, L, Lc)

    for i in range(depth):
        with_ctx = i < depth - 1
        mod = small_matmul_bias(s8, mod_w, mod_b, i)[:B + 1].reshape(B + 1, 1, N_MOD * D)
        sh1, sc1, g1, sh2, sc2, g2 = (mod[:, :, j * D:(j + 1) * D] for j in range(N_MOD))
        w_bf = w_in[i].astype(BF16)
        proj = functools.partial(normmod_matmul, xs, norm1_g[i], sh1, sc1, grp_all)
        hy = short_conv(proj(w_bf[:, :col_hy], HY_WIDTH, BF16), hy_conv[i], first, last, HY_WIDTH)
        na = proj(w_bf[:, col_hy:col_na], NA_WIDTH, BF16)
        rw = proj(w_bf[:, col_na:col_rw], RW_COLS // 3, F32)
        gates = proj(w_bf[:, col_rw:], D // 2, BF16)

        hy_args = (hy_w1[i], hy_b1[i], hy_w2[i], hy_b2[i], hy_freq[i], hy_w3[i])
        h_raw, ss = hyena_filters_raw(L, *hy_args)
        hr, hi = hyena_filter_spectrum(h_raw, _filter_scale(ss), L)
        z = hyena_conv(hy, 0, hy, 2, hr, hi, 0, hy_bias[i][0], B, L)
        o_a = hyena_conv(hy, 1, z, 0, hr, hi, 1, hy_bias[i][1], B, L).astype(BF16)
        o_b = na_latent(na, na_rpb[i], B, L, Lc)
        rw = short_conv(rw, rw_shift[i], first, last, RW_COLS // 3)
        r_, v_, kk_, gg_, lw_, av_, kd_ = rwkv_prep(rw, rw_w0[i], rw_w2[i], rw_a0[i], rw_a2[i], rw_g2[i], rw_kk[i],
                                                    rw_ka[i])
        y_f, y_b = wkv_scan(r_, v_, kk_, lw_, av_, kd_, B, L, Lc)
        o_c = rwkv_out(y_f, y_b, r_, v_, gg_, kd_, rw_rk[i], rw_ln_w[i], rw_ln_b[i])

        if with_ctx:
            h_raw_c, ss_c = hyena_filters_raw(Lc, *hy_args)
            o_a_c = hyena_small(hy, n_lat // Lc, B, Lc, h_raw_c, _filter_scale(ss_c), hy_bias[i])
            o_a = jnp.concatenate([o_a, o_a_c], axis=0)
            o_b = jnp.concatenate([o_b, ctx_attn(na, B, L, Lc)], axis=0)
            m_rows = n_lat + n_ctx
        else:
            m_rows = n_lat
        grp = grp_all[:m_rows // TM]
        merged = branch_merge(m_rows, o_a, o_b, o_c, gates, proj_a[i].astype(BF16), proj_b[i].astype(BF16),
                              proj_c[i].astype(BF16))
        xs = resid_matmul(merged, w_out[i].astype(BF16), xs, g1, grp)

        h2, e_idx, e_w, rank, counts = route(xs, norm2_g[i], sh2, sc2, grp, router_w[i], router_b[i])
        yg, shared = _moe(h2, e_idx[:TOP_K], rank[:TOP_K], counts[:, 0], exp_gate, exp_up, exp_down, sh_gate, sh_up,
                          sh_down, i)
        xs = moe_combine(xs, yg, e_w.T, shared, g2, grp)

    return rmsnorm_rows(xs, final_g).reshape(B, L, D)
```

```python
import functools
import math

import jax
import jax.numpy as jnp
import numpy as np
from jax import lax
from jax.experimental import pallas as pl
from jax.experimental.pallas import tpu as pltpu

F32 = jnp.float32
BF16 = jnp.bfloat16
HI = lax.Precision.HIGHEST

GRID_W = 64
NORM_EPS = 1e-6
N_MOD = 6
SHORT_CONV = 3
HY_WIDTH = 1024
HY_BANDS = 16
HY_EMB = 2 * HY_BANDS + 1
HY_FILTER_ORDER = 64
HY_FAST_DECAY = 0.3
HY_SLOW_DECAY = 1.5
HY_DECAY_TARGET = 1e-2
NA_HEADS = 16
NA_HEAD_DIM = 64
NA_WIDTH = NA_HEADS * NA_HEAD_DIM
NA_ROWS = 8
NA_COLS = 16
RW_HEADS = 16
RW_HEAD_DIM = 64
RW_WIDTH = RW_HEADS * RW_HEAD_DIM
RW_DECAY_LORA = 64
RW_AAA_LORA = 64
RW_GATE_LORA = 128
RW_GN_EPS = 64e-5
RW_COLS = 3 * RW_WIDTH + 2 * RW_DECAY_LORA + 2 * RW_AAA_LORA + RW_GATE_LORA
RW_SPLITS = [RW_WIDTH, 2 * RW_WIDTH, 3 * RW_WIDTH, 3 * RW_WIDTH + 2 * RW_DECAY_LORA,
             3 * RW_WIDTH + 2 * RW_DECAY_LORA + 2 * RW_AAA_LORA]
N_BRANCH = 3
N_EXPERTS = 64
TOP_K = 6
N_GROUPS = 8
TOPK_GROUPS = 4
ROUTE_SCALE = 2.5

LANES = 128
VMEM_LIMIT = 56 * 1024 * 1024
TM = 512
MOE_BLOCK = 256
WKV_CHUNK = 64
WKV_GROUP = 4


def _cparams(sem):
    return pltpu.CompilerParams(dimension_semantics=sem, vmem_limit_bytes=VMEM_LIMIT)


def _dot(a, b, prec=None):
    return jnp.dot(a, b, preferred_element_type=F32, precision=prec)


def _dot_nt(a, b, prec=None):
    return lax.dot_general(a, b, (((1,), (1,)), ((), ())), preferred_element_type=F32, precision=prec)


def _dot_tn(a, b, prec=None):
    return lax.dot_general(a, b, (((0,), (0,)), ((), ())), preferred_element_type=F32, precision=prec)


def _small_mm_kernel(a_ref, w_ref, b_ref, o_ref):
    o_ref[...] = _dot(a_ref[...], w_ref[0], HI) + b_ref[0]


def small_matmul_bias(a, w, b, layer, tn=1536):
    m, k = a.shape
    n = w.shape[2]
    return pl.pallas_call(
        _small_mm_kernel,
        grid=(n // tn,),
        in_specs=[pl.BlockSpec((m, k), lambda j: (0, 0)),
                  pl.BlockSpec((1, k, tn), lambda j: (layer, 0, j)),
                  pl.BlockSpec((1, 1, tn), lambda j: (layer, 0, j))],
        out_specs=pl.BlockSpec((m, tn), lambda j: (0, j)),
        out_shape=jax.ShapeDtypeStruct((m, n), F32),
        compiler_params=_cparams(("arbitrary",)),
        name="mod_matmul",
    )(a, w, b.reshape(b.shape[0], 1, n))


def _normmod_mm_kernel(grp_ref, x_ref, g_ref, sh_ref, sc_ref, w_ref, o_ref, h_ref, *, hi):
    del grp_ref

    @pl.when(pl.program_id(1) == 0)
    def _():
        x = x_ref[...]
        y = x * lax.rsqrt(jnp.mean(x * x, axis=-1, keepdims=True) + NORM_EPS)
        y = y * g_ref[...]
        h_ref[...] = (y * (1.0 + sc_ref[0]) + sh_ref[0]).astype(h_ref.dtype)

    o_ref[...] = _dot(h_ref[...], w_ref[...], HI if hi else None).astype(o_ref.dtype)


def normmod_matmul(x, g, shift3, scale3, grp, w, tn, out_dtype, hi=False):
    m, d = x.shape
    n = w.shape[1]
    grid_spec = pltpu.PrefetchScalarGridSpec(
        num_scalar_prefetch=1,
        grid=(m // TM, n // tn),
        in_specs=[pl.BlockSpec((TM, d), lambda i, j, grp: (i, 0)),
                  pl.BlockSpec((1, d), lambda i, j, grp: (0, 0)),
                  pl.BlockSpec((1, 1, d), lambda i, j, grp: (grp[i], 0, 0)),
                  pl.BlockSpec((1, 1, d), lambda i, j, grp: (grp[i], 0, 0)),
                  pl.BlockSpec((d, tn), lambda i, j, grp: (0, j))],
        out_specs=pl.BlockSpec((TM, tn), lambda i, j, grp: (i, j)),
        scratch_shapes=[pltpu.VMEM((TM, d), F32 if hi else BF16)],
    )
    return pl.pallas_call(
        functools.partial(_normmod_mm_kernel, hi=hi),
        grid_spec=grid_spec,
        out_shape=jax.ShapeDtypeStruct((m, n), out_dtype),
        compiler_params=_cparams(("arbitrary", "arbitrary")),
        name="normmod_matmul",
    )(grp, x, g.reshape(1, d), shift3, scale3, w)


def _normmod_kernel(grp_ref, x_ref, g_ref, sh_ref, sc_ref, o_ref):
    del grp_ref
    x = x_ref[...]
    y = x * lax.rsqrt(jnp.mean(x * x, axis=-1, keepdims=True) + NORM_EPS)
    o_ref[...] = ((y * g_ref[...]) * (1.0 + sc_ref[0]) + sh_ref[0]).astype(o_ref.dtype)


def normmod(x, g, shift3, scale3, grp, out_dtype):
    m, d = x.shape
    grid_spec = pltpu.PrefetchScalarGridSpec(
        num_scalar_prefetch=1,
        grid=(m // TM,),
        in_specs=[pl.BlockSpec((TM, d), lambda i, grp: (i, 0)),
                  pl.BlockSpec((1, d), lambda i, grp: (0, 0)),
                  pl.BlockSpec((1, 1, d), lambda i, grp: (grp[i], 0, 0)),
                  pl.BlockSpec((1, 1, d), lambda i, grp: (grp[i], 0, 0))],
        out_specs=pl.BlockSpec((TM, d), lambda i, grp: (i, 0)),
    )
    return pl.pallas_call(
        _normmod_kernel,
        grid_spec=grid_spec,
        out_shape=jax.ShapeDtypeStruct((m, d), out_dtype),
        compiler_params=_cparams(("arbitrary",)),
        name="normmod",
    )(grp, x, g.reshape(1, d), shift3, scale3)


def _merge_kernel(oa_ref, ob_ref, oc_ref, ga_ref, gb_ref, gc_ref, pa_ref, pb_ref, pc_ref, o_ref):
    gate = lambda ref: jax.nn.sigmoid(ref[...].astype(F32))
    m = gate(ga_ref) * _dot(oa_ref[...], pa_ref[...])
    m = m + gate(gb_ref) * _dot(ob_ref[...], pb_ref[...])
    m = m + gate(gc_ref) * _dot(oc_ref[...], pc_ref[...])
    o_ref[...] = m.astype(o_ref.dtype)


def branch_merge(m, o_a, o_b, o_c, gates, pa, pb, pc, tn=1024):
    k = o_a.shape[1]
    d = pa.shape[1]
    nj = d // tn
    o_spec = pl.BlockSpec((TM, k), lambda i, j: (i, 0))
    p_spec = pl.BlockSpec((k, tn), lambda i, j: (0, j))
    return pl.pallas_call(
        _merge_kernel,
        grid=(m // TM, nj),
        in_specs=[o_spec, o_spec, o_spec,
                  pl.BlockSpec((TM, tn), lambda i, j: (i, j)),
                  pl.BlockSpec((TM, tn), lambda i, j: (i, j + nj)),
                  pl.BlockSpec((TM, tn), lambda i, j: (i, j + 2 * nj)),
                  p_spec, p_spec, p_spec],
        out_specs=pl.BlockSpec((TM, tn), lambda i, j: (i, j)),
        out_shape=jax.ShapeDtypeStruct((m, d), BF16),
        compiler_params=_cparams(("arbitrary", "arbitrary")),
        name="branch_merge",
    )(o_a, o_b, o_c, gates, gates, gates, pa, pb, pc)


def _resid_mm_kernel(grp_ref, a_ref, w_ref, x_ref, gate_ref, o_ref):
    del grp_ref
    o_ref[...] = x_ref[...] + gate_ref[0] * _dot(a_ref[...], w_ref[...])


def resid_matmul(a, w, x, gate3, grp, tn=1024):
    m, k = a.shape
    d = w.shape[1]
    grid_spec = pltpu.PrefetchScalarGridSpec(
        num_scalar_prefetch=1,
        grid=(m // TM, d // tn),
        in_specs=[pl.BlockSpec((TM, k), lambda i, j, grp: (i, 0)),
                  pl.BlockSpec((k, tn), lambda i, j, grp: (0, j)),
                  pl.BlockSpec((TM, tn), lambda i, j, grp: (i, j)),
                  pl.BlockSpec((1, 1, tn), lambda i, j, grp: (grp[i], 0, j))],
        out_specs=pl.BlockSpec((TM, tn), lambda i, j, grp: (i, j)),
    )
    return pl.pallas_call(
        _resid_mm_kernel,
        grid_spec=grid_spec,
        out_shape=jax.ShapeDtypeStruct((m, d), F32),
        compiler_params=_cparams(("arbitrary", "arbitrary")),
        name="resid_matmul",
    )(grp, a, w, x, gate3)


def _short_conv_kernel(x_ref, prev_ref, next_ref, w_ref, first_ref, last_ref, o_ref):
    x = x_ref[...].astype(F32)
    tm = x.shape[0]
    row = lax.broadcasted_iota(jnp.int32, x.shape, 0)
    halo = prev_ref.shape[0]
    before = prev_ref[...].astype(F32)[halo - 1:halo, :]
    after = next_ref[...].astype(F32)[0:1, :]
    prev = jnp.where(row == 0, before, pltpu.roll(x, 1, 0)) * first_ref[...]
    nxt = jnp.where(row == tm - 1, after, pltpu.roll(x, tm - 1, 0)) * last_ref[...]
    o_ref[...] = prev * w_ref[0:1, :] + x * w_ref[1:2, :] + nxt * w_ref[2:3, :]


def short_conv(u, w, first, last, tn):
    assert SHORT_CONV == 3
    m, c = u.shape
    halo = 16
    per = TM // halo
    n_halo = m // halo
    return pl.pallas_call(
        _short_conv_kernel,
        grid=(m // TM, c // tn),
        in_specs=[pl.BlockSpec((TM, tn), lambda i, j: (i, j)),
                  pl.BlockSpec((halo, tn), lambda i, j: (jnp.maximum(i * per - 1, 0), j)),
                  pl.BlockSpec((halo, tn), lambda i, j: (jnp.minimum((i + 1) * per, n_halo - 1), j)),
                  pl.BlockSpec((SHORT_CONV, tn), lambda i, j: (0, j)),
                  pl.BlockSpec((TM, 1), lambda i, j: (i, 0)),
                  pl.BlockSpec((TM, 1), lambda i, j: (i, 0))],
        out_specs=pl.BlockSpec((TM, tn), lambda i, j: (i, j)),
        out_shape=jax.ShapeDtypeStruct((m, c), F32),
        compiler_params=_cparams(("arbitrary", "arbitrary")),
        name="short_conv",
    )(u, u, u, w, first, last)


def _rmsnorm_kernel(x_ref, g_ref, o_ref):
    x = x_ref[...]
    y = x * lax.rsqrt(jnp.mean(x * x, axis=-1, keepdims=True) + NORM_EPS)
    o_ref[...] = y * g_ref[...]


def rmsnorm_rows(x, g):
    m, d = x.shape
    return pl.pallas_call(
        _rmsnorm_kernel,
        grid=(m // TM,),
        in_specs=[pl.BlockSpec((TM, d), lambda i: (i, 0)), pl.BlockSpec((1, d), lambda i: (0, 0))],
        out_specs=pl.BlockSpec((TM, d), lambda i: (i, 0)),
        out_shape=jax.ShapeDtypeStruct((m, d), F32),
        compiler_params=_cparams(("arbitrary",)),
        name="final_rmsnorm",
    )(x, g.reshape(1, d))


def _hyfilt_kernel(z_ref, w1_ref, b1_ref, w2_ref, b2_ref, fr_ref, w3_ref, dl_ref, h_ref, ss_ref):
    z = z_ref[...]
    hdn = jnp.sin(fr_ref[0:1, :] * (_dot(z, w1_ref[...], HI) + b1_ref[...]))
    hdn = jnp.sin(fr_ref[1:2, :] * (_dot(hdn, w2_ref[...], HI) + b2_ref[...]))
    h = _dot(hdn, w3_ref[...], HI)
    h = h * jnp.exp(-z[:, 0:1] * dl_ref[...])
    h_ref[...] = h

    @pl.when(pl.program_id(0) == 0)
    def _():
        ss_ref[...] = jnp.zeros_like(ss_ref)

    ss_ref[...] += jnp.sum(h * h, axis=0, keepdims=True)


def hyena_filters_raw(L, w1, b1, w2, b2, freq, w3):
    t = np.linspace(0.0, 1.0, L, dtype=np.float32)[:, None]
    omega = np.float32(2.0 * math.pi / L) * np.arange(L, dtype=np.float32)[:, None]
    bands = np.linspace(1e-4, HY_BANDS - 1, HY_BANDS, dtype=np.float32)[None, :]
    z = np.concatenate([t, np.cos(omega * bands), -np.sin(omega * bands),
                        np.zeros((L, HY_FILTER_ORDER - HY_EMB), np.float32)], axis=-1).astype(np.float32)
    w1p = jnp.concatenate([w1, jnp.zeros((HY_FILTER_ORDER - HY_EMB, HY_FILTER_ORDER), F32)], axis=0)
    deltas = np.abs(np.linspace(math.log(HY_DECAY_TARGET) / HY_SLOW_DECAY,
                                math.log(HY_DECAY_TARGET) / HY_FAST_DECAY, HY_WIDTH, dtype=np.float32))
    dl4 = np.tile(deltas, 4)[None, :]
    tl = min(L, 256)
    n = 4 * HY_WIDTH
    fo = HY_FILTER_ORDER
    full = lambda shape: pl.BlockSpec(shape, lambda i: (0, 0))
    return pl.pallas_call(
        _hyfilt_kernel,
        grid=(L // tl,),
        in_specs=[pl.BlockSpec((tl, fo), lambda i: (i, 0)), full((fo, fo)), full((1, fo)), full((fo, fo)),
                  full((1, fo)), full((2, fo)), full((fo, n)), full((1, n))],
        out_specs=[pl.BlockSpec((tl, n), lambda i: (i, 0)), full((1, n))],
        out_shape=[jax.ShapeDtypeStruct((L, n), F32), jax.ShapeDtypeStruct((1, n), F32)],
        compiler_params=_cparams(("arbitrary",)),
        name="hyena_filters",
    )(jnp.asarray(z), w1p, b1.reshape(1, fo), w2, b2.reshape(1, fo), freq, w3, jnp.asarray(dl4))


def _filter_scale(ss):
    s = ss.reshape(2, 2, HY_WIDTH)
    rs = lax.rsqrt(jnp.sum(s, axis=1, keepdims=True))
    return jnp.broadcast_to(rs, (2, 2, HY_WIDTH)).reshape(1, 4 * HY_WIDTH)


FFT_N1 = 128
FFT_N2 = 64
FFT_PITCH = 72
FFT_BATCH = 4


@functools.lru_cache(maxsize=None)
def _fft_tables():
    n1, n2 = FFT_N1, FFT_N2
    n = n1 * n2
    a = np.arange(n1 // 2)[None, None, :]
    k1 = np.arange(n1)[None, :, None]
    b = np.arange(n2)[:, None, None]
    theta = 2.0 * np.pi * ((a * k1 % n1) / n1 + (b * k1) / n)
    g = np.concatenate([np.cos(theta), -np.sin(theta)], axis=1)
    ig = np.concatenate([np.cos(theta), -np.sin(theta)], axis=1).transpose(0, 2, 1) / n
    k2 = np.arange(n2)[:, None]
    bb = np.arange(n2)[None, :]
    ph = 2.0 * np.pi * (k2 * bb % n2) / n2
    fr, fi = np.cos(ph), -np.sin(ph)
    f2 = np.block([[fr, -fi], [fi, fr]])
    if2 = np.block([[fr, fi], [-fi, fr]])
    return (jnp.asarray(g, BF16), jnp.asarray(f2, BF16), jnp.asarray(if2, BF16), jnp.asarray(ig, BF16))


def _fft_stage1(u_ref, g_ref, sr_ref, si_ref):
    n1, n2, p = FFT_N1, FFT_N2, FFT_PITCH

    def body(i, carry):
        bs = [i * FFT_BATCH + j for j in range(FFT_BATCH)]
        xs = [u_ref[pl.ds(b, n1 // 2, stride=n2), :].astype(BF16) for b in bs]
        outs = [_dot(g_ref[b], x) for b, x in zip(bs, xs)]
        for b, a in zip(bs, outs):
            sr_ref[pl.ds(b, n1, stride=p), :] = a[:n1]
            si_ref[pl.ds(b, n1, stride=p), :] = a[n1:]
        return carry

    lax.fori_loop(0, n2 // FFT_BATCH, body, 0)


def _bin_rows(k1):
    return pl.ds(pl.multiple_of(k1 * FFT_PITCH, 8), FFT_N2)


def _hyconv_kernel(xm_ref, u_ref, hr_ref, hi_ref, bias_ref, g_ref, f2_ref, if2_ref, ig_ref, o_ref, sr_ref, si_ref):
    n1, n2, p = FFT_N1, FFT_N2, FFT_PITCH
    _fft_stage1(u_ref, g_ref, sr_ref, si_ref)

    def pair(ref, k):
        return jnp.concatenate([ref[_bin_rows(2 * k), :], ref[_bin_rows(2 * k + 1), :]], axis=1)

    def pair_h(ref, k):
        blk = ref[pl.ds(pl.multiple_of(k * 2 * n2, 2 * n2), 2 * n2), :]
        return jnp.concatenate([blk[:n2], blk[n2:]], axis=1)

    def unpair(ref, k, val):
        w = val.shape[1] // 2
        ref[_bin_rows(2 * k), :] = val[:, :w]
        ref[_bin_rows(2 * k + 1), :] = val[:, w:]

    def stage2(i, carry):
        ks = [i * FFT_BATCH + j for j in range(FFT_BATCH)]
        zs = [jnp.concatenate([pair(sr_ref, k), pair(si_ref, k)], axis=0).astype(BF16) for k in ks]
        xs = [_dot(f2_ref[...], z) for z in zs]
        ys = []
        for k, x in zip(ks, xs):
            xr, xi = x[:n2], x[n2:]
            hr, hi = pair_h(hr_ref, k), pair_h(hi_ref, k)
            ys.append(jnp.concatenate([xr * hr - xi * hi, xr * hi + xi * hr], axis=0).astype(BF16))
        bbs = [_dot(if2_ref[...], y) for y in ys]
        for k, bb in zip(ks, bbs):
            unpair(sr_ref, k, bb[:n2])
            unpair(si_ref, k, bb[n2:])
        return carry

    lax.fori_loop(0, n1 // 2 // FFT_BATCH, stage2, 0)

    def stage3(i, carry):
        bs = [i * FFT_BATCH + j for j in range(FFT_BATCH)]
        sts = [jnp.concatenate([sr_ref[pl.ds(b, n1, stride=p), :], si_ref[pl.ds(b, n1, stride=p), :]],
                               axis=0).astype(BF16) for b in bs]
        outs = [_dot(ig_ref[b], st) for b, st in zip(bs, sts)]
        for b, o in zip(bs, outs):
            o_ref[pl.ds(b, n1 // 2, stride=n2), :] = o
        return carry

    lax.fori_loop(0, n2 // FFT_BATCH, stage3, 0)
    bias = bias_ref[...]
    rows_per_pass = 512

    def finish(i, carry):
        rows = pl.ds(pl.multiple_of(i * rows_per_pass, rows_per_pass), rows_per_pass)
        o_ref[rows, :] = xm_ref[rows, :] * (o_ref[rows, :] + u_ref[rows, :] * bias)
        return carry

    lax.fori_loop(0, o_ref.shape[0] // rows_per_pass, finish, 0)


def hyena_conv(xm_arr, xm_col, u_arr, u_col, hr, hi, h_col, bias, n_batch, L):
    assert L == FFT_N1 * FFT_N2 // 2
    cb = LANES
    nct = HY_WIDTH // cb
    n = 2 * L
    g, f2, if2, ig = _fft_tables()
    const3 = lambda shape: pl.BlockSpec(shape, lambda b, c: (0, 0, 0))
    const2 = lambda shape: pl.BlockSpec(shape, lambda b, c: (0, 0))
    return pl.pallas_call(
        _hyconv_kernel,
        grid=(n_batch, nct),
        in_specs=[pl.BlockSpec((L, cb), lambda b, c: (b, xm_col * nct + c)),
                  pl.BlockSpec((L, cb), lambda b, c: (b, u_col * nct + c)),
                  pl.BlockSpec((n, cb), lambda b, c: (0, h_col * nct + c)),
                  pl.BlockSpec((n, cb), lambda b, c: (0, h_col * nct + c)),
                  pl.BlockSpec((1, cb), lambda b, c: (0, c)),
                  const3(g.shape), const2(f2.shape), const2(if2.shape), const3(ig.shape)],
        out_specs=pl.BlockSpec((L, cb), lambda b, c: (b, c)),
        out_shape=jax.ShapeDtypeStruct((n_batch * L, HY_WIDTH), F32),
        scratch_shapes=[pltpu.VMEM((FFT_N1 * FFT_PITCH, cb), F32)] * 2,
        compiler_params=_cparams(("arbitrary", "arbitrary")),
        name="hyena_conv",
    )(xm_arr, u_arr, hr, hi, bias.reshape(1, HY_WIDTH), g, f2, if2, ig)


def _hyspec_kernel(h0_ref, h1_ref, rs_ref, g_ref, f2_ref, hr_ref, hi_ref, s0r, s0i, s1r, s1i):
    n1, n2 = FFT_N1, FFT_N2
    _fft_stage1(h0_ref, g_ref, s0r, s0i)
    _fft_stage1(h1_ref, g_ref, s1r, s1i)
    rs = rs_ref[...]
    h10 = h1_ref[0:1, :]

    def stage2(i, carry):
        ks = [i * FFT_BATCH + j for j in range(FFT_BATCH)]
        zs = [jnp.concatenate([jnp.concatenate([s0r[_bin_rows(k), :], s1r[_bin_rows(k), :]], axis=1),
                               jnp.concatenate([s0i[_bin_rows(k), :], s1i[_bin_rows(k), :]], axis=1)],
                              axis=0).astype(BF16) for k in ks]
        xs = [_dot(f2_ref[...], z) for z in zs]
        for k, x in zip(ks, xs):
            rows = pl.ds(pl.multiple_of(k * n2, n2), n2)
            w = x.shape[1] // 2
            hr_ref[rows, :] = rs * (x[:n2, :w] + x[:n2, w:] - h10)
            hi_ref[rows, :] = rs * (x[n2:, :w] - x[n2:, w:])
        return carry

    lax.fori_loop(0, n1 // FFT_BATCH, stage2, 0)


def hyena_filter_spectrum(h_raw, rs, L):
    assert L == FFT_N1 * FFT_N2 // 2
    cb = LANES
    nct = HY_WIDTH // cb
    n = 2 * L
    g, f2, _, _ = _fft_tables()
    out_spec = pl.BlockSpec((n, cb), lambda o, c: (0, o * nct + c))
    scr = pltpu.VMEM((FFT_N1 * FFT_PITCH, cb), F32)
    return pl.pallas_call(
        _hyspec_kernel,
        grid=(2, nct),
        in_specs=[pl.BlockSpec((L, cb), lambda o, c: (0, (2 * o) * nct + c)),
                  pl.BlockSpec((L, cb), lambda o, c: (0, (2 * o + 1) * nct + c)),
                  pl.BlockSpec((1, cb), lambda o, c: (0, (2 * o) * nct + c)),
                  pl.BlockSpec(g.shape, lambda o, c: (0, 0, 0)),
                  pl.BlockSpec(f2.shape, lambda o, c: (0, 0))],
        out_specs=[out_spec, out_spec],
        out_shape=[jax.ShapeDtypeStruct((n, 2 * HY_WIDTH), F32)] * 2,
        scratch_shapes=[scr, scr, scr, scr],
        compiler_params=_cparams(("arbitrary", "arbitrary")),
        name="hyena_filter_spectrum",
    )(h_raw, h_raw, rs, g, f2)


@functools.lru_cache(maxsize=None)
def _dense_dft_tables(L):
    n = 2 * L
    k = np.arange(n)[:, None]
    t = np.arange(L)[None, :]
    ph = 2.0 * np.pi * (k * t % n) / n
    fwd = np.concatenate([np.cos(ph), -np.sin(ph)], axis=0)
    inv = np.concatenate([np.cos(ph), -np.sin(ph)], axis=0).T / n
    return jnp.asarray(fwd, F32), jnp.asarray(inv, F32)


def _hyena_small_kernel(x1_ref, x2_ref, v_ref, h_ref0a, h_ref0b, h_ref1a, h_ref1b, rs0_ref, rs1_ref,
                        b0_ref, b1_ref, fwd_ref, inv_ref, o_ref, *, L):
    n = 2 * L
    fwd = fwd_ref[...]
    inv = inv_ref[...]

    def conv(u, ha_ref, hb_ref, rs_ref, bias_ref):
        ha, hb = ha_ref[...], hb_ref[...]
        ka = _dot(fwd, ha, HI)
        kb = _dot(fwd, hb, HI)
        rs = rs_ref[...]
        kr = rs * (ka[:n] + kb[:n] - hb[0:1, :])
        ki = rs * (ka[n:] - kb[n:])
        uf = _dot(fwd, u, HI)
        ur, ui = uf[:n], uf[n:]
        y = jnp.concatenate([ur * kr - ui * ki, ur * ki + ui * kr], axis=0)
        return _dot(inv, y, HI) + u * bias_ref[...]

    v = v_ref[...]
    z = x1_ref[...] * conv(v, h_ref0a, h_ref0b, rs0_ref, b0_ref)
    o_ref[...] = (x2_ref[...] * conv(z, h_ref1a, h_ref1b, rs1_ref, b1_ref)).astype(o_ref.dtype)


def hyena_small(u_arr, row0_blocks, n_batch, L, h_raw, rs, bias):
    cb = LANES
    nct = HY_WIDTH // cb
    fwd, inv = _dense_dft_tables(L)
    uspec = lambda col: pl.BlockSpec((L, cb), lambda b, c: (row0_blocks + b, col * nct + c))
    hspec = lambda col: pl.BlockSpec((L, cb), lambda b, c: (0, col * nct + c))
    rspec = lambda col: pl.BlockSpec((1, cb), lambda b, c: (0, col * nct + c))
    bspec = pl.BlockSpec((1, cb), lambda b, c: (0, c))
    bias_0 = bias[0].reshape(1, HY_WIDTH)
    bias_1 = bias[1].reshape(1, HY_WIDTH)
    return pl.pallas_call(
        functools.partial(_hyena_small_kernel, L=L),
        grid=(n_batch, nct),
        in_specs=[uspec(0), uspec(1), uspec(2), hspec(0), hspec(1), hspec(2), hspec(3), rspec(0), rspec(2),
                  bspec, bspec,
                  pl.BlockSpec(fwd.shape, lambda b, c: (0, 0)), pl.BlockSpec(inv.shape, lambda b, c: (0, 0))],
        out_specs=pl.BlockSpec((L, cb), lambda b, c: (b, c)),
        out_shape=jax.ShapeDtypeStruct((n_batch * L, HY_WIDTH), BF16),
        compiler_params=_cparams(("arbitrary", "arbitrary")),
        name="hyena_ctx",
    )(u_arr, u_arr, u_arr, h_raw, h_raw, h_raw, h_raw, rs, rs, bias_0, bias_1, fwd, inv)


def _na_bias_table(rpb):
    cols = np.arange(GRID_W)
    col_start = np.clip(cols - NA_COLS // 2, 0, GRID_W - NA_COLS)[:, None]
    in_win = (cols[None, :] >= col_start) & (cols[None, :] < col_start + NA_COLS)
    rel_col = np.clip(cols[None, :] - cols[:, None], 1 - NA_COLS, NA_COLS - 1) + NA_COLS - 1
    tbl = rpb.astype(F32)[:, :, rel_col]
    tbl = jnp.where(jnp.asarray(in_win)[None, None], tbl, -jnp.inf)
    return jnp.concatenate([tbl[:, :-1], tbl[:, 1:]], axis=-1)


def _na_kernel(*refs, n_rows):
    q_ref = refs[0]
    k_refs = refs[1:1 + NA_ROWS]
    v_refs = refs[1 + NA_ROWS:1 + 2 * NA_ROWS]
    kc_ref, vc_ref, tbl_ref, o_ref = refs[1 + 2 * NA_ROWS:]
    r = pl.program_id(1)
    start = jnp.clip(r - NA_ROWS // 2, 0, n_rows - NA_ROWS)
    d0 = start - r + NA_ROWS - 1
    dh = NA_HEAD_DIM
    q = q_ref[...] * (dh ** -0.5)
    n_pairs = NA_ROWS // 2
    heads = [slice(h * dh, (h + 1) * dh) for h in range(NA_HEADS)]
    scores = []
    for h, hs in enumerate(heads):
        qh = q[:, hs]
        tiles = [_dot_nt(qh, jnp.concatenate([k_refs[2 * p][:, hs], k_refs[2 * p + 1][:, hs]], axis=0))
                 + tbl_ref[h, d0 + 2 * p] for p in range(n_pairs)]
        scores.append(tiles + [_dot_nt(qh, kc_ref[:, hs])])
    probs, denoms = [], []
    for tiles in scores:
        m = tiles[0].max(axis=-1, keepdims=True)
        for s in tiles[1:]:
            m = jnp.maximum(m, s.max(axis=-1, keepdims=True))
        ps = [jnp.exp(s - m) for s in tiles]
        l = ps[0].sum(axis=-1, keepdims=True)
        for p_ in ps[1:]:
            l = l + p_.sum(axis=-1, keepdims=True)
        probs.append([p_.astype(BF16) for p_ in ps])
        denoms.append(l)
    for hs, ps, l in zip(heads, probs, denoms):
        acc = _dot(ps[-1], vc_ref[:, hs])
        for p in range(n_pairs):
            acc = acc + _dot(ps[p], jnp.concatenate([v_refs[2 * p][:, hs], v_refs[2 * p + 1][:, hs]], axis=0))
        o_ref[:, hs] = (acc / l).astype(o_ref.dtype)


def na_latent(na, rpb, n_batch, L, Lc):
    n_rows = L // GRID_W
    assert n_rows >= NA_ROWS
    tbl = _na_bias_table(rpb)
    w = NA_WIDTH
    ctx_blk0 = n_batch * L // Lc

    def kv_spec(i, col):
        def imap(b, r):
            start = jnp.clip(r - NA_ROWS // 2, 0, n_rows - NA_ROWS)
            return (b * n_rows + start + i, col)
        return pl.BlockSpec((GRID_W, w), imap)

    in_specs = ([pl.BlockSpec((GRID_W, w), lambda b, r: (b * n_rows + r, 0))]
                + [kv_spec(i, 1) for i in range(NA_ROWS)] + [kv_spec(i, 2) for i in range(NA_ROWS)]
                + [pl.BlockSpec((Lc, w), lambda b, r: (ctx_blk0 + b, 1)),
                   pl.BlockSpec((Lc, w), lambda b, r: (ctx_blk0 + b, 2)),
                   pl.BlockSpec(tbl.shape, lambda b, r: (0, 0, 0, 0))])
    return pl.pallas_call(
        functools.partial(_na_kernel, n_rows=n_rows),
        grid=(n_batch, n_rows),
        in_specs=in_specs,
        out_specs=pl.BlockSpec((GRID_W, w), lambda b, r: (b * n_rows + r, 0)),
        out_shape=jax.ShapeDtypeStruct((n_batch * L, w), BF16),
        compiler_params=_cparams(("arbitrary", "arbitrary")),
        name="na_latent",
    )(*([na] * (3 + 2 * NA_ROWS)), tbl)


def _ctx_attn_kernel(q_ref, k_ref, v_ref, o_ref):
    dh = NA_HEAD_DIM
    q = q_ref[...] * (dh ** -0.5)
    for h in range(NA_HEADS):
        hs = slice(h * dh, (h + 1) * dh)
        s = _dot_nt(q[:, hs], k_ref[:, hs])
        p_ = jnp.exp(s - s.max(axis=-1, keepdims=True))
        acc = _dot(p_.astype(BF16), v_ref[:, hs])
        o_ref[:, hs] = (acc / p_.sum(axis=-1, keepdims=True)).astype(o_ref.dtype)


def ctx_attn(na, n_batch, L, Lc):
    w = NA_WIDTH
    blk0 = n_batch * L // Lc
    spec = lambda col: pl.BlockSpec((Lc, w), lambda b: (blk0 + b, col))
    return pl.pallas_call(
        _ctx_attn_kernel,
        grid=(n_batch,),
        in_specs=[spec(0), spec(1), spec(2)],
        out_specs=pl.BlockSpec((Lc, w), lambda b: (b, 0)),
        out_shape=jax.ShapeDtypeStruct((n_batch * Lc, w), BF16),
        compiler_params=_cparams(("arbitrary",)),
        name="ctx_attn",
    )(na, na, na)


@functools.lru_cache(maxsize=None)
def _wkv_masks():
    c, g = WKV_CHUNK, WKV_GROUP
    t = np.arange(c)[:, None]
    s = np.arange(c)[None, :]
    tinc = np.stack([(s <= t), (s >= t)]).astype(np.float32)
    strict = np.stack([(s < t), (s > t)]).astype(np.float32)
    tile = lambda m: np.tile(m, (1,) * (m.ndim - 1) + (g,))
    blk = lambda n: (t // n == s // n)
    blk16 = tile(blk(16).astype(np.float32))
    off32 = tile((blk(32) & ~blk(16)).astype(np.float32))
    off64 = tile((~blk(32)).astype(np.float32))
    eye = tile((t == s).astype(np.float32))
    rr = np.arange(g * c)
    hm = (rr[:, None] // c == np.arange(g * RW_HEAD_DIM)[None, :] // RW_HEAD_DIM).astype(np.float32)
    masks = tuple(jnp.asarray(m) for m in (tinc, tile(strict), tile(tinc), blk16, off32, off64, eye, hm))
    return masks + (jnp.asarray(hm, BF16),)


def _wkv_kernel(*refs):
    (r0_ref, v0_ref, kk0_ref, r1_ref, v1_ref, kk1_ref, lw0_ref, av0_ref, kd0_ref, lw1_ref, av1_ref, kd1_ref,
     tinc_ref, strict_ref, incl_ref, blk16_ref, off32_ref, off64_ref, eye_ref, hm_ref, hmb_ref,
     y0_ref, y1_ref, state_ref) = refs
    c, g = WKV_CHUNK, WKV_GROUP
    gw = g * RW_HEAD_DIM

    @pl.when(pl.program_id(1) == 0)
    def _():
        state_ref[...] = jnp.zeros_like(state_ref)

    hm = hm_ref[...]
    hm_bf = hmb_ref[...]
    blk16, off32, off64, eye = blk16_ref[...], off32_ref[...], off64_ref[...], eye_ref[...]

    def bdiag(z):
        return jnp.concatenate([z.astype(BF16)] * g, axis=0) * hm_bf

    def pm(x4, zd):
        return _dot(x4.astype(BF16), zd)

    def prepare(d, r_ref, kk_ref, lw_ref, av_ref, kd_ref):
        lw = lw_ref[0]
        cum = _dot(tinc_ref[d], lw, HI)
        tot = jnp.sum(lw, axis=0, keepdims=True)
        e_neg = jnp.exp(-cum)
        e_rem = jnp.exp(tot - cum)
        kk = kk_ref[...]
        b_vec = kk * av_ref[0]
        kd = kd_ref[0]
        return dict(at=-kk * jnp.exp(cum - lw), rt=r_ref[...] * jnp.exp(cum), bt=b_vec * e_neg, kt=kd * e_neg,
                    bp=b_vec * e_rem, kp=kd * e_rem, e_tot=jnp.exp(tot))

    qs = (prepare(0, r0_ref, kk0_ref, lw0_ref, av0_ref, kd0_ref), prepare(1, r1_ref, kk1_ref, lw1_ref, av1_ref, kd1_ref))
    v_refs, y_refs = (v0_ref, v1_ref), (y0_ref, y1_ref)
    chains = [(d, gi) for gi in range(RW_HEADS // g) for d in range(2)]
    sl = lambda gi: slice(gi * gw, (gi + 1) * gw)
    each = lambda f, *lists: [f(*args) for args in zip(*lists)]
    pm_all = lambda xs, zs: each(lambda x, z: pm(x, bdiag(z)), xs, zs)

    ar = [jnp.concatenate([qs[d]["at"][:, sl(gi)], qs[d]["rt"][:, sl(gi)]], axis=0).astype(BF16) for d, gi in chains]
    pb = [_dot_nt(a, bdiag(qs[d]["bt"][:, sl(gi)])) for a, (d, gi) in zip(ar, chains)]
    pk = [_dot_nt(a, bdiag(qs[d]["kt"][:, sl(gi)])) for a, (d, gi) in zip(ar, chains)]
    a_ab = [p[:c] * strict_ref[d] for p, (d, gi) in zip(pb, chains)]
    a_rb = [p[c:] * incl_ref[d] for p, (d, gi) in zip(pb, chains)]
    a_ak = [p[:c] * strict_ref[d] for p, (d, gi) in zip(pk, chains)]
    a_rk = [p[c:] * incl_ref[d] for p, (d, gi) in zip(pk, chains)]
    ad = [a * blk16 for a in a_ab]
    a2 = pm_all(ad, ad)
    a4 = pm_all(a2, a2)
    a8 = pm_all(a4, a4)
    tinv = [eye + a for a in ad]
    for powr in (a2, a4, a8):
        tinv = each(lambda t, p_: t + p_, tinv, pm_all(tinv, powr))
    for off in (off32, off64):
        mid = pm_all(tinv, [a * off for a in a_ab])
        tinv = each(lambda t, p_: t + p_, tinv, pm_all(mid, tinv))
    s0 = [state_ref[d, gi] for d, gi in chains]
    vv = [v_refs[d][:, sl(gi)] for d, gi in chains]
    vd = [bdiag(v_) for v_ in vv]
    ars = each(lambda a, s_: _dot_nt(a, s_.astype(BF16)), ar, s0)
    akv = each(pm, a_ak, vd)
    u = pm_all(tinv, each(lambda x, y_: x[:c] + y_, ars, akv))
    yu = pm_all(a_rb, u)
    yv = each(pm, a_rk, vd)
    for (d, gi), x, y1_, y2_ in zip(chains, ars, yu, yv):
        y_refs[d][:, sl(gi)] = x[c:] + y1_ + y2_
    upd = [_dot_tn(jnp.concatenate([u_, v_], axis=0).astype(BF16),
                   jnp.concatenate([qs[d]["bp"][:, sl(gi)], qs[d]["kp"][:, sl(gi)]], axis=0).astype(BF16))
           for u_, v_, (d, gi) in zip(u, vv, chains)]
    for (d, gi), s_, up in zip(chains, s0, upd):
        state_ref[d, gi] = s_ * qs[d]["e_tot"][:, sl(gi)] + hm * up


def wkv_scan(r, v, kk, lw, av, kd, n_batch, L, Lc):
    c = WKV_CHUNK
    rows, w = r.shape
    nc, nl = Lc // c, L // c
    masks = _wkv_masks()

    def blk(d, b, s):
        j_ctx = s if d == 0 else nc - 1 - s
        j_lat = s - nc if d == 0 else nl - 1 - (s - nc)
        return jnp.where(s < nc, (n_batch * L + b * Lc) // c + j_ctx, (b * L) // c + j_lat)

    shared = lambda d: pl.BlockSpec((c, w), lambda b, s: (blk(d, b, s), 0))
    perdir = lambda d: pl.BlockSpec((1, c, w), lambda b, s: (d, blk(d, b, s), 0))
    full = lambda m: pl.BlockSpec(m.shape, lambda b, s: (0,) * m.ndim)
    gw = WKV_GROUP * RW_HEAD_DIM
    return pl.pallas_call(
        _wkv_kernel,
        grid=(n_batch, nc + nl),
        in_specs=[shared(0)] * 3 + [shared(1)] * 3 + [perdir(0)] * 3 + [perdir(1)] * 3 + [full(m) for m in masks],
        out_specs=[shared(0), shared(1)],
        out_shape=[jax.ShapeDtypeStruct((rows, w), F32)] * 2,
        scratch_shapes=[pltpu.VMEM((2, RW_HEADS // WKV_GROUP, gw, gw), F32)],
        compiler_params=_cparams(("arbitrary", "arbitrary")),
        name="wkv_scan",
    )(r, v, kk, r, v, kk, lw, av, kd, lw, av, kd, *masks)


def _pack_bf16_pairs(h):
    half = h.shape[1] // 2
    bits = lambda t: lax.bitcast_convert_type(t.astype(BF16).astype(F32), jnp.uint32)
    return (bits(h[:, :half]) >> 16) | (bits(h[:, half:]) & jnp.uint32(0xFFFF0000))


def _unpack_bf16_pairs(p):
    lo = lax.bitcast_convert_type(p << 16, F32).astype(BF16)
    hi = lax.bitcast_convert_type(p & jnp.uint32(0xFFFF0000), F32).astype(BF16)
    return jnp.concatenate([lo, hi], axis=1)


def _moe_kernel(be_ref, nb_ref, x_ref, wg_ref, wu_ref, wd_ref, o_ref, wg_s, wu_s, wd_s):
    i = pl.program_id(0)
    prev = be_ref[jnp.maximum(i - 1, 0)]

    @pl.when((i == 0) | (be_ref[i] != prev))
    def _():
        wg_s[...] = wg_ref[0, 0].astype(BF16)
        wu_s[...] = wu_ref[0, 0].astype(BF16)
        wd_s[...] = wd_ref[0, 0].astype(BF16)

    @pl.when(i < nb_ref[0])
    def _():
        o_ref[...] = _swiglu_block(x_ref, wg_s, wu_s, wd_s).astype(o_ref.dtype)

    @pl.when(i >= nb_ref[0])
    def _():
        o_ref[...] = jnp.zeros_like(o_ref)


def _swiglu_block(x_tiles, wg_s, wu_s, wd_s):
    x = _unpack_bf16_pairs(jnp.concatenate([x_tiles[:, s, :] for s in range(x_tiles.shape[1])], axis=1))
    hmid = (jax.nn.silu(_dot(x, wg_s[...])) * _dot(x, wu_s[...])).astype(BF16)
    return _dot(hmid, wd_s[...])


def _moe_gather_kernel(be_ref, nb_ref, cur_ref, nxt_ref, tab_ref, wg_ref, wu_ref, wd_ref, o_ref,
                       wg_s, wu_s, wd_s, xbuf, sems):
    i = pl.program_id(0)
    n_used = nb_ref[0]
    rows = xbuf.shape[1]
    batch = 8

    def start_gather(idx_ref, half):
        def issue(j, carry):
            for u in range(batch):
                r = j * batch + u
                pltpu.async_copy(tab_ref.at[pl.ds(idx_ref[0, 0, r], 1)], xbuf.at[half, pl.ds(r, 1)], sems.at[half])
            return carry

        lax.fori_loop(0, rows // batch, issue, 0)

    @pl.when((i == 0) & (n_used > 0))
    def _():
        start_gather(cur_ref, 0)

    @pl.when(i + 1 < n_used)
    def _():
        start_gather(nxt_ref, (i + 1) % 2)

    prev = be_ref[jnp.maximum(i - 1, 0)]

    @pl.when((i == 0) | (be_ref[i] != prev))
    def _():
        wg_s[...] = wg_ref[0, 0].astype(BF16)
        wu_s[...] = wu_ref[0, 0].astype(BF16)
        wd_s[...] = wd_ref[0, 0].astype(BF16)

    @pl.when(i < n_used)
    def _():
        half = i % 2
        pltpu.make_async_copy(tab_ref.at[pl.ds(0, rows)], xbuf.at[half], sems.at[half]).wait()
        o_ref[...] = _swiglu_block(xbuf.at[half], wg_s, wu_s, wd_s).astype(o_ref.dtype)

    @pl.when(i >= n_used)
    def _():
        o_ref[...] = jnp.zeros_like(o_ref)


def gathered_swiglu(table, slot_tok, block_e, n_used, w_gate, w_up, w_down, layer):
    d, ff = w_gate.shape[-2:]
    nb = slot_tok.shape[0] // MOE_BLOCK
    assert table.shape[0] >= MOE_BLOCK
    idx = slot_tok.reshape(nb, 1, MOE_BLOCK)
    grid_spec = pltpu.PrefetchScalarGridSpec(
        num_scalar_prefetch=2,
        grid=(nb,),
        in_specs=[pl.BlockSpec((1, 1, MOE_BLOCK), lambda i, be, nu: (i, 0, 0), memory_space=pltpu.SMEM),
                  pl.BlockSpec((1, 1, MOE_BLOCK), lambda i, be, nu: (jnp.minimum(i + 1, nb - 1), 0, 0),
                               memory_space=pltpu.SMEM),
                  pl.BlockSpec(memory_space=pl.ANY),
                  pl.BlockSpec((1, 1, d, ff), lambda i, be, nu: (layer, be[i], 0, 0)),
                  pl.BlockSpec((1, 1, d, ff), lambda i, be, nu: (layer, be[i], 0, 0)),
                  pl.BlockSpec((1, 1, ff, d), lambda i, be, nu: (layer, be[i], 0, 0))],
        out_specs=pl.BlockSpec((MOE_BLOCK, d), lambda i, be, nu: (i, 0)),
        scratch_shapes=[pltpu.VMEM((d, ff), BF16), pltpu.VMEM((d, ff), BF16), pltpu.VMEM((ff, d), BF16),
                        pltpu.VMEM((2, MOE_BLOCK) + table.shape[1:], table.dtype), pltpu.SemaphoreType.DMA((2,))],
    )
    return pl.pallas_call(
        _moe_gather_kernel,
        grid_spec=grid_spec,
        out_shape=jax.ShapeDtypeStruct((nb * MOE_BLOCK, d), BF16),
        compiler_params=pltpu.CompilerParams(dimension_semantics=("arbitrary",), vmem_limit_bytes=VMEM_LIMIT,
                                             disable_bounds_checks=True),
        name="gathered_swiglu",
    )(block_e, n_used, idx, idx, table, w_gate, w_up, w_down)


def grouped_swiglu(x, block_e, n_used, w_gate, w_up, w_down, layer):
    d, ff = w_gate.shape[-2:]
    nb = x.shape[0] // MOE_BLOCK
    grid_spec = pltpu.PrefetchScalarGridSpec(
        num_scalar_prefetch=2,
        grid=(nb,),
        in_specs=[pl.BlockSpec((MOE_BLOCK,) + x.shape[1:], lambda i, be, nu: (i, 0, 0)),
                  pl.BlockSpec((1, 1, d, ff), lambda i, be, nu: (layer, be[i], 0, 0)),
                  pl.BlockSpec((1, 1, d, ff), lambda i, be, nu: (layer, be[i], 0, 0)),
                  pl.BlockSpec((1, 1, ff, d), lambda i, be, nu: (layer, be[i], 0, 0))],
        out_specs=pl.BlockSpec((MOE_BLOCK, d), lambda i, be, nu: (i, 0)),
        scratch_shapes=[pltpu.VMEM((d, ff), BF16), pltpu.VMEM((d, ff), BF16), pltpu.VMEM((ff, d), BF16)],
    )
    return pl.pallas_call(
        _moe_kernel,
        grid_spec=grid_spec,
        out_shape=jax.ShapeDtypeStruct((nb * MOE_BLOCK, d), BF16),
        compiler_params=_cparams(("arbitrary",)),
        name="grouped_swiglu",
    )(block_e, n_used, x, w_gate, w_up, w_down)


def _combine_kernel(grp_ref, x_ref, y_ref, w_ref, s_ref, gate_ref, o_ref):
    del grp_ref
    f = s_ref[...].astype(F32)
    for k in range(TOP_K):
        f = f + w_ref[:, k:k + 1] * y_ref[k].astype(F32)
    o_ref[...] = x_ref[...] + gate_ref[0] * f


def moe_combine(x, yg, e_w, shared, gate3, grp):
    m, d = x.shape
    tm = TM // 2
    grid_spec = pltpu.PrefetchScalarGridSpec(
        num_scalar_prefetch=1,
        grid=(m // tm,),
        in_specs=[pl.BlockSpec((tm, d), lambda i, grp: (i, 0)),
                  pl.BlockSpec((TOP_K, tm, d), lambda i, grp: (0, i, 0)),
                  pl.BlockSpec((tm, e_w.shape[1]), lambda i, grp: (i, 0)),
                  pl.BlockSpec((tm, d), lambda i, grp: (i, 0)),
                  pl.BlockSpec((1, 1, d), lambda i, grp: (grp[i // 2], 0, 0))],
        out_specs=pl.BlockSpec((tm, d), lambda i, grp: (i, 0)),
    )
    return pl.pallas_call(
        _combine_kernel,
        grid_spec=grid_spec,
        out_shape=jax.ShapeDtypeStruct((m, d), F32),
        compiler_params=_cparams(("arbitrary",)),
        name="moe_combine",
    )(grp, x, yg, e_w, shared, gate3)


def _route_kernel(grp_ref, x_ref, g_ref, sh_ref, sc_ref, wt_ref, rb_ref, tri_ref, ones_ref,
                  h_ref, idx_ref, w_ref, rank_ref, cnt_ref, carry_ref):
    del grp_ref
    tm = x_ref.shape[0]
    gs = N_EXPERTS // N_GROUPS
    neg = -jnp.inf

    @pl.when(pl.program_id(0) == 0)
    def _():
        carry_ref[...] = jnp.zeros_like(carry_ref)

    x = x_ref[...]
    y = x * lax.rsqrt(jnp.mean(x * x, axis=-1, keepdims=True) + NORM_EPS)
    h = (y * g_ref[...]) * (1.0 + sc_ref[0]) + sh_ref[0]
    h_ref[...] = _pack_bf16_pairs(h)
    scores = jax.nn.sigmoid(_dot_nt(wt_ref[...], h, HI))
    biased = scores + rb_ref[...]

    def first_argmax(v, iota, n):
        m = jnp.max(v, axis=0, keepdims=True)
        return m, jnp.min(jnp.where(v == m, iota, float(n)), axis=0, keepdims=True)

    def stack_rows(rows):
        iota8 = lax.broadcasted_iota(jnp.int32, (8, tm), 0)
        out = jnp.zeros((8, tm), F32)
        for k, row in enumerate(rows):
            out = jnp.where(iota8 == k, row, out)
        return out

    assert gs == 8 and N_GROUPS == 8
    iota_g = lax.broadcasted_iota(jnp.int32, (gs, tm), 0).astype(F32)
    g_rows = []
    for g in range(N_GROUPS):
        bg = biased[g * gs:(g + 1) * gs]
        m1, i1 = first_argmax(bg, iota_g, gs)
        m2 = jnp.max(jnp.where(iota_g == i1, neg, bg), axis=0, keepdims=True)
        g_rows.append(m1 + m2)
    g_score = stack_rows(g_rows)
    g_sel = jnp.zeros((N_GROUPS, tm), F32)
    for _ in range(TOPK_GROUPS):
        _, ig = first_argmax(g_score, iota_g, N_GROUPS)
        hit = iota_g == ig
        g_sel = jnp.where(hit, 1.0, g_sel)
        g_score = jnp.where(hit, neg, g_score)
    e_sel = jnp.concatenate([jnp.broadcast_to(g_sel[g:g + 1], (gs, tm)) for g in range(N_GROUPS)], axis=0)
    masked = jnp.where(e_sel > 0.0, biased, neg)

    iota_e = lax.broadcasted_iota(jnp.int32, (N_EXPERTS, tm), 0).astype(F32)
    chosen = jnp.zeros((N_EXPERTS, tm), F32)
    hits, idx_rows, w_rows = [], [], []
    for _ in range(TOP_K):
        _, ie = first_argmax(masked, iota_e, N_EXPERTS)
        hit = iota_e == ie
        hits.append(hit)
        idx_rows.append(ie)
        w_rows.append(jnp.sum(jnp.where(hit, scores, 0.0), axis=0, keepdims=True))
        chosen = jnp.where(hit, 1.0, chosen)
        masked = jnp.where(hit, neg, masked)
    w_sum = w_rows[0]
    for wk in w_rows[1:]:
        w_sum = w_sum + wk
    idx_ref[...] = stack_rows(idx_rows).astype(jnp.int32)
    w_ref[...] = stack_rows([wk / w_sum * ROUTE_SCALE for wk in w_rows])

    chosen_b = chosen.astype(BF16)
    before = carry_ref[...] + _dot(chosen_b, tri_ref[...])
    rank_ref[...] = stack_rows([jnp.sum(jnp.where(hit, before, 0.0), axis=0, keepdims=True)
                                for hit in hits]).astype(jnp.int32)
    carry_ref[...] += _dot(chosen_b, ones_ref[...])
    cnt_ref[...] = carry_ref[:, :LANES].astype(jnp.int32)


def route(x, g, shift3, scale3, grp, router_w, router_b):
    m, d = x.shape
    ne = N_EXPERTS
    tri = jnp.asarray(np.triu(np.ones((TM, TM), np.float32), 1), BF16)
    ones = jnp.ones((TM, TM), BF16)
    rb = jnp.broadcast_to(router_b.astype(F32)[:, None], (ne, TM))
    row = lambda r: pl.BlockSpec((r, TM), lambda i, grp: (0, i))
    const = lambda shape: pl.BlockSpec(shape, lambda i, grp: (0, 0))
    grid_spec = pltpu.PrefetchScalarGridSpec(
        num_scalar_prefetch=1,
        grid=(m // TM,),
        in_specs=[pl.BlockSpec((TM, d), lambda i, grp: (i, 0)),
                  const((1, d)),
                  pl.BlockSpec((1, 1, d), lambda i, grp: (grp[i], 0, 0)),
                  pl.BlockSpec((1, 1, d), lambda i, grp: (grp[i], 0, 0)),
                  const((ne, d)), const((ne, TM)), const((TM, TM)), const((TM, TM))],
        out_specs=[pl.BlockSpec((TM, d // 2), lambda i, grp: (i, 0)), row(8), row(8), row(8), const((ne, LANES))],
        scratch_shapes=[pltpu.VMEM((ne, TM), F32)],
    )
    return pl.pallas_call(
        _route_kernel,
        grid_spec=grid_spec,
        out_shape=[jax.ShapeDtypeStruct((m, d // 2), jnp.uint32), jax.ShapeDtypeStruct((8, m), jnp.int32),
                   jax.ShapeDtypeStruct((8, m), F32), jax.ShapeDtypeStruct((8, m), jnp.int32),
                   jax.ShapeDtypeStruct((ne, LANES), jnp.int32)],
        compiler_params=_cparams(("arbitrary",)),
        name="route",
    )(grp, x, g.reshape(1, d), shift3, scale3, router_w.T, rb, tri, ones)


def _seq_edge_masks(n_batch, L, Lc):
    n_lat = n_batch * L
    starts = np.concatenate([np.arange(n_batch) * L, n_lat + np.arange(n_batch) * Lc])
    ends = np.concatenate([(np.arange(n_batch) + 1) * L, n_lat + (np.arange(n_batch) + 1) * Lc]) - 1
    first = np.ones((n_batch * (L + Lc), 1), np.float32)
    last = first.copy()
    first[starts] = 0.0
    last[ends] = 0.0
    return jnp.asarray(first), jnp.asarray(last)


RW_TM = 256


@functools.lru_cache(maxsize=None)
def _head_ones():
    h = np.arange(RW_WIDTH) // RW_HEAD_DIM
    return jnp.asarray(h[:, None] == h[None, :], BF16)


def _head_sum(x, ones):
    return _dot(x.astype(BF16), ones)


def _rwkv_prep_kernel(u_ref, w0_ref, w2_ref, a0_ref, a2_ref, g2_ref, kk_w_ref, ka_ref, ones_ref,
                      r_ref, v_ref, kk_ref, g_ref, lw_ref, av_ref, kd_ref):
    w = RW_WIDTH
    lo = 3 * w
    r_ref[...] = u_ref[:, 0:w]
    k = u_ref[:, w:2 * w]
    v_ref[...] = u_ref[:, 2 * w:lo]
    wl = jnp.tanh(u_ref[:, lo:lo + 2 * RW_DECAY_LORA]).astype(BF16)
    al = u_ref[:, lo + 2 * RW_DECAY_LORA:lo + 2 * RW_DECAY_LORA + 2 * RW_AAA_LORA].astype(BF16)
    gl = jax.nn.sigmoid(u_ref[:, lo + 2 * RW_DECAY_LORA + 2 * RW_AAA_LORA:]).astype(BF16)
    for d in range(2):
        z = -(w0_ref[d:d + 1, :] + _dot(wl, w2_ref[d]))
        softplus = jnp.maximum(z, 0.0) + jnp.log(1.0 + jnp.exp(-jnp.abs(z)))
        lw_ref[d] = -jnp.exp(-softplus - 0.5)
        a = jax.nn.sigmoid(a0_ref[d:d + 1, :] + _dot(al, a2_ref[d]))
        av_ref[d] = a
        kd_ref[d] = k * (1.0 + (a - 1.0) * ka_ref[...])
    g_ref[...] = _dot(gl, g2_ref[...])
    kk = k * kk_w_ref[...]
    norm = jnp.sqrt(_head_sum(kk * kk, ones_ref[...]))
    kk_ref[...] = kk / jnp.maximum(norm, 1e-12)


def rwkv_prep(u, w0, w2, a0, a2, g2, k_k, k_a):
    m = u.shape[0]
    w = RW_WIDTH
    zeros = jnp.zeros((RW_DECAY_LORA, w), F32)
    pad2 = lambda t: jnp.stack([jnp.concatenate([t[0], zeros], axis=0),
                                jnp.concatenate([zeros, t[1]], axis=0)]).astype(BF16)
    assert RW_DECAY_LORA == RW_AAA_LORA and 2 * RW_DECAY_LORA == LANES and RW_GATE_LORA == LANES
    full = lambda shape: pl.BlockSpec(shape, lambda i: (0,) * len(shape))
    row = pl.BlockSpec((RW_TM, w), lambda i: (i, 0))
    row2 = pl.BlockSpec((2, RW_TM, w), lambda i: (0, i, 0))
    one = jax.ShapeDtypeStruct((m, w), F32)
    two = jax.ShapeDtypeStruct((2, m, w), F32)
    return pl.pallas_call(
        _rwkv_prep_kernel,
        grid=(m // RW_TM,),
        in_specs=[pl.BlockSpec((RW_TM, RW_COLS), lambda i: (i, 0)), full((2, w)), full((2, LANES, w)), full((2, w)),
                  full((2, LANES, w)), full((LANES, w)), full((1, w)), full((1, w)), full((w, w))],
        out_specs=[row, row, row, row, row2, row2, row2],
        out_shape=[one, one, one, one, two, two, two],
        compiler_params=_cparams(("arbitrary",)),
        name="rwkv_prep",
    )(u, w0, pad2(w2), a0, pad2(a2), g2.astype(BF16), k_k.reshape(1, w), k_a.reshape(1, w), _head_ones())


def _rwkv_out_kernel(y0_ref, y1_ref, r_ref, v_ref, g_ref, kd_ref, rk_ref, lnw_ref, lnb_ref, ones_ref, o_ref):
    ones = ones_ref[...]
    inv_n = 1.0 / RW_HEAD_DIM
    y = y0_ref[...] + y1_ref[...]
    yc = y - _head_sum(y, ones) * inv_n
    var = _head_sum(yc * yc, ones) * inv_n
    yn = yc * lax.rsqrt(var + RW_GN_EPS)
    bonus = _head_sum(r_ref[...] * (kd_ref[0] + kd_ref[1]) * rk_ref[...], ones) * v_ref[...]
    o_ref[...] = ((yn * lnw_ref[...] + lnb_ref[...] + bonus) * g_ref[...]).astype(o_ref.dtype)


def rwkv_out(y0, y1, r, v, g, kd, r_k, ln_w, ln_b):
    m, w = r.shape
    full = lambda shape: pl.BlockSpec(shape, lambda i: (0,) * len(shape))
    row = pl.BlockSpec((RW_TM, w), lambda i: (i, 0))
    vec = lambda t: t.reshape(1, w)
    return pl.pallas_call(
        _rwkv_out_kernel,
        grid=(m // RW_TM,),
        in_specs=[row, row, row, row, row, pl.BlockSpec((2, RW_TM, w), lambda i: (0, i, 0)),
                  full((1, w)), full((1, w)), full((1, w)), full((w, w))],
        out_specs=row,
        out_shape=jax.ShapeDtypeStruct((m, w), BF16),
        compiler_params=_cparams(("arbitrary",)),
        name="rwkv_out",
    )(y0, y1, r, v, g, kd, vec(r_k), vec(ln_w), vec(ln_b), _head_ones())


def _moe(h, e_idx, rank, counts, exp_gate, exp_up, exp_down, sh_gate, sh_up, sh_down, layer):
    T = h.shape[0]
    D = exp_gate.shape[-2]
    n = T * TOP_K
    padded = (counts + MOE_BLOCK - 1) // MOE_BLOCK * MOE_BLOCK
    pad_end = jnp.cumsum(padded)
    pad_start = pad_end - padded
    experts = jnp.arange(N_EXPERTS, dtype=jnp.int32)
    dest = rank + jnp.sum(jnp.where(e_idx[:, :, None] == experts, pad_start.astype(jnp.int32), 0), axis=-1)
    n_blocks = -(-n // MOE_BLOCK) + N_EXPERTS
    n_slots = n_blocks * MOE_BLOCK
    flat_dest = dest.reshape(-1)
    tok = jnp.tile(jnp.arange(T, dtype=jnp.int32), TOP_K)
    slot_tok = jnp.zeros((n_slots,), jnp.int32).at[flat_dest].set(tok)
    block_start = jnp.arange(n_blocks, dtype=jnp.int32) * MOE_BLOCK
    block_e = jnp.minimum(jnp.sum(block_start[:, None] >= pad_end[None, :], axis=1), N_EXPERTS - 1).astype(jnp.int32)
    n_used = (pad_end[-1] // MOE_BLOCK).astype(jnp.int32).reshape(1)
    h = h.reshape(T, -1, LANES)
    y = gathered_swiglu(h, slot_tok, block_e, n_used, exp_gate, exp_up, exp_down, layer)
    yg = jnp.take(y, flat_dest, axis=0, mode="clip").reshape(TOP_K, T, D)
    nb_sh = T // MOE_BLOCK
    sh4 = lambda w: w.reshape((w.shape[0], 1) + w.shape[1:])
    shared = grouped_swiglu(h, jnp.zeros((nb_sh,), jnp.int32), jnp.full((1,), nb_sh, jnp.int32),
                            sh4(sh_gate), sh4(sh_up), sh4(sh_down), layer)
    return yg, shared


def kernel(x, c, ctx, c_ctx, mod_w, mod_b, norm1_g, norm2_g, w_in, hy_conv, hy_w1, hy_b1, hy_w2, hy_b2, hy_freq,
           hy_w3, hy_bias, na_rpb, rw_shift, rw_w0, rw_w2, rw_a0, rw_a2, rw_g2, rw_kk, rw_ka, rw_rk, rw_ln_w,
           rw_ln_b, proj_a, proj_b, proj_c, w_out, router_w, router_b, exp_gate, exp_up, exp_down, sh_gate, sh_up,
           sh_down, final_g):
    B, L, D = x.shape
    Lc = ctx.shape[1]
    depth = mod_w.shape[0]
    n_lat, n_ctx = B * L, B * Lc
    assert L % TM == 0 and n_ctx % TM == 0 and L % WKV_CHUNK == 0 and Lc % WKV_CHUNK == 0
    col_hy = 3 * HY_WIDTH
    col_na = col_hy + 3 * NA_WIDTH
    col_rw = col_na + RW_COLS

    xs = jnp.concatenate([x.reshape(n_lat, D), ctx.reshape(n_ctx, D)], axis=0)
    grp_all = jnp.asarray(np.concatenate([np.repeat(np.arange(B), L // TM), np.full(n_ctx // TM, B)]), jnp.int32)
    s8 = jnp.zeros((8, D), F32).at[:B].set(jax.nn.silu(c)).at[B].set(jax.nn.silu(c_ctx))
    first, last = _seq_edge_masks(B, L, Lc)

    for i in range(depth):
        with_ctx = i < depth - 1
        mod = small_matmul_bias(s8, mod_w, mod_b, i)[:B + 1].reshape(B + 1, 1, N_MOD * D)
        sh1, sc1, g1, sh2, sc2, g2 = (mod[:, :, j * D:(j + 1) * D] for j in range(N_MOD))
        w_bf = w_in[i].astype(BF16)
        proj = functools.partial(normmod_matmul, xs, norm1_g[i], sh1, sc1, grp_all)
        hy = short_conv(proj(w_bf[:, :col_hy], HY_WIDTH, BF16), hy_conv[i], first, last, HY_WIDTH)
        na = proj(w_bf[:, col_hy:col_na], NA_WIDTH, BF16)
        rw = proj(w_bf[:, col_na:col_rw], RW_COLS // 3, F32)
        gates = proj(w_bf[:, col_rw:], D // 2, BF16)

        hy_args = (hy_w1[i], hy_b1[i], hy_w2[i], hy_b2[i], hy_freq[i], hy_w3[i])
        h_raw, ss = hyena_filters_raw(L, *hy_args)
        hr, hi = hyena_filter_spectrum(h_raw, _filter_scale(ss), L)
        z = hyena_conv(hy, 0, hy, 2, hr, hi, 0, hy_bias[i][0], B, L)
        o_a = hyena_conv(hy, 1, z, 0, hr, hi, 1, hy_bias[i][1], B, L).astype(BF16)
        o_b = na_latent(na, na_rpb[i], B, L, Lc)
        rw = short_conv(rw, rw_shift[i], first, last, RW_COLS // 3)
        r_, v_, kk_, gg_, lw_, av_, kd_ = rwkv_prep(rw, rw_w0[i], rw_w2[i], rw_a0[i], rw_a2[i], rw_g2[i], rw_kk[i],
                                                    rw_ka[i])
        y_f, y_b = wkv_scan(r_, v_, kk_, lw_, av_, kd_, B, L, Lc)
        o_c = rwkv_out(y_f, y_b, r_, v_, gg_, kd_, rw_rk[i], rw_ln_w[i], rw_ln_b[i])

        if with_ctx:
            h_raw_c, ss_c = hyena_filters_raw(Lc, *hy_args)
            o_a_c = hyena_small(hy, n_lat // Lc, B, Lc, h_raw_c, _filter_scale(ss_c), hy_bias[i])
            o_a = jnp.concatenate([o_a, o_a_c], axis=0)
            o_b = jnp.concatenate([o_b, ctx_attn(na, B, L, Lc)], axis=0)
            m_rows = n_lat + n_ctx
        else:
            m_rows = n_lat
        grp = grp_all[:m_rows // TM]
        merged = branch_merge(m_rows, o_a, o_b, o_c, gates, proj_a[i].astype(BF16), proj_b[i].astype(BF16),
                              proj_c[i].astype(BF16))
        xs = resid_matmul(merged, w_out[i].astype(BF16), xs, g1, grp)

        h2, e_idx, e_w, rank, counts = route(xs, norm2_g[i], sh2, sc2, grp, router_w[i], router_b[i])
        yg, shared = _moe(h2, e_idx[:TOP_K], rank[:TOP_K], counts[:, 0], exp_gate, exp_up, exp_down, sh_gate, sh_up,
                          sh_down, i)
        xs = moe_combine(xs, yg, e_w.T, shared, g2, grp)

    return rmsnorm_rows(xs, final_g).reshape(B, L, D)
```

```python
import functools
import math

import jax
import jax.numpy as jnp
import numpy as np
from jax import lax
from jax.experimental import pallas as pl
from jax.experimental.pallas import tpu as pltpu

F32 = jnp.float32
BF16 = jnp.bfloat16
HI = lax.Precision.HIGHEST

GRID_W = 64
NORM_EPS = 1e-6
N_MOD = 6
SHORT_CONV = 3
HY_WIDTH = 1024
HY_BANDS = 16
HY_EMB = 2 * HY_BANDS + 1
HY_FILTER_ORDER = 64
HY_FAST_DECAY = 0.3
HY_SLOW_DECAY = 1.5
HY_DECAY_TARGET = 1e-2
NA_HEADS = 16
NA_HEAD_DIM = 64
NA_WIDTH = NA_HEADS * NA_HEAD_DIM
NA_ROWS = 8
NA_COLS = 16
RW_HEADS = 16
RW_HEAD_DIM = 64
RW_WIDTH = RW_HEADS * RW_HEAD_DIM
RW_DECAY_LORA = 64
RW_AAA_LORA = 64
RW_GATE_LORA = 128
RW_GN_EPS = 64e-5
RW_COLS = 3 * RW_WIDTH + 2 * RW_DECAY_LORA + 2 * RW_AAA_LORA + RW_GATE_LORA
RW_SPLITS = [RW_WIDTH, 2 * RW_WIDTH, 3 * RW_WIDTH, 3 * RW_WIDTH + 2 * RW_DECAY_LORA,
             3 * RW_WIDTH + 2 * RW_DECAY_LORA + 2 * RW_AAA_LORA]
N_BRANCH = 3
N_EXPERTS = 64
TOP_K = 6
N_GROUPS = 8
TOPK_GROUPS = 4
ROUTE_SCALE = 2.5

LANES = 128
VMEM_LIMIT = 56 * 1024 * 1024
TM = 512
MOE_BLOCK = 256
WKV_CHUNK = 64
WKV_GROUP = 4


def _cparams(sem):
    return pltpu.CompilerParams(dimension_semantics=sem, vmem_limit_bytes=VMEM_LIMIT)


def _dot(a, b, prec=None):
    return jnp.dot(a, b, preferred_element_type=F32, precision=prec)


def _dot_nt(a, b, prec=None):
    return lax.dot_general(a, b, (((1,), (1,)), ((), ())), preferred_element_type=F32, precision=prec)


def _dot_tn(a, b, prec=None):
    return lax.dot_general(a, b, (((0,), (0,)), ((), ())), preferred_element_type=F32, precision=prec)


def _small_mm_kernel(a_ref, w_ref, b_ref, o_ref):
    o_ref[...] = _dot(a_ref[...], w_ref[0], HI) + b_ref[0]


def small_matmul_bias(a, w, b, layer, tn=1536):
    m, k = a.shape
    n = w.shape[2]
    return pl.pallas_call(
        _small_mm_kernel,
        grid=(n // tn,),
        in_specs=[pl.BlockSpec((m, k), lambda j: (0, 0)),
                  pl.BlockSpec((1, k, tn), lambda j: (layer, 0, j)),
                  pl.BlockSpec((1, 1, tn), lambda j: (layer, 0, j))],
        out_specs=pl.BlockSpec((m, tn), lambda j: (0, j)),
        out_shape=jax.ShapeDtypeStruct((m, n), F32),
        compiler_params=_cparams(("arbitrary",)),
        name="mod_matmul",
    )(a, w, b.reshape(b.shape[0], 1, n))


def _normmod_mm_kernel(grp_ref, x_ref, g_ref, sh_ref, sc_ref, w_ref, o_ref, h_ref):
    del grp_ref

    @pl.when(pl.program_id(1) == 0)
    def _():
        x = x_ref[...]
        y = x * lax.rsqrt(jnp.mean(x * x, axis=-1, keepdims=True) + NORM_EPS)
        y = y * g_ref[...]
        h_ref[...] = (y * (1.0 + sc_ref[0]) + sh_ref[0]).astype(h_ref.dtype)

    o_ref[...] = _dot(h_ref[...], w_ref[...]).astype(o_ref.dtype)


def normmod_matmul(x, g, shift3, scale3, grp, w, tn, out_dtype):
    m, d = x.shape
    n = w.shape[1]
    grid_spec = pltpu.PrefetchScalarGridSpec(
        num_scalar_prefetch=1,
        grid=(m // TM, n // tn),
        in_specs=[pl.BlockSpec((TM, d), lambda i, j, grp: (i, 0)),
                  pl.BlockSpec((1, d), lambda i, j, grp: (0, 0)),
                  pl.BlockSpec((1, 1, d), lambda i, j, grp: (grp[i], 0, 0)),
                  pl.BlockSpec((1, 1, d), lambda i, j, grp: (grp[i], 0, 0)),
                  pl.BlockSpec((d, tn), lambda i, j, grp: (0, j))],
        out_specs=pl.BlockSpec((TM, tn), lambda i, j, grp: (i, j)),
        scratch_shapes=[pltpu.VMEM((TM, d), BF16)],
    )
    return pl.pallas_call(
        _normmod_mm_kernel,
        grid_spec=grid_spec,
        out_shape=jax.ShapeDtypeStruct((m, n), out_dtype),
        compiler_params=_cparams(("arbitrary", "arbitrary")),
        name="normmod_matmul",
    )(grp, x, g.reshape(1, d), shift3, scale3, w)


def _merge_kernel(oa_ref, ob_ref, oc_ref, ga_ref, gb_ref, gc_ref, pa_ref, pb_ref, pc_ref, o_ref):
    gate = lambda ref: jax.nn.sigmoid(ref[...].astype(F32))
    m = gate(ga_ref) * _dot(oa_ref[...], pa_ref[...])
    m = m + gate(gb_ref) * _dot(ob_ref[...], pb_ref[...])
    m = m + gate(gc_ref) * _dot(oc_ref[...], pc_ref[...])
    o_ref[...] = m.astype(o_ref.dtype)


def branch_merge(m, o_a, o_b, o_c, gates, pa, pb, pc, tn=1024):
    k = o_a.shape[1]
    d = pa.shape[1]
    nj = d // tn
    o_spec = pl.BlockSpec((TM, k), lambda i, j: (i, 0))
    p_spec = pl.BlockSpec((k, tn), lambda i, j: (0, j))
    return pl.pallas_call(
        _merge_kernel,
        grid=(m // TM, nj),
        in_specs=[o_spec, o_spec, o_spec,
                  pl.BlockSpec((TM, tn), lambda i, j: (i, j)),
                  pl.BlockSpec((TM, tn), lambda i, j: (i, j + nj)),
                  pl.BlockSpec((TM, tn), lambda i, j: (i, j + 2 * nj)),
                  p_spec, p_spec, p_spec],
        out_specs=pl.BlockSpec((TM, tn), lambda i, j: (i, j)),
        out_shape=jax.ShapeDtypeStruct((m, d), BF16),
        compiler_params=_cparams(("arbitrary", "arbitrary")),
        name="branch_merge",
    )(o_a, o_b, o_c, gates, gates, gates, pa, pb, pc)


def _resid_mm_kernel(grp_ref, a_ref, w_ref, x_ref, gate_ref, o_ref):
    del grp_ref
    o_ref[...] = x_ref[...] + gate_ref[0] * _dot(a_ref[...], w_ref[...])


def resid_matmul(a, w, x, gate3, grp, tn=1024):
    m, k = a.shape
    d = w.shape[1]
    grid_spec = pltpu.PrefetchScalarGridSpec(
        num_scalar_prefetch=1,
        grid=(m // TM, d // tn),
        in_specs=[pl.BlockSpec((TM, k), lambda i, j, grp: (i, 0)),
                  pl.BlockSpec((k, tn), lambda i, j, grp: (0, j)),
                  pl.BlockSpec((TM, tn), lambda i, j, grp: (i, j)),
                  pl.BlockSpec((1, 1, tn), lambda i, j, grp: (grp[i], 0, j))],
        out_specs=pl.BlockSpec((TM, tn), lambda i, j, grp: (i, j)),
    )
    return pl.pallas_call(
        _resid_mm_kernel,
        grid_spec=grid_spec,
        out_shape=jax.ShapeDtypeStruct((m, d), F32),
        compiler_params=_cparams(("arbitrary", "arbitrary")),
        name="resid_matmul",
    )(grp, a, w, x, gate3)


def _short_conv_kernel(x_ref, prev_ref, next_ref, w_ref, first_ref, last_ref, o_ref):
    x = x_ref[...].astype(F32)
    tm = x.shape[0]
    row = lax.broadcasted_iota(jnp.int32, x.shape, 0)
    halo = prev_ref.shape[0]
    before = prev_ref[...].astype(F32)[halo - 1:halo, :]
    after = next_ref[...].astype(F32)[0:1, :]
    prev = jnp.where(row == 0, before, pltpu.roll(x, 1, 0)) * first_ref[...]
    nxt = jnp.where(row == tm - 1, after, pltpu.roll(x, tm - 1, 0)) * last_ref[...]
    o_ref[...] = prev * w_ref[0:1, :] + x * w_ref[1:2, :] + nxt * w_ref[2:3, :]


def short_conv(u, w, first, last, tn):
    assert SHORT_CONV == 3
    m, c = u.shape
    halo = 16
    per = TM // halo
    n_halo = m // halo
    return pl.pallas_call(
        _short_conv_kernel,
        grid=(m // TM, c // tn),
        in_specs=[pl.BlockSpec((TM, tn), lambda i, j: (i, j)),
                  pl.BlockSpec((halo, tn), lambda i, j: (jnp.maximum(i * per - 1, 0), j)),
                  pl.BlockSpec((halo, tn), lambda i, j: (jnp.minimum((i + 1) * per, n_halo - 1), j)),
                  pl.BlockSpec((SHORT_CONV, tn), lambda i, j: (0, j)),
                  pl.BlockSpec((TM, 1), lambda i, j: (i, 0)),
                  pl.BlockSpec((TM, 1), lambda i, j: (i, 0))],
        out_specs=pl.BlockSpec((TM, tn), lambda i, j: (i, j)),
        out_shape=jax.ShapeDtypeStruct((m, c), F32),
        compiler_params=_cparams(("arbitrary", "arbitrary")),
        name="short_conv",
    )(u, u, u, w, first, last)


def _rmsnorm_kernel(x_ref, g_ref, o_ref):
    x = x_ref[...]
    y = x * lax.rsqrt(jnp.mean(x * x, axis=-1, keepdims=True) + NORM_EPS)
    o_ref[...] = y * g_ref[...]


def rmsnorm_rows(x, g):
    m, d = x.shape
    return pl.pallas_call(
        _rmsnorm_kernel,
        grid=(m // TM,),
        in_specs=[pl.BlockSpec((TM, d), lambda i: (i, 0)), pl.BlockSpec((1, d), lambda i: (0, 0))],
        out_specs=pl.BlockSpec((TM, d), lambda i: (i, 0)),
        out_shape=jax.ShapeDtypeStruct((m, d), F32),
        compiler_params=_cparams(("arbitrary",)),
        name="final_rmsnorm",
    )(x, g.reshape(1, d))


def _hyfilt_kernel(z_ref, w1_ref, b1_ref, w2_ref, b2_ref, fr_ref, w3_ref, dl_ref, h_ref, ss_ref):
    z = z_ref[...]
    hdn = jnp.sin(fr_ref[0:1, :] * (_dot(z, w1_ref[...], HI) + b1_ref[...]))
    hdn = jnp.sin(fr_ref[1:2, :] * (_dot(hdn, w2_ref[...], HI) + b2_ref[...]))
    h = _dot(hdn, w3_ref[...], HI)
    h = h * jnp.exp(-z[:, 0:1] * dl_ref[...])
    h_ref[...] = h

    @pl.when(pl.program_id(0) == 0)
    def _():
        ss_ref[...] = jnp.zeros_like(ss_ref)

    ss_ref[...] += jnp.sum(h * h, axis=0, keepdims=True)


def hyena_filters_raw(L, w1, b1, w2, b2, freq, w3):
    t = np.linspace(0.0, 1.0, L, dtype=np.float32)[:, None]
    omega = np.float32(2.0 * math.pi / L) * np.arange(L, dtype=np.float32)[:, None]
    bands = np.linspace(1e-4, HY_BANDS - 1, HY_BANDS, dtype=np.float32)[None, :]
    z = np.concatenate([t, np.cos(omega * bands), -np.sin(omega * bands),
                        np.zeros((L, HY_FILTER_ORDER - HY_EMB), np.float32)], axis=-1).astype(np.float32)
    w1p = jnp.concatenate([w1, jnp.zeros((HY_FILTER_ORDER - HY_EMB, HY_FILTER_ORDER), F32)], axis=0)
    deltas = np.abs(np.linspace(math.log(HY_DECAY_TARGET) / HY_SLOW_DECAY,
                                math.log(HY_DECAY_TARGET) / HY_FAST_DECAY, HY_WIDTH, dtype=np.float32))
    dl4 = np.tile(deltas, 4)[None, :]
    tl = min(L, 256)
    n = 4 * HY_WIDTH
    fo = HY_FILTER_ORDER
    full = lambda shape: pl.BlockSpec(shape, lambda i: (0, 0))
    return pl.pallas_call(
        _hyfilt_kernel,
        grid=(L // tl,),
        in_specs=[pl.BlockSpec((tl, fo), lambda i: (i, 0)), full((fo, fo)), full((1, fo)), full((fo, fo)),
                  full((1, fo)), full((2, fo)), full((fo, n)), full((1, n))],
        out_specs=[pl.BlockSpec((tl, n), lambda i: (i, 0)), full((1, n))],
        out_shape=[jax.ShapeDtypeStruct((L, n), F32), jax.ShapeDtypeStruct((1, n), F32)],
        compiler_params=_cparams(("arbitrary",)),
        name="hyena_filters",
    )(jnp.asarray(z), w1p, b1.reshape(1, fo), w2, b2.reshape(1, fo), freq, w3, jnp.asarray(dl4))


def _filter_scale(ss):
    s = ss.reshape(2, 2, HY_WIDTH)
    rs = lax.rsqrt(jnp.sum(s, axis=1, keepdims=True))
    return jnp.broadcast_to(rs, (2, 2, HY_WIDTH)).reshape(1, 4 * HY_WIDTH)


FFT_N1 = 128
FFT_N2 = 64
FFT_PITCH = 72
FFT_BATCH = 4


@functools.lru_cache(maxsize=None)
def _fft_tables():
    n1, n2 = FFT_N1, FFT_N2
    n = n1 * n2
    a = np.arange(n1 // 2)[None, None, :]
    k1 = np.arange(n1)[None, :, None]
    b = np.arange(n2)[:, None, None]
    theta = 2.0 * np.pi * ((a * k1 % n1) / n1 + (b * k1) / n)
    g = np.concatenate([np.cos(theta), -np.sin(theta)], axis=1)
    ig = np.concatenate([np.cos(theta), -np.sin(theta)], axis=1).transpose(0, 2, 1) / n
    k2 = np.arange(n2)[:, None]
    bb = np.arange(n2)[None, :]
    ph = 2.0 * np.pi * (k2 * bb % n2) / n2
    fr, fi = np.cos(ph), -np.sin(ph)
    f2 = np.block([[fr, -fi], [fi, fr]])
    if2 = np.block([[fr, fi], [-fi, fr]])
    return (jnp.asarray(g, BF16), jnp.asarray(f2, BF16), jnp.asarray(if2, BF16), jnp.asarray(ig, BF16))


def _fft_stage1(u_ref, g_ref, sr_ref, si_ref):
    n1, n2, p = FFT_N1, FFT_N2, FFT_PITCH

    def body(i, carry):
        bs = [i * FFT_BATCH + j for j in range(FFT_BATCH)]
        xs = [u_ref[pl.ds(b, n1 // 2, stride=n2), :].astype(BF16) for b in bs]
        outs = [_dot(g_ref[b], x) for b, x in zip(bs, xs)]
        for b, a in zip(bs, outs):
            sr_ref[pl.ds(b, n1, stride=p), :] = a[:n1]
            si_ref[pl.ds(b, n1, stride=p), :] = a[n1:]
        return carry

    lax.fori_loop(0, n2 // FFT_BATCH, body, 0)


def _bin_rows(k1):
    return pl.ds(pl.multiple_of(k1 * FFT_PITCH, 8), FFT_N2)


def _hyconv_kernel(xm_ref, u_ref, hr_ref, hi_ref, bias_ref, g_ref, f2_ref, if2_ref, ig_ref, o_ref, sr_ref, si_ref):
    n1, n2, p = FFT_N1, FFT_N2, FFT_PITCH
    _fft_stage1(u_ref, g_ref, sr_ref, si_ref)

    def pair(ref, k):
        return jnp.concatenate([ref[_bin_rows(2 * k), :], ref[_bin_rows(2 * k + 1), :]], axis=1)

    def pair_h(ref, k):
        blk = ref[pl.ds(pl.multiple_of(k * 2 * n2, 2 * n2), 2 * n2), :]
        return jnp.concatenate([blk[:n2], blk[n2:]], axis=1)

    def unpair(ref, k, val):
        w = val.shape[1] // 2
        ref[_bin_rows(2 * k), :] = val[:, :w]
        ref[_bin_rows(2 * k + 1), :] = val[:, w:]

    def stage2(i, carry):
        ks = [i * FFT_BATCH + j for j in range(FFT_BATCH)]
        zs = [jnp.concatenate([pair(sr_ref, k), pair(si_ref, k)], axis=0).astype(BF16) for k in ks]
        xs = [_dot(f2_ref[...], z) for z in zs]
        ys = []
        for k, x in zip(ks, xs):
            xr, xi = x[:n2], x[n2:]
            hr, hi = pair_h(hr_ref, k), pair_h(hi_ref, k)
            ys.append(jnp.concatenate([xr * hr - xi * hi, xr * hi + xi * hr], axis=0).astype(BF16))
        bbs = [_dot(if2_ref[...], y) for y in ys]
        for k, bb in zip(ks, bbs):
            unpair(sr_ref, k, bb[:n2])
            unpair(si_ref, k, bb[n2:])
        return carry

    lax.fori_loop(0, n1 // 2 // FFT_BATCH, stage2, 0)

    def stage3(i, carry):
        bs = [i * FFT_BATCH + j for j in range(FFT_BATCH)]
        sts = [jnp.concatenate([sr_ref[pl.ds(b, n1, stride=p), :], si_ref[pl.ds(b, n1, stride=p), :]],
                               axis=0).astype(BF16) for b in bs]
        outs = [_dot(ig_ref[b], st) for b, st in zip(bs, sts)]
        for b, o in zip(bs, outs):
            o_ref[pl.ds(b, n1 // 2, stride=n2), :] = o
        return carry

    lax.fori_loop(0, n2 // FFT_BATCH, stage3, 0)
    bias = bias_ref[...]
    rows_per_pass = 512

    def finish(i, carry):
        rows = pl.ds(pl.multiple_of(i * rows_per_pass, rows_per_pass), rows_per_pass)
        o_ref[rows, :] = xm_ref[rows, :] * (o_ref[rows, :] + u_ref[rows, :] * bias)
        return carry

    lax.fori_loop(0, o_ref.shape[0] // rows_per_pass, finish, 0)


def hyena_conv(xm_arr, xm_col, u_arr, u_col, hr, hi, h_col, bias, n_batch, L):
    assert L == FFT_N1 * FFT_N2 // 2
    cb = LANES
    nct = HY_WIDTH // cb
    n = 2 * L
    g, f2, if2, ig = _fft_tables()
    const3 = lambda shape: pl.BlockSpec(shape, lambda b, c: (0, 0, 0))
    const2 = lambda shape: pl.BlockSpec(shape, lambda b, c: (0, 0))
    return pl.pallas_call(
        _hyconv_kernel,
        grid=(n_batch, nct),
        in_specs=[pl.BlockSpec((L, cb), lambda b, c: (b, xm_col * nct + c)),
                  pl.BlockSpec((L, cb), lambda b, c: (b, u_col * nct + c)),
                  pl.BlockSpec((n, cb), lambda b, c: (0, h_col * nct + c)),
                  pl.BlockSpec((n, cb), lambda b, c: (0, h_col * nct + c)),
                  pl.BlockSpec((1, cb), lambda b, c: (0, c)),
                  const3(g.shape), const2(f2.shape), const2(if2.shape), const3(ig.shape)],
        out_specs=pl.BlockSpec((L, cb), lambda b, c: (b, c)),
        out_shape=jax.ShapeDtypeStruct((n_batch * L, HY_WIDTH), F32),
        scratch_shapes=[pltpu.VMEM((FFT_N1 * FFT_PITCH, cb), F32)] * 2,
        compiler_params=_cparams(("arbitrary", "arbitrary")),
        name="hyena_conv",
    )(xm_arr, u_arr, hr, hi, bias.reshape(1, HY_WIDTH), g, f2, if2, ig)


def _hyspec_kernel(h0_ref, h1_ref, rs_ref, g_ref, f2_ref, hr_ref, hi_ref, s0r, s0i, s1r, s1i):
    n1, n2 = FFT_N1, FFT_N2
    _fft_stage1(h0_ref, g_ref, s0r, s0i)
    _fft_stage1(h1_ref, g_ref, s1r, s1i)
    rs = rs_ref[...]
    h10 = h1_ref[0:1, :]

    def stage2(i, carry):
        ks = [i * FFT_BATCH + j for j in range(FFT_BATCH)]
        zs = [jnp.concatenate([jnp.concatenate([s0r[_bin_rows(k), :], s1r[_bin_rows(k), :]], axis=1),
                               jnp.concatenate([s0i[_bin_rows(k), :], s1i[_bin_rows(k), :]], axis=1)],
                              axis=0).astype(BF16) for k in ks]
        xs = [_dot(f2_ref[...], z) for z in zs]
        for k, x in zip(ks, xs):
            rows = pl.ds(pl.multiple_of(k * n2, n2), n2)
            w = x.shape[1] // 2
            hr_ref[rows, :] = rs * (x[:n2, :w] + x[:n2, w:] - h10)
            hi_ref[rows, :] = rs * (x[n2:, :w] - x[n2:, w:])
        return carry

    lax.fori_loop(0, n1 // FFT_BATCH, stage2, 0)


def hyena_filter_spectrum(h_raw, rs, L):
    assert L == FFT_N1 * FFT_N2 // 2
    cb = LANES
    nct = HY_WIDTH // cb
    n = 2 * L
    g, f2, _, _ = _fft_tables()
    out_spec = pl.BlockSpec((n, cb), lambda o, c: (0, o * nct + c))
    scr = pltpu.VMEM((FFT_N1 * FFT_PITCH, cb), F32)
    return pl.pallas_call(
        _hyspec_kernel,
        grid=(2, nct),
        in_specs=[pl.BlockSpec((L, cb), lambda o, c: (0, (2 * o) * nct + c)),
                  pl.BlockSpec((L, cb), lambda o, c: (0, (2 * o + 1) * nct + c)),
                  pl.BlockSpec((1, cb), lambda o, c: (0, (2 * o) * nct + c)),
                  pl.BlockSpec(g.shape, lambda o, c: (0, 0, 0)),
                  pl.BlockSpec(f2.shape, lambda o, c: (0, 0))],
        out_specs=[out_spec, out_spec],
        out_shape=[jax.ShapeDtypeStruct((n, 2 * HY_WIDTH), F32)] * 2,
        scratch_shapes=[scr, scr, scr, scr],
        compiler_params=_cparams(("arbitrary", "arbitrary")),
        name="hyena_filter_spectrum",
    )(h_raw, h_raw, rs, g, f2)


@functools.lru_cache(maxsize=None)
def _dense_dft_tables(L):
    n = 2 * L
    k = np.arange(n)[:, None]
    t = np.arange(L)[None, :]
    ph = 2.0 * np.pi * (k * t % n) / n
    fwd = np.concatenate([np.cos(ph), -np.sin(ph)], axis=0)
    inv = np.concatenate([np.cos(ph), -np.sin(ph)], axis=0).T / n
    return jnp.asarray(fwd, BF16), jnp.asarray(inv, BF16)


def _hyena_small_kernel(x1_ref, x2_ref, v_ref, h_ref0a, h_ref0b, h_ref1a, h_ref1b, rs0_ref, rs1_ref,
                        b0_ref, b1_ref, fwd_ref, inv_ref, o_ref, *, L):
    n = 2 * L
    fwd = fwd_ref[...]
    inv = inv_ref[...]

    def conv(u, ha_ref, hb_ref, rs_ref, bias_ref):
        ha, hb = ha_ref[...], hb_ref[...]
        ka = _dot(fwd, ha.astype(BF16))
        kb = _dot(fwd, hb.astype(BF16))
        rs = rs_ref[...]
        kr = rs * (ka[:n] + kb[:n] - hb[0:1, :])
        ki = rs * (ka[n:] - kb[n:])
        uf = _dot(fwd, u.astype(BF16))
        ur, ui = uf[:n], uf[n:]
        y = jnp.concatenate([ur * kr - ui * ki, ur * ki + ui * kr], axis=0)
        return _dot(inv, y.astype(BF16)) + u * bias_ref[...]

    v = v_ref[...]
    z = x1_ref[...] * conv(v, h_ref0a, h_ref0b, rs0_ref, b0_ref)
    o_ref[...] = (x2_ref[...] * conv(z, h_ref1a, h_ref1b, rs1_ref, b1_ref)).astype(o_ref.dtype)


def hyena_small(u_arr, row0_blocks, n_batch, L, h_raw, rs, bias):
    cb = LANES
    nct = HY_WIDTH // cb
    fwd, inv = _dense_dft_tables(L)
    uspec = lambda col: pl.BlockSpec((L, cb), lambda b, c: (row0_blocks + b, col * nct + c))
    hspec = lambda col: pl.BlockSpec((L, cb), lambda b, c: (0, col * nct + c))
    rspec = lambda col: pl.BlockSpec((1, cb), lambda b, c: (0, col * nct + c))
    bspec = pl.BlockSpec((1, cb), lambda b, c: (0, c))
    bias_0 = bias[0].reshape(1, HY_WIDTH)
    bias_1 = bias[1].reshape(1, HY_WIDTH)
    return pl.pallas_call(
        functools.partial(_hyena_small_kernel, L=L),
        grid=(n_batch, nct),
        in_specs=[uspec(0), uspec(1), uspec(2), hspec(0), hspec(1), hspec(2), hspec(3), rspec(0), rspec(2),
                  bspec, bspec,
                  pl.BlockSpec(fwd.shape, lambda b, c: (0, 0)), pl.BlockSpec(inv.shape, lambda b, c: (0, 0))],
        out_specs=pl.BlockSpec((L, cb), lambda b, c: (b, c)),
        out_shape=jax.ShapeDtypeStruct((n_batch * L, HY_WIDTH), BF16),
        compiler_params=_cparams(("arbitrary", "arbitrary")),
        name="hyena_ctx",
    )(u_arr, u_arr, u_arr, h_raw, h_raw, h_raw, h_raw, rs, rs, bias_0, bias_1, fwd, inv)


def _na_bias_table(rpb):
    cols = np.arange(GRID_W)
    col_start = np.clip(cols - NA_COLS // 2, 0, GRID_W - NA_COLS)[:, None]
    in_win = (cols[None, :] >= col_start) & (cols[None, :] < col_start + NA_COLS)
    rel_col = np.clip(cols[None, :] - cols[:, None], 1 - NA_COLS, NA_COLS - 1) + NA_COLS - 1
    tbl = rpb.astype(F32)[:, :, rel_col]
    tbl = jnp.where(jnp.asarray(in_win)[None, None], tbl, -jnp.inf)
    return jnp.concatenate([tbl[:, :-1], tbl[:, 1:]], axis=-1)


def _na_kernel(*refs, n_rows):
    q_ref = refs[0]
    k_refs = refs[1:1 + NA_ROWS]
    v_refs = refs[1 + NA_ROWS:1 + 2 * NA_ROWS]
    kc_ref, vc_ref, tbl_ref, o_ref = refs[1 + 2 * NA_ROWS:]
    r = pl.program_id(1)
    start = jnp.clip(r - NA_ROWS // 2, 0, n_rows - NA_ROWS)
    d0 = start - r + NA_ROWS - 1
    dh = NA_HEAD_DIM
    q = q_ref[...] * (dh ** -0.5)
    n_pairs = NA_ROWS // 2
    heads = [slice(h * dh, (h + 1) * dh) for h in range(NA_HEADS)]
    scores = []
    for h, hs in enumerate(heads):
        qh = q[:, hs]
        tiles = [_dot_nt(qh, jnp.concatenate([k_refs[2 * p][:, hs], k_refs[2 * p + 1][:, hs]], axis=0))
                 + tbl_ref[h, d0 + 2 * p] for p in range(n_pairs)]
        scores.append(tiles + [_dot_nt(qh, kc_ref[:, hs])])
    probs, denoms = [], []
    for tiles in scores:
        m = tiles[0].max(axis=-1, keepdims=True)
        for s in tiles[1:]:
            m = jnp.maximum(m, s.max(axis=-1, keepdims=True))
        ps = [jnp.exp(s - m) for s in tiles]
        l = ps[0].sum(axis=-1, keepdims=True)
        for p_ in ps[1:]:
            l = l + p_.sum(axis=-1, keepdims=True)
        probs.append([p_.astype(BF16) for p_ in ps])
        denoms.append(l)
    for hs, ps, l in zip(heads, probs, denoms):
        acc = _dot(ps[-1], vc_ref[:, hs])
        for p in range(n_pairs):
            acc = acc + _dot(ps[p], jnp.concatenate([v_refs[2 * p][:, hs], v_refs[2 * p + 1][:, hs]], axis=0))
        o_ref[:, hs] = (acc / l).astype(o_ref.dtype)


def na_latent(na, rpb, n_batch, L, Lc):
    n_rows = L // GRID_W
    assert n_rows >= NA_ROWS
    tbl = _na_bias_table(rpb)
    w = NA_WIDTH
    ctx_blk0 = n_batch * L // Lc

    def kv_spec(i, col):
        def imap(b, r):
            start = jnp.clip(r - NA_ROWS // 2, 0, n_rows - NA_ROWS)
            return (b * n_rows + start + i, col)
        return pl.BlockSpec((GRID_W, w), imap)

    in_specs = ([pl.BlockSpec((GRID_W, w), lambda b, r: (b * n_rows + r, 0))]
                + [kv_spec(i, 1) for i in range(NA_ROWS)] + [kv_spec(i, 2) for i in range(NA_ROWS)]
                + [pl.BlockSpec((Lc, w), lambda b, r: (ctx_blk0 + b, 1)),
                   pl.BlockSpec((Lc, w), lambda b, r: (ctx_blk0 + b, 2)),
                   pl.BlockSpec(tbl.shape, lambda b, r: (0, 0, 0, 0))])
    return pl.pallas_call(
        functools.partial(_na_kernel, n_rows=n_rows),
        grid=(n_batch, n_rows),
        in_specs=in_specs,
        out_specs=pl.BlockSpec((GRID_W, w), lambda b, r: (b * n_rows + r, 0)),
        out_shape=jax.ShapeDtypeStruct((n_batch * L, w), BF16),
        compiler_params=_cparams(("arbitrary", "arbitrary")),
        name="na_latent",
    )(*([na] * (3 + 2 * NA_ROWS)), tbl)


def _ctx_attn_kernel(q_ref, k_ref, v_ref, o_ref):
    dh = NA_HEAD_DIM
    q = q_ref[...] * (dh ** -0.5)
    for h in range(NA_HEADS):
        hs = slice(h * dh, (h + 1) * dh)
        s = _dot_nt(q[:, hs], k_ref[:, hs])
        p_ = jnp.exp(s - s.max(axis=-1, keepdims=True))
        acc = _dot(p_.astype(BF16), v_ref[:, hs])
        o_ref[:, hs] = (acc / p_.sum(axis=-1, keepdims=True)).astype(o_ref.dtype)


def ctx_attn(na, n_batch, L, Lc):
    w = NA_WIDTH
    blk0 = n_batch * L // Lc
    spec = lambda col: pl.BlockSpec((Lc, w), lambda b: (blk0 + b, col))
    return pl.pallas_call(
        _ctx_attn_kernel,
        grid=(n_batch,),
        in_specs=[spec(0), spec(1), spec(2)],
        out_specs=pl.BlockSpec((Lc, w), lambda b: (b, 0)),
        out_shape=jax.ShapeDtypeStruct((n_batch * Lc, w), BF16),
        compiler_params=_cparams(("arbitrary",)),
        name="ctx_attn",
    )(na, na, na)


@functools.lru_cache(maxsize=None)
def _wkv_masks():
    c, g = WKV_CHUNK, WKV_GROUP
    t = np.arange(c)[:, None]
    s = np.arange(c)[None, :]
    tinc = np.stack([(s <= t), (s >= t)]).astype(np.float32)
    strict = np.stack([(s < t), (s > t)]).astype(np.float32)
    tile = lambda m: np.tile(m, (1,) * (m.ndim - 1) + (g,))
    blk = lambda n: (t // n == s // n)
    blk16 = tile(blk(16).astype(np.float32))
    off32 = tile((blk(32) & ~blk(16)).astype(np.float32))
    off64 = tile((~blk(32)).astype(np.float32))
    eye = tile((t == s).astype(np.float32))
    rr = np.arange(g * c)
    hm = (rr[:, None] // c == np.arange(g * RW_HEAD_DIM)[None, :] // RW_HEAD_DIM).astype(np.float32)
    masks = tuple(jnp.asarray(m) for m in (tinc, tile(strict), tile(tinc), blk16, off32, off64, eye, hm))
    return masks + (jnp.asarray(hm, BF16),)


def _wkv_kernel(*refs):
    (r0_ref, v0_ref, kk0_ref, r1_ref, v1_ref, kk1_ref, lw0_ref, av0_ref, kd0_ref, lw1_ref, av1_ref, kd1_ref,
     tinc_ref, strict_ref, incl_ref, blk16_ref, off32_ref, off64_ref, eye_ref, hm_ref, hmb_ref,
     y0_ref, y1_ref, state_ref) = refs
    c, g = WKV_CHUNK, WKV_GROUP
    gw = g * RW_HEAD_DIM

    @pl.when(pl.program_id(1) == 0)
    def _():
        state_ref[...] = jnp.zeros_like(state_ref)

    hm = hm_ref[...]
    hm_bf = hmb_ref[...]
    blk16, off32, off64, eye = blk16_ref[...], off32_ref[...], off64_ref[...], eye_ref[...]

    def bdiag(z):
        return jnp.concatenate([z.astype(BF16)] * g, axis=0) * hm_bf

    def pm(x4, zd):
        return _dot(x4.astype(BF16), zd)

    def prepare(d, r_ref, kk_ref, lw_ref, av_ref, kd_ref):
        lw = lw_ref[0]
        cum = _dot(tinc_ref[d], lw, HI)
        tot = jnp.sum(lw, axis=0, keepdims=True)
        e_neg = jnp.exp(-cum)
        e_rem = jnp.exp(tot - cum)
        kk = kk_ref[...]
        b_vec = kk * av_ref[0]
        kd = kd_ref[0]
        return dict(at=-kk * jnp.exp(cum - lw), rt=r_ref[...] * jnp.exp(cum), bt=b_vec * e_neg, kt=kd * e_neg,
                    bp=b_vec * e_rem, kp=kd * e_rem, e_tot=jnp.exp(tot))

    qs = (prepare(0, r0_ref, kk0_ref, lw0_ref, av0_ref, kd0_ref), prepare(1, r1_ref, kk1_ref, lw1_ref, av1_ref, kd1_ref))
    v_refs, y_refs = (v0_ref, v1_ref), (y0_ref, y1_ref)
    chains = [(d, gi) for gi in range(RW_HEADS // g) for d in range(2)]
    sl = lambda gi: slice(gi * gw, (gi + 1) * gw)
    each = lambda f, *lists: [f(*args) for args in zip(*lists)]
    pm_all = lambda xs, zs: each(lambda x, z: pm(x, bdiag(z)), xs, zs)

    ar = [jnp.concatenate([qs[d]["at"][:, sl(gi)], qs[d]["rt"][:, sl(gi)]], axis=0).astype(BF16) for d, gi in chains]
    pb = [_dot_nt(a, bdiag(qs[d]["bt"][:, sl(gi)])) for a, (d, gi) in zip(ar, chains)]
    pk = [_dot_nt(a, bdiag(qs[d]["kt"][:, sl(gi)])) for a, (d, gi) in zip(ar, chains)]
    a_ab = [p[:c] * strict_ref[d] for p, (d, gi) in zip(pb, chains)]
    a_rb = [p[c:] * incl_ref[d] for p, (d, gi) in zip(pb, chains)]
    a_ak = [p[:c] * strict_ref[d] for p, (d, gi) in zip(pk, chains)]
    a_rk = [p[c:] * incl_ref[d] for p, (d, gi) in zip(pk, chains)]
    ad = [a * blk16 for a in a_ab]
    a2 = pm_all(ad, ad)
    a4 = pm_all(a2, a2)
    a8 = pm_all(a4, a4)
    tinv = [eye + a for a in ad]
    for powr in (a2, a4, a8):
        tinv = each(lambda t, p_: t + p_, tinv, pm_all(tinv, powr))
    for off in (off32, off64):
        mid = pm_all(tinv, [a * off for a in a_ab])
        tinv = each(lambda t, p_: t + p_, tinv, pm_all(mid, tinv))
    s0 = [state_ref[d, gi] for d, gi in chains]
    vv = [v_refs[d][:, sl(gi)] for d, gi in chains]
    vd = [bdiag(v_) for v_ in vv]
    ars = each(lambda a, s_: _dot_nt(a, s_.astype(BF16)), ar, s0)
    akv = each(pm, a_ak, vd)
    u = pm_all(tinv, each(lambda x, y_: x[:c] + y_, ars, akv))
    yu = pm_all(a_rb, u)
    yv = each(pm, a_rk, vd)
    for (d, gi), x, y1_, y2_ in zip(chains, ars, yu, yv):
        y_refs[d][:, sl(gi)] = x[c:] + y1_ + y2_
    upd = [_dot_tn(jnp.concatenate([u_, v_], axis=0).astype(BF16),
                   jnp.concatenate([qs[d]["bp"][:, sl(gi)], qs[d]["kp"][:, sl(gi)]], axis=0).astype(BF16))
           for u_, v_, (d, gi) in zip(u, vv, chains)]
    for (d, gi), s_, up in zip(chains, s0, upd):
        state_ref[d, gi] = s_ * qs[d]["e_tot"][:, sl(gi)] + hm * up


def wkv_scan(r, v, kk, lw, av, kd, n_batch, L, Lc):
    c = WKV_CHUNK
    rows, w = r.shape
    nc, nl = Lc // c, L // c
    masks = _wkv_masks()

    def blk(d, b, s):
        j_ctx = s if d == 0 else nc - 1 - s
        j_lat = s - nc if d == 0 else nl - 1 - (s - nc)
        return jnp.where(s < nc, (n_batch * L + b * Lc) // c + j_ctx, (b * L) // c + j_lat)

    shared = lambda d: pl.BlockSpec((c, w), lambda b, s: (blk(d, b, s), 0))
    perdir = lambda d: pl.BlockSpec((1, c, w), lambda b, s: (d, blk(d, b, s), 0))
    full = lambda m: pl.BlockSpec(m.shape, lambda b, s: (0,) * m.ndim)
    gw = WKV_GROUP * RW_HEAD_DIM
    return pl.pallas_call(
        _wkv_kernel,
        grid=(n_batch, nc + nl),
        in_specs=[shared(0)] * 3 + [shared(1)] * 3 + [perdir(0)] * 3 + [perdir(1)] * 3 + [full(m) for m in masks],
        out_specs=[shared(0), shared(1)],
        out_shape=[jax.ShapeDtypeStruct((rows, w), F32)] * 2,
        scratch_shapes=[pltpu.VMEM((2, RW_HEADS // WKV_GROUP, gw, gw), F32)],
        compiler_params=_cparams(("arbitrary", "arbitrary")),
        name="wkv_scan",
    )(r, v, kk, r, v, kk, lw, av, kd, lw, av, kd, *masks)


def _pack_bf16_pairs(h):
    half = h.shape[1] // 2
    bits = lambda t: lax.bitcast_convert_type(t.astype(BF16).astype(F32), jnp.uint32)
    return (bits(h[:, :half]) >> 16) | (bits(h[:, half:]) & jnp.uint32(0xFFFF0000))


def _unpack_bf16_pairs(p):
    lo = lax.bitcast_convert_type(p << 16, F32).astype(BF16)
    hi = lax.bitcast_convert_type(p & jnp.uint32(0xFFFF0000), F32).astype(BF16)
    return jnp.concatenate([lo, hi], axis=1)


def _moe_kernel(be_ref, nb_ref, x_ref, wg_ref, wu_ref, wd_ref, o_ref, wg_s, wu_s, wd_s):
    i = pl.program_id(0)
    prev = be_ref[jnp.maximum(i - 1, 0)]

    @pl.when((i == 0) | (be_ref[i] != prev))
    def _():
        wg_s[...] = wg_ref[0, 0].astype(BF16)
        wu_s[...] = wu_ref[0, 0].astype(BF16)
        wd_s[...] = wd_ref[0, 0].astype(BF16)

    @pl.when(i < nb_ref[0])
    def _():
        o_ref[...] = _swiglu_block(x_ref, wg_s, wu_s, wd_s).astype(o_ref.dtype)

    @pl.when(i >= nb_ref[0])
    def _():
        o_ref[...] = jnp.zeros_like(o_ref)


def _swiglu_block(x_tiles, wg_s, wu_s, wd_s):
    x = _unpack_bf16_pairs(jnp.concatenate([x_tiles[:, s, :] for s in range(x_tiles.shape[1])], axis=1))
    hmid = (jax.nn.silu(_dot(x, wg_s[...])) * _dot(x, wu_s[...])).astype(BF16)
    return _dot(hmid, wd_s[...])


def _moe_gather_kernel(be_ref, nb_ref, cur_ref, nxt_ref, tab_ref, wg_ref, wu_ref, wd_ref, o_ref,
                       wg_s, wu_s, wd_s, xbuf, sems):
    i = pl.program_id(0)
    n_used = nb_ref[0]
    rows = xbuf.shape[1]
    batch = 8

    def start_gather(idx_ref, half):
        def issue(j, carry):
            for u in range(batch):
                r = j * batch + u
                pltpu.async_copy(tab_ref.at[pl.ds(idx_ref[0, 0, r], 1)], xbuf.at[half, pl.ds(r, 1)], sems.at[half])
            return carry

        lax.fori_loop(0, rows // batch, issue, 0)

    @pl.when((i == 0) & (n_used > 0))
    def _():
        start_gather(cur_ref, 0)

    @pl.when(i + 1 < n_used)
    def _():
        start_gather(nxt_ref, (i + 1) % 2)

    prev = be_ref[jnp.maximum(i - 1, 0)]

    @pl.when((i == 0) | (be_ref[i] != prev))
    def _():
        wg_s[...] = wg_ref[0, 0].astype(BF16)
        wu_s[...] = wu_ref[0, 0].astype(BF16)
        wd_s[...] = wd_ref[0, 0].astype(BF16)

    @pl.when(i < n_used)
    def _():
        half = i % 2
        pltpu.make_async_copy(tab_ref.at[pl.ds(0, rows)], xbuf.at[half], sems.at[half]).wait()
        o_ref[...] = _swiglu_block(xbuf.at[half], wg_s, wu_s, wd_s).astype(o_ref.dtype)

    @pl.when(i >= n_used)
    def _():
        o_ref[...] = jnp.zeros_like(o_ref)


def gathered_swiglu(table, slot_tok, block_e, n_used, w_gate, w_up, w_down, layer):
    d, ff = w_gate.shape[-2:]
    nb = slot_tok.shape[0] // MOE_BLOCK
    assert table.shape[0] >= MOE_BLOCK
    idx = slot_tok.reshape(nb, 1, MOE_BLOCK)
    grid_spec = pltpu.PrefetchScalarGridSpec(
        num_scalar_prefetch=2,
        grid=(nb,),
        in_specs=[pl.BlockSpec((1, 1, MOE_BLOCK), lambda i, be, nu: (i, 0, 0), memory_space=pltpu.SMEM),
                  pl.BlockSpec((1, 1, MOE_BLOCK), lambda i, be, nu: (jnp.minimum(i + 1, nb - 1), 0, 0),
                               memory_space=pltpu.SMEM),
                  pl.BlockSpec(memory_space=pl.ANY),
                  pl.BlockSpec((1, 1, d, ff), lambda i, be, nu: (layer, be[i], 0, 0)),
                  pl.BlockSpec((1, 1, d, ff), lambda i, be, nu: (layer, be[i], 0, 0)),
                  pl.BlockSpec((1, 1, ff, d), lambda i, be, nu: (layer, be[i], 0, 0))],
        out_specs=pl.BlockSpec((MOE_BLOCK, d), lambda i, be, nu: (i, 0)),
        scratch_shapes=[pltpu.VMEM((d, ff), BF16), pltpu.VMEM((d, ff), BF16), pltpu.VMEM((ff, d), BF16),
                        pltpu.VMEM((2, MOE_BLOCK) + table.shape[1:], table.dtype), pltpu.SemaphoreType.DMA((2,))],
    )
    return pl.pallas_call(
        _moe_gather_kernel,
        grid_spec=grid_spec,
        out_shape=jax.ShapeDtypeStruct((nb * MOE_BLOCK, d), BF16),
        compiler_params=pltpu.CompilerParams(dimension_semantics=("arbitrary",), vmem_limit_bytes=VMEM_LIMIT,
                                             disable_bounds_checks=True),
        name="gathered_swiglu",
    )(block_e, n_used, idx, idx, table, w_gate, w_up, w_down)


def grouped_swiglu(x, block_e, n_used, w_gate, w_up, w_down, layer):
    d, ff = w_gate.shape[-2:]
    nb = x.shape[0] // MOE_BLOCK
    grid_spec = pltpu.PrefetchScalarGridSpec(
        num_scalar_prefetch=2,
        grid=(nb,),
        in_specs=[pl.BlockSpec((MOE_BLOCK,) + x.shape[1:], lambda i, be, nu: (i, 0, 0)),
                  pl.BlockSpec((1, 1, d, ff), lambda i, be, nu: (layer, be[i], 0, 0)),
                  pl.BlockSpec((1, 1, d, ff), lambda i, be, nu: (layer, be[i], 0, 0)),
                  pl.BlockSpec((1, 1, ff, d), lambda i, be, nu: (layer, be[i], 0, 0))],
        out_specs=pl.BlockSpec((MOE_BLOCK, d), lambda i, be, nu: (i, 0)),
        scratch_shapes=[pltpu.VMEM((d, ff), BF16), pltpu.VMEM((d, ff), BF16), pltpu.VMEM((ff, d), BF16)],
    )
    return pl.pallas_call(
        _moe_kernel,
        grid_spec=grid_spec,
        out_shape=jax.ShapeDtypeStruct((nb * MOE_BLOCK, d), BF16),
        compiler_params=_cparams(("arbitrary",)),
        name="grouped_swiglu",
    )(block_e, n_used, x, w_gate, w_up, w_down)


def _combine_kernel(grp_ref, x_ref, y_ref, w_ref, s_ref, gate_ref, o_ref):
    del grp_ref
    f = s_ref[...].astype(F32)
    for k in range(TOP_K):
        f = f + w_ref[:, k:k + 1] * y_ref[k].astype(F32)
    o_ref[...] = x_ref[...] + gate_ref[0] * f


def moe_combine(x, yg, e_w, shared, gate3, grp):
    m, d = x.shape
    tm = TM // 2
    grid_spec = pltpu.PrefetchScalarGridSpec(
        num_scalar_prefetch=1,
        grid=(m // tm,),
        in_specs=[pl.BlockSpec((tm, d), lambda i, grp: (i, 0)),
                  pl.BlockSpec((TOP_K, tm, d), lambda i, grp: (0, i, 0)),
                  pl.BlockSpec((tm, e_w.shape[1]), lambda i, grp: (i, 0)),
                  pl.BlockSpec((tm, d), lambda i, grp: (i, 0)),
                  pl.BlockSpec((1, 1, d), lambda i, grp: (grp[i // 2], 0, 0))],
        out_specs=pl.BlockSpec((tm, d), lambda i, grp: (i, 0)),
    )
    return pl.pallas_call(
        _combine_kernel,
        grid_spec=grid_spec,
        out_shape=jax.ShapeDtypeStruct((m, d), F32),
        compiler_params=_cparams(("arbitrary",)),
        name="moe_combine",
    )(grp, x, yg, e_w, shared, gate3)


def _route_kernel(grp_ref, x_ref, g_ref, sh_ref, sc_ref, wt_ref, rb_ref, tri_ref, ones_ref,
                  h_ref, idx_ref, w_ref, rank_ref, cnt_ref, carry_ref):
    del grp_ref
    tm = x_ref.shape[0]
    gs = N_EXPERTS // N_GROUPS
    neg = -jnp.inf

    @pl.when(pl.program_id(0) == 0)
    def _():
        carry_ref[...] = jnp.zeros_like(carry_ref)

    x = x_ref[...]
    y = x * lax.rsqrt(jnp.mean(x * x, axis=-1, keepdims=True) + NORM_EPS)
    h = (y * g_ref[...]) * (1.0 + sc_ref[0]) + sh_ref[0]
    h_ref[...] = _pack_bf16_pairs(h)
    scores = jax.nn.sigmoid(_dot_nt(wt_ref[...], h, HI))
    biased = scores + rb_ref[...]

    def first_argmax(v, iota, n):
        m = jnp.max(v, axis=0, keepdims=True)
        return m, jnp.min(jnp.where(v == m, iota, float(n)), axis=0, keepdims=True)

    def stack_rows(rows):
        iota8 = lax.broadcasted_iota(jnp.int32, (8, tm), 0)
        out = jnp.zeros((8, tm), F32)
        for k, row in enumerate(rows):
            out = jnp.where(iota8 == k, row, out)
        return out

    assert gs == 8 and N_GROUPS == 8
    iota_g = lax.broadcasted_iota(jnp.int32, (gs, tm), 0).astype(F32)
    g_rows = []
    for g in range(N_GROUPS):
        bg = biased[g * gs:(g + 1) * gs]
        m1, i1 = first_argmax(bg, iota_g, gs)
        m2 = jnp.max(jnp.where(iota_g == i1, neg, bg), axis=0, keepdims=True)
        g_rows.append(m1 + m2)
    g_score = stack_rows(g_rows)
    g_sel = jnp.zeros((N_GROUPS, tm), F32)
    for _ in range(TOPK_GROUPS):
        _, ig = first_argmax(g_score, iota_g, N_GROUPS)
        hit = iota_g == ig
        g_sel = jnp.where(hit, 1.0, g_sel)
        g_score = jnp.where(hit, neg, g_score)
    e_sel = jnp.concatenate([jnp.broadcast_to(g_sel[g:g + 1], (gs, tm)) for g in range(N_GROUPS)], axis=0)
    masked = jnp.where(e_sel > 0.0, biased, neg)

    iota_e = lax.broadcasted_iota(jnp.int32, (N_EXPERTS, tm), 0).astype(F32)
    chosen = jnp.zeros((N_EXPERTS, tm), F32)
    hits, idx_rows, w_rows = [], [], []
    for _ in range(TOP_K):
        _, ie = first_argmax(masked, iota_e, N_EXPERTS)
        hit = iota_e == ie
        hits.append(hit)
        idx_rows.append(ie)
        w_rows.append(jnp.sum(jnp.where(hit, scores, 0.0), axis=0, keepdims=True))
        chosen = jnp.where(hit, 1.0, chosen)
        masked = jnp.where(hit, neg, masked)
    w_sum = w_rows[0]
    for wk in w_rows[1:]:
        w_sum = w_sum + wk
    idx_ref[...] = stack_rows(idx_rows).astype(jnp.int32)
    w_ref[...] = stack_rows([wk / w_sum * ROUTE_SCALE for wk in w_rows])

    chosen_b = chosen.astype(BF16)
    before = carry_ref[...] + _dot(chosen_b, tri_ref[...])
    rank_ref[...] = stack_rows([jnp.sum(jnp.where(hit, before, 0.0), axis=0, keepdims=True)
                                for hit in hits]).astype(jnp.int32)
    carry_ref[...] += _dot(chosen_b, ones_ref[...])
    cnt_ref[...] = carry_ref[:, :LANES].astype(jnp.int32)


def route(x, g, shift3, scale3, grp, router_w, router_b):
    m, d = x.shape
    ne = N_EXPERTS
    tri = jnp.asarray(np.triu(np.ones((TM, TM), np.float32), 1), BF16)
    ones = jnp.ones((TM, TM), BF16)
    rb = jnp.broadcast_to(router_b.astype(F32)[:, None], (ne, TM))
    row = lambda r: pl.BlockSpec((r, TM), lambda i, grp: (0, i))
    const = lambda shape: pl.BlockSpec(shape, lambda i, grp: (0, 0))
    grid_spec = pltpu.PrefetchScalarGridSpec(
        num_scalar_prefetch=1,
        grid=(m // TM,),
        in_specs=[pl.BlockSpec((TM, d), lambda i, grp: (i, 0)),
                  const((1, d)),
                  pl.BlockSpec((1, 1, d), lambda i, grp: (grp[i], 0, 0)),
                  pl.BlockSpec((1, 1, d), lambda i, grp: (grp[i], 0, 0)),
                  const((ne, d)), const((ne, TM)), const((TM, TM)), const((TM, TM))],
        out_specs=[pl.BlockSpec((TM, d // 2), lambda i, grp: (i, 0)), row(8), row(8), row(8), const((ne, LANES))],
        scratch_shapes=[pltpu.VMEM((ne, TM), F32)],
    )
    return pl.pallas_call(
        _route_kernel,
        grid_spec=grid_spec,
        out_shape=[jax.ShapeDtypeStruct((m, d // 2), jnp.uint32), jax.ShapeDtypeStruct((8, m), jnp.int32),
                   jax.ShapeDtypeStruct((8, m), F32), jax.ShapeDtypeStruct((8, m), jnp.int32),
                   jax.ShapeDtypeStruct((ne, LANES), jnp.int32)],
        compiler_params=_cparams(("arbitrary",)),
        name="route",
    )(grp, x, g.reshape(1, d), shift3, scale3, router_w.T, rb, tri, ones)


def _seq_edge_masks(n_batch, L, Lc):
    n_lat = n_batch * L
    starts = np.concatenate([np.arange(n_batch) * L, n_lat + np.arange(n_batch) * Lc])
    ends = np.concatenate([(np.arange(n_batch) + 1) * L, n_lat + (np.arange(n_batch) + 1) * Lc]) - 1
    first = np.ones((n_batch * (L + Lc), 1), np.float32)
    last = first.copy()
    first[starts] = 0.0
    last[ends] = 0.0
    return jnp.asarray(first), jnp.asarray(last)


RW_TM = 256


@functools.lru_cache(maxsize=None)
def _head_ones():
    h = np.arange(RW_WIDTH) // RW_HEAD_DIM
    return jnp.asarray(h[:, None] == h[None, :], BF16)


def _head_sum(x, ones):
    return _dot(x.astype(BF16), ones)


def _rwkv_prep_kernel(u_ref, w0_ref, w2_ref, a0_ref, a2_ref, g2_ref, kk_w_ref, ka_ref, ones_ref,
                      r_ref, v_ref, kk_ref, g_ref, lw_ref, av_ref, kd_ref):
    w = RW_WIDTH
    lo = 3 * w
    r_ref[...] = u_ref[:, 0:w]
    k = u_ref[:, w:2 * w]
    v_ref[...] = u_ref[:, 2 * w:lo]
    wl = jnp.tanh(u_ref[:, lo:lo + 2 * RW_DECAY_LORA]).astype(BF16)
    al = u_ref[:, lo + 2 * RW_DECAY_LORA:lo + 2 * RW_DECAY_LORA + 2 * RW_AAA_LORA].astype(BF16)
    gl = jax.nn.sigmoid(u_ref[:, lo + 2 * RW_DECAY_LORA + 2 * RW_AAA_LORA:]).astype(BF16)
    for d in range(2):
        z = -(w0_ref[d:d + 1, :] + _dot(wl, w2_ref[d]))
        softplus = jnp.maximum(z, 0.0) + jnp.log(1.0 + jnp.exp(-jnp.abs(z)))
        lw_ref[d] = -jnp.exp(-softplus - 0.5)
        a = jax.nn.sigmoid(a0_ref[d:d + 1, :] + _dot(al, a2_ref[d]))
        av_ref[d] = a
        kd_ref[d] = k * (1.0 + (a - 1.0) * ka_ref[...])
    g_ref[...] = _dot(gl, g2_ref[...])
    kk = k * kk_w_ref[...]
    norm = jnp.sqrt(_head_sum(kk * kk, ones_ref[...]))
    kk_ref[...] = kk / jnp.maximum(norm, 1e-12)


def rwkv_prep(u, w0, w2, a0, a2, g2, k_k, k_a):
    m = u.shape[0]
    w = RW_WIDTH
    zeros = jnp.zeros((RW_DECAY_LORA, w), F32)
    pad2 = lambda t: jnp.stack([jnp.concatenate([t[0], zeros], axis=0),
                                jnp.concatenate([zeros, t[1]], axis=0)]).astype(BF16)
    assert RW_DECAY_LORA == RW_AAA_LORA and 2 * RW_DECAY_LORA == LANES and RW_GATE_LORA == LANES
    full = lambda shape: pl.BlockSpec(shape, lambda i: (0,) * len(shape))
    row = pl.BlockSpec((RW_TM, w), lambda i: (i, 0))
    row2 = pl.BlockSpec((2, RW_TM, w), lambda i: (0, i, 0))
    one = jax.ShapeDtypeStruct((m, w), F32)
    two = jax.ShapeDtypeStruct((2, m, w), F32)
    return pl.pallas_call(
        _rwkv_prep_kernel,
        grid=(m // RW_TM,),
        in_specs=[pl.BlockSpec((RW_TM, RW_COLS), lambda i: (i, 0)), full((2, w)), full((2, LANES, w)), full((2, w)),
                  full((2, LANES, w)), full((LANES, w)), full((1, w)), full((1, w)), full((w, w))],
        out_specs=[row, row, row, row, row2, row2, row2],
        out_shape=[one, one, one, one, two, two, two],
        compiler_params=_cparams(("arbitrary",)),
        name="rwkv_prep",
    )(u, w0, pad2(w2), a0, pad2(a2), g2.astype(BF16), k_k.reshape(1, w), k_a.reshape(1, w), _head_ones())


def _rwkv_out_kernel(y0_ref, y1_ref, r_ref, v_ref, g_ref, kd_ref, rk_ref, lnw_ref, lnb_ref, ones_ref, o_ref):
    ones = ones_ref[...]
    inv_n = 1.0 / RW_HEAD_DIM
    y = y0_ref[...] + y1_ref[...]
    yc = y - _head_sum(y, ones) * inv_n
    var = _head_sum(yc * yc, ones) * inv_n
    yn = yc * lax.rsqrt(var + RW_GN_EPS)
    bonus = _head_sum(r_ref[...] * (kd_ref[0] + kd_ref[1]) * rk_ref[...], ones) * v_ref[...]
    o_ref[...] = ((yn * lnw_ref[...] + lnb_ref[...] + bonus) * g_ref[...]).astype(o_ref.dtype)


def rwkv_out(y0, y1, r, v, g, kd, r_k, ln_w, ln_b):
    m, w = r.shape
    full = lambda shape: pl.BlockSpec(shape, lambda i: (0,) * len(shape))
    row = pl.BlockSpec((RW_TM, w), lambda i: (i, 0))
    vec = lambda t: t.reshape(1, w)
    return pl.pallas_call(
        _rwkv_out_kernel,
        grid=(m // RW_TM,),
        in_specs=[row, row, row, row, row, pl.BlockSpec((2, RW_TM, w), lambda i: (0, i, 0)),
                  full((1, w)), full((1, w)), full((1, w)), full((w, w))],
        out_specs=row,
        out_shape=jax.ShapeDtypeStruct((m, w), BF16),
        compiler_params=_cparams(("arbitrary",)),
        name="rwkv_out",
    )(y0, y1, r, v, g, kd, vec(r_k), vec(ln_w), vec(ln_b), _head_ones())


def _moe(h, e_idx, rank, counts, exp_gate, exp_up, exp_down, sh_gate, sh_up, sh_down, layer):
    T = h.shape[0]
    D = exp_gate.shape[-2]
    n = T * TOP_K
    padded = (counts + MOE_BLOCK - 1) // MOE_BLOCK * MOE_BLOCK
    pad_end = jnp.cumsum(padded)
    pad_start = pad_end - padded
    experts = jnp.arange(N_EXPERTS, dtype=jnp.int32)
    dest = rank + jnp.sum(jnp.where(e_idx[:, :, None] == experts, pad_start.astype(jnp.int32), 0), axis=-1)
    n_blocks = -(-n // MOE_BLOCK) + N_EXPERTS
    n_slots = n_blocks * MOE_BLOCK
    flat_dest = dest.reshape(-1)
    tok = jnp.tile(jnp.arange(T, dtype=jnp.int32), TOP_K)
    slot_tok = jnp.zeros((n_slots,), jnp.int32).at[flat_dest].set(tok, unique_indices=True)
    block_start = jnp.arange(n_blocks, dtype=jnp.int32) * MOE_BLOCK
    block_e = jnp.minimum(jnp.sum(block_start[:, None] >= pad_end[None, :], axis=1), N_EXPERTS - 1).astype(jnp.int32)
    n_used = (pad_end[-1] // MOE_BLOCK).astype(jnp.int32).reshape(1)
    h = h.reshape(T, -1, LANES)
    y = gathered_swiglu(h, slot_tok, block_e, n_used, exp_gate, exp_up, exp_down, layer)
    yg = jnp.take(y, flat_dest, axis=0, mode="clip").reshape(TOP_K, T, D)
    nb_sh = T // MOE_BLOCK
    sh4 = lambda w: w.reshape((w.shape[0], 1) + w.shape[1:])
    shared = grouped_swiglu(h, jnp.zeros((nb_sh,), jnp.int32), jnp.full((1,), nb_sh, jnp.int32),
                            sh4(sh_gate), sh4(sh_up), sh4(sh_down), layer)
    return yg, shared


def kernel(x, c, ctx, c_ctx, mod_w, mod_b, norm1_g, norm2_g, w_in, hy_conv, hy_w1, hy_b1, hy_w2, hy_b2, hy_freq,
           hy_w3, hy_bias, na_rpb, rw_shift, rw_w0, rw_w2, rw_a0, rw_a2, rw_g2, rw_kk, rw_ka, rw_rk, rw_ln_w,
           rw_ln_b, proj_a, proj_b, proj_c, w_out, router_w, router_b, exp_gate, exp_up, exp_down, sh_gate, sh_up,
           sh_down, final_g):
    B, L, D = x.shape
    Lc = ctx.shape[1]
    depth = mod_w.shape[0]
    n_lat, n_ctx = B * L, B * Lc
    assert L % TM == 0 and n_ctx % TM == 0 and L % WKV_CHUNK == 0 and Lc % WKV_CHUNK == 0
    col_hy = 3 * HY_WIDTH
    col_na = col_hy + 3 * NA_WIDTH
    col_rw = col_na + RW_COLS

    xs = jnp.concatenate([x.reshape(n_lat, D), ctx.reshape(n_ctx, D)], axis=0)
    grp_all = jnp.asarray(np.concatenate([np.repeat(np.arange(B), L // TM), np.full(n_ctx // TM, B)]), jnp.int32)
    s8 = jnp.zeros((8, D), F32).at[:B].set(jax.nn.silu(c)).at[B].set(jax.nn.silu(c_ctx))
    first, last = _seq_edge_masks(B, L, Lc)

    for i in range(depth):
        with_ctx = i < depth - 1
        mod = small_matmul_bias(s8, mod_w, mod_b, i)[:B + 1].reshape(B + 1, 1, N_MOD * D)
        sh1, sc1, g1, sh2, sc2, g2 = (mod[:, :, j * D:(j + 1) * D] for j in range(N_MOD))
        w_bf = w_in[i].astype(BF16)
        proj = functools.partial(normmod_matmul, xs, norm1_g[i], sh1, sc1, grp_all)
        hy = short_conv(proj(w_bf[:, :col_hy], HY_WIDTH, BF16), hy_conv[i], first, last, HY_WIDTH)
        na = proj(w_bf[:, col_hy:col_na], NA_WIDTH, BF16)
        rw = proj(w_bf[:, col_na:col_rw], RW_COLS // 3, F32)
        gates = proj(w_bf[:, col_rw:], D // 2, BF16)

        hy_args = (hy_w1[i], hy_b1[i], hy_w2[i], hy_b2[i], hy_freq[i], hy_w3[i])
        h_raw, ss = hyena_filters_raw(L, *hy_args)
        hr, hi = hyena_filter_spectrum(h_raw, _filter_scale(ss), L)
        z = hyena_conv(hy, 0, hy, 2, hr, hi, 0, hy_bias[i][0], B, L)
        o_a = hyena_conv(hy, 1, z, 0, hr, hi, 1, hy_bias[i][1], B, L).astype(BF16)
        o_b = na_latent(na, na_rpb[i], B, L, Lc)
        rw = short_conv(rw, rw_shift[i], first, last, RW_COLS // 3)
        r_, v_, kk_, gg_, lw_, av_, kd_ = rwkv_prep(rw, rw_w0[i], rw_w2[i], rw_a0[i], rw_a2[i], rw_g2[i], rw_kk[i],
                                                    rw_ka[i])
        y_f, y_b = wkv_scan(r_, v_, kk_, lw_, av_, kd_, B, L, Lc)
        o_c = rwkv_out(y_f, y_b, r_, v_, gg_, kd_, rw_rk[i], rw_ln_w[i], rw_ln_b[i])

        if with_ctx:
            h_raw_c, ss_c = hyena_filters_raw(Lc, *hy_args)
            o_a_c = hyena_small(hy, n_lat // Lc, B, Lc, h_raw_c, _filter_scale(ss_c), hy_bias[i])
            o_a = jnp.concatenate([o_a, o_a_c], axis=0)
            o_b = jnp.concatenate([o_b, ctx_attn(na, B, L, Lc)], axis=0)
            m_rows = n_lat + n_ctx
        else:
            m_rows = n_lat
        grp = grp_all[:m_rows // TM]
        merged = branch_merge(m_rows, o_a, o_b, o_c, gates, proj_a[i].astype(BF16), proj_b[i].astype(BF16),
                              proj_c[i].astype(BF16))
        xs = resid_matmul(merged, w_out[i].astype(BF16), xs, g1, grp)

        h2, e_idx, e_w, rank, counts = route(xs, norm2_g[i], sh2, sc2, grp, router_w[i], router_b[i])
        yg, shared = _moe(h2, e_idx[:TOP_K], rank[:TOP_K], counts[:, 0], exp_gate, exp_up, exp_down, sh_gate, sh_up,
                          sh_down, i)
        xs = moe_combine(xs, yg, e_w.T, shared, g2, grp)

    return rmsnorm_rows(xs, final_g).reshape(B, L, D)
```

```python
import functools
import math

import jax
import jax.numpy as jnp
import numpy as np
from jax import lax
from jax.experimental import pallas as pl
from jax.experimental.pallas import tpu as pltpu

F32 = jnp.float32
BF16 = jnp.bfloat16
HI = lax.Precision.HIGHEST

GRID_W = 64
NORM_EPS = 1e-6
N_MOD = 6
SHORT_CONV = 3
HY_WIDTH = 1024
HY_BANDS = 16
HY_EMB = 2 * HY_BANDS + 1
HY_FILTER_ORDER = 64
HY_FAST_DECAY = 0.3
HY_SLOW_DECAY = 1.5
HY_DECAY_TARGET = 1e-2
NA_HEADS = 16
NA_HEAD_DIM = 64
NA_WIDTH = NA_HEADS * NA_HEAD_DIM
NA_ROWS = 8
NA_COLS = 16
RW_HEADS = 16
RW_HEAD_DIM = 64
RW_WIDTH = RW_HEADS * RW_HEAD_DIM
RW_DECAY_LORA = 64
RW_AAA_LORA = 64
RW_GATE_LORA = 128
RW_GN_EPS = 64e-5
RW_COLS = 3 * RW_WIDTH + 2 * RW_DECAY_LORA + 2 * RW_AAA_LORA + RW_GATE_LORA
RW_SPLITS = [RW_WIDTH, 2 * RW_WIDTH, 3 * RW_WIDTH, 3 * RW_WIDTH + 2 * RW_DECAY_LORA,
             3 * RW_WIDTH + 2 * RW_DECAY_LORA + 2 * RW_AAA_LORA]
N_BRANCH = 3
N_EXPERTS = 64
TOP_K = 6
N_GROUPS = 8
TOPK_GROUPS = 4
ROUTE_SCALE = 2.5

LANES = 128
VMEM_LIMIT = 56 * 1024 * 1024
TM = 512
MOE_BLOCK = 256
WKV_CHUNK = 64
WKV_GROUP = 4


def _cparams(sem):
    return pltpu.CompilerParams(dimension_semantics=sem, vmem_limit_bytes=VMEM_LIMIT)


def _dot(a, b, prec=None):
    return jnp.dot(a, b, preferred_element_type=F32, precision=prec)


def _dot_nt(a, b, prec=None):
    return lax.dot_general(a, b, (((1,), (1,)), ((), ())), preferred_element_type=F32, precision=prec)


def _dot_tn(a, b, prec=None):
    return lax.dot_general(a, b, (((0,), (0,)), ((), ())), preferred_element_type=F32, precision=prec)


def _small_mm_kernel(a_ref, w_ref, b_ref, o_ref):
    o_ref[...] = _dot(a_ref[...], w_ref[0], HI) + b_ref[0]


def small_matmul_bias(a, w, b, layer, tn=1536):
    m, k = a.shape
    n = w.shape[2]
    return pl.pallas_call(
        _small_mm_kernel,
        grid=(n // tn,),
        in_specs=[pl.BlockSpec((m, k), lambda j: (0, 0)),
                  pl.BlockSpec((1, k, tn), lambda j: (layer, 0, j)),
                  pl.BlockSpec((1, 1, tn), lambda j: (layer, 0, j))],
        out_specs=pl.BlockSpec((m, tn), lambda j: (0, j)),
        out_shape=jax.ShapeDtypeStruct((m, n), F32),
        compiler_params=_cparams(("arbitrary",)),
        name="mod_matmul",
    )(a, w, b.reshape(b.shape[0], 1, n))


def _normmod_mm_kernel(grp_ref, x_ref, g_ref, sh_ref, sc_ref, w_ref, o_ref, h_ref):
    del grp_ref

    @pl.when(pl.program_id(1) == 0)
    def _():
        x = x_ref[...]
        y = x * lax.rsqrt(jnp.mean(x * x, axis=-1, keepdims=True) + NORM_EPS)
        y = y * g_ref[...]
        h_ref[...] = (y * (1.0 + sc_ref[0]) + sh_ref[0]).astype(h_ref.dtype)

    o_ref[...] = _dot(h_ref[...], w_ref[...]).astype(o_ref.dtype)


def normmod_matmul(x, g, shift3, scale3, grp, w, tn, out_dtype):
    m, d = x.shape
    n = w.shape[1]
    grid_spec = pltpu.PrefetchScalarGridSpec(
        num_scalar_prefetch=1,
        grid=(m // TM, n // tn),
        in_specs=[pl.BlockSpec((TM, d), lambda i, j, grp: (i, 0)),
                  pl.BlockSpec((1, d), lambda i, j, grp: (0, 0)),
                  pl.BlockSpec((1, 1, d), lambda i, j, grp: (grp[i], 0, 0)),
                  pl.BlockSpec((1, 1, d), lambda i, j, grp: (grp[i], 0, 0)),
                  pl.BlockSpec((d, tn), lambda i, j, grp: (0, j))],
        out_specs=pl.BlockSpec((TM, tn), lambda i, j, grp: (i, j)),
        scratch_shapes=[pltpu.VMEM((TM, d), BF16)],
    )
    return pl.pallas_call(
        _normmod_mm_kernel,
        grid_spec=grid_spec,
        out_shape=jax.ShapeDtypeStruct((m, n), out_dtype),
        compiler_params=_cparams(("arbitrary", "arbitrary")),
        name="normmod_matmul",
    )(grp, x, g.reshape(1, d), shift3, scale3, w)


def _merge_kernel(oa_ref, ob_ref, oc_ref, ga_ref, gb_ref, gc_ref, pa_ref, pb_ref, pc_ref, o_ref):
    gate = lambda ref: jax.nn.sigmoid(ref[...].astype(F32))
    m = gate(ga_ref) * _dot(oa_ref[...], pa_ref[...])
    m = m + gate(gb_ref) * _dot(ob_ref[...], pb_ref[...])
    m = m + gate(gc_ref) * _dot(oc_ref[...], pc_ref[...])
    o_ref[...] = m.astype(o_ref.dtype)


def branch_merge(m, o_a, o_b, o_c, gates, pa, pb, pc, tn=1024):
    k = o_a.shape[1]
    d = pa.shape[1]
    nj = d // tn
    o_spec = pl.BlockSpec((TM, k), lambda i, j: (i, 0))
    p_spec = pl.BlockSpec((k, tn), lambda i, j: (0, j))
    return pl.pallas_call(
        _merge_kernel,
        grid=(m // TM, nj),
        in_specs=[o_spec, o_spec, o_spec,
                  pl.BlockSpec((TM, tn), lambda i, j: (i, j)),
                  pl.BlockSpec((TM, tn), lambda i, j: (i, j + nj)),
                  pl.BlockSpec((TM, tn), lambda i, j: (i, j + 2 * nj)),
                  p_spec, p_spec, p_spec],
        out_specs=pl.BlockSpec((TM, tn), lambda i, j: (i, j)),
        out_shape=jax.ShapeDtypeStruct((m, d), BF16),
        compiler_params=_cparams(("arbitrary", "arbitrary")),
        name="branch_merge",
    )(o_a, o_b, o_c, gates, gates, gates, pa, pb, pc)


def _resid_mm_kernel(grp_ref, a_ref, w_ref, x_ref, gate_ref, o_ref):
    del grp_ref
    o_ref[...] = x_ref[...] + gate_ref[0] * _dot(a_ref[...], w_ref[...])


def resid_matmul(a, w, x, gate3, grp, tn=1024):
    m, k = a.shape
    d = w.shape[1]
    grid_spec = pltpu.PrefetchScalarGridSpec(
        num_scalar_prefetch=1,
        grid=(m // TM, d // tn),
        in_specs=[pl.BlockSpec((TM, k), lambda i, j, grp: (i, 0)),
                  pl.BlockSpec((k, tn), lambda i, j, grp: (0, j)),
                  pl.BlockSpec((TM, tn), lambda i, j, grp: (i, j)),
                  pl.BlockSpec((1, 1, tn), lambda i, j, grp: (grp[i], 0, j))],
        out_specs=pl.BlockSpec((TM, tn), lambda i, j, grp: (i, j)),
    )
    return pl.pallas_call(
        _resid_mm_kernel,
        grid_spec=grid_spec,
        out_shape=jax.ShapeDtypeStruct((m, d), F32),
        compiler_params=_cparams(("arbitrary", "arbitrary")),
        name="resid_matmul",
    )(grp, a, w, x, gate3)


def _short_conv_kernel(x_ref, prev_ref, next_ref, w_ref, first_ref, last_ref, o_ref):
    x = x_ref[...].astype(F32)
    tm = x.shape[0]
    row = lax.broadcasted_iota(jnp.int32, x.shape, 0)
    halo = prev_ref.shape[0]
    before = prev_ref[...].astype(F32)[halo - 1:halo, :]
    after = next_ref[...].astype(F32)[0:1, :]
    prev = jnp.where(row == 0, before, pltpu.roll(x, 1, 0)) * first_ref[...]
    nxt = jnp.where(row == tm - 1, after, pltpu.roll(x, tm - 1, 0)) * last_ref[...]
    o_ref[...] = prev * w_ref[0:1, :] + x * w_ref[1:2, :] + nxt * w_ref[2:3, :]


def short_conv(u, w, first, last, tn):
    assert SHORT_CONV == 3
    m, c = u.shape
    halo = 16
    per = TM // halo
    n_halo = m // halo
    return pl.pallas_call(
        _short_conv_kernel,
        grid=(m // TM, c // tn),
        in_specs=[pl.BlockSpec((TM, tn), lambda i, j: (i, j)),
                  pl.BlockSpec((halo, tn), lambda i, j: (jnp.maximum(i * per - 1, 0), j)),
                  pl.BlockSpec((halo, tn), lambda i, j: (jnp.minimum((i + 1) * per, n_halo - 1), j)),
                  pl.BlockSpec((SHORT_CONV, tn), lambda i, j: (0, j)),
                  pl.BlockSpec((TM, 1), lambda i, j: (i, 0)),
                  pl.BlockSpec((TM, 1), lambda i, j: (i, 0))],
        out_specs=pl.BlockSpec((TM, tn), lambda i, j: (i, j)),
        out_shape=jax.ShapeDtypeStruct((m, c), F32),
        compiler_params=_cparams(("arbitrary", "arbitrary")),
        name="short_conv",
    )(u, u, u, w, first, last)


def _rmsnorm_kernel(x_ref, g_ref, o_ref):
    x = x_ref[...]
    y = x * lax.rsqrt(jnp.mean(x * x, axis=-1, keepdims=True) + NORM_EPS)
    o_ref[...] = y * g_ref[...]


def rmsnorm_rows(x, g):
    m, d = x.shape
    return pl.pallas_call(
        _rmsnorm_kernel,
        grid=(m // TM,),
        in_specs=[pl.BlockSpec((TM, d), lambda i: (i, 0)), pl.BlockSpec((1, d), lambda i: (0, 0))],
        out_specs=pl.BlockSpec((TM, d), lambda i: (i, 0)),
        out_shape=jax.ShapeDtypeStruct((m, d), F32),
        compiler_params=_cparams(("arbitrary",)),
        name="final_rmsnorm",
    )(x, g.reshape(1, d))


def _hyfilt_kernel(z_ref, w1_ref, b1_ref, w2_ref, b2_ref, fr_ref, w3_ref, dl_ref, h_ref, ss_ref):
    z = z_ref[...]
    hdn = jnp.sin(fr_ref[0:1, :] * (_dot(z, w1_ref[...], HI) + b1_ref[...]))
    hdn = jnp.sin(fr_ref[1:2, :] * (_dot(hdn, w2_ref[...], HI) + b2_ref[...]))
    h = _dot(hdn, w3_ref[...], HI)
    h = h * jnp.exp(-z[:, 0:1] * dl_ref[...])
    h_ref[...] = h

    @pl.when(pl.program_id(0) == 0)
    def _():
        ss_ref[...] = jnp.zeros_like(ss_ref)

    ss_ref[...] += jnp.sum(h * h, axis=0, keepdims=True)


def hyena_filters_raw(L, w1, b1, w2, b2, freq, w3):
    t = np.linspace(0.0, 1.0, L, dtype=np.float32)[:, None]
    omega = np.float32(2.0 * math.pi / L) * np.arange(L, dtype=np.float32)[:, None]
    bands = np.linspace(1e-4, HY_BANDS - 1, HY_BANDS, dtype=np.float32)[None, :]
    z = np.concatenate([t, np.cos(omega * bands), -np.sin(omega * bands),
                        np.zeros((L, HY_FILTER_ORDER - HY_EMB), np.float32)], axis=-1).astype(np.float32)
    w1p = jnp.concatenate([w1, jnp.zeros((HY_FILTER_ORDER - HY_EMB, HY_FILTER_ORDER), F32)], axis=0)
    deltas = np.abs(np.linspace(math.log(HY_DECAY_TARGET) / HY_SLOW_DECAY,
                                math.log(HY_DECAY_TARGET) / HY_FAST_DECAY, HY_WIDTH, dtype=np.float32))
    dl4 = np.tile(deltas, 4)[None, :]
    tl = min(L, 256)
    n = 4 * HY_WIDTH
    fo = HY_FILTER_ORDER
    full = lambda shape: pl.BlockSpec(shape, lambda i: (0, 0))
    return pl.pallas_call(
        _hyfilt_kernel,
        grid=(L // tl,),
        in_specs=[pl.BlockSpec((tl, fo), lambda i: (i, 0)), full((fo, fo)), full((1, fo)), full((fo, fo)),
                  full((1, fo)), full((2, fo)), full((fo, n)), full((1, n))],
        out_specs=[pl.BlockSpec((tl, n), lambda i: (i, 0)), full((1, n))],
        out_shape=[jax.ShapeDtypeStruct((L, n), F32), jax.ShapeDtypeStruct((1, n), F32)],
        compiler_params=_cparams(("arbitrary",)),
        name="hyena_filters",
    )(jnp.asarray(z), w1p, b1.reshape(1, fo), w2, b2.reshape(1, fo), freq, w3, jnp.asarray(dl4))


def _filter_scale(ss):
    s = ss.reshape(2, 2, HY_WIDTH)
    rs = lax.rsqrt(jnp.sum(s, axis=1, keepdims=True))
    return jnp.broadcast_to(rs, (2, 2, HY_WIDTH)).reshape(1, 4 * HY_WIDTH)


FFT_N1 = 128
FFT_N2 = 64
FFT_PITCH = 72
FFT_BATCH = 4


@functools.lru_cache(maxsize=None)
def _fft_tables():
    n1, n2 = FFT_N1, FFT_N2
    n = n1 * n2
    a = np.arange(n1 // 2)[None, None, :]
    k1 = np.arange(n1)[None, :, None]
    b = np.arange(n2)[:, None, None]
    theta = 2.0 * np.pi * ((a * k1 % n1) / n1 + (b * k1) / n)
    g = np.concatenate([np.cos(theta), -np.sin(theta)], axis=1)
    ig = np.concatenate([np.cos(theta), -np.sin(theta)], axis=1).transpose(0, 2, 1) / n
    k2 = np.arange(n2)[:, None]
    bb = np.arange(n2)[None, :]
    ph = 2.0 * np.pi * (k2 * bb % n2) / n2
    fr, fi = np.cos(ph), -np.sin(ph)
    f2 = np.block([[fr, -fi], [fi, fr]])
    if2 = np.block([[fr, fi], [-fi, fr]])
    return (jnp.asarray(g, BF16), jnp.asarray(f2, BF16), jnp.asarray(if2, BF16), jnp.asarray(ig, BF16))


def _fft_stage1(u_ref, g_ref, sr_ref, si_ref):
    n1, n2, p = FFT_N1, FFT_N2, FFT_PITCH

    def body(i, carry):
        bs = [i * FFT_BATCH + j for j in range(FFT_BATCH)]
        xs = [u_ref[pl.ds(b, n1 // 2, stride=n2), :].astype(BF16) for b in bs]
        outs = [_dot(g_ref[b], x) for b, x in zip(bs, xs)]
        for b, a in zip(bs, outs):
            sr_ref[pl.ds(b, n1, stride=p), :] = a[:n1]
            si_ref[pl.ds(b, n1, stride=p), :] = a[n1:]
        return carry

    lax.fori_loop(0, n2 // FFT_BATCH, body, 0)


def _bin_rows(k1):
    return pl.ds(pl.multiple_of(k1 * FFT_PITCH, 8), FFT_N2)


def _hyconv_kernel(xm_ref, u_ref, hr_ref, hi_ref, bias_ref, g_ref, f2_ref, if2_ref, ig_ref, o_ref, sr_ref, si_ref):
    n1, n2, p = FFT_N1, FFT_N2, FFT_PITCH
    _fft_stage1(u_ref, g_ref, sr_ref, si_ref)

    def pair(ref, k):
        return jnp.concatenate([ref[_bin_rows(2 * k), :], ref[_bin_rows(2 * k + 1), :]], axis=1)

    def pair_h(ref, k):
        blk = ref[pl.ds(pl.multiple_of(k * 2 * n2, 2 * n2), 2 * n2), :]
        return jnp.concatenate([blk[:n2], blk[n2:]], axis=1)

    def unpair(ref, k, val):
        w = val.shape[1] // 2
        ref[_bin_rows(2 * k), :] = val[:, :w]
        ref[_bin_rows(2 * k + 1), :] = val[:, w:]

    def stage2(i, carry):
        ks = [i * FFT_BATCH + j for j in range(FFT_BATCH)]
        zs = [jnp.concatenate([pair(sr_ref, k), pair(si_ref, k)], axis=0).astype(BF16) for k in ks]
        xs = [_dot(f2_ref[...], z) for z in zs]
        ys = []
        for k, x in zip(ks, xs):
            xr, xi = x[:n2], x[n2:]
            hr, hi = pair_h(hr_ref, k), pair_h(hi_ref, k)
            ys.append(jnp.concatenate([xr * hr - xi * hi, xr * hi + xi * hr], axis=0).astype(BF16))
        bbs = [_dot(if2_ref[...], y) for y in ys]
        for k, bb in zip(ks, bbs):
            unpair(sr_ref, k, bb[:n2])
            unpair(si_ref, k, bb[n2:])
        return carry

    lax.fori_loop(0, n1 // 2 // FFT_BATCH, stage2, 0)

    def stage3(i, carry):
        bs = [i * FFT_BATCH + j for j in range(FFT_BATCH)]
        sts = [jnp.concatenate([sr_ref[pl.ds(b, n1, stride=p), :], si_ref[pl.ds(b, n1, stride=p), :]],
                               axis=0).astype(BF16) for b in bs]
        outs = [_dot(ig_ref[b], st) for b, st in zip(bs, sts)]
        for b, o in zip(bs, outs):
            o_ref[pl.ds(b, n1 // 2, stride=n2), :] = o
        return carry

    lax.fori_loop(0, n2 // FFT_BATCH, stage3, 0)
    bias = bias_ref[...]
    rows_per_pass = 512

    def finish(i, carry):
        rows = pl.ds(pl.multiple_of(i * rows_per_pass, rows_per_pass), rows_per_pass)
        o_ref[rows, :] = xm_ref[rows, :] * (o_ref[rows, :] + u_ref[rows, :] * bias)
        return carry

    lax.fori_loop(0, o_ref.shape[0] // rows_per_pass, finish, 0)


def hyena_conv(xm_arr, xm_col, u_arr, u_col, hr, hi, h_col, bias, n_batch, L):
    assert L == FFT_N1 * FFT_N2 // 2
    cb = LANES
    nct = HY_WIDTH // cb
    n = 2 * L
    g, f2, if2, ig = _fft_tables()
    const3 = lambda shape: pl.BlockSpec(shape, lambda b, c: (0, 0, 0))
    const2 = lambda shape: pl.BlockSpec(shape, lambda b, c: (0, 0))
    return pl.pallas_call(
        _hyconv_kernel,
        grid=(n_batch, nct),
        in_specs=[pl.BlockSpec((L, cb), lambda b, c: (b, xm_col * nct + c)),
                  pl.BlockSpec((L, cb), lambda b, c: (b, u_col * nct + c)),
                  pl.BlockSpec((n, cb), lambda b, c: (0, h_col * nct + c)),
                  pl.BlockSpec((n, cb), lambda b, c: (0, h_col * nct + c)),
                  pl.BlockSpec((1, cb), lambda b, c: (0, c)),
                  const3(g.shape), const2(f2.shape), const2(if2.shape), const3(ig.shape)],
        out_specs=pl.BlockSpec((L, cb), lambda b, c: (b, c)),
        out_shape=jax.ShapeDtypeStruct((n_batch * L, HY_WIDTH), F32),
        scratch_shapes=[pltpu.VMEM((FFT_N1 * FFT_PITCH, cb), F32)] * 2,
        compiler_params=_cparams(("arbitrary", "arbitrary")),
        name="hyena_conv",
    )(xm_arr, u_arr, hr, hi, bias.reshape(1, HY_WIDTH), g, f2, if2, ig)


def _hyspec_kernel(h0_ref, h1_ref, rs_ref, g_ref, f2_ref, hr_ref, hi_ref, s0r, s0i, s1r, s1i):
    n1, n2 = FFT_N1, FFT_N2
    _fft_stage1(h0_ref, g_ref, s0r, s0i)
    _fft_stage1(h1_ref, g_ref, s1r, s1i)
    rs = rs_ref[...]
    h10 = h1_ref[0:1, :]

    def stage2(i, carry):
        ks = [i * FFT_BATCH + j for j in range(FFT_BATCH)]
        zs = [jnp.concatenate([jnp.concatenate([s0r[_bin_rows(k), :], s1r[_bin_rows(k), :]], axis=1),
                               jnp.concatenate([s0i[_bin_rows(k), :], s1i[_bin_rows(k), :]], axis=1)],
                              axis=0).astype(BF16) for k in ks]
        xs = [_dot(f2_ref[...], z) for z in zs]
        for k, x in zip(ks, xs):
            rows = pl.ds(pl.multiple_of(k * n2, n2), n2)
            w = x.shape[1] // 2
            hr_ref[rows, :] = rs * (x[:n2, :w] + x[:n2, w:] - h10)
            hi_ref[rows, :] = rs * (x[n2:, :w] - x[n2:, w:])
        return carry

    lax.fori_loop(0, n1 // FFT_BATCH, stage2, 0)


def hyena_filter_spectrum(h_raw, rs, L):
    assert L == FFT_N1 * FFT_N2 // 2
    cb = LANES
    nct = HY_WIDTH // cb
    n = 2 * L
    g, f2, _, _ = _fft_tables()
    out_spec = pl.BlockSpec((n, cb), lambda o, c: (0, o * nct + c))
    scr = pltpu.VMEM((FFT_N1 * FFT_PITCH, cb), F32)
    return pl.pallas_call(
        _hyspec_kernel,
        grid=(2, nct),
        in_specs=[pl.BlockSpec((L, cb), lambda o, c: (0, (2 * o) * nct + c)),
                  pl.BlockSpec((L, cb), lambda o, c: (0, (2 * o + 1) * nct + c)),
                  pl.BlockSpec((1, cb), lambda o, c: (0, (2 * o) * nct + c)),
                  pl.BlockSpec(g.shape, lambda o, c: (0, 0, 0)),
                  pl.BlockSpec(f2.shape, lambda o, c: (0, 0))],
        out_specs=[out_spec, out_spec],
        out_shape=[jax.ShapeDtypeStruct((n, 2 * HY_WIDTH), F32)] * 2,
        scratch_shapes=[scr, scr, scr, scr],
        compiler_params=_cparams(("arbitrary", "arbitrary")),
        name="hyena_filter_spectrum",
    )(h_raw, h_raw, rs, g, f2)


@functools.lru_cache(maxsize=None)
def _dense_dft_tables(L):
    n = 2 * L
    k = np.arange(n)[:, None]
    t = np.arange(L)[None, :]
    ph = 2.0 * np.pi * (k * t % n) / n
    fwd = np.concatenate([np.cos(ph), -np.sin(ph)], axis=0)
    inv = np.concatenate([np.cos(ph), -np.sin(ph)], axis=0).T / n
    return jnp.asarray(fwd, BF16), jnp.asarray(inv, BF16)


def _hyena_small_kernel(x1_ref, x2_ref, v_ref, h_ref0a, h_ref0b, h_ref1a, h_ref1b, rs0_ref, rs1_ref,
                        b0_ref, b1_ref, fwd_ref, inv_ref, o_ref, *, L):
    n = 2 * L
    fwd = fwd_ref[...]
    inv = inv_ref[...]

    def conv(u, ha_ref, hb_ref, rs_ref, bias_ref):
        ha, hb = ha_ref[...], hb_ref[...]
        ka = _dot(fwd, ha.astype(BF16))
        kb = _dot(fwd, hb.astype(BF16))
        rs = rs_ref[...]
        kr = rs * (ka[:n] + kb[:n] - hb[0:1, :])
        ki = rs * (ka[n:] - kb[n:])
        uf = _dot(fwd, u.astype(BF16))
        ur, ui = uf[:n], uf[n:]
        y = jnp.concatenate([ur * kr - ui * ki, ur * ki + ui * kr], axis=0)
        return _dot(inv, y.astype(BF16)) + u * bias_ref[...]

    v = v_ref[...]
    z = x1_ref[...] * conv(v, h_ref0a, h_ref0b, rs0_ref, b0_ref)
    o_ref[...] = (x2_ref[...] * conv(z, h_ref1a, h_ref1b, rs1_ref, b1_ref)).astype(o_ref.dtype)


def hyena_small(u_arr, row0_blocks, n_batch, L, h_raw, rs, bias):
    cb = LANES
    nct = HY_WIDTH // cb
    fwd, inv = _dense_dft_tables(L)
    uspec = lambda col: pl.BlockSpec((L, cb), lambda b, c: (row0_blocks + b, col * nct + c))
    hspec = lambda col: pl.BlockSpec((L, cb), lambda b, c: (0, col * nct + c))
    rspec = lambda col: pl.BlockSpec((1, cb), lambda b, c: (0, col * nct + c))
    bspec = pl.BlockSpec((1, cb), lambda b, c: (0, c))
    bias_0 = bias[0].reshape(1, HY_WIDTH)
    bias_1 = bias[1].reshape(1, HY_WIDTH)
    return pl.pallas_call(
        functools.partial(_hyena_small_kernel, L=L),
        grid=(n_batch, nct),
        in_specs=[uspec(0), uspec(1), uspec(2), hspec(0), hspec(1), hspec(2), hspec(3), rspec(0), rspec(2),
                  bspec, bspec,
                  pl.BlockSpec(fwd.shape, lambda b, c: (0, 0)), pl.BlockSpec(inv.shape, lambda b, c: (0, 0))],
        out_specs=pl.BlockSpec((L, cb), lambda b, c: (b, c)),
        out_shape=jax.ShapeDtypeStruct((n_batch * L, HY_WIDTH), BF16),
        compiler_params=_cparams(("arbitrary", "arbitrary")),
        name="hyena_ctx",
    )(u_arr, u_arr, u_arr, h_raw, h_raw, h_raw, h_raw, rs, rs, bias_0, bias_1, fwd, inv)


def _na_bias_table(rpb):
    cols = np.arange(GRID_W)
    col_start = np.clip(cols - NA_COLS // 2, 0, GRID_W - NA_COLS)[:, None]
    in_win = (cols[None, :] >= col_start) & (cols[None, :] < col_start + NA_COLS)
    rel_col = np.clip(cols[None, :] - cols[:, None], 1 - NA_COLS, NA_COLS - 1) + NA_COLS - 1
    tbl = rpb.astype(F32)[:, :, rel_col]
    tbl = jnp.where(jnp.asarray(in_win)[None, None], tbl, -jnp.inf)
    return jnp.concatenate([tbl[:, :-1], tbl[:, 1:]], axis=-1)


def _na_kernel(*refs, n_rows):
    q_ref = refs[0]
    k_refs = refs[1:1 + NA_ROWS]
    v_refs = refs[1 + NA_ROWS:1 + 2 * NA_ROWS]
    kc_ref, vc_ref, tbl_ref, o_ref = refs[1 + 2 * NA_ROWS:]
    r = pl.program_id(1)
    start = jnp.clip(r - NA_ROWS // 2, 0, n_rows - NA_ROWS)
    d0 = start - r + NA_ROWS - 1
    dh = NA_HEAD_DIM
    q = q_ref[...] * (dh ** -0.5)
    n_pairs = NA_ROWS // 2
    heads = [slice(h * dh, (h + 1) * dh) for h in range(NA_HEADS)]
    scores = []
    for h, hs in enumerate(heads):
        qh = q[:, hs]
        tiles = [_dot_nt(qh, jnp.concatenate([k_refs[2 * p][:, hs], k_refs[2 * p + 1][:, hs]], axis=0))
                 + tbl_ref[h, d0 + 2 * p] for p in range(n_pairs)]
        scores.append(tiles + [_dot_nt(qh, kc_ref[:, hs])])
    probs, denoms = [], []
    for tiles in scores:
        m = tiles[0].max(axis=-1, keepdims=True)
        for s in tiles[1:]:
            m = jnp.maximum(m, s.max(axis=-1, keepdims=True))
        ps = [jnp.exp(s - m) for s in tiles]
        l = ps[0].sum(axis=-1, keepdims=True)
        for p_ in ps[1:]:
            l = l + p_.sum(axis=-1, keepdims=True)
        probs.append([p_.astype(BF16) for p_ in ps])
        denoms.append(l)
    for hs, ps, l in zip(heads, probs, denoms):
        acc = _dot(ps[-1], vc_ref[:, hs])
        for p in range(n_pairs):
            acc = acc + _dot(ps[p], jnp.concatenate([v_refs[2 * p][:, hs], v_refs[2 * p + 1][:, hs]], axis=0))
        o_ref[:, hs] = (acc / l).astype(o_ref.dtype)


def na_latent(na, rpb, n_batch, L, Lc):
    n_rows = L // GRID_W
    assert n_rows >= NA_ROWS
    tbl = _na_bias_table(rpb)
    w = NA_WIDTH
    ctx_blk0 = n_batch * L // Lc

    def kv_spec(i, col):
        def imap(b, r):
            start = jnp.clip(r - NA_ROWS // 2, 0, n_rows - NA_ROWS)
            return (b * n_rows + start + i, col)
        return pl.BlockSpec((GRID_W, w), imap)

    in_specs = ([pl.BlockSpec((GRID_W, w), lambda b, r: (b * n_rows + r, 0))]
                + [kv_spec(i, 1) for i in range(NA_ROWS)] + [kv_spec(i, 2) for i in range(NA_ROWS)]
                + [pl.BlockSpec((Lc, w), lambda b, r: (ctx_blk0 + b, 1)),
                   pl.BlockSpec((Lc, w), lambda b, r: (ctx_blk0 + b, 2)),
                   pl.BlockSpec(tbl.shape, lambda b, r: (0, 0, 0, 0))])
    return pl.pallas_call(
        functools.partial(_na_kernel, n_rows=n_rows),
        grid=(n_batch, n_rows),
        in_specs=in_specs,
        out_specs=pl.BlockSpec((GRID_W, w), lambda b, r: (b * n_rows + r, 0)),
        out_shape=jax.ShapeDtypeStruct((n_batch * L, w), BF16),
        compiler_params=_cparams(("arbitrary", "arbitrary")),
        name="na_latent",
    )(*([na] * (3 + 2 * NA_ROWS)), tbl)


def _ctx_attn_kernel(q_ref, k_ref, v_ref, o_ref):
    dh = NA_HEAD_DIM
    q = q_ref[...] * (dh ** -0.5)
    for h in range(NA_HEADS):
        hs = slice(h * dh, (h + 1) * dh)
        s = _dot_nt(q[:, hs], k_ref[:, hs])
        p_ = jnp.exp(s - s.max(axis=-1, keepdims=True))
        acc = _dot(p_.astype(BF16), v_ref[:, hs])
        o_ref[:, hs] = (acc / p_.sum(axis=-1, keepdims=True)).astype(o_ref.dtype)


def ctx_attn(na, n_batch, L, Lc):
    w = NA_WIDTH
    blk0 = n_batch * L // Lc
    spec = lambda col: pl.BlockSpec((Lc, w), lambda b: (blk0 + b, col))
    return pl.pallas_call(
        _ctx_attn_kernel,
        grid=(n_batch,),
        in_specs=[spec(0), spec(1), spec(2)],
        out_specs=pl.BlockSpec((Lc, w), lambda b: (b, 0)),
        out_shape=jax.ShapeDtypeStruct((n_batch * Lc, w), BF16),
        compiler_params=_cparams(("arbitrary",)),
        name="ctx_attn",
    )(na, na, na)


@functools.lru_cache(maxsize=None)
def _wkv_masks():
    c, g = WKV_CHUNK, WKV_GROUP
    t = np.arange(c)[:, None]
    s = np.arange(c)[None, :]
    tinc = np.stack([(s <= t), (s >= t)]).astype(np.float32)
    strict = np.stack([(s < t), (s > t)]).astype(np.float32)
    tile = lambda m: np.tile(m, (1,) * (m.ndim - 1) + (g,))
    blk = lambda n: (t // n == s // n)
    blk16 = tile(blk(16).astype(np.float32))
    off32 = tile((blk(32) & ~blk(16)).astype(np.float32))
    off64 = tile((~blk(32)).astype(np.float32))
    eye = tile((t == s).astype(np.float32))
    rr = np.arange(g * c)
    hm = (rr[:, None] // c == np.arange(g * RW_HEAD_DIM)[None, :] // RW_HEAD_DIM).astype(np.float32)
    masks = tuple(jnp.asarray(m) for m in (tinc, tile(strict), tile(tinc), blk16, off32, off64, eye, hm))
    return masks + (jnp.asarray(hm, BF16),)


def _wkv_kernel(*refs):
    (r0_ref, v0_ref, kk0_ref, r1_ref, v1_ref, kk1_ref, lw0_ref, av0_ref, kd0_ref, lw1_ref, av1_ref, kd1_ref,
     tinc_ref, strict_ref, incl_ref, blk16_ref, off32_ref, off64_ref, eye_ref, hm_ref, hmb_ref,
     y0_ref, y1_ref, state_ref) = refs
    c, g = WKV_CHUNK, WKV_GROUP
    gw = g * RW_HEAD_DIM

    @pl.when(pl.program_id(1) == 0)
    def _():
        state_ref[...] = jnp.zeros_like(state_ref)

    hm = hm_ref[...]
    hm_bf = hmb_ref[...]
    blk16, off32, off64, eye = blk16_ref[...], off32_ref[...], off64_ref[...], eye_ref[...]

    def bdiag(z):
        return jnp.concatenate([z.astype(BF16)] * g, axis=0) * hm_bf

    def pm(x4, zd):
        return _dot(x4.astype(BF16), zd)

    def prepare(d, r_ref, kk_ref, lw_ref, av_ref, kd_ref):
        lw = lw_ref[0]
        cum = _dot(tinc_ref[d], lw, HI)
        tot = jnp.sum(lw, axis=0, keepdims=True)
        e_neg = jnp.exp(-cum)
        e_rem = jnp.exp(tot - cum)
        kk = kk_ref[...]
        b_vec = kk * av_ref[0]
        kd = kd_ref[0]
        return dict(at=-kk * jnp.exp(cum - lw), rt=r_ref[...] * jnp.exp(cum), bt=b_vec * e_neg, kt=kd * e_neg,
                    bp=b_vec * e_rem, kp=kd * e_rem, e_tot=jnp.exp(tot))

    qs = (prepare(0, r0_ref, kk0_ref, lw0_ref, av0_ref, kd0_ref), prepare(1, r1_ref, kk1_ref, lw1_ref, av1_ref, kd1_ref))
    v_refs, y_refs = (v0_ref, v1_ref), (y0_ref, y1_ref)
    chains = [(d, gi) for gi in range(RW_HEADS // g) for d in range(2)]
    sl = lambda gi: slice(gi * gw, (gi + 1) * gw)
    each = lambda f, *lists: [f(*args) for args in zip(*lists)]
    pm_all = lambda xs, zs: each(lambda x, z: pm(x, bdiag(z)), xs, zs)

    ar = [jnp.concatenate([qs[d]["at"][:, sl(gi)], qs[d]["rt"][:, sl(gi)]], axis=0).astype(BF16) for d, gi in chains]
    pb = [_dot_nt(a, bdiag(qs[d]["bt"][:, sl(gi)])) for a, (d, gi) in zip(ar, chains)]
    pk = [_dot_nt(a, bdiag(qs[d]["kt"][:, sl(gi)])) for a, (d, gi) in zip(ar, chains)]
    a_ab = [p[:c] * strict_ref[d] for p, (d, gi) in zip(pb, chains)]
    a_rb = [p[c:] * incl_ref[d] for p, (d, gi) in zip(pb, chains)]
    a_ak = [p[:c] * strict_ref[d] for p, (d, gi) in zip(pk, chains)]
    a_rk = [p[c:] * incl_ref[d] for p, (d, gi) in zip(pk, chains)]
    ad = [a * blk16 for a in a_ab]
    a2 = pm_all(ad, ad)
    a4 = pm_all(a2, a2)
    a8 = pm_all(a4, a4)
    tinv = [eye + a for a in ad]
    for powr in (a2, a4, a8):
        tinv = each(lambda t, p_: t + p_, tinv, pm_all(tinv, powr))
    for off in (off32, off64):
        mid = pm_all(tinv, [a * off for a in a_ab])
        tinv = each(lambda t, p_: t + p_, tinv, pm_all(mid, tinv))
    s0 = [state_ref[d, gi] for d, gi in chains]
    vv = [v_refs[d][:, sl(gi)] for d, gi in chains]
    vd = [bdiag(v_) for v_ in vv]
    ars = each(lambda a, s_: _dot_nt(a, s_.astype(BF16)), ar, s0)
    akv = each(pm, a_ak, vd)
    u = pm_all(tinv, each(lambda x, y_: x[:c] + y_, ars, akv))
    yu = pm_all(a_rb, u)
    yv = each(pm, a_rk, vd)
    for (d, gi), x, y1_, y2_ in zip(chains, ars, yu, yv):
        y_refs[d][:, sl(gi)] = x[c:] + y1_ + y2_
    upd = [_dot_tn(jnp.concatenate([u_, v_], axis=0).astype(BF16),
                   jnp.concatenate([qs[d]["bp"][:, sl(gi)], qs[d]["kp"][:, sl(gi)]], axis=0).astype(BF16))
           for u_, v_, (d, gi) in zip(u, vv, chains)]
    for (d, gi), s_, up in zip(chains, s0, upd):
        state_ref[d, gi] = s_ * qs[d]["e_tot"][:, sl(gi)] + hm * up


def wkv_scan(r, v, kk, lw, av, kd, n_batch, L, Lc):
    c = WKV_CHUNK
    rows, w = r.shape
    nc, nl = Lc // c, L // c
    masks = _wkv_masks()

    def blk(d, b, s):
        j_ctx = s if d == 0 else nc - 1 - s
        j_lat = s - nc if d == 0 else nl - 1 - (s - nc)
        return jnp.where(s < nc, (n_batch * L + b * Lc) // c + j_ctx, (b * L) // c + j_lat)

    shared = lambda d: pl.BlockSpec((c, w), lambda b, s: (blk(d, b, s), 0))
    perdir = lambda d: pl.BlockSpec((1, c, w), lambda b, s: (d, blk(d, b, s), 0))
    full = lambda m: pl.BlockSpec(m.shape, lambda b, s: (0,) * m.ndim)
    gw = WKV_GROUP * RW_HEAD_DIM
    return pl.pallas_call(
        _wkv_kernel,
        grid=(n_batch, nc + nl),
        in_specs=[shared(0)] * 3 + [shared(1)] * 3 + [perdir(0)] * 3 + [perdir(1)] * 3 + [full(m) for m in masks],
        out_specs=[shared(0), shared(1)],
        out_shape=[jax.ShapeDtypeStruct((rows, w), F32)] * 2,
        scratch_shapes=[pltpu.VMEM((2, RW_HEADS // WKV_GROUP, gw, gw), F32)],
        compiler_params=_cparams(("arbitrary", "arbitrary")),
        name="wkv_scan",
    )(r, v, kk, r, v, kk, lw, av, kd, lw, av, kd, *masks)


def _pack_bf16_pairs(h):
    half = h.shape[1] // 2
    bits = lambda t: lax.bitcast_convert_type(t.astype(BF16).astype(F32), jnp.uint32)
    return (bits(h[:, :half]) >> 16) | (bits(h[:, half:]) & jnp.uint32(0xFFFF0000))


def _unpack_bf16_pairs(p):
    lo = lax.bitcast_convert_type(p << 16, F32).astype(BF16)
    hi = lax.bitcast_convert_type(p & jnp.uint32(0xFFFF0000), F32).astype(BF16)
    return jnp.concatenate([lo, hi], axis=1)


def _moe_kernel(be_ref, nb_ref, x_ref, wg_ref, wu_ref, wd_ref, o_ref, wg_s, wu_s, wd_s):
    i = pl.program_id(0)
    prev = be_ref[jnp.maximum(i - 1, 0)]

    @pl.when((i == 0) | (be_ref[i] != prev))
    def _():
        wg_s[...] = wg_ref[0, 0].astype(BF16)
        wu_s[...] = wu_ref[0, 0].astype(BF16)
        wd_s[...] = wd_ref[0, 0].astype(BF16)

    @pl.when(i < nb_ref[0])
    def _():
        o_ref[...] = _swiglu_block(x_ref, wg_s, wu_s, wd_s).astype(o_ref.dtype)

    @pl.when(i >= nb_ref[0])
    def _():
        o_ref[...] = jnp.zeros_like(o_ref)


def _swiglu_block(x_tiles, wg_s, wu_s, wd_s):
    x = _unpack_bf16_pairs(jnp.concatenate([x_tiles[:, s, :] for s in range(x_tiles.shape[1])], axis=1))
    hmid = (jax.nn.silu(_dot(x, wg_s[...])) * _dot(x, wu_s[...])).astype(BF16)
    return _dot(hmid, wd_s[...])


def _moe_gather_kernel(be_ref, nb_ref, cur_ref, nxt_ref, tab_ref, wg_ref, wu_ref, wd_ref, o_ref,
                       wg_s, wu_s, wd_s, xbuf, sems):
    i = pl.program_id(0)
    n_used = nb_ref[0]
    rows = xbuf.shape[1]
    batch = 16

    def start_gather(idx_ref, half):
        def issue(j, carry):
            for u in range(batch):
                r = j * batch + u
                pltpu.async_copy(tab_ref.at[pl.ds(idx_ref[0, 0, r], 1)], xbuf.at[half, pl.ds(r, 1)], sems.at[half])
            return carry

        lax.fori_loop(0, rows // batch, issue, 0)

    @pl.when((i == 0) & (n_used > 0))
    def _():
        start_gather(cur_ref, 0)

    @pl.when(i + 1 < n_used)
    def _():
        start_gather(nxt_ref, (i + 1) % 2)

    prev = be_ref[jnp.maximum(i - 1, 0)]

    @pl.when((i == 0) | (be_ref[i] != prev))
    def _():
        wg_s[...] = wg_ref[0, 0].astype(BF16)
        wu_s[...] = wu_ref[0, 0].astype(BF16)
        wd_s[...] = wd_ref[0, 0].astype(BF16)

    @pl.when(i < n_used)
    def _():
        half = i % 2
        pltpu.make_async_copy(tab_ref.at[pl.ds(0, rows)], xbuf.at[half], sems.at[half]).wait()
        o_ref[...] = _swiglu_block(xbuf.at[half], wg_s, wu_s, wd_s).astype(o_ref.dtype)

    @pl.when(i >= n_used)
    def _():
        o_ref[...] = jnp.zeros_like(o_ref)


def gathered_swiglu(table, slot_tok, block_e, n_used, w_gate, w_up, w_down, layer):
    d, ff = w_gate.shape[-2:]
    nb = slot_tok.shape[0] // MOE_BLOCK
    assert table.shape[0] >= MOE_BLOCK
    idx = slot_tok.reshape(nb, 1, MOE_BLOCK)
    grid_spec = pltpu.PrefetchScalarGridSpec(
        num_scalar_prefetch=2,
        grid=(nb,),
        in_specs=[pl.BlockSpec((1, 1, MOE_BLOCK), lambda i, be, nu: (i, 0, 0), memory_space=pltpu.SMEM),
                  pl.BlockSpec((1, 1, MOE_BLOCK), lambda i, be, nu: (jnp.minimum(i + 1, nb - 1), 0, 0),
                               memory_space=pltpu.SMEM),
                  pl.BlockSpec(memory_space=pl.ANY),
                  pl.BlockSpec((1, 1, d, ff), lambda i, be, nu: (layer, be[i], 0, 0)),
                  pl.BlockSpec((1, 1, d, ff), lambda i, be, nu: (layer, be[i], 0, 0)),
                  pl.BlockSpec((1, 1, ff, d), lambda i, be, nu: (layer, be[i], 0, 0))],
        out_specs=pl.BlockSpec((MOE_BLOCK, d), lambda i, be, nu: (i, 0)),
        scratch_shapes=[pltpu.VMEM((d, ff), BF16), pltpu.VMEM((d, ff), BF16), pltpu.VMEM((ff, d), BF16),
                        pltpu.VMEM((2, MOE_BLOCK) + table.shape[1:], table.dtype), pltpu.SemaphoreType.DMA((2,))],
    )
    return pl.pallas_call(
        _moe_gather_kernel,
        grid_spec=grid_spec,
        out_shape=jax.ShapeDtypeStruct((nb * MOE_BLOCK, d), BF16),
        compiler_params=pltpu.CompilerParams(dimension_semantics=("arbitrary",), vmem_limit_bytes=VMEM_LIMIT,
                                             disable_bounds_checks=True),
        name="gathered_swiglu",
    )(block_e, n_used, idx, idx, table, w_gate, w_up, w_down)


def grouped_swiglu(x, block_e, n_used, w_gate, w_up, w_down, layer):
    d, ff = w_gate.shape[-2:]
    nb = x.shape[0] // MOE_BLOCK
    grid_spec = pltpu.PrefetchScalarGridSpec(
        num_scalar_prefetch=2,
        grid=(nb,),
        in_specs=[pl.BlockSpec((MOE_BLOCK,) + x.shape[1:], lambda i, be, nu: (i, 0, 0)),
                  pl.BlockSpec((1, 1, d, ff), lambda i, be, nu: (layer, be[i], 0, 0)),
                  pl.BlockSpec((1, 1, d, ff), lambda i, be, nu: (layer, be[i], 0, 0)),
                  pl.BlockSpec((1, 1, ff, d), lambda i, be, nu: (layer, be[i], 0, 0))],
        out_specs=pl.BlockSpec((MOE_BLOCK, d), lambda i, be, nu: (i, 0)),
        scratch_shapes=[pltpu.VMEM((d, ff), BF16), pltpu.VMEM((d, ff), BF16), pltpu.VMEM((ff, d), BF16)],
    )
    return pl.pallas_call(
        _moe_kernel,
        grid_spec=grid_spec,
        out_shape=jax.ShapeDtypeStruct((nb * MOE_BLOCK, d), BF16),
        compiler_params=_cparams(("arbitrary",)),
        name="grouped_swiglu",
    )(block_e, n_used, x, w_gate, w_up, w_down)


def _combine_kernel(grp_ref, x_ref, y_ref, w_ref, s_ref, gate_ref, o_ref):
    del grp_ref
    f = s_ref[...].astype(F32)
    for k in range(TOP_K):
        f = f + w_ref[:, k:k + 1] * y_ref[k].astype(F32)
    o_ref[...] = x_ref[...] + gate_ref[0] * f


def moe_combine(x, yg, e_w, shared, gate3, grp):
    m, d = x.shape
    tm = TM
    grid_spec = pltpu.PrefetchScalarGridSpec(
        num_scalar_prefetch=1,
        grid=(m // tm,),
        in_specs=[pl.BlockSpec((tm, d), lambda i, grp: (i, 0)),
                  pl.BlockSpec((TOP_K, tm, d), lambda i, grp: (0, i, 0)),
                  pl.BlockSpec((tm, e_w.shape[1]), lambda i, grp: (i, 0)),
                  pl.BlockSpec((tm, d), lambda i, grp: (i, 0)),
                  pl.BlockSpec((1, 1, d), lambda i, grp: (grp[i], 0, 0))],
        out_specs=pl.BlockSpec((tm, d), lambda i, grp: (i, 0)),
    )
    return pl.pallas_call(
        _combine_kernel,
        grid_spec=grid_spec,
        out_shape=jax.ShapeDtypeStruct((m, d), F32),
        compiler_params=_cparams(("arbitrary",)),
        name="moe_combine",
    )(grp, x, yg, e_w, shared, gate3)


def _route_kernel(grp_ref, x_ref, g_ref, sh_ref, sc_ref, wt_ref, rb_ref, tri_ref, ones_ref,
                  h_ref, idx_ref, w_ref, rank_ref, cnt_ref, carry_ref):
    del grp_ref
    tm = x_ref.shape[0]
    gs = N_EXPERTS // N_GROUPS
    neg = -jnp.inf

    @pl.when(pl.program_id(0) == 0)
    def _():
        carry_ref[...] = jnp.zeros_like(carry_ref)

    x = x_ref[...]
    y = x * lax.rsqrt(jnp.mean(x * x, axis=-1, keepdims=True) + NORM_EPS)
    h = (y * g_ref[...]) * (1.0 + sc_ref[0]) + sh_ref[0]
    h_ref[...] = _pack_bf16_pairs(h)
    scores = jax.nn.sigmoid(_dot_nt(wt_ref[...], h, HI))
    biased = scores + rb_ref[...]

    def first_argmax(v, iota, n):
        m = jnp.max(v, axis=0, keepdims=True)
        return m, jnp.min(jnp.where(v == m, iota, float(n)), axis=0, keepdims=True)

    def stack_rows(rows):
        iota8 = lax.broadcasted_iota(jnp.int32, (8, tm), 0)
        out = jnp.zeros((8, tm), F32)
        for k, row in enumerate(rows):
            out = jnp.where(iota8 == k, row, out)
        return out

    assert gs == 8 and N_GROUPS == 8
    iota_g = lax.broadcasted_iota(jnp.int32, (gs, tm), 0).astype(F32)
    g_rows = []
    for g in range(N_GROUPS):
        bg = biased[g * gs:(g + 1) * gs]
        m1, i1 = first_argmax(bg, iota_g, gs)
        m2 = jnp.max(jnp.where(iota_g == i1, neg, bg), axis=0, keepdims=True)
        g_rows.append(m1 + m2)
    g_score = stack_rows(g_rows)
    g_sel = jnp.zeros((N_GROUPS, tm), F32)
    for _ in range(TOPK_GROUPS):
        _, ig = first_argmax(g_score, iota_g, N_GROUPS)
        hit = iota_g == ig
        g_sel = jnp.where(hit, 1.0, g_sel)
        g_score = jnp.where(hit, neg, g_score)
    e_sel = jnp.concatenate([jnp.broadcast_to(g_sel[g:g + 1], (gs, tm)) for g in range(N_GROUPS)], axis=0)
    masked = jnp.where(e_sel > 0.0, biased, neg)

    iota_e = lax.broadcasted_iota(jnp.int32, (N_EXPERTS, tm), 0).astype(F32)
    chosen = jnp.zeros((N_EXPERTS, tm), F32)
    hits, idx_rows, w_rows = [], [], []
    for _ in range(TOP_K):
        _, ie = first_argmax(masked, iota_e, N_EXPERTS)
        hit = iota_e == ie
        hits.append(hit)
        idx_rows.append(ie)
        w_rows.append(jnp.sum(jnp.where(hit, scores, 0.0), axis=0, keepdims=True))
        chosen = jnp.where(hit, 1.0, chosen)
        masked = jnp.where(hit, neg, masked)
    w_sum = w_rows[0]
    for wk in w_rows[1:]:
        w_sum = w_sum + wk
    idx_ref[...] = stack_rows(idx_rows).astype(jnp.int32)
    w_ref[...] = stack_rows([wk / w_sum * ROUTE_SCALE for wk in w_rows])

    chosen_b = chosen.astype(BF16)
    before = carry_ref[...] + _dot(chosen_b, tri_ref[...])
    rank_ref[...] = stack_rows([jnp.sum(jnp.where(hit, before, 0.0), axis=0, keepdims=True)
                                for hit in hits]).astype(jnp.int32)
    carry_ref[...] += _dot(chosen_b, ones_ref[...])
    cnt_ref[...] = carry_ref[:, :LANES].astype(jnp.int32)


def route(x, g, shift3, scale3, grp, router_w, router_b):
    m, d = x.shape
    ne = N_EXPERTS
    tri = jnp.asarray(np.triu(np.ones((TM, TM), np.float32), 1), BF16)
    ones = jnp.ones((TM, TM), BF16)
    rb = jnp.broadcast_to(router_b.astype(F32)[:, None], (ne, TM))
    row = lambda r: pl.BlockSpec((r, TM), lambda i, grp: (0, i))
    const = lambda shape: pl.BlockSpec(shape, lambda i, grp: (0, 0))
    grid_spec = pltpu.PrefetchScalarGridSpec(
        num_scalar_prefetch=1,
        grid=(m // TM,),
        in_specs=[pl.BlockSpec((TM, d), lambda i, grp: (i, 0)),
                  const((1, d)),
                  pl.BlockSpec((1, 1, d), lambda i, grp: (grp[i], 0, 0)),
                  pl.BlockSpec((1, 1, d), lambda i, grp: (grp[i], 0, 0)),
                  const((ne, d)), const((ne, TM)), const((TM, TM)), const((TM, TM))],
        out_specs=[pl.BlockSpec((TM, d // 2), lambda i, grp: (i, 0)), row(8), row(8), row(8), const((ne, LANES))],
        scratch_shapes=[pltpu.VMEM((ne, TM), F32)],
    )
    return pl.pallas_call(
        _route_kernel,
        grid_spec=grid_spec,
        out_shape=[jax.ShapeDtypeStruct((m, d // 2), jnp.uint32), jax.ShapeDtypeStruct((8, m), jnp.int32),
                   jax.ShapeDtypeStruct((8, m), F32), jax.ShapeDtypeStruct((8, m), jnp.int32),
                   jax.ShapeDtypeStruct((ne, LANES), jnp.int32)],
        compiler_params=_cparams(("arbitrary",)),
        name="route",
    )(grp, x, g.reshape(1, d), shift3, scale3, router_w.T, rb, tri, ones)


def _seq_edge_masks(n_batch, L, Lc):
    n_lat = n_batch * L
    starts = np.concatenate([np.arange(n_batch) * L, n_lat + np.arange(n_batch) * Lc])
    ends = np.concatenate([(np.arange(n_batch) + 1) * L, n_lat + (np.arange(n_batch) + 1) * Lc]) - 1
    first = np.ones((n_batch * (L + Lc), 1), np.float32)
    last = first.copy()
    first[starts] = 0.0
    last[ends] = 0.0
    return jnp.asarray(first), jnp.asarray(last)


RW_TM = 256


@functools.lru_cache(maxsize=None)
def _head_ones():
    h = np.arange(RW_WIDTH) // RW_HEAD_DIM
    return jnp.asarray(h[:, None] == h[None, :], BF16)


def _head_sum(x, ones):
    return _dot(x.astype(BF16), ones)


def _rwkv_prep_kernel(u_ref, w0_ref, w2_ref, a0_ref, a2_ref, g2_ref, kk_w_ref, ka_ref, ones_ref,
                      r_ref, v_ref, kk_ref, g_ref, lw_ref, av_ref, kd_ref):
    w = RW_WIDTH
    lo = 3 * w
    r_ref[...] = u_ref[:, 0:w]
    k = u_ref[:, w:2 * w]
    v_ref[...] = u_ref[:, 2 * w:lo]
    wl = jnp.tanh(u_ref[:, lo:lo + 2 * RW_DECAY_LORA]).astype(BF16)
    al = u_ref[:, lo + 2 * RW_DECAY_LORA:lo + 2 * RW_DECAY_LORA + 2 * RW_AAA_LORA].astype(BF16)
    gl = jax.nn.sigmoid(u_ref[:, lo + 2 * RW_DECAY_LORA + 2 * RW_AAA_LORA:]).astype(BF16)
    for d in range(2):
        z = -(w0_ref[d:d + 1, :] + _dot(wl, w2_ref[d]))
        softplus = jnp.maximum(z, 0.0) + jnp.log(1.0 + jnp.exp(-jnp.abs(z)))
        lw_ref[d] = -jnp.exp(-softplus - 0.5)
        a = jax.nn.sigmoid(a0_ref[d:d + 1, :] + _dot(al, a2_ref[d]))
        av_ref[d] = a
        kd_ref[d] = k * (1.0 + (a - 1.0) * ka_ref[...])
    g_ref[...] = _dot(gl, g2_ref[...])
    kk = k * kk_w_ref[...]
    norm = jnp.sqrt(_head_sum(kk * kk, ones_ref[...]))
    kk_ref[...] = kk / jnp.maximum(norm, 1e-12)


def rwkv_prep(u, w0, w2, a0, a2, g2, k_k, k_a):
    m = u.shape[0]
    w = RW_WIDTH
    zeros = jnp.zeros((RW_DECAY_LORA, w), F32)
    pad2 = lambda t: jnp.stack([jnp.concatenate([t[0], zeros], axis=0),
                                jnp.concatenate([zeros, t[1]], axis=0)]).astype(BF16)
    assert RW_DECAY_LORA == RW_AAA_LORA and 2 * RW_DECAY_LORA == LANES and RW_GATE_LORA == LANES
    full = lambda shape: pl.BlockSpec(shape, lambda i: (0,) * len(shape))
    row = pl.BlockSpec((RW_TM, w), lambda i: (i, 0))
    row2 = pl.BlockSpec((2, RW_TM, w), lambda i: (0, i, 0))
    one = jax.ShapeDtypeStruct((m, w), F32)
    two = jax.ShapeDtypeStruct((2, m, w), F32)
    return pl.pallas_call(
        _rwkv_prep_kernel,
        grid=(m // RW_TM,),
        in_specs=[pl.BlockSpec((RW_TM, RW_COLS), lambda i: (i, 0)), full((2, w)), full((2, LANES, w)), full((2, w)),
                  full((2, LANES, w)), full((LANES, w)), full((1, w)), full((1, w)), full((w, w))],
        out_specs=[row, row, row, row, row2, row2, row2],
        out_shape=[one, one, one, one, two, two, two],
        compiler_params=_cparams(("arbitrary",)),
        name="rwkv_prep",
    )(u, w0, pad2(w2), a0, pad2(a2), g2.astype(BF16), k_k.reshape(1, w), k_a.reshape(1, w), _head_ones())


def _rwkv_out_kernel(y0_ref, y1_ref, r_ref, v_ref, g_ref, kd_ref, rk_ref, lnw_ref, lnb_ref, ones_ref, o_ref):
    ones = ones_ref[...]
    inv_n = 1.0 / RW_HEAD_DIM
    y = y0_ref[...] + y1_ref[...]
    yc = y - _head_sum(y, ones) * inv_n
    var = _head_sum(yc * yc, ones) * inv_n
    yn = yc * lax.rsqrt(var + RW_GN_EPS)
    bonus = _head_sum(r_ref[...] * (kd_ref[0] + kd_ref[1]) * rk_ref[...], ones) * v_ref[...]
    o_ref[...] = ((yn * lnw_ref[...] + lnb_ref[...] + bonus) * g_ref[...]).astype(o_ref.dtype)


def rwkv_out(y0, y1, r, v, g, kd, r_k, ln_w, ln_b):
    m, w = r.shape
    full = lambda shape: pl.BlockSpec(shape, lambda i: (0,) * len(shape))
    row = pl.BlockSpec((RW_TM, w), lambda i: (i, 0))
    vec = lambda t: t.reshape(1, w)
    return pl.pallas_call(
        _rwkv_out_kernel,
        grid=(m // RW_TM,),
        in_specs=[row, row, row, row, row, pl.BlockSpec((2, RW_TM, w), lambda i: (0, i, 0)),
                  full((1, w)), full((1, w)), full((1, w)), full((w, w))],
        out_specs=row,
        out_shape=jax.ShapeDtypeStruct((m, w), BF16),
        compiler_params=_cparams(("arbitrary",)),
        name="rwkv_out",
    )(y0, y1, r, v, g, kd, vec(r_k), vec(ln_w), vec(ln_b), _head_ones())


def _moe(h, e_idx, rank, counts, exp_gate, exp_up, exp_down, sh_gate, sh_up, sh_down, layer):
    T = h.shape[0]
    D = exp_gate.shape[-2]
    n = T * TOP_K
    padded = (counts + MOE_BLOCK - 1) // MOE_BLOCK * MOE_BLOCK
    pad_end = jnp.cumsum(padded)
    pad_start = pad_end - padded
    experts = jnp.arange(N_EXPERTS, dtype=jnp.int32)
    dest = rank + jnp.sum(jnp.where(e_idx[:, :, None] == experts, pad_start.astype(jnp.int32), 0), axis=-1)
    n_blocks = -(-n // MOE_BLOCK) + N_EXPERTS
    n_slots = n_blocks * MOE_BLOCK
    flat_dest = dest.reshape(-1)
    tok = jnp.tile(jnp.arange(T, dtype=jnp.int32), TOP_K)
    slot_tok = jnp.zeros((n_slots,), jnp.int32).at[flat_dest].set(tok, unique_indices=True)
    block_start = jnp.arange(n_blocks, dtype=jnp.int32) * MOE_BLOCK
    block_e = jnp.minimum(jnp.sum(block_start[:, None] >= pad_end[None, :], axis=1), N_EXPERTS - 1).astype(jnp.int32)
    n_used = (pad_end[-1] // MOE_BLOCK).astype(jnp.int32).reshape(1)
    h = h.reshape(T, -1, LANES)
    y = gathered_swiglu(h, slot_tok, block_e, n_used, exp_gate, exp_up, exp_down, layer)
    yg = jnp.take(y, flat_dest, axis=0, mode="clip").reshape(TOP_K, T, D)
    nb_sh = T // MOE_BLOCK
    sh4 = lambda w: w.reshape((w.shape[0], 1) + w.shape[1:])
    shared = grouped_swiglu(h, jnp.zeros((nb_sh,), jnp.int32), jnp.full((1,), nb_sh, jnp.int32),
                            sh4(sh_gate), sh4(sh_up), sh4(sh_down), layer)
    return yg, shared


def kernel(x, c, ctx, c_ctx, mod_w, mod_b, norm1_g, norm2_g, w_in, hy_conv, hy_w1, hy_b1, hy_w2, hy_b2, hy_freq,
           hy_w3, hy_bias, na_rpb, rw_shift, rw_w0, rw_w2, rw_a0, rw_a2, rw_g2, rw_kk, rw_ka, rw_rk, rw_ln_w,
           rw_ln_b, proj_a, proj_b, proj_c, w_out, router_w, router_b, exp_gate, exp_up, exp_down, sh_gate, sh_up,
           sh_down, final_g):
    B, L, D = x.shape
    Lc = ctx.shape[1]
    depth = mod_w.shape[0]
    n_lat, n_ctx = B * L, B * Lc
    assert L % TM == 0 and n_ctx % TM == 0 and L % WKV_CHUNK == 0 and Lc % WKV_CHUNK == 0
    col_hy = 3 * HY_WIDTH
    col_na = col_hy + 3 * NA_WIDTH
    col_rw = col_na + RW_COLS

    xs = jnp.concatenate([x.reshape(n_lat, D), ctx.reshape(n_ctx, D)], axis=0)
    grp_all = jnp.asarray(np.concatenate([np.repeat(np.arange(B), L // TM), np.full(n_ctx // TM, B)]), jnp.int32)
    s8 = jnp.zeros((8, D), F32).at[:B].set(jax.nn.silu(c)).at[B].set(jax.nn.silu(c_ctx))
    first, last = _seq_edge_masks(B, L, Lc)

    for i in range(depth):
        with_ctx = i < depth - 1
        mod = small_matmul_bias(s8, mod_w, mod_b, i)[:B + 1].reshape(B + 1, 1, N_MOD * D)
        sh1, sc1, g1, sh2, sc2, g2 = (mod[:, :, j * D:(j + 1) * D] for j in range(N_MOD))
        w_bf = w_in[i].astype(BF16)
        proj = functools.partial(normmod_matmul, xs, norm1_g[i], sh1, sc1, grp_all)
        hy = short_conv(proj(w_bf[:, :col_hy], HY_WIDTH, BF16), hy_conv[i], first, last, HY_WIDTH)
        na = proj(w_bf[:, col_hy:col_na], NA_WIDTH, BF16)
        rw = proj(w_bf[:, col_na:col_rw], RW_COLS // 3, F32)
        gates = proj(w_bf[:, col_rw:], D // 2, BF16)

        hy_args = (hy_w1[i], hy_b1[i], hy_w2[i], hy_b2[i], hy_freq[i], hy_w3[i])
        h_raw, ss = hyena_filters_raw(L, *hy_args)
        hr, hi = hyena_filter_spectrum(h_raw, _filter_scale(ss), L)
        z = hyena_conv(hy, 0, hy, 2, hr, hi, 0, hy_bias[i][0], B, L)
        o_a = hyena_conv(hy, 1, z, 0, hr, hi, 1, hy_bias[i][1], B, L).astype(BF16)
        o_b = na_latent(na, na_rpb[i], B, L, Lc)
        rw = short_conv(rw, rw_shift[i], first, last, RW_COLS // 3)
        r_, v_, kk_, gg_, lw_, av_, kd_ = rwkv_prep(rw, rw_w0[i], rw_w2[i], rw_a0[i], rw_a2[i], rw_g2[i], rw_kk[i],
                                                    rw_ka[i])
        y_f, y_b = wkv_scan(r_, v_, kk_, lw_, av_, kd_, B, L, Lc)
        o_c = rwkv_out(y_f, y_b, r_, v_, gg_, kd_, rw_rk[i], rw_ln_w[i], rw_ln_b[i])

        if with_ctx:
            h_raw_c, ss_c = hyena_filters_raw(Lc, *hy_args)
            o_a_c = hyena_small(hy, n_lat // Lc, B, Lc, h_raw_c, _filter_scale(ss_c), hy_bias[i])
            o_a = jnp.concatenate([o_a, o_a_c], axis=0)
            o_b = jnp.concatenate([o_b, ctx_attn(na, B, L, Lc)], axis=0)
            m_rows = n_lat + n_ctx
        else:
            m_rows = n_lat
        grp = grp_all[:m_rows // TM]
        merged = branch_merge(m_rows, o_a, o_b, o_c, gates, proj_a[i].astype(BF16), proj_b[i].astype(BF16),
                              proj_c[i].astype(BF16))
        xs = resid_matmul(merged, w_out[i].astype(BF16), xs, g1, grp)

        h2, e_idx, e_w, rank, counts = route(xs, norm2_g[i], sh2, sc2, grp, router_w[i], router_b[i])
        yg, shared = _moe(h2, e_idx[:TOP_K], rank[:TOP_K], counts[:, 0], exp_gate, exp_up, exp_down, sh_gate, sh_up,
                          sh_down, i)
        xs = moe_combine(xs, yg, e_w.T, shared, g2, grp)

    return rmsnorm_rows(xs, final_g).reshape(B, L, D)
```
